```python
import math
import jax
import jax.numpy as jnp
from jax import lax
import numpy as np

D_MODEL = 1024
BATCH = 32
SEQ = 256
DEPTH = 4
DEC_BATCH = 4
DEC_SEQ = 1024
PAST_LEN = 256

GRID_W = 64
N_EVEN = (DEPTH + 1) // 2
N_ODD = DEPTH // 2
H_A = 8
DK_A = D_MODEL // 16
DV_A = D_MODEL // 16
CHUNK = 64
CONV_W = 4
H_B = 8
KV_B = 2
G_B = H_B // KV_B
HD_B = D_MODEL // 16
WINDOW = 128
QBLK = 128
ROPE_AXIS = HD_B // 2
ROPE_BASE = 10000.0
D_RNN = D_MODEL
LRU_BLOCKS = 16
LRU_BW = D_RNN // LRU_BLOCKS
RG_C = 8.0
D_FF = 4 * D_MODEL
CONV_CH = 2 * H_A * DK_A + H_A * DV_A
MIX_W = H_A * DV_A + H_B * HD_B
IN_AB = CONV_CH + H_A * DV_A + 4 * H_A + (H_B + 2 * KV_B) * HD_B
EPS = 1e-6
NEG_INF = -1e30

kernel_name = 'hybrid_diffusion_step_deltanet_swa_rglru'

f32 = jnp.float32


def rms_norm(x, g):
    xf = x.astype(f32)
    y = xf * lax.rsqrt(jnp.mean(xf * xf, axis=-1, keepdims=True) + EPS)
    return (y * g.astype(f32)).astype(x.dtype)


def l2_norm(x):
    xf = x.astype(f32)
    return xf * lax.rsqrt(jnp.sum(xf * xf, axis=-1, keepdims=True) + EPS)


def adaln(cond, w, b):
    mod = jax.nn.silu(cond) @ w + b
    return jnp.split(mod[..., None, :], 6, axis=-1)


def modulate(x, g, shift, scale):
    return (rms_norm(x, g) * (1.0 + scale) + shift).astype(x.dtype)


def centred_depthwise_conv(x, w, b):
    left = CONV_W // 2
    right = CONV_W - 1 - left
    y = lax.conv_general_dilated(x, w[:, None, :].astype(x.dtype), window_strides=(1,),
                                 padding=[(left, right)], dimension_numbers=('NWC', 'WIO', 'NWC'),
                                 feature_group_count=x.shape[-1])
    return y + b.astype(x.dtype)


def axial_angles(n_tokens):
    rows = n_tokens // GRID_W
    r, cc = jnp.meshgrid(jnp.arange(rows, dtype=f32), jnp.arange(GRID_W, dtype=f32), indexing='ij')
    inv = ROPE_BASE ** (-jnp.arange(0, ROPE_AXIS, 2, dtype=f32) / ROPE_AXIS)
    return r.reshape(-1)[:, None] * inv, cc.reshape(-1)[:, None] * inv


def _rotate(x, ang):
    x1, x2 = jnp.split(x, 2, axis=-1)
    cos = jnp.cos(ang)[None, :, None, :]
    sin = jnp.sin(ang)[None, :, None, :]
    return jnp.concatenate([x1 * cos - x2 * sin, x2 * cos + x1 * sin], axis=-1)


def apply_axial_rope(x, ang_r, ang_c):
    xf = x.astype(f32)
    y = jnp.concatenate([_rotate(xf[..., :ROPE_AXIS], ang_r), _rotate(xf[..., ROPE_AXIS:], ang_c)], axis=-1)
    return y.astype(x.dtype)


def chunk_gated_delta(q, k, v, g, beta, s0):
    B, T, H, dk = q.shape
    dv = v.shape[-1]
    n = T // CHUNK

    def to_chunks(x):
        x = x.astype(f32).reshape((B, n, CHUNK, H) + x.shape[3:])
        return jnp.moveaxis(x, 3, 1)

    q = to_chunks(q) * (dk ** -0.5)
    k = to_chunks(k)
    v = to_chunks(v)
    beta = to_chunks(beta)
    gc = jnp.cumsum(to_chunks(g), axis=-1)
    causal = jnp.tril(jnp.ones((CHUNK, CHUNK), dtype=bool))
    strict = causal & ~jnp.eye(CHUNK, dtype=bool)
    diff = gc[..., :, None] - gc[..., None, :]
    decay = jnp.where(causal, jnp.exp(jnp.where(causal, diff, 0.0)), 0.0)
    k_beta = k * beta[..., None]
    v_beta = v * beta[..., None]
    a_low = jnp.where(strict, jnp.einsum('bhncd,bhnsd->bhncs', k_beta, k) * decay, 0.0)
    eye = jnp.eye(CHUNK, dtype=f32)
    t_inv = lax.linalg.triangular_solve(a_low + eye, jnp.broadcast_to(eye, a_low.shape),
                                        left_side=True, lower=True)
    u = t_inv @ v_beta
    w = t_inv @ (k_beta * jnp.exp(gc)[..., None])
    intra = jnp.where(causal, jnp.einsum('bhncd,bhnsd->bhncs', q, k) * decay, 0.0)
    xs = tuple(jnp.moveaxis(t, 2, 0) for t in (q, k, u, w, gc, intra))

    def step(S, inp):
        q_i, k_i, u_i, w_i, g_i, a_i = inp
        v_new = u_i - w_i @ S
        o_i = (q_i * jnp.exp(g_i)[..., None]) @ S + a_i @ v_new
        g_last = g_i[..., -1:]
        S = S * jnp.exp(g_last)[..., None] + jnp.einsum(
            'bhcd,bhce->bhde', k_i * jnp.exp(g_last - g_i)[..., None], v_new)
        return S, o_i

    s_fin, o = lax.scan(step, s0.astype(f32), xs)
    o = jnp.transpose(o, (1, 0, 3, 2, 4)).reshape(B, T, H, dv)
    return o, s_fin


def bidir_delta(q, k, v, g2, beta2, s0):
    out = 0.0
    finals = []
    for d in range(2):
        args = (q, k, v, g2[:, :, d], beta2[:, :, d])
        if d == 1:
            args = tuple(jnp.flip(a, axis=1) for a in args)
        o, s = chunk_gated_delta(*args, s0[:, d])
        if d == 1:
            o = jnp.flip(o, axis=1)
        out = out + o
        finals.append(s)
    return out, jnp.stack(finals, axis=1)


def _sink_softmax(scores, sink):
    s = jnp.broadcast_to(sink.astype(f32).reshape(1, KV_B, G_B, 1, 1), scores.shape[:-1] + (1,))
    p = jax.nn.softmax(jnp.concatenate([scores, s], axis=-1), axis=-1)
    return p[..., :-1]


def context_attention(q, k, v, sink):
    B, S, H, hd = q.shape
    nb = S // QBLK
    qb = jnp.moveaxis(q.astype(f32).reshape(B, nb, QBLK, KV_B, G_B, hd), 1, 0)
    kf = k.astype(f32)
    vf = v.astype(f32)
    scale = hd ** -0.5

    def block(q_blk):
        sc = jnp.einsum('bqkgd,bskd->bkgqs', q_blk, kf) * scale
        p = _sink_softmax(sc, sink)
        return jnp.einsum('bkgqs,bskd->bqkgd', p, vf)

    o = lax.map(block, qb)
    return jnp.moveaxis(o, 0, 1).reshape(B, S, H * hd)


def latent_window_attention(q, k, v, k_ctx, v_ctx, sink):
    B, L, H, hd = q.shape
    nb = L // QBLK
    span = QBLK + 2 * WINDOW
    qg = q.astype(f32).reshape(B, L, KV_B, G_B, hd)
    pad = ((0, 0), (WINDOW, WINDOW), (0, 0), (0, 0))
    kp = jnp.pad(k.astype(f32), pad)
    vp = jnp.pad(v.astype(f32), pad)
    kc = k_ctx.astype(f32)
    vc = v_ctx.astype(f32)
    scale = hd ** -0.5

    def block(i):
        start = i * QBLK
        q_blk = lax.dynamic_slice_in_dim(qg, start, QBLK, axis=1)
        k_win = lax.dynamic_slice_in_dim(kp, start, span, axis=1)
        v_win = lax.dynamic_slice_in_dim(vp, start, span, axis=1)
        q_pos = start + jnp.arange(QBLK)
        k_pos = start - WINDOW + jnp.arange(span)
        valid = ((jnp.abs(q_pos[:, None] - k_pos[None, :]) <= WINDOW)
                 & (k_pos[None, :] >= 0) & (k_pos[None, :] < L))
        s_win = jnp.where(valid, jnp.einsum('bqkgd,bskd->bkgqs', q_blk, k_win) * scale, NEG_INF)
        s_ctx = jnp.einsum('bqkgd,bskd->bkgqs', q_blk, kc) * scale
        p = _sink_softmax(jnp.concatenate([s_win, s_ctx], axis=-1), sink)
        return (jnp.einsum('bkgqs,bskd->bqkgd', p[..., :span], v_win)
                + jnp.einsum('bkgqs,bskd->bqkgd', p[..., span:], vc))

    o = lax.map(block, jnp.arange(nb))
    return jnp.moveaxis(o, 0, 1).reshape(B, L, H * hd)


def mixer_ab(h, w_in, conv_w, conv_b, a_log, dt_bias, dn_norm_g, q_norm_g, k_norm_g, sink, w_out, ctx=None):
    B, T, _ = h.shape
    s1 = CONV_CH
    s2 = s1 + H_A * DV_A
    s3 = s2 + 2 * H_A
    s4 = s3 + 2 * H_A
    s5 = s4 + H_B * HD_B
    s6 = s5 + KV_B * HD_B
    qkv_a, gate_a, alpha, beta_l, q_b, k_b, v_b = jnp.split(h @ w_in, [s1, s2, s3, s4, s5, s6], axis=-1)
    qkv_a = jax.nn.silu(centred_depthwise_conv(qkv_a, conv_w, conv_b))
    q_a, k_a, v_a = jnp.split(qkv_a, [H_A * DK_A, 2 * H_A * DK_A], axis=-1)
    q_a = l2_norm(q_a.reshape(B, T, H_A, DK_A))
    k_a = l2_norm(k_a.reshape(B, T, H_A, DK_A))
    v_a = v_a.reshape(B, T, H_A, DV_A)
    alpha = alpha.astype(f32).reshape(B, T, 2, H_A)
    beta = jax.nn.sigmoid(beta_l.astype(f32).reshape(B, T, 2, H_A))
    g = -jnp.exp(a_log.astype(f32)) * jax.nn.softplus(alpha + dt_bias.astype(f32))
    s0 = jnp.zeros((B, 2, H_A, DK_A, DV_A), f32) if ctx is None else ctx[0]
    o_a, s_fin = bidir_delta(q_a, k_a, v_a, g, beta, s0)
    o_a = rms_norm(o_a, dn_norm_g) * jax.nn.silu(gate_a.astype(f32)).reshape(B, T, H_A, DV_A)
    o_a = o_a.reshape(B, T, H_A * DV_A)
    q_b = rms_norm(q_b.reshape(B, T, H_B, HD_B), q_norm_g)
    k_b = rms_norm(k_b.reshape(B, T, KV_B, HD_B), k_norm_g)
    v_b = v_b.reshape(B, T, KV_B, HD_B)
    if ctx is None:
        o_b = context_attention(q_b, k_b, v_b, sink)
        new = (s_fin, k_b, v_b)
    else:
        ang_r, ang_c = axial_angles(T)
        o_b = latent_window_attention(apply_axial_rope(q_b, ang_r, ang_c), apply_axial_rope(k_b, ang_r, ang_c),
                                      v_b, ctx[1], ctx[2], sink)
        new = None
    out = jnp.concatenate([o_a.astype(h.dtype), o_b.astype(h.dtype)], axis=-1) @ w_out
    return out, new


def linear_scan(a, b, h0):
    b = b.at[:, 0].add(a[:, 0] * h0)

    def comb(l, r):
        return (l[0] * r[0], r[0] * l[1] + r[1])

    _, hs = lax.associative_scan(comb, (a, b), axis=1)
    return hs, hs[:, -1]


def mixer_c(h, w_in, conv_w, conv_b, w_a, b_a, w_x, b_x, lam, w_out, h0=None):
    B, T, _ = h.shape
    x, gate = jnp.split(h @ w_in, 2, axis=-1)
    x = centred_depthwise_conv(x, conv_w, conv_b).astype(f32)
    xb = x.reshape(B, T, LRU_BLOCKS, LRU_BW)
    if h0 is None:
        h0 = jnp.zeros((B, 2, D_RNN), f32)
    acc = 0.0
    finals = []
    for d in range(2):
        r = jax.nn.sigmoid(jnp.einsum('btnc,ncd->btnd', xb, w_a[d].astype(f32)).reshape(B, T, D_RNN)
                           + b_a[d].astype(f32))
        i = jax.nn.sigmoid(jnp.einsum('btnc,ncd->btnd', xb, w_x[d].astype(f32)).reshape(B, T, D_RNN)
                           + b_x[d].astype(f32))
        log_a = -RG_C * r * jax.nn.softplus(-lam[d].astype(f32))
        a = jnp.exp(log_a)
        u = jnp.sqrt(-jnp.expm1(2.0 * log_a)) * (i * x)
        if d == 1:
            a, u = jnp.flip(a, axis=1), jnp.flip(u, axis=1)
        hs, h_last = linear_scan(a, u, h0[:, d].astype(f32))
        if d == 1:
            hs = jnp.flip(hs, axis=1)
        acc = acc + hs
        finals.append(h_last)
    y = acc * jax.nn.gelu(gate.astype(f32))
    return y.astype(h.dtype) @ w_out, jnp.stack(finals, axis=1)


def sq_relu_mlp(h, w1, w2):
    return jnp.square(jax.nn.relu(h @ w1)) @ w2


def setup_inputs(seed: int = 0) -> dict:
    key = jax.random.key(seed)
    ks = iter(jax.random.split(key, 48))

    def nrm(shape, scale):
        return jax.random.normal(next(ks), shape, f32) * scale

    def unif(shape, lo, hi):
        return jax.random.uniform(next(ks), shape, f32, lo, hi)

    dt = jnp.exp(unif((N_EVEN, 2, H_A), math.log(1e-3), math.log(1e-1)))
    a0 = unif((N_ODD, 2, D_RNN), 0.9, 0.999) ** (1.0 / RG_C)
    return {
        'x_prompt': nrm((BATCH, SEQ, D_MODEL), 1.0),
        'x_sample': nrm((DEC_BATCH, DEC_SEQ, D_MODEL), 1.0),
        'state_delta': nrm((DEC_BATCH, N_EVEN, 2, H_A, DK_A, DV_A), 0.5),
        'cache_k': nrm((DEC_BATCH, N_EVEN, PAST_LEN, KV_B, HD_B), 1.0),
        'cache_v': nrm((DEC_BATCH, N_EVEN, PAST_LEN, KV_B, HD_B), 1.0),
        'state_lru': nrm((DEC_BATCH, N_ODD, 2, D_RNN), 0.5),
        'c': nrm((DEC_BATCH, D_MODEL), 1.0),
        'c_ctx': nrm((D_MODEL,), 1.0),
        'ada_w': nrm((DEPTH, D_MODEL, 6 * D_MODEL), 0.5 * D_MODEL ** -0.5),
        'ada_b': nrm((DEPTH, 6 * D_MODEL), 0.01),
        'norm1_g': 1.0 + nrm((DEPTH, D_MODEL), 0.01),
        'norm2_g': 1.0 + nrm((DEPTH, D_MODEL), 0.01),
        'ff_w1': nrm((DEPTH, D_MODEL, D_FF), D_MODEL ** -0.5),
        'ff_w2': nrm((DEPTH, D_FF, D_MODEL), D_FF ** -0.5),
        'ab_w_in': nrm((N_EVEN, D_MODEL, IN_AB), D_MODEL ** -0.5),
        'ab_conv_w': nrm((N_EVEN, CONV_W, CONV_CH), CONV_W ** -0.5),
        'ab_conv_b': nrm((N_EVEN, CONV_CH), 0.01),
        'dn_a_log': jnp.log(unif((N_EVEN, 2, H_A), 1.0, 16.0)),
        'dn_dt_bias': dt + jnp.log(-jnp.expm1(-dt)),
        'dn_norm_g': 1.0 + nrm((N_EVEN, DV_A), 0.01),
        'attn_q_norm_g': 1.0 + nrm((N_EVEN, HD_B), 0.01),
        'attn_k_norm_g': 1.0 + nrm((N_EVEN, HD_B), 0.01),
        'attn_sink': nrm((N_EVEN, H_B), 0.5),
        'ab_w_out': nrm((N_EVEN, MIX_W, D_MODEL), MIX_W ** -0.5),
        'c_w_in': nrm((N_ODD, D_MODEL, 2 * D_RNN), D_MODEL ** -0.5),
        'c_conv_w': nrm((N_ODD, CONV_W, D_RNN), CONV_W ** -0.5),
        'c_conv_b': nrm((N_ODD, D_RNN), 0.01),
        'lru_w_a': nrm((N_ODD, 2, LRU_BLOCKS, LRU_BW, LRU_BW), LRU_BW ** -0.5),
        'lru_b_a': nrm((N_ODD, 2, D_RNN), 0.01),
        'lru_w_x': nrm((N_ODD, 2, LRU_BLOCKS, LRU_BW, LRU_BW), LRU_BW ** -0.5),
        'lru_b_x': nrm((N_ODD, 2, D_RNN), 0.01),
        'lru_lambda': jnp.log(a0) - jnp.log1p(-a0),
        'c_w_out': nrm((N_ODD, D_RNN, D_MODEL), D_RNN ** -0.5),
    }


def reference(x_prompt, x_sample, state_delta, cache_k, cache_v, state_lru, c, c_ctx,
              ada_w, ada_b, norm1_g, norm2_g, ff_w1, ff_w2,
              ab_w_in, ab_conv_w, ab_conv_b, dn_a_log, dn_dt_bias, dn_norm_g,
              attn_q_norm_g, attn_k_norm_g, attn_sink, ab_w_out,
              c_w_in, c_conv_w, c_conv_b, lru_w_a, lru_b_a, lru_w_x, lru_b_x, lru_lambda, c_w_out):
    yp = x_prompt
    ys = x_sample
    new_dn, new_k, new_v, new_lru = [], [], [], []
    for l in range(DEPTH):
        j = l // 2
        mp = adaln(c_ctx, ada_w[l], ada_b[l])
        ms = adaln(c, ada_w[l], ada_b[l])
        hp = modulate(yp, norm1_g[l], mp[0], mp[1])
        hs = modulate(ys, norm1_g[l], ms[0], ms[1])
        if l % 2 == 0:
            ab = (ab_w_in[j], ab_conv_w[j], ab_conv_b[j], dn_a_log[j], dn_dt_bias[j], dn_norm_g[j],
                  attn_q_norm_g[j], attn_k_norm_g[j], attn_sink[j], ab_w_out[j])
            op, (s_dn, k_ctx, v_ctx) = mixer_ab(hp, *ab)
            os_, _ = mixer_ab(hs, *ab, ctx=(state_delta[:, j], cache_k[:, j], cache_v[:, j]))
            new_dn.append(s_dn)
            new_k.append(k_ctx)
            new_v.append(v_ctx)
        else:
            cp = (c_w_in[j], c_conv_w[j], c_conv_b[j], lru_w_a[j], lru_b_a[j], lru_w_x[j], lru_b_x[j],
                  lru_lambda[j], c_w_out[j])
            op, s_lru = mixer_c(hp, *cp)
            os_, _ = mixer_c(hs, *cp, h0=state_lru[:, j])
            new_lru.append(s_lru)
        yp = yp + (mp[2] * op).astype(yp.dtype)
        ys = ys + (ms[2] * os_).astype(ys.dtype)
        yp = yp + (mp[5] * sq_relu_mlp(modulate(yp, norm2_g[l], mp[3], mp[4]), ff_w1[l], ff_w2[l])).astype(yp.dtype)
        ys = ys + (ms[5] * sq_relu_mlp(modulate(ys, norm2_g[l], ms[3], ms[4]), ff_w1[l], ff_w2[l])).astype(ys.dtype)
    return (yp, ys, jnp.stack(new_dn, axis=1), jnp.stack(new_k, axis=1), jnp.stack(new_v, axis=1), jnp.stack(new_lru, axis=1))
```

```python
import functools
import math

import jax
import jax.numpy as jnp
from jax import lax
from jax.experimental import pallas as pl
from jax.experimental.pallas import tpu as pltpu

f32 = jnp.float32
bf16 = jnp.bfloat16

D_MODEL = 1024
DEPTH = 4
GRID_W = 64
H_A = 8
DK_A = 64
DV_A = 64
CHUNK = 64
CONV_W = 4
H_B = 8
KV_B = 2
G_B = H_B // KV_B
HD_B = 64
WINDOW = 128
QBLK = 128
ROPE_AXIS = HD_B // 2
ROPE_BASE = 10000.0
D_RNN = D_MODEL
LRU_BLOCKS = 16
LRU_BW = D_RNN // LRU_BLOCKS
RG_C = 8.0
D_FF = 4 * D_MODEL
EPS = 1e-6
NEG_INF = -1e30

TM = 1024
LANES = 128
HEAD_COLS = 4 * DK_A
N_DELTA = H_A * HEAD_COLS
N_ATTN = H_A * DV_A + (H_B + 2 * KV_B) * HD_B
VMEM_LIMIT = 56 * 1024 * 1024


def _cparams(sem):
    return pltpu.CompilerParams(dimension_semantics=sem, vmem_limit_bytes=VMEM_LIMIT)


def _dot(a, b):
    return jnp.dot(a.astype(bf16), b.astype(bf16), preferred_element_type=f32)


def _dot_nt(a, b):
    return lax.dot_general(a.astype(bf16), b.astype(bf16), (((1,), (1,)), ((), ())),
                           preferred_element_type=f32)


def _dot_tn(a, b):
    return lax.dot_general(a.astype(bf16), b.astype(bf16), (((0,), (0,)), ((), ())),
                           preferred_element_type=f32)


def _sigmoid(x):
    return 1.0 / (1.0 + jnp.exp(-x))


def _softplus(x):
    return jnp.maximum(x, 0.0) + jnp.log1p(jnp.exp(-jnp.abs(x)))


def _modulate(x, g, shift, scale):
    ms = jnp.mean(x * x, axis=-1, keepdims=True)
    y = x * lax.rsqrt(ms + EPS) * g
    return y * (1.0 + scale) + shift


def _split2(x):
    hi = x.astype(bf16)
    lo = (x - hi.astype(f32)).astype(bf16)
    return hi, lo


def _group_sum64(v):
    w = v.shape[-1]
    r = lax.broadcasted_iota(jnp.int32, (w, w), 0) >> 6
    c = lax.broadcasted_iota(jnp.int32, (w, w), 1) >> 6
    ones_bd = jnp.where(r == c, 1.0, 0.0).astype(bf16)
    hi, lo = _split2(v)
    return (jnp.dot(hi, ones_bd, preferred_element_type=f32)
            + jnp.dot(lo, ones_bd, preferred_element_type=f32))


def _seq_conv(z, cw_ref, cb_ref, pos, tseq):
    n = z.shape[0]
    acc = jnp.zeros_like(z) + cb_ref[...]
    left = CONV_W // 2
    for j in range(CONV_W):
        o = j - left
        zs = z if o == 0 else pltpu.roll(z, (-o) % n, axis=0)
        valid = (pos + o >= 0) & (pos + o < tseq)
        acc = acc + jnp.where(valid, zs, 0.0) * cw_ref[j:j + 1, :]
    return acc


def _adaln_kernel(c_ref, w_ref, b_ref, o_ref):
    c = c_ref[...]
    a = c * _sigmoid(c)
    o_ref[...] = _dot(a, w_ref[...]) + b_ref[...]


def _adaln(cond, ada_w, ada_b):
    rows = cond.shape[0]
    out = pl.pallas_call(
        _adaln_kernel,
        grid=(DEPTH, 6),
        in_specs=[
            pl.BlockSpec((rows, D_MODEL), lambda l, k: (0, 0)),
            pl.BlockSpec((None, D_MODEL, D_MODEL), lambda l, k: (l, 0, k)),
            pl.BlockSpec((None, None, 1, D_MODEL), lambda l, k: (l, k, 0, 0)),
        ],
        out_specs=pl.BlockSpec((None, None, rows, D_MODEL), lambda l, k: (l, k, 0, 0)),
        out_shape=jax.ShapeDtypeStruct((DEPTH, 6, rows, D_MODEL), f32),
        compiler_params=_cparams(("arbitrary", "arbitrary")),
        name="adaln",
    )(cond, ada_w, ada_b.reshape(DEPTH, 6, 1, D_MODEL))
    return out.reshape(DEPTH, 6, rows, 1, D_MODEL)


def _mod_spec(layer, k, row_of_tile, ngrid):
    if ngrid == 1:
        return pl.BlockSpec((None, None, None, 1, D_MODEL), lambda i: (layer, k, row_of_tile(i), 0, 0))
    return pl.BlockSpec((None, None, None, 1, D_MODEL), lambda i, j: (layer, k, row_of_tile(i), 0, 0))


def _delta_proj_kernel(np_tiles, t_p, t_s, x_ref, g_ref, sh_ref, sc_ref, w_ref, cw_ref, cb_ref,
                       al_ref, dt_ref, qk_ref, vx_ref, h_s):
    i = pl.program_id(0)

    @pl.when(pl.program_id(1) == 0)
    def _():
        h_s[...] = _modulate(x_ref[...], g_ref[...], sh_ref[...], sc_ref[...]).astype(bf16)

    z = jnp.dot(h_s[...], w_ref[...].astype(bf16), preferred_element_type=f32)
    tseq = jnp.where(i < np_tiles, t_p, t_s)
    pos = lax.broadcasted_iota(jnp.int32, (TM, 1), 0) & (tseq - 1)
    y = _seq_conv(z, cw_ref, cb_ref, pos, tseq)
    y = y * _sigmoid(y)
    lane = lax.broadcasted_iota(jnp.int32, (1, z.shape[1]), 1) & (HEAD_COLS - 1)
    yn = y * lax.rsqrt(_group_sum64(y * y) + EPS)
    g_val = -jnp.exp(al_ref[...]) * _softplus(z + dt_ref[...])
    b_val = _sigmoid(z)
    e0 = 3 * DK_A
    extras = jnp.where(lane < e0 + 2, g_val, jnp.where(lane < e0 + 4, b_val, 0.0))
    out = jnp.where(lane < 2 * DK_A, yn, jnp.where(lane < e0, y, extras))
    for u in range(2):
        qk_ref[u] = out[:, HEAD_COLS * u:HEAD_COLS * u + LANES]
        vx_ref[u] = out[:, HEAD_COLS * u + LANES:HEAD_COLS * (u + 1)]


def _delta_proj(y, mods, layer, row_of_tile, np_tiles, t_p, t_s, norm_g, w_a, cw_a, cb_a, al_row, dt_row):
    ntok = y.shape[0]
    nt = ntok // TM
    nc = 2 * HEAD_COLS
    kern = functools.partial(_delta_proj_kernel, np_tiles, t_p, t_s)
    return pl.pallas_call(
        kern,
        grid=(nt, H_A // 2),
        in_specs=[
            pl.BlockSpec((TM, D_MODEL), lambda i, n: (i, 0)),
            pl.BlockSpec((1, D_MODEL), lambda i, n: (0, 0)),
            _mod_spec(layer, 0, row_of_tile, 2),
            _mod_spec(layer, 1, row_of_tile, 2),
            pl.BlockSpec((D_MODEL, nc), lambda i, n: (0, n)),
            pl.BlockSpec((CONV_W, nc), lambda i, n: (0, n)),
            pl.BlockSpec((1, nc), lambda i, n: (0, n)),
            pl.BlockSpec((1, nc), lambda i, n: (0, n)),
            pl.BlockSpec((1, nc), lambda i, n: (0, n)),
        ],
        out_specs=[
            pl.BlockSpec((2, TM, LANES), lambda i, n: (n, i, 0)),
            pl.BlockSpec((2, TM, LANES), lambda i, n: (n, i, 0)),
        ],
        out_shape=[jax.ShapeDtypeStruct((H_A, ntok, LANES), f32),
                   jax.ShapeDtypeStruct((H_A, ntok, LANES), f32)],
        scratch_shapes=[pltpu.VMEM((TM, D_MODEL), bf16)],
        compiler_params=_cparams(("arbitrary", "arbitrary")),
        name="delta_proj",
    )(y, norm_g, mods, mods, w_a, cw_a, cb_a, al_row, dt_row)


def _rope_swap(x):
    w = x.shape[1]
    lane = lax.broadcasted_iota(jnp.int32, (1, w), 1)
    first = (lane & (ROPE_AXIS - 1)) < ROPE_AXIS // 2
    return jnp.where(first, pltpu.roll(x, w - ROPE_AXIS // 2, axis=1), pltpu.roll(x, ROPE_AXIS // 2, axis=1))


def _attn_proj_kernel(x_ref, g_ref, sh_ref, sc_ref, w_ref, qn_ref, kn_ref, cos_ref, sin_ref,
                      gate_ref, q_ref, k_ref, v_ref):
    h = _modulate(x_ref[...], g_ref[...], sh_ref[...], sc_ref[...]).astype(bf16)
    z = jnp.dot(h, w_ref[...].astype(bf16), preferred_element_type=f32)
    nq = H_B * HD_B
    nkv = KV_B * HD_B
    o1 = H_A * DV_A
    gz = z[:, :o1]
    gate_ref[...] = gz * _sigmoid(gz)
    q = z[:, o1:o1 + nq]
    k = z[:, o1 + nq:o1 + nq + nkv]
    v_ref[...] = z[:, o1 + nq + nkv:]
    qn = q * lax.rsqrt(_group_sum64(q * q) * (1.0 / HD_B) + EPS) * qn_ref[...]
    kn = k * lax.rsqrt(_group_sum64(k * k) * (1.0 / HD_B) + EPS) * kn_ref[...]
    cos = cos_ref[...]
    sin = sin_ref[...]
    q_ref[...] = qn * cos + _rope_swap(qn) * sin
    k_ref[...] = kn * cos[:, :nkv] + _rope_swap(kn) * sin[:, :nkv]


def _attn_proj(y, mods, layer, row_of_tile, np_tiles, norm_g, w_b, qn_row, kn_row, cos_tab, sin_tab):
    ntok = y.shape[0]
    nt = ntok // TM
    nq = H_B * HD_B
    nkv = KV_B * HD_B
    tab_idx = lambda i: (jnp.where(i < np_tiles, 0, 1), 0)
    return pl.pallas_call(
        _attn_proj_kernel,
        grid=(nt,),
        in_specs=[
            pl.BlockSpec((TM, D_MODEL), lambda i: (i, 0)),
            pl.BlockSpec((1, D_MODEL), lambda i: (0, 0)),
            _mod_spec(layer, 0, row_of_tile, 1),
            _mod_spec(layer, 1, row_of_tile, 1),
            pl.BlockSpec((D_MODEL, N_ATTN), lambda i: (0, 0)),
            pl.BlockSpec((1, nq), lambda i: (0, 0)),
            pl.BlockSpec((1, nkv), lambda i: (0, 0)),
            pl.BlockSpec((TM, nq), tab_idx),
            pl.BlockSpec((TM, nq), tab_idx),
        ],
        out_specs=[
            pl.BlockSpec((TM, H_A * DV_A), lambda i: (i, 0)),
            pl.BlockSpec((TM, nq), lambda i: (i, 0)),
            pl.BlockSpec((TM, nkv), lambda i: (i, 0)),
            pl.BlockSpec((TM, nkv), lambda i: (i, 0)),
        ],
        out_shape=[jax.ShapeDtypeStruct((ntok, H_A * DV_A), f32),
                   jax.ShapeDtypeStruct((ntok, nq), f32),
                   jax.ShapeDtypeStruct((ntok, nkv), f32),
                   jax.ShapeDtypeStruct((ntok, nkv), f32)],
        compiler_params=_cparams(("arbitrary",)),
        name="attn_proj",
    )(y, norm_g, mods, mods, w_b, qn_row, kn_row, cos_tab, sin_tab)


HG = 4
CG = 4


def _bmm(a, b):
    return jnp.einsum('bij,bjk->bik', a, b, preferred_element_type=f32)


def _bmm_nt(a, b):
    return jnp.einsum('bik,bjk->bij', a, b, preferred_element_type=f32)


def _bmm3(a, b):
    return _bmm(a[0], b[0]) + (_bmm(a[0], b[1]) + _bmm(a[1], b[0]))


def _chunk_scan(x, reverse):
    t = x.shape[0]
    r = lax.broadcasted_iota(jnp.int32, (t, 1), 0) & (CHUNK - 1)
    s = 1
    while s < CHUNK:
        if reverse:
            x = x + jnp.where(r + s < CHUNK, pltpu.roll(x, t - s, axis=0), 0.0)
        else:
            x = x + jnp.where(r >= s, pltpu.roll(x, s, axis=0), 0.0)
        s *= 2
    return x


def _delta_kernel(t, has_s0, *refs):
    if has_s0:
        qk_ref, vx_ref, s0_ref, o_ref, sf_ref, sc_s, ti_s, in_s, st_s = refs
    else:
        qk_ref, vx_ref, o_ref, sf_ref, sc_s, ti_s, in_s, st_s = refs
        s0_ref = None
    n = t // CHUNK
    ng = n // CG
    rows_g = CG * CHUNK
    e0 = DK_A

    lane = lax.broadcasted_iota(jnp.int32, (1, LANES), 1)
    for hh in range(HG):
        slab = vx_ref[hh]
        pre = _chunk_scan(slab, False)
        suf = _chunk_scan(slab, True)
        sc_s[hh] = jnp.where(lane == e0, pre, jnp.where(lane == e0 + 1, suf, slab))

    ri = lax.broadcasted_iota(jnp.int32, (CHUNK, CHUNK), 0)
    ci = lax.broadcasted_iota(jnp.int32, (CHUNK, CHUNK), 1)
    eye = ri == ci
    eye_f = jnp.where(eye, 1.0, 0.0)
    ones_b = jnp.ones((CG, CHUNK, CHUNK), bf16)
    scale = DK_A ** -0.5

    def pre_body(it, carry):
        hh = it // ng
        grp = it - hh * ng
        r0 = pl.multiple_of(grp * rows_g, rows_g)
        qkv = qk_ref[hh, pl.ds(r0, rows_g), :]
        q = qkv[:, :DK_A]
        k = qkv[:, DK_A:]
        sc = sc_s[hh, pl.ds(r0, rows_g), :]
        k3 = k.reshape(CG, CHUNK, DK_A).astype(bf16)
        q3 = (q * scale).reshape(CG, CHUNK, DK_A).astype(bf16)
        qk = _bmm_nt(q3, k3)
        for d in range(2):
            gc = sc[:, e0 + d:e0 + d + 1]
            beta = sc[:, e0 + 2 + d:e0 + 3 + d]
            kb3 = (k * beta).reshape(CG, CHUNK, DK_A).astype(bf16)
            kk = _bmm_nt(kb3, k3)
            gc3 = gc.reshape(CG, CHUNK, 1)
            dg = jnp.where(eye[None], gc3, 0.0)
            p1 = dg.astype(bf16)
            r1 = dg - p1.astype(f32)
            p2 = r1.astype(bf16)
            p3 = (r1 - p2.astype(f32)).astype(bf16)
            grow = (_bmm(ones_b, p1) + _bmm(ones_b, p2)) + _bmm(ones_b, p3)
            keep = (ri >= ci) if d == 0 else (ri <= ci)
            strict = (ri > ci) if d == 0 else (ri < ci)
            diff = gc3 - grow
            decay = jnp.where(keep[None], jnp.exp(jnp.where(keep[None], diff, 0.0)), 0.0)
            a = jnp.where(strict[None], kk * decay, 0.0)
            intra = jnp.where(keep[None], qk * decay, 0.0)
            x = _split2(a)
            p = eye_f[None] - a
            m = 2
            while m < CHUNK:
                x2 = _bmm3(x, x)
                x = _split2(x2)
                p = p + _bmm3(_split2(p), x)
                m *= 2
            ti_s[hh, d, pl.ds(grp * CG, CG)] = p
            in_s[hh, d, pl.ds(grp * CG, CG)] = intra
        return carry

    lax.fori_loop(0, HG * ng, pre_body, 0)

    for hh in range(HG):
        for d in range(2):
            st_s[hh, d] = s0_ref[d, hh] if has_s0 else jnp.zeros((DK_A, DV_A), f32)
    o_ref[...] = jnp.zeros_like(o_ref)

    def seq_body(j, carry):
        for d in range(2):
            c = j if d == 0 else n - 1 - j
            r0 = pl.multiple_of(c * CHUNK, CHUNK)
            outs = []
            for hh in range(HG):
                qkv = qk_ref[hh, pl.ds(r0, CHUNK), :]
                q = qkv[:, :DK_A]
                k = qkv[:, DK_A:]
                v = vx_ref[hh, pl.ds(r0, CHUNK), :][:, :DV_A]
                sc = sc_s[hh, pl.ds(r0, CHUNK), :]
                gc = sc[:, e0 + d:e0 + d + 1]
                beta = sc[:, e0 + 2 + d:e0 + 3 + d]
                gtot = gc[CHUNK - 1:CHUNK, :] if d == 0 else gc[0:1, :]
                eg = jnp.exp(gc)
                kb = k * beta
                s_old = st_s[hh, d]
                lhs = jnp.concatenate([kb * eg, q * (scale * eg)], axis=0)
                both = _dot(lhs, s_old)
                resid = v * beta - both[:CHUNK]
                v_new = _dot(ti_s[hh, d, c], resid)
                outs.append(both[CHUNK:] + _dot(in_s[hh, d, c], v_new))
                st_s[hh, d] = s_old * jnp.exp(gtot) + _dot_tn(k * jnp.exp(gtot - gc), v_new)
            o_ref[pl.ds(r0, CHUNK), :] = o_ref[pl.ds(r0, CHUNK), :] + jnp.concatenate(outs, axis=1)
        return carry

    lax.fori_loop(0, n, seq_body, 0)
    for hh in range(HG):
        for d in range(2):
            sf_ref[d, hh] = st_s[hh, d]


def _delta(qk, vx, t, nseq, tok_block0, s0, s0_layer):
    n = t // CHUNK
    has_s0 = s0 is not None
    kern = functools.partial(_delta_kernel, t, has_s0)
    in_specs = [
        pl.BlockSpec((HG, t, LANES), lambda b, g: (g, tok_block0 + b, 0)),
        pl.BlockSpec((HG, t, LANES), lambda b, g: (g, tok_block0 + b, 0)),
    ]
    args = [qk, vx]
    if has_s0:
        in_specs.append(pl.BlockSpec((None, None, 2, HG, DK_A, DV_A), lambda b, g: (b, s0_layer, 0, g, 0, 0)))
        args.append(s0)
    return pl.pallas_call(
        kern,
        grid=(nseq, H_A // HG),
        in_specs=in_specs,
        out_specs=[
            pl.BlockSpec((t, HG * DV_A), lambda b, g: (b, g)),
            pl.BlockSpec((None, 2, HG, DK_A, DV_A), lambda b, g: (b, 0, g, 0, 0)),
        ],
        out_shape=[jax.ShapeDtypeStruct((nseq * t, H_A * DV_A), f32),
                   jax.ShapeDtypeStruct((nseq, 2, H_A, DK_A, DV_A), f32)],
        scratch_shapes=[
            pltpu.VMEM((HG, t, LANES), f32),
            pltpu.VMEM((HG, 2, n, CHUNK, CHUNK), f32),
            pltpu.VMEM((HG, 2, n, CHUNK, CHUNK), f32),
            pltpu.VMEM((HG, 2, DK_A, DV_A), f32),
        ],
        compiler_params=_cparams(("arbitrary", "arbitrary")),
        name="delta_p" if not has_s0 else "delta_s",
    )(*args)


def _softmax_pv(scores, values, sink):
    m = sink
    for s in scores:
        m = jnp.maximum(m, jnp.max(s, axis=-1, keepdims=True))
    den = jnp.exp(sink - m)
    acc = None
    for s, v in zip(scores, values):
        p = jnp.exp(s - m)
        den = den + jnp.sum(p, axis=-1, keepdims=True)
        pv = _dot(p, v)
        acc = pv if acc is None else acc + pv
    return acc / den


def _ctx_attn_kernel(sink_ref, q_ref, k_ref, v_ref, o_ref):
    scale = HD_B ** -0.5
    q_all = q_ref[...] * scale
    k_all = k_ref[...]
    v_all = v_ref[...]
    outs = []
    for h in range(H_B):
        kv = h // G_B
        q = q_all[:, h * HD_B:(h + 1) * HD_B]
        k = k_all[:, kv * HD_B:(kv + 1) * HD_B]
        v = v_all[:, kv * HD_B:(kv + 1) * HD_B]
        s = _dot_nt(q, k)
        sink = jnp.zeros((q.shape[0], 1), f32) + sink_ref[h]
        outs.append(_softmax_pv([s], [v], sink))
    o_ref[...] = jnp.concatenate(outs, axis=1)


def _ctx_attn(sink, qb, kb, vb, t, nseq):
    nq = H_B * HD_B
    nkv = KV_B * HD_B
    return pl.pallas_call(
        _ctx_attn_kernel,
        grid=(nseq,),
        in_specs=[
            pl.BlockSpec(memory_space=pltpu.SMEM),
            pl.BlockSpec((t, nq), lambda b: (b, 0)),
            pl.BlockSpec((t, nkv), lambda b: (b, 0)),
            pl.BlockSpec((t, nkv), lambda b: (b, 0)),
        ],
        out_specs=pl.BlockSpec((t, nq), lambda b: (b, 0)),
        out_shape=jax.ShapeDtypeStruct((nseq * t, nq), f32),
        compiler_params=_cparams(("arbitrary",)),
        name="ctx_attn",
    )(sink, qb, kb, vb)


def _win_attn_kernel(t, sink_ref, q_ref, k_ref, v_ref, kc_ref, vc_ref, o_ref):
    i = pl.program_id(1)
    scale = HD_B ** -0.5
    span = QBLK + 2 * WINDOW
    start = i * QBLK
    lo = jnp.clip(start - WINDOW, 0, t - span)
    lo = pl.multiple_of(lo, QBLK)
    q_pos = start + lax.broadcasted_iota(jnp.int32, (QBLK, span), 0)
    k_pos = lo + lax.broadcasted_iota(jnp.int32, (QBLK, span), 1)
    valid = jnp.abs(q_pos - k_pos) <= WINDOW
    q_all = q_ref[...] * scale
    kw_all = k_ref[pl.ds(lo, span), :]
    vw_all = v_ref[pl.ds(lo, span), :]
    kc_all = kc_ref[...]
    vc_all = vc_ref[...]
    outs = []
    for h in range(H_B):
        kv = h // G_B
        q = q_all[:, h * HD_B:(h + 1) * HD_B]
        kw = kw_all[:, kv * HD_B:(kv + 1) * HD_B]
        vw = vw_all[:, kv * HD_B:(kv + 1) * HD_B]
        kc = kc_all[:, kv * HD_B:(kv + 1) * HD_B]
        vc = vc_all[:, kv * HD_B:(kv + 1) * HD_B]
        s_win = jnp.where(valid, _dot_nt(q, kw), NEG_INF)
        s_ctx = _dot_nt(q, kc)
        sink = jnp.zeros((QBLK, 1), f32) + sink_ref[h]
        outs.append(_softmax_pv([s_win, s_ctx], [vw, vc], sink))
    o_ref[...] = jnp.concatenate(outs, axis=1)


def _win_attn(sink, qb, kb, vb, cache_k, cache_v, cache_layer, t, nseq, tok_block0):
    nq = H_B * HD_B
    nkv = KV_B * HD_B
    nb = t // QBLK
    past = cache_k.shape[2]
    kern = functools.partial(_win_attn_kernel, t)
    return pl.pallas_call(
        kern,
        grid=(nseq, nb),
        in_specs=[
            pl.BlockSpec(memory_space=pltpu.SMEM),
            pl.BlockSpec((QBLK, nq), lambda b, i: ((tok_block0 + b) * nb + i, 0)),
            pl.BlockSpec((t, nkv), lambda b, i: (tok_block0 + b, 0)),
            pl.BlockSpec((t, nkv), lambda b, i: (tok_block0 + b, 0)),
            pl.BlockSpec((None, None, past, nkv), lambda b, i: (b, cache_layer, 0, 0)),
            pl.BlockSpec((None, None, past, nkv), lambda b, i: (b, cache_layer, 0, 0)),
        ],
        out_specs=pl.BlockSpec((QBLK, nq), lambda b, i: (b * nb + i, 0)),
        out_shape=jax.ShapeDtypeStruct((nseq * t, nq), f32),
        compiler_params=_cparams(("arbitrary", "arbitrary")),
        name="win_attn",
    )(sink, qb, kb, vb, cache_k, cache_v)


def _mix_out_kernel(np_tiles, y_ref, oap_ref, oas_ref, obp_ref, obs_ref, gate_ref, dn_ref, w_ref, gm_ref, o_ref):
    i = pl.program_id(0)
    is_p = i < np_tiles
    oa = jnp.where(is_p, oap_ref[...], oas_ref[...])
    ob = jnp.where(is_p, obp_ref[...], obs_ref[...])
    oan = oa * lax.rsqrt(_group_sum64(oa * oa) * (1.0 / DV_A) + EPS) * dn_ref[...] * gate_ref[...]
    ka = H_A * DV_A
    mix = _dot(oan, w_ref[:ka, :]) + _dot(ob, w_ref[ka:, :])
    o_ref[...] = y_ref[...] + gm_ref[...] * mix


def _mix_out(y, mods, layer, row_of_tile, np_tiles, oa_p, oa_s, ob_p, ob_s, gate, dn_row, w_out):
    ntok = y.shape[0]
    nt = ntok // TM
    ka = H_A * DV_A
    p_idx = lambda i: (jnp.minimum(i, np_tiles - 1), 0)
    s_idx = lambda i: (jnp.maximum(i - np_tiles, 0), 0)
    kern = functools.partial(_mix_out_kernel, np_tiles)
    return pl.pallas_call(
        kern,
        grid=(nt,),
        in_specs=[
            pl.BlockSpec((TM, D_MODEL), lambda i: (i, 0)),
            pl.BlockSpec((TM, ka), p_idx),
            pl.BlockSpec((TM, ka), s_idx),
            pl.BlockSpec((TM, ka), p_idx),
            pl.BlockSpec((TM, ka), s_idx),
            pl.BlockSpec((TM, ka), lambda i: (i, 0)),
            pl.BlockSpec((1, ka), lambda i: (0, 0)),
            pl.BlockSpec((2 * ka, D_MODEL), lambda i: (0, 0)),
            _mod_spec(layer, 2, row_of_tile, 1),
        ],
        out_specs=pl.BlockSpec((TM, D_MODEL), lambda i: (i, 0)),
        out_shape=jax.ShapeDtypeStruct((ntok, D_MODEL), f32),
        compiler_params=_cparams(("arbitrary",)),
        name="mix_out",
    )(y, oa_p, oa_s, ob_p, ob_s, gate, dn_row, w_out, mods)


def _lin_scan(a, b, reverse):
    t = a.shape[0]
    r = lax.broadcasted_iota(jnp.int32, (t, 1), 0)
    s = 1
    while s < t:
        if reverse:
            ok = r + s < t
            sh = t - s
        else:
            ok = r >= s
            sh = s
        b = b + a * jnp.where(ok, pltpu.roll(b, sh, axis=0), 0.0)
        a = a * jnp.where(ok, pltpu.roll(a, sh, axis=0), 1.0)
        s *= 2
    return b


def _gelu_tanh(x):
    return 0.5 * x * (1.0 + jnp.tanh(math.sqrt(2.0 / math.pi) * (x + 0.044715 * (x * x * x))))


def _lru_kernel(np_tiles, t_p, t_s, y_ref, g_ref, sh_ref, sc_ref, wx_ref, wg_ref, cw_ref, cb_ref,
                wl_ref, bl_ref, lam_ref, h0_ref, o_ref, fin_ref, h_s):
    i = pl.program_id(0)

    @pl.when(pl.program_id(1) == 0)
    def _():
        h_s[...] = _modulate(y_ref[...], g_ref[...], sh_ref[...], sc_ref[...]).astype(bf16)

    h = h_s[...]
    zx = jnp.dot(h, wx_ref[...].astype(bf16), preferred_element_type=f32)
    zg = jnp.dot(h, wg_ref[...].astype(bf16), preferred_element_type=f32)
    tseq = jnp.where(i < np_tiles, t_p, t_s)
    row = lax.broadcasted_iota(jnp.int32, (TM, 1), 0)
    pos = row & (tseq - 1)
    x = _seq_conv(zx, cw_ref, cb_ref, pos, tseq)
    gates = _dot(x, wl_ref[...]) + bl_ref[...]
    nseg = TM // t_p
    seg = jnp.where(i < np_tiles, row >> (t_p.bit_length() - 1), 0)
    acc = None
    for d in range(2):
        r = _sigmoid(gates[:, (2 * d) * LANES:(2 * d + 1) * LANES])
        ig = _sigmoid(gates[:, (2 * d + 1) * LANES:(2 * d + 2) * LANES])
        lam = lam_ref[:, d * LANES:(d + 1) * LANES]
        log_a = -RG_C * r * _softplus(-lam)
        a = jnp.exp(log_a)
        u = jnp.sqrt(-jnp.tanh(log_a) * (a * a + 1.0)) * (ig * x)
        h0 = jnp.zeros((TM, LANES), f32)
        for sg in range(nseg):
            h0 = jnp.where(seg == sg, h0_ref[sg, d:d + 1, :], h0)
        first = (pos == 0) if d == 0 else (pos == tseq - 1)
        u = jnp.where(first, u + a * h0, u)
        a = jnp.where(first, 0.0, a)
        hs = _lin_scan(a, u, d == 1)
        for sg in range(nseg):
            rr = sg * t_p + (t_p - 1 if d == 0 else 0)
            fin_ref[sg, d:d + 1, :] = hs[rr:rr + 1, :]
        acc = hs if acc is None else acc + hs
    o_ref[...] = acc * _gelu_tanh(zg)


def _lru(y, mods, layer, row_of_tile, np_tiles, t_p, t_s, norm_g, w_in, cw, cb, wl, bl, lam, h0):
    ntok = y.shape[0]
    nt = ntok // TM
    ncol = D_RNN // LANES
    nseg = TM // t_p
    kern = functools.partial(_lru_kernel, np_tiles, t_p, t_s)
    return pl.pallas_call(
        kern,
        grid=(nt, ncol),
        in_specs=[
            pl.BlockSpec((TM, D_MODEL), lambda i, j: (i, 0)),
            pl.BlockSpec((1, D_MODEL), lambda i, j: (0, 0)),
            _mod_spec(layer, 0, row_of_tile, 2),
            _mod_spec(layer, 1, row_of_tile, 2),
            pl.BlockSpec((D_MODEL, LANES), lambda i, j: (0, j)),
            pl.BlockSpec((D_MODEL, LANES), lambda i, j: (0, ncol + j)),
            pl.BlockSpec((CONV_W, LANES), lambda i, j: (0, j)),
            pl.BlockSpec((1, LANES), lambda i, j: (0, j)),
            pl.BlockSpec((None, LANES, 4 * LANES), lambda i, j: (j, 0, 0)),
            pl.BlockSpec((None, 1, 4 * LANES), lambda i, j: (j, 0, 0)),
            pl.BlockSpec((None, 1, 2 * LANES), lambda i, j: (j, 0, 0)),
            pl.BlockSpec((None, nseg, 2, LANES), lambda i, j: (i, 0, 0, j)),
        ],
        out_specs=[
            pl.BlockSpec((TM, LANES), lambda i, j: (i, j)),
            pl.BlockSpec((None, nseg, 2, LANES), lambda i, j: (i, 0, 0, j)),
        ],
        out_shape=[jax.ShapeDtypeStruct((ntok, D_RNN), f32),
                   jax.ShapeDtypeStruct((nt, nseg, 2, D_RNN), f32)],
        scratch_shapes=[pltpu.VMEM((TM, D_MODEL), bf16)],
        compiler_params=_cparams(("arbitrary", "arbitrary")),
        name="lru",
    )(y, norm_g, mods, mods, w_in, w_in, cw, cb, wl, bl, lam, h0)


def _proj_out_kernel(y_ref, a_ref, w_ref, gm_ref, o_ref):
    o_ref[...] = y_ref[...] + gm_ref[...] * _dot(a_ref[...], w_ref[...])


def _proj_out(y, mods, layer, row_of_tile, a, w):
    ntok = y.shape[0]
    nt = ntok // TM
    k = a.shape[1]
    return pl.pallas_call(
        _proj_out_kernel,
        grid=(nt,),
        in_specs=[
            pl.BlockSpec((TM, D_MODEL), lambda i: (i, 0)),
            pl.BlockSpec((TM, k), lambda i: (i, 0)),
            pl.BlockSpec((k, D_MODEL), lambda i: (0, 0)),
            _mod_spec(layer, 2, row_of_tile, 1),
        ],
        out_specs=pl.BlockSpec((TM, D_MODEL), lambda i: (i, 0)),
        out_shape=jax.ShapeDtypeStruct((ntok, D_MODEL), f32),
        compiler_params=_cparams(("arbitrary",)),
        name="proj_out",
    )(y, a, w, mods)


FF_BLK = 1024


def _mlp_kernel(y_ref, g_ref, sh_ref, sc_ref, gm_ref, w1_ref, w2_ref, o_ref, h_s, acc_s):
    k = pl.program_id(1)

    @pl.when(k == 0)
    def _():
        h_s[...] = _modulate(y_ref[...], g_ref[...], sh_ref[...], sc_ref[...]).astype(bf16)
        acc_s[...] = jnp.zeros_like(acc_s)

    a = jnp.dot(h_s[...], w1_ref[...].astype(bf16), preferred_element_type=f32)
    a = jnp.maximum(a, 0.0)
    acc_s[...] += _dot(a * a, w2_ref[...])

    @pl.when(k == pl.num_programs(1) - 1)
    def _():
        o_ref[...] = y_ref[...] + gm_ref[...] * acc_s[...]


def _mlp(y, mods, layer, row_of_tile, norm_g, w1, w2):
    ntok = y.shape[0]
    nt = ntok // TM
    return pl.pallas_call(
        _mlp_kernel,
        grid=(nt, D_FF // FF_BLK),
        in_specs=[
            pl.BlockSpec((TM, D_MODEL), lambda i, k: (i, 0)),
            pl.BlockSpec((1, D_MODEL), lambda i, k: (0, 0)),
            _mod_spec(layer, 3, row_of_tile, 2),
            _mod_spec(layer, 4, row_of_tile, 2),
            _mod_spec(layer, 5, row_of_tile, 2),
            pl.BlockSpec((None, D_MODEL, FF_BLK), lambda i, k: (layer, 0, k)),
            pl.BlockSpec((None, FF_BLK, D_MODEL), lambda i, k: (layer, k, 0)),
        ],
        out_specs=pl.BlockSpec((TM, D_MODEL), lambda i, k: (i, 0)),
        out_shape=jax.ShapeDtypeStruct((ntok, D_MODEL), f32),
        scratch_shapes=[pltpu.VMEM((TM, D_MODEL), bf16), pltpu.VMEM((TM, D_MODEL), f32)],
        compiler_params=_cparams(("arbitrary", "arbitrary")),
        name="mlp",
    )(y, norm_g, mods, mods, mods, w1, w2)


def _delta_cols(w_in, conv_w, conv_b, a_log, dt_bias):
    nqk = H_A * DK_A
    conv_ch = 2 * nqk + H_A * DV_A
    s2 = conv_ch + H_A * DV_A
    s3 = s2 + 2 * H_A

    def per_head(m, ex):
        lead = m.shape[0]
        q = m[:, :nqk].reshape(lead, H_A, DK_A)
        k = m[:, nqk:2 * nqk].reshape(lead, H_A, DK_A)
        v = m[:, 2 * nqk:conv_ch].reshape(lead, H_A, DV_A)
        pad = jnp.zeros((lead, H_A, DK_A - ex.shape[-1]), f32)
        return jnp.concatenate([q, k, v, ex, pad], axis=-1).reshape(lead, N_DELTA)

    al = w_in[:, s2:s3].reshape(D_MODEL, 2, H_A).transpose(0, 2, 1)
    be = w_in[:, s3:s3 + 2 * H_A].reshape(D_MODEL, 2, H_A).transpose(0, 2, 1)
    w_a = per_head(w_in, jnp.concatenate([al, be], axis=-1))
    no_ex = lambda lead: jnp.zeros((lead, H_A, 0), f32)
    cw_a = per_head(conv_w, no_ex(CONV_W))
    cb_a = per_head(conv_b[None], no_ex(1))
    zero_cols = jnp.zeros((1, conv_ch), f32)
    al_row = per_head(zero_cols, a_log.T[None])
    dt_row = per_head(zero_cols, dt_bias.T[None])
    w_b = w_in[:, conv_ch:conv_ch + H_A * DV_A]
    w_b = jnp.concatenate([w_b, w_in[:, s3 + 2 * H_A:]], axis=1)
    return w_a, cw_a, cb_a, al_row, dt_row, w_b


def _rope_tables(t):
    rows = t // GRID_W
    r = jnp.repeat(jnp.arange(rows, dtype=f32), GRID_W)
    cc = jnp.tile(jnp.arange(GRID_W, dtype=f32), rows)
    inv = ROPE_BASE ** (-jnp.arange(0, ROPE_AXIS, 2, dtype=f32) / ROPE_AXIS)
    ang_r = r[:, None] * inv
    ang_c = cc[:, None] * inv
    cos = jnp.concatenate([jnp.cos(ang_r), jnp.cos(ang_r), jnp.cos(ang_c), jnp.cos(ang_c)], axis=-1)
    sin = jnp.concatenate([-jnp.sin(ang_r), jnp.sin(ang_r), -jnp.sin(ang_c), jnp.sin(ang_c)], axis=-1)
    cos = jnp.tile(cos, (1, H_B))
    sin = jnp.tile(sin, (1, H_B))
    ident_c = jnp.ones((TM, H_B * HD_B), f32)
    ident_s = jnp.zeros((TM, H_B * HD_B), f32)
    return jnp.concatenate([ident_c, cos], axis=0), jnp.concatenate([ident_s, sin], axis=0)


def _lru_cols(w_a, b_a, w_x, b_x, lam):
    ncol = D_RNN // LANES
    per = LANES // LRU_BW

    def bd(w):
        w = w.reshape(ncol, per, LRU_BW, LRU_BW)
        eye = jnp.eye(per, dtype=f32)
        return jnp.einsum('gpij,pq->gpiqj', w, eye).reshape(ncol, LANES, LANES)

    wl = jnp.concatenate([bd(w_a[0]), bd(w_x[0]), bd(w_a[1]), bd(w_x[1])], axis=-1)
    rows = lambda v: v.reshape(ncol, 1, LANES)
    bl = jnp.concatenate([rows(b_a[0]), rows(b_x[0]), rows(b_a[1]), rows(b_x[1])], axis=-1)
    lm = jnp.concatenate([rows(lam[0]), rows(lam[1])], axis=-1)
    return wl, bl, lm


def kernel(x_prompt, x_sample, state_delta, cache_k, cache_v, state_lru, c, c_ctx, ada_w, ada_b, norm1_g, norm2_g, ff_w1, ff_w2, ab_w_in, ab_conv_w, ab_conv_b, dn_a_log, dn_dt_bias, dn_norm_g, attn_q_norm_g, attn_k_norm_g, attn_sink, ab_w_out, c_w_in, c_conv_w, c_conv_b, lru_w_a, lru_b_a, lru_w_x, lru_b_x, lru_lambda, c_w_out):
    b_p, t_p, _ = x_prompt.shape
    b_s, t_s, _ = x_sample.shape
    assert t_s == TM and TM % t_p == 0 and (b_p * t_p) % TM == 0
    np_tok = b_p * t_p
    np_tiles = np_tok // TM
    nkv = KV_B * HD_B

    y = jnp.concatenate([x_prompt.reshape(np_tok, D_MODEL), x_sample.reshape(b_s * t_s, D_MODEL)], axis=0)
    nt = y.shape[0] // TM

    rows = -(-(b_s + 1) // 8) * 8
    cond = jnp.zeros((rows, D_MODEL), f32).at[:b_s].set(c).at[b_s].set(c_ctx)
    mods = _adaln(cond, ada_w, ada_b)
    row_of_tile = lambda i: jnp.where(i < np_tiles, b_s, i - np_tiles)

    cos_tab, sin_tab = _rope_tables(t_s)
    ck = cache_k.reshape(cache_k.shape[0], cache_k.shape[1], cache_k.shape[2], nkv)
    cv = cache_v.reshape(ck.shape)

    new_dn, new_k, new_v, new_lru = [], [], [], []
    for l in range(DEPTH):
        j = l // 2
        n1 = norm1_g[l].reshape(1, D_MODEL)
        if l % 2 == 0:
            w_a, cw_a, cb_a, al_row, dt_row, w_b = _delta_cols(
                ab_w_in[j], ab_conv_w[j], ab_conv_b[j], dn_a_log[j], dn_dt_bias[j])
            qk, vx = _delta_proj(y, mods, l, row_of_tile, np_tiles, t_p, t_s, n1, w_a, cw_a, cb_a, al_row, dt_row)
            qn_row = jnp.tile(attn_q_norm_g[j], H_B).reshape(1, H_B * HD_B)
            kn_row = jnp.tile(attn_k_norm_g[j], KV_B).reshape(1, nkv)
            gate, qb, kb, vb = _attn_proj(y, mods, l, row_of_tile, np_tiles, n1, w_b, qn_row, kn_row,
                                          cos_tab, sin_tab)
            oa_p, s_fin = _delta(qk, vx, t_p, b_p, 0, None, 0)
            oa_s, _ = _delta(qk, vx, t_s, b_s, np_tok // t_s, state_delta, j)
            ob_p = _ctx_attn(attn_sink[j], qb, kb, vb, t_p, b_p)
            ob_s = _win_attn(attn_sink[j], qb, kb, vb, ck, cv, j, t_s, b_s, np_tok // t_s)
            dn_row = jnp.tile(dn_norm_g[j], H_A).reshape(1, H_A * DV_A)
            y = _mix_out(y, mods, l, row_of_tile, np_tiles, oa_p, oa_s, ob_p, ob_s, gate, dn_row, ab_w_out[j])
            new_dn.append(s_fin)
            new_k.append(kb[:np_tok].reshape(b_p, t_p, KV_B, HD_B))
            new_v.append(vb[:np_tok].reshape(b_p, t_p, KV_B, HD_B))
        else:
            wl, bl, lm = _lru_cols(lru_w_a[j], lru_b_a[j], lru_w_x[j], lru_b_x[j], lru_lambda[j])
            nseg = TM // t_p
            h0 = jnp.zeros((nt, nseg, 2, D_RNN), f32).at[np_tiles:, 0].set(state_lru[:, j])
            mixed, fin = _lru(y, mods, l, row_of_tile, np_tiles, t_p, t_s, n1, c_w_in[j], c_conv_w[j],
                              c_conv_b[j].reshape(1, D_RNN), wl, bl, lm, h0)
            y = _proj_out(y, mods, l, row_of_tile, mixed, c_w_out[j])
            new_lru.append(fin[:np_tiles].reshape(b_p, 2, D_RNN))
        y = _mlp(y, mods, l, row_of_tile, norm2_g[l].reshape(1, D_MODEL), ff_w1, ff_w2)

    yp = y[:np_tok].reshape(b_p, t_p, D_MODEL)
    ys = y[np_tok:].reshape(b_s, t_s, D_MODEL)
    return (yp, ys, jnp.stack(new_dn, axis=1), jnp.stack(new_k, axis=1), jnp.stack(new_v, axis=1),
            jnp.stack(new_lru, axis=1))
```

```python
import functools
import math

import jax
import jax.numpy as jnp
from jax import lax
from jax.experimental import pallas as pl
from jax.experimental.pallas import tpu as pltpu

f32 = jnp.float32
bf16 = jnp.bfloat16

D_MODEL = 1024
DEPTH = 4
GRID_W = 64
H_A = 8
DK_A = 64
DV_A = 64
CHUNK = 64
CONV_W = 4
H_B = 8
KV_B = 2
G_B = H_B // KV_B
HD_B = 64
WINDOW = 128
QBLK = 128
ROPE_AXIS = HD_B // 2
ROPE_BASE = 10000.0
D_RNN = D_MODEL
LRU_BLOCKS = 16
LRU_BW = D_RNN // LRU_BLOCKS
RG_C = 8.0
D_FF = 4 * D_MODEL
EPS = 1e-6
NEG_INF = -1e30

TM = 1024
LANES = 128
MXU_DIM = 256
N_QKV = 3 * H_A * DK_A
N_ATTN = H_A * DV_A + (H_B + 2 * KV_B) * HD_B + LANES
VMEM_LIMIT = 56 * 1024 * 1024


def _cparams(sem):
    return pltpu.CompilerParams(dimension_semantics=sem, vmem_limit_bytes=VMEM_LIMIT)


def _dot(a, b):
    return jnp.dot(a.astype(bf16), b.astype(bf16), preferred_element_type=f32)


def _dot_nt(a, b):
    return lax.dot_general(a.astype(bf16), b.astype(bf16), (((1,), (1,)), ((), ())),
                           preferred_element_type=f32)


def _dot_tn(a, b):
    return lax.dot_general(a.astype(bf16), b.astype(bf16), (((0,), (0,)), ((), ())),
                           preferred_element_type=f32)


def _sigmoid(x):
    return 1.0 / (1.0 + jnp.exp(-x))


def _softplus(x):
    return jnp.maximum(x, 0.0) + jnp.log1p(jnp.exp(-jnp.abs(x)))


def _modulate(x, g, shift, scale):
    ms = jnp.mean(x * x, axis=-1, keepdims=True)
    y = x * lax.rsqrt(ms + EPS) * g
    return y * (1.0 + scale) + shift


def _split2(x):
    hi = x.astype(bf16)
    lo = (x - hi.astype(f32)).astype(bf16)
    return hi, lo


def _split3(x):
    p1 = x.astype(bf16)
    r1 = x - p1.astype(f32)
    p2 = r1.astype(bf16)
    p3 = (r1 - p2.astype(f32)).astype(bf16)
    return p1, p2, p3


def _group_sum64(v):
    w = v.shape[-1]
    r = lax.broadcasted_iota(jnp.int32, (w, w), 0) >> 6
    c = lax.broadcasted_iota(jnp.int32, (w, w), 1) >> 6
    ones_bd = jnp.where(r == c, 1.0, 0.0).astype(bf16)
    hi, lo = _split2(v)
    return (jnp.dot(hi, ones_bd, preferred_element_type=f32)
            + jnp.dot(lo, ones_bd, preferred_element_type=f32))


def _seq_conv(z, cw_ref, cb_ref, pos, tseq):
    n = z.shape[0]
    acc = jnp.zeros_like(z) + cb_ref[...]
    left = CONV_W // 2
    for j in range(CONV_W):
        o = j - left
        zs = z if o == 0 else pltpu.roll(z, (-o) % n, axis=0)
        valid = (pos + o >= 0) & (pos + o < tseq)
        acc = acc + jnp.where(valid, zs, 0.0) * cw_ref[j:j + 1, :]
    return acc


def _adaln_kernel(c_ref, w_ref, b_ref, o_ref):
    c = c_ref[...]
    a = c * _sigmoid(c)
    o_ref[...] = _dot(a, w_ref[...]) + b_ref[...]


def _adaln(cond, ada_w, ada_b):
    rows = cond.shape[0]
    out = pl.pallas_call(
        _adaln_kernel,
        grid=(DEPTH, 6),
        in_specs=[
            pl.BlockSpec((rows, D_MODEL), lambda l, k: (0, 0)),
            pl.BlockSpec((None, D_MODEL, D_MODEL), lambda l, k: (l, 0, k)),
            pl.BlockSpec((None, None, 1, D_MODEL), lambda l, k: (l, k, 0, 0)),
        ],
        out_specs=pl.BlockSpec((None, None, rows, D_MODEL), lambda l, k: (l, k, 0, 0)),
        out_shape=jax.ShapeDtypeStruct((DEPTH, 6, rows, D_MODEL), f32),
        compiler_params=_cparams(("arbitrary", "arbitrary")),
        name="adaln",
    )(cond, ada_w, ada_b.reshape(DEPTH, 6, 1, D_MODEL))
    return out.reshape(DEPTH, 6, rows, 1, D_MODEL)


def _mod_spec(layer, k, row_of_tile, ngrid):
    if ngrid == 1:
        return pl.BlockSpec((None, None, None, 1, D_MODEL), lambda i: (layer, k, row_of_tile(i), 0, 0))
    return pl.BlockSpec((None, None, None, 1, D_MODEL), lambda i, j: (layer, k, row_of_tile(i), 0, 0))


QKV_BLK = H_A * DK_A


def _delta_proj_kernel(np_tiles, t_p, t_s, x_ref, g_ref, sh_ref, sc_ref, w_ref, cw_ref, cb_ref, o_ref, h_s):
    i = pl.program_id(0)
    n = pl.program_id(1)

    @pl.when(n == 0)
    def _():
        h_s[...] = _modulate(x_ref[...], g_ref[...], sh_ref[...], sc_ref[...]).astype(bf16)

    z = jnp.dot(h_s[...], w_ref[...].astype(bf16), preferred_element_type=f32)
    tseq = jnp.where(i < np_tiles, t_p, t_s)
    pos = lax.broadcasted_iota(jnp.int32, (TM, 1), 0) & (tseq - 1)
    y = _seq_conv(z, cw_ref, cb_ref, pos, tseq)
    y = y * _sigmoid(y)

    @pl.when(n < 2)
    def _():
        o_ref[...] = y * lax.rsqrt(_group_sum64(y * y) + EPS)

    @pl.when(n == 2)
    def _():
        o_ref[...] = y


def _delta_proj(y, mods, layer, row_of_tile, np_tiles, t_p, t_s, norm_g, w_in, conv_w, conv_b):
    ntok = y.shape[0]
    nt = ntok // TM
    kern = functools.partial(_delta_proj_kernel, np_tiles, t_p, t_s)
    return pl.pallas_call(
        kern,
        grid=(nt, N_QKV // QKV_BLK),
        in_specs=[
            pl.BlockSpec((TM, D_MODEL), lambda i, n: (i, 0)),
            pl.BlockSpec((1, D_MODEL), lambda i, n: (0, 0)),
            _mod_spec(layer, 0, row_of_tile, 2),
            _mod_spec(layer, 1, row_of_tile, 2),
            pl.BlockSpec((D_MODEL, QKV_BLK), lambda i, n: (0, n)),
            pl.BlockSpec((CONV_W, QKV_BLK), lambda i, n: (0, n)),
            pl.BlockSpec((1, QKV_BLK), lambda i, n: (0, n)),
        ],
        out_specs=pl.BlockSpec((TM, QKV_BLK), lambda i, n: (i, n)),
        out_shape=jax.ShapeDtypeStruct((ntok, N_QKV), f32),
        scratch_shapes=[pltpu.VMEM((TM, D_MODEL), bf16)],
        compiler_params=_cparams(("arbitrary", "arbitrary")),
        name="delta_proj",
    )(y, norm_g, mods, mods, w_in, conv_w, conv_b)


def _rope_swap(x):
    w = x.shape[1]
    lane = lax.broadcasted_iota(jnp.int32, (1, w), 1)
    first = (lane & (ROPE_AXIS - 1)) < ROPE_AXIS // 2
    return jnp.where(first, pltpu.roll(x, w - ROPE_AXIS // 2, axis=1), pltpu.roll(x, ROPE_AXIS // 2, axis=1))


def _attn_proj_kernel(x_ref, g_ref, sh_ref, sc_ref, w_ref, qn_ref, kn_ref, cos_ref, sin_ref, al_ref, dt_ref,
                      gate_ref, q_ref, k_ref, v_ref, gb_ref):
    h = _modulate(x_ref[...], g_ref[...], sh_ref[...], sc_ref[...]).astype(bf16)
    z = jnp.dot(h, w_ref[...].astype(bf16), preferred_element_type=f32)
    nq = H_B * HD_B
    nkv = KV_B * HD_B
    o1 = H_A * DV_A
    o2 = o1 + nq + 2 * nkv
    gz = z[:, :o1]
    gate_ref[...] = gz * _sigmoid(gz)
    q = z[:, o1:o1 + nq]
    k = z[:, o1 + nq:o1 + nq + nkv]
    v_ref[...] = z[:, o1 + nq + nkv:o2]
    qn = q * lax.rsqrt(_group_sum64(q * q) * (1.0 / HD_B) + EPS) * qn_ref[...]
    kn = k * lax.rsqrt(_group_sum64(k * k) * (1.0 / HD_B) + EPS) * kn_ref[...]
    cos = cos_ref[...]
    sin = sin_ref[...]
    q_ref[...] = qn * cos + _rope_swap(qn) * sin
    k_ref[...] = kn * cos[:, :nkv] + _rope_swap(kn) * sin[:, :nkv]
    zs = z[:, o2:]
    lane = lax.broadcasted_iota(jnp.int32, (1, LANES), 1)
    g_val = -jnp.exp(al_ref[...]) * _softplus(zs + dt_ref[...])
    gb_ref[...] = jnp.where(lane < 2 * H_A, g_val, jnp.where(lane < 4 * H_A, _sigmoid(zs), 0.0))


def _attn_proj(y, mods, layer, row_of_tile, np_tiles, norm_g, w_b, qn_row, kn_row, cos_tab, sin_tab,
               al_row, dt_row):
    ntok = y.shape[0]
    nt = ntok // TM
    nq = H_B * HD_B
    nkv = KV_B * HD_B
    tab_idx = lambda i: (jnp.where(i < np_tiles, 0, 1), 0)
    return pl.pallas_call(
        _attn_proj_kernel,
        grid=(nt,),
        in_specs=[
            pl.BlockSpec((TM, D_MODEL), lambda i: (i, 0)),
            pl.BlockSpec((1, D_MODEL), lambda i: (0, 0)),
            _mod_spec(layer, 0, row_of_tile, 1),
            _mod_spec(layer, 1, row_of_tile, 1),
            pl.BlockSpec((D_MODEL, N_ATTN), lambda i: (0, 0)),
            pl.BlockSpec((1, nq), lambda i: (0, 0)),
            pl.BlockSpec((1, nkv), lambda i: (0, 0)),
            pl.BlockSpec((TM, nq), tab_idx),
            pl.BlockSpec((TM, nq), tab_idx),
            pl.BlockSpec((1, LANES), lambda i: (0, 0)),
            pl.BlockSpec((1, LANES), lambda i: (0, 0)),
        ],
        out_specs=[
            pl.BlockSpec((TM, H_A * DV_A), lambda i: (i, 0)),
            pl.BlockSpec((TM, nq), lambda i: (i, 0)),
            pl.BlockSpec((TM, nkv), lambda i: (i, 0)),
            pl.BlockSpec((TM, nkv), lambda i: (i, 0)),
            pl.BlockSpec((TM, LANES), lambda i: (i, 0)),
        ],
        out_shape=[jax.ShapeDtypeStruct((ntok, H_A * DV_A), f32),
                   jax.ShapeDtypeStruct((ntok, nq), f32),
                   jax.ShapeDtypeStruct((ntok, nkv), f32),
                   jax.ShapeDtypeStruct((ntok, nkv), f32),
                   jax.ShapeDtypeStruct((ntok, LANES), f32)],
        compiler_params=_cparams(("arbitrary",)),
        name="attn_proj",
    )(y, norm_g, mods, mods, w_b, qn_row, kn_row, cos_tab, sin_tab, al_row, dt_row)


INV_BASE = 8
HPG = MXU_DIM // DK_A
NLG = H_A // HPG
W_ALL = H_A * DK_A


def _chunk_scan(x, reverse):
    t = x.shape[0]
    r = lax.broadcasted_iota(jnp.int32, (t, 1), 0) & (CHUNK - 1)
    s = 1
    while s < CHUNK:
        if reverse:
            x = x + jnp.where(r + s < CHUNK, pltpu.roll(x, t - s, axis=0), 0.0)
        else:
            x = x + jnp.where(r >= s, pltpu.roll(x, s, axis=0), 0.0)
        s *= 2
    return x


def _delta_kernel(t, has_s0, *refs):
    if has_s0:
        q_ref, k_ref, v_ref, gb_ref, s0_ref = refs[:5]
        rest = refs[5:]
    else:
        q_ref, k_ref, v_ref, gb_ref = refs[:4]
        s0_ref = None
        rest = refs[4:]
    o_ref, sf_ref, exp_s, kbg_s, qg_s, kd_s, vb_s, egt_s, ti_s, in_s, st_s = rest
    n = t // CHUNK
    scale = DK_A ** -0.5

    gb = gb_ref[...]
    lane = lax.broadcasted_iota(jnp.int32, (1, LANES), 1)
    sc = jnp.where(lane < H_A, _chunk_scan(gb, False), jnp.where(lane < 2 * H_A, _chunk_scan(gb, True), gb))
    er = lax.broadcasted_iota(jnp.int32, (LANES, 4 * W_ALL), 0)
    ec = lax.broadcasted_iota(jnp.int32, (LANES, 4 * W_ALL), 1) >> 6
    expand = jnp.where(er == ec, 1.0, 0.0).astype(bf16)
    rb = min(t, 256)
    for r0 in range(0, t, rb):
        p1, p2, p3 = _split3(sc[r0:r0 + rb])
        exp_s[r0:r0 + rb, :] = ((jnp.dot(p1, expand, preferred_element_type=f32)
                                 + jnp.dot(p2, expand, preferred_element_type=f32))
                                + jnp.dot(p3, expand, preferred_element_type=f32))

    ri = lax.broadcasted_iota(jnp.int32, (CHUNK, MXU_DIM), 0)
    ci = lax.broadcasted_iota(jnp.int32, (CHUNK, MXU_DIM), 1) & (CHUNK - 1)
    eye_ls = ri == ci
    bd_mask = ((lax.broadcasted_iota(jnp.int32, (MXU_DIM, MXU_DIM), 0) >> 6)
               == (lax.broadcasted_iota(jnp.int32, (MXU_DIM, MXU_DIM), 1) >> 6))
    ones_c = jnp.ones((CHUNK, CHUNK), bf16)

    def bd(x):
        xb = x.astype(bf16)
        return jnp.where(bd_mask, jnp.concatenate([xb] * HPG, axis=0), jnp.zeros((), bf16))

    def blk(b):
        sh = b.bit_length() - 1
        return (ri >> sh) == (ci >> sh)

    def mm3(xh, xl, yh, yl):
        r1 = jnp.dot(jnp.concatenate([xh, xl], axis=0), bd(yh), preferred_element_type=f32)
        r2 = jnp.dot(xh, bd(yl), preferred_element_type=f32)
        return (r1[:CHUNK] + r1[CHUNK:]) + r2

    def pre_body(c, carry):
        r0 = pl.multiple_of(c * CHUNK, CHUNK)
        rows = pl.ds(r0, CHUNK)
        q = q_ref[rows, :]
        k = k_ref[rows, :]
        v = v_ref[rows, :]
        gcs, kbs = [], []
        for d in range(2):
            gc = exp_s[rows, d * W_ALL:(d + 1) * W_ALL]
            beta = exp_s[rows, (2 + d) * W_ALL:(3 + d) * W_ALL]
            gtot = gc[CHUNK - 1:CHUNK, :] if d == 0 else gc[0:1, :]
            eg = jnp.exp(gc)
            kb = k * beta
            kbg_s[d, rows, :] = (kb * eg).astype(bf16)
            qg_s[d, rows, :] = (q * (scale * eg)).astype(bf16)
            kd_s[d, rows, :] = (k * jnp.exp(gtot - gc)).astype(bf16)
            vb_s[d, rows, :] = v * beta
            egt_s[d, c] = jnp.zeros((8, W_ALL), f32) + jnp.exp(gtot)
            gcs.append(gc)
            kbs.append(kb)
        units = [(d, lg) for lg in range(NLG) for d in range(2)]
        prods = []
        for lg in range(NLG):
            cols = slice(lg * MXU_DIM, (lg + 1) * MXU_DIM)
            lhs = jnp.concatenate([kbs[0][:, cols], kbs[1][:, cols], q[:, cols] * scale], axis=0)
            prods.append(_dot_nt(lhs, bd(k[:, cols])))
        ps, ys, avs = [], [], []
        for d, lg in units:
            cols = slice(lg * MXU_DIM, (lg + 1) * MXU_DIM)
            gc = gcs[d][:, cols]
            d1, d2, d3 = _split3(jnp.where(eye_ls, gc, 0.0))
            grow = ((jnp.dot(ones_c, d1, preferred_element_type=f32)
                     + jnp.dot(ones_c, d2, preferred_element_type=f32))
                    + jnp.dot(ones_c, d3, preferred_element_type=f32))
            keep = (ri >= ci) if d == 0 else (ri <= ci)
            strict = (ri > ci) if d == 0 else (ri < ci)
            decay = jnp.where(keep, jnp.exp(jnp.where(keep, gc - grow, 0.0)), 0.0)
            a = jnp.where(strict, prods[lg][d * CHUNK:(d + 1) * CHUNK] * decay, 0.0)
            in_s[d, c, :, cols] = jnp.where(keep, prods[lg][2 * CHUNK:] * decay, 0.0).astype(bf16)
            a_d = jnp.where(blk(INV_BASE), a, 0.0)
            avs.append(a)
            ys.append(_split2(-a_d))
            ps.append(jnp.where(eye_ls, 1.0, 0.0) - a_d)
        m = 1
        while m < INV_BASE:
            last = 2 * m >= INV_BASE
            for u in range(len(units)):
                yh, yl = ys[u]
                p = ps[u]
                byh, byl = bd(yh), bd(yl)
                if m == 1:
                    r1 = jnp.dot(jnp.concatenate([yh, yl], axis=0), byh, preferred_element_type=f32)
                    r2 = jnp.dot(yh, byl, preferred_element_type=f32)
                    y2 = (r1[:CHUNK] + r1[CHUNK:]) + r2
                else:
                    ph, plo = _split2(p)
                    if last:
                        r1 = jnp.dot(jnp.concatenate([ph, plo], axis=0), byh, preferred_element_type=f32)
                        r2 = jnp.dot(ph, byl, preferred_element_type=f32)
                        p = p + ((r1[:CHUNK] + r1[CHUNK:]) + r2)
                    else:
                        r1 = jnp.dot(jnp.concatenate([ph, plo, yh, yl], axis=0), byh,
                                     preferred_element_type=f32)
                        r2 = jnp.dot(jnp.concatenate([ph, yh], axis=0), byl, preferred_element_type=f32)
                        p = p + ((r1[:CHUNK] + r1[CHUNK:2 * CHUNK]) + r2[:CHUNK])
                        y2 = (r1[2 * CHUNK:3 * CHUNK] + r1[3 * CHUNK:]) + r2[CHUNK:]
                ps[u] = p
                if not last:
                    ys[u] = _split2(y2)
            m *= 2
        b = INV_BASE
        while b < CHUNK:
            off = blk(2 * b) & jnp.logical_not(blk(b))
            ws = []
            for u in range(len(units)):
                eh, el = _split2(jnp.where(off, avs[u], 0.0))
                ph, plo = _split2(ps[u])
                ws.append(mm3(eh, el, ph, plo))
            for u in range(len(units)):
                ph, plo = _split2(ps[u])
                wh, wl = _split2(ws[u])
                ps[u] = ps[u] - mm3(ph, plo, wh, wl)
            b *= 2
        for u, (d, lg) in enumerate(units):
            ti_s[d, c, :, lg * MXU_DIM:(lg + 1) * MXU_DIM] = ps[u].astype(bf16)
        return carry

    lax.fori_loop(0, n, pre_body, 0)

    for d in range(2):
        for lg in range(NLG):
            if has_s0:
                blocks = []
                for hh in range(HPG):
                    s_h = s0_ref[d, lg * HPG + hh]
                    z_l = jnp.zeros((DK_A, hh * DV_A), f32)
                    z_r = jnp.zeros((DK_A, (HPG - 1 - hh) * DV_A), f32)
                    parts = ([z_l] if hh > 0 else []) + [s_h] + ([z_r] if hh < HPG - 1 else [])
                    blocks.append(jnp.concatenate(parts, axis=1) if len(parts) > 1 else s_h)
                st_s[d, lg] = jnp.concatenate(blocks, axis=0)
            else:
                st_s[d, lg] = jnp.zeros((MXU_DIM, MXU_DIM), f32)
    o_ref[...] = jnp.zeros_like(o_ref)

    def seq_body(j, carry):
        units = [(d, lg) for lg in range(NLG) for d in range(2)]
        cs = [j, n - 1 - j]
        rws = [pl.ds(pl.multiple_of(cc * CHUNK, CHUNK), CHUNK) for cc in cs]
        cls = [slice(lg * MXU_DIM, (lg + 1) * MXU_DIM) for lg in range(NLG)]
        boths, vnbs = [], []
        for d, lg in units:
            lhs = jnp.concatenate([kbg_s[d, rws[d], cls[lg]], qg_s[d, rws[d], cls[lg]]], axis=0)
            boths.append(jnp.dot(lhs, st_s[d, lg].astype(bf16), preferred_element_type=f32))
        for u, (d, lg) in enumerate(units):
            resid = vb_s[d, rws[d], cls[lg]] - boths[u][:CHUNK]
            v_new = jnp.dot(ti_s[d, cs[d], :, cls[lg]], bd(resid), preferred_element_type=f32)
            vnbs.append(v_new.astype(bf16))
        for u, (d, lg) in enumerate(units):
            o = boths[u][CHUNK:] + jnp.dot(in_s[d, cs[d], :, cls[lg]], bd(vnbs[u]), preferred_element_type=f32)
            upd = lax.dot_general(kd_s[d, rws[d], cls[lg]], vnbs[u], (((0,), (0,)), ((), ())),
                                  preferred_element_type=f32)
            st_s[d, lg] = st_s[d, lg] * egt_s[d, cs[d], 0:1, cls[lg]] + jnp.where(bd_mask, upd, 0.0)
            o_ref[rws[d], cls[lg]] = o_ref[rws[d], cls[lg]] + o
        return carry

    lax.fori_loop(0, n, seq_body, 0)
    for d in range(2):
        for lg in range(NLG):
            s_fin = st_s[d, lg]
            for hh in range(HPG):
                sf_ref[d, lg * HPG + hh] = s_fin[hh * DK_A:(hh + 1) * DK_A, hh * DV_A:(hh + 1) * DV_A]


def _delta(qkv, gb, t, nseq, tok_block0, s0, s0_layer):
    n = t // CHUNK
    has_s0 = s0 is not None
    kern = functools.partial(_delta_kernel, t, has_s0)
    in_specs = [
        pl.BlockSpec((t, W_ALL), lambda b: (tok_block0 + b, 0)),
        pl.BlockSpec((t, W_ALL), lambda b: (tok_block0 + b, 1)),
        pl.BlockSpec((t, W_ALL), lambda b: (tok_block0 + b, 2)),
        pl.BlockSpec((t, LANES), lambda b: (tok_block0 + b, 0)),
    ]
    args = [qkv, qkv, qkv, gb]
    if has_s0:
        in_specs.append(pl.BlockSpec((None, None, 2, H_A, DK_A, DV_A), lambda b: (b, s0_layer, 0, 0, 0, 0)))
        args.append(s0)
    return pl.pallas_call(
        kern,
        grid=(nseq,),
        in_specs=in_specs,
        out_specs=[
            pl.BlockSpec((t, W_ALL), lambda b: (b, 0)),
            pl.BlockSpec((None, 2, H_A, DK_A, DV_A), lambda b: (b, 0, 0, 0, 0)),
        ],
        out_shape=[jax.ShapeDtypeStruct((nseq * t, W_ALL), f32),
                   jax.ShapeDtypeStruct((nseq, 2, H_A, DK_A, DV_A), f32)],
        scratch_shapes=[
            pltpu.VMEM((t, 4 * W_ALL), f32),
            pltpu.VMEM((2, t, W_ALL), bf16),
            pltpu.VMEM((2, t, W_ALL), bf16),
            pltpu.VMEM((2, t, W_ALL), bf16),
            pltpu.VMEM((2, t, W_ALL), f32),
            pltpu.VMEM((2, n, 8, W_ALL), f32),
            pltpu.VMEM((2, n, CHUNK, W_ALL), bf16),
            pltpu.VMEM((2, n, CHUNK, W_ALL), bf16),
            pltpu.VMEM((2, NLG, MXU_DIM, MXU_DIM), f32),
        ],
        compiler_params=_cparams(("arbitrary",)),
        name="delta_p" if not has_s0 else "delta_s",
    )(*args)


def _softmax_pv(scores, values, sink):
    m = sink
    for s in scores:
        m = jnp.maximum(m, jnp.max(s, axis=-1, keepdims=True))
    den = jnp.exp(sink - m)
    acc = None
    for s, v in zip(scores, values):
        p = jnp.exp(s - m)
        den = den + jnp.sum(p, axis=-1, keepdims=True)
        pv = _dot(p, v)
        acc = pv if acc is None else acc + pv
    return acc / den


def _ctx_attn_kernel(sink_ref, q_ref, k_ref, v_ref, o_ref):
    scale = HD_B ** -0.5
    q_all = q_ref[...] * scale
    k_all = k_ref[...]
    v_all = v_ref[...]
    outs = []
    for h in range(H_B):
        kv = h // G_B
        q = q_all[:, h * HD_B:(h + 1) * HD_B]
        k = k_all[:, kv * HD_B:(kv + 1) * HD_B]
        v = v_all[:, kv * HD_B:(kv + 1) * HD_B]
        s = _dot_nt(q, k)
        sink = jnp.zeros((q.shape[0], 1), f32) + sink_ref[h]
        outs.append(_softmax_pv([s], [v], sink))
    o_ref[...] = jnp.concatenate(outs, axis=1)


def _ctx_attn(sink, qb, kb, vb, t, nseq):
    nq = H_B * HD_B
    nkv = KV_B * HD_B
    return pl.pallas_call(
        _ctx_attn_kernel,
        grid=(nseq,),
        in_specs=[
            pl.BlockSpec(memory_space=pltpu.SMEM),
            pl.BlockSpec((t, nq), lambda b: (b, 0)),
            pl.BlockSpec((t, nkv), lambda b: (b, 0)),
            pl.BlockSpec((t, nkv), lambda b: (b, 0)),
        ],
        out_specs=pl.BlockSpec((t, nq), lambda b: (b, 0)),
        out_shape=jax.ShapeDtypeStruct((nseq * t, nq), f32),
        compiler_params=_cparams(("arbitrary",)),
        name="ctx_attn",
    )(sink, qb, kb, vb)


def _win_attn_kernel(t, sink_ref, q_ref, k_ref, v_ref, kc_ref, vc_ref, o_ref):
    i = pl.program_id(1)
    scale = HD_B ** -0.5
    span = QBLK + 2 * WINDOW
    start = i * QBLK
    lo = jnp.clip(start - WINDOW, 0, t - span)
    lo = pl.multiple_of(lo, QBLK)
    q_pos = start + lax.broadcasted_iota(jnp.int32, (QBLK, span), 0)
    k_pos = lo + lax.broadcasted_iota(jnp.int32, (QBLK, span), 1)
    valid = jnp.abs(q_pos - k_pos) <= WINDOW
    q_all = q_ref[...] * scale
    kw_all = k_ref[pl.ds(lo, span), :]
    vw_all = v_ref[pl.ds(lo, span), :]
    kc_all = kc_ref[...]
    vc_all = vc_ref[...]
    outs = []
    for h in range(H_B):
        kv = h // G_B
        q = q_all[:, h * HD_B:(h + 1) * HD_B]
        kw = kw_all[:, kv * HD_B:(kv + 1) * HD_B]
        vw = vw_all[:, kv * HD_B:(kv + 1) * HD_B]
        kc = kc_all[:, kv * HD_B:(kv + 1) * HD_B]
        vc = vc_all[:, kv * HD_B:(kv + 1) * HD_B]
        s_win = jnp.where(valid, _dot_nt(q, kw), NEG_INF)
        s_ctx = _dot_nt(q, kc)
        sink = jnp.zeros((QBLK, 1), f32) + sink_ref[h]
        outs.append(_softmax_pv([s_win, s_ctx], [vw, vc], sink))
    o_ref[...] = jnp.concatenate(outs, axis=1)


def _win_attn(sink, qb, kb, vb, cache_k, cache_v, cache_layer, t, nseq, tok_block0):
    nq = H_B * HD_B
    nkv = KV_B * HD_B
    nb = t // QBLK
    past = cache_k.shape[2]
    kern = functools.partial(_win_attn_kernel, t)
    return pl.pallas_call(
        kern,
        grid=(nseq, nb),
        in_specs=[
            pl.BlockSpec(memory_space=pltpu.SMEM),
            pl.BlockSpec((QBLK, nq), lambda b, i: ((tok_block0 + b) * nb + i, 0)),
            pl.BlockSpec((t, nkv), lambda b, i: (tok_block0 + b, 0)),
            pl.BlockSpec((t, nkv), lambda b, i: (tok_block0 + b, 0)),
            pl.BlockSpec((None, None, past, nkv), lambda b, i: (b, cache_layer, 0, 0)),
            pl.BlockSpec((None, None, past, nkv), lambda b, i: (b, cache_layer, 0, 0)),
        ],
        out_specs=pl.BlockSpec((QBLK, nq), lambda b, i: (b * nb + i, 0)),
        out_shape=jax.ShapeDtypeStruct((nseq * t, nq), f32),
        compiler_params=_cparams(("arbitrary", "arbitrary")),
        name="win_attn",
    )(sink, qb, kb, vb, cache_k, cache_v)


def _mix_out_kernel(np_tiles, y_ref, oap_ref, oas_ref, obp_ref, obs_ref, gate_ref, dn_ref, w_ref, gm_ref, o_ref):
    i = pl.program_id(0)
    is_p = i < np_tiles
    oa = jnp.where(is_p, oap_ref[...], oas_ref[...])
    ob = jnp.where(is_p, obp_ref[...], obs_ref[...])
    oan = oa * lax.rsqrt(_group_sum64(oa * oa) * (1.0 / DV_A) + EPS) * dn_ref[...] * gate_ref[...]
    ka = H_A * DV_A
    mix = _dot(oan, w_ref[:ka, :]) + _dot(ob, w_ref[ka:, :])
    o_ref[...] = y_ref[...] + gm_ref[...] * mix


def _mix_out(y, mods, layer, row_of_tile, np_tiles, oa_p, oa_s, ob_p, ob_s, gate, dn_row, w_out):
    ntok = y.shape[0]
    nt = ntok // TM
    ka = H_A * DV_A
    p_idx = lambda i: (jnp.minimum(i, np_tiles - 1), 0)
    s_idx = lambda i: (jnp.maximum(i - np_tiles, 0), 0)
    kern = functools.partial(_mix_out_kernel, np_tiles)
    return pl.pallas_call(
        kern,
        grid=(nt,),
        in_specs=[
            pl.BlockSpec((TM, D_MODEL), lambda i: (i, 0)),
            pl.BlockSpec((TM, ka), p_idx),
            pl.BlockSpec((TM, ka), s_idx),
            pl.BlockSpec((TM, ka), p_idx),
            pl.BlockSpec((TM, ka), s_idx),
            pl.BlockSpec((TM, ka), lambda i: (i, 0)),
            pl.BlockSpec((1, ka), lambda i: (0, 0)),
            pl.BlockSpec((2 * ka, D_MODEL), lambda i: (0, 0)),
            _mod_spec(layer, 2, row_of_tile, 1),
        ],
        out_specs=pl.BlockSpec((TM, D_MODEL), lambda i: (i, 0)),
        out_shape=jax.ShapeDtypeStruct((ntok, D_MODEL), f32),
        compiler_params=_cparams(("arbitrary",)),
        name="mix_out",
    )(y, oa_p, oa_s, ob_p, ob_s, gate, dn_row, w_out, mods)


def _lin_scan(a, b, reverse):
    t = a.shape[0]
    r = lax.broadcasted_iota(jnp.int32, (t, 1), 0)
    s = 1
    while s < t:
        if reverse:
            ok = r + s < t
            sh = t - s
        else:
            ok = r >= s
            sh = s
        b = b + a * jnp.where(ok, pltpu.roll(b, sh, axis=0), 0.0)
        a = a * jnp.where(ok, pltpu.roll(a, sh, axis=0), 1.0)
        s *= 2
    return b


def _gelu_tanh(x):
    return 0.5 * x * (1.0 + jnp.tanh(math.sqrt(2.0 / math.pi) * (x + 0.044715 * (x * x * x))))


def _lru_kernel(np_tiles, t_p, t_s, y_ref, g_ref, sh_ref, sc_ref, wx_ref, wg_ref, cw_ref, cb_ref,
                wl_ref, bl_ref, lam_ref, h0_ref, o_ref, fin_ref, h_s):
    i = pl.program_id(0)

    @pl.when(pl.program_id(1) == 0)
    def _():
        h_s[...] = _modulate(y_ref[...], g_ref[...], sh_ref[...], sc_ref[...]).astype(bf16)

    h = h_s[...]
    zx = jnp.dot(h, wx_ref[...].astype(bf16), preferred_element_type=f32)
    zg = jnp.dot(h, wg_ref[...].astype(bf16), preferred_element_type=f32)
    tseq = jnp.where(i < np_tiles, t_p, t_s)
    row = lax.broadcasted_iota(jnp.int32, (TM, 1), 0)
    pos = row & (tseq - 1)
    x = _seq_conv(zx, cw_ref, cb_ref, pos, tseq)
    gates = _dot(x, wl_ref[...]) + bl_ref[...]
    nseg = TM // t_p
    seg = jnp.where(i < np_tiles, row >> (t_p.bit_length() - 1), 0)
    acc = None
    for d in range(2):
        r = _sigmoid(gates[:, (2 * d) * LANES:(2 * d + 1) * LANES])
        ig = _sigmoid(gates[:, (2 * d + 1) * LANES:(2 * d + 2) * LANES])
        lam = lam_ref[:, d * LANES:(d + 1) * LANES]
        log_a = -RG_C * r * _softplus(-lam)
        a = jnp.exp(log_a)
        u = jnp.sqrt(-jnp.tanh(log_a) * (a * a + 1.0)) * (ig * x)
        h0 = jnp.zeros((TM, LANES), f32)
        for sg in range(nseg):
            h0 = jnp.where(seg == sg, h0_ref[sg, d:d + 1, :], h0)
        first = (pos == 0) if d == 0 else (pos == tseq - 1)
        u = jnp.where(first, u + a * h0, u)
        a = jnp.where(first, 0.0, a)
        hs = _lin_scan(a, u, d == 1)
        for sg in range(nseg):
            rr = sg * t_p + (t_p - 1 if d == 0 else 0)
            fin_ref[sg, d:d + 1, :] = hs[rr:rr + 1, :]
        acc = hs if acc is None else acc + hs
    o_ref[...] = acc * _gelu_tanh(zg)


def _lru(y, mods, layer, row_of_tile, np_tiles, t_p, t_s, norm_g, w_in, cw, cb, wl, bl, lam, h0):
    ntok = y.shape[0]
    nt = ntok // TM
    ncol = D_RNN // LANES
    nseg = TM // t_p
    kern = functools.partial(_lru_kernel, np_tiles, t_p, t_s)
    return pl.pallas_call(
        kern,
        grid=(nt, ncol),
        in_specs=[
            pl.BlockSpec((TM, D_MODEL), lambda i, j: (i, 0)),
            pl.BlockSpec((1, D_MODEL), lambda i, j: (0, 0)),
            _mod_spec(layer, 0, row_of_tile, 2),
            _mod_spec(layer, 1, row_of_tile, 2),
            pl.BlockSpec((D_MODEL, LANES), lambda i, j: (0, j)),
            pl.BlockSpec((D_MODEL, LANES), lambda i, j: (0, ncol + j)),
            pl.BlockSpec((CONV_W, LANES), lambda i, j: (0, j)),
            pl.BlockSpec((1, LANES), lambda i, j: (0, j)),
            pl.BlockSpec((None, LANES, 4 * LANES), lambda i, j: (j, 0, 0)),
            pl.BlockSpec((None, 1, 4 * LANES), lambda i, j: (j, 0, 0)),
            pl.BlockSpec((None, 1, 2 * LANES), lambda i, j: (j, 0, 0)),
            pl.BlockSpec((None, nseg, 2, LANES), lambda i, j: (i, 0, 0, j)),
        ],
        out_specs=[
            pl.BlockSpec((TM, LANES), lambda i, j: (i, j)),
            pl.BlockSpec((None, nseg, 2, LANES), lambda i, j: (i, 0, 0, j)),
        ],
        out_shape=[jax.ShapeDtypeStruct((ntok, D_RNN), f32),
                   jax.ShapeDtypeStruct((nt, nseg, 2, D_RNN), f32)],
        scratch_shapes=[pltpu.VMEM((TM, D_MODEL), bf16)],
        compiler_params=_cparams(("arbitrary", "arbitrary")),
        name="lru",
    )(y, norm_g, mods, mods, w_in, w_in, cw, cb, wl, bl, lam, h0)


def _proj_out_kernel(y_ref, a_ref, w_ref, gm_ref, o_ref):
    o_ref[...] = y_ref[...] + gm_ref[...] * _dot(a_ref[...], w_ref[...])


def _proj_out(y, mods, layer, row_of_tile, a, w):
    ntok = y.shape[0]
    nt = ntok // TM
    k = a.shape[1]
    return pl.pallas_call(
        _proj_out_kernel,
        grid=(nt,),
        in_specs=[
            pl.BlockSpec((TM, D_MODEL), lambda i: (i, 0)),
            pl.BlockSpec((TM, k), lambda i: (i, 0)),
            pl.BlockSpec((k, D_MODEL), lambda i: (0, 0)),
            _mod_spec(layer, 2, row_of_tile, 1),
        ],
        out_specs=pl.BlockSpec((TM, D_MODEL), lambda i: (i, 0)),
        out_shape=jax.ShapeDtypeStruct((ntok, D_MODEL), f32),
        compiler_params=_cparams(("arbitrary",)),
        name="proj_out",
    )(y, a, w, mods)


FF_BLK = 1024


def _mlp_kernel(y_ref, g_ref, sh_ref, sc_ref, gm_ref, w1_ref, w2_ref, o_ref, h_s, acc_s):
    k = pl.program_id(1)

    @pl.when(k == 0)
    def _():
        h_s[...] = _modulate(y_ref[...], g_ref[...], sh_ref[...], sc_ref[...]).astype(bf16)
        acc_s[...] = jnp.zeros_like(acc_s)

    a = jnp.dot(h_s[...], w1_ref[...].astype(bf16), preferred_element_type=f32)
    a = jnp.maximum(a, 0.0)
    acc_s[...] += _dot(a * a, w2_ref[...])

    @pl.when(k == pl.num_programs(1) - 1)
    def _():
        o_ref[...] = y_ref[...] + gm_ref[...] * acc_s[...]


def _mlp(y, mods, layer, row_of_tile, norm_g, w1, w2):
    ntok = y.shape[0]
    nt = ntok // TM
    return pl.pallas_call(
        _mlp_kernel,
        grid=(nt, D_FF // FF_BLK),
        in_specs=[
            pl.BlockSpec((TM, D_MODEL), lambda i, k: (i, 0)),
            pl.BlockSpec((1, D_MODEL), lambda i, k: (0, 0)),
            _mod_spec(layer, 3, row_of_tile, 2),
            _mod_spec(layer, 4, row_of_tile, 2),
            _mod_spec(layer, 5, row_of_tile, 2),
            pl.BlockSpec((None, D_MODEL, FF_BLK), lambda i, k: (layer, 0, k)),
            pl.BlockSpec((None, FF_BLK, D_MODEL), lambda i, k: (layer, k, 0)),
        ],
        out_specs=pl.BlockSpec((TM, D_MODEL), lambda i, k: (i, 0)),
        out_shape=jax.ShapeDtypeStruct((ntok, D_MODEL), f32),
        scratch_shapes=[pltpu.VMEM((TM, D_MODEL), bf16), pltpu.VMEM((TM, D_MODEL), f32)],
        compiler_params=_cparams(("arbitrary", "arbitrary")),
        name="mlp",
    )(y, norm_g, mods, mods, mods, w1, w2)


def _attn_cols(w_in, a_log, dt_bias):
    s2 = N_QKV + H_A * DV_A
    s4 = s2 + 4 * H_A
    pad = jnp.zeros((D_MODEL, LANES - 4 * H_A), f32)
    w_b = jnp.concatenate([w_in[:, N_QKV:s2], w_in[:, s4:], w_in[:, s2:s4], pad], axis=1)
    row_pad = jnp.zeros((LANES - 2 * H_A,), f32)
    al_row = jnp.concatenate([a_log.reshape(-1), row_pad]).reshape(1, LANES)
    dt_row = jnp.concatenate([dt_bias.reshape(-1), row_pad]).reshape(1, LANES)
    return w_b, al_row, dt_row


def _rope_tables(t):
    rows = t // GRID_W
    r = jnp.repeat(jnp.arange(rows, dtype=f32), GRID_W)
    cc = jnp.tile(jnp.arange(GRID_W, dtype=f32), rows)
    inv = ROPE_BASE ** (-jnp.arange(0, ROPE_AXIS, 2, dtype=f32) / ROPE_AXIS)
    ang_r = r[:, None] * inv
    ang_c = cc[:, None] * inv
    cos = jnp.concatenate([jnp.cos(ang_r), jnp.cos(ang_r), jnp.cos(ang_c), jnp.cos(ang_c)], axis=-1)
    sin = jnp.concatenate([-jnp.sin(ang_r), jnp.sin(ang_r), -jnp.sin(ang_c), jnp.sin(ang_c)], axis=-1)
    cos = jnp.tile(cos, (1, H_B))
    sin = jnp.tile(sin, (1, H_B))
    ident_c = jnp.ones((TM, H_B * HD_B), f32)
    ident_s = jnp.zeros((TM, H_B * HD_B), f32)
    return jnp.concatenate([ident_c, cos], axis=0), jnp.concatenate([ident_s, sin], axis=0)


def _lru_cols(w_a, b_a, w_x, b_x, lam):
    ncol = D_RNN // LANES
    per = LANES // LRU_BW

    def bd(w):
        w = w.reshape(ncol, per, LRU_BW, LRU_BW)
        eye = jnp.eye(per, dtype=f32)
        return jnp.einsum('gpij,pq->gpiqj', w, eye).reshape(ncol, LANES, LANES)

    wl = jnp.concatenate([bd(w_a[0]), bd(w_x[0]), bd(w_a[1]), bd(w_x[1])], axis=-1)
    rows = lambda v: v.reshape(ncol, 1, LANES)
    bl = jnp.concatenate([rows(b_a[0]), rows(b_x[0]), rows(b_a[1]), rows(b_x[1])], axis=-1)
    lm = jnp.concatenate([rows(lam[0]), rows(lam[1])], axis=-1)
    return wl, bl, lm


def kernel(x_prompt, x_sample, state_delta, cache_k, cache_v, state_lru, c, c_ctx, ada_w, ada_b, norm1_g, norm2_g, ff_w1, ff_w2, ab_w_in, ab_conv_w, ab_conv_b, dn_a_log, dn_dt_bias, dn_norm_g, attn_q_norm_g, attn_k_norm_g, attn_sink, ab_w_out, c_w_in, c_conv_w, c_conv_b, lru_w_a, lru_b_a, lru_w_x, lru_b_x, lru_lambda, c_w_out):
    b_p, t_p, _ = x_prompt.shape
    b_s, t_s, _ = x_sample.shape
    assert t_s == TM and TM % t_p == 0 and (b_p * t_p) % TM == 0
    np_tok = b_p * t_p
    np_tiles = np_tok // TM
    nkv = KV_B * HD_B

    y = jnp.concatenate([x_prompt.reshape(np_tok, D_MODEL), x_sample.reshape(b_s * t_s, D_MODEL)], axis=0)
    nt = y.shape[0] // TM

    rows = -(-(b_s + 1) // 8) * 8
    cond = jnp.zeros((rows, D_MODEL), f32).at[:b_s].set(c).at[b_s].set(c_ctx)
    mods = _adaln(cond, ada_w, ada_b)
    row_of_tile = lambda i: jnp.where(i < np_tiles, b_s, i - np_tiles)

    cos_tab, sin_tab = _rope_tables(t_s)
    ck = cache_k.reshape(cache_k.shape[0], cache_k.shape[1], cache_k.shape[2], nkv)
    cv = cache_v.reshape(ck.shape)

    new_dn, new_k, new_v, new_lru = [], [], [], []
    for l in range(DEPTH):
        j = l // 2
        n1 = norm1_g[l].reshape(1, D_MODEL)
        if l % 2 == 0:
            qkv = _delta_proj(y, mods, l, row_of_tile, np_tiles, t_p, t_s, n1, ab_w_in[j], ab_conv_w[j],
                              ab_conv_b[j].reshape(1, N_QKV))
            w_b, al_row, dt_row = _attn_cols(ab_w_in[j], dn_a_log[j], dn_dt_bias[j])
            qn_row = jnp.tile(attn_q_norm_g[j], H_B).reshape(1, H_B * HD_B)
            kn_row = jnp.tile(attn_k_norm_g[j], KV_B).reshape(1, nkv)
            gate, qb, kb, vb, gb = _attn_proj(y, mods, l, row_of_tile, np_tiles, n1, w_b, qn_row, kn_row,
                                              cos_tab, sin_tab, al_row, dt_row)
            oa_p, s_fin = _delta(qkv, gb, t_p, b_p, 0, None, 0)
            oa_s, _ = _delta(qkv, gb, t_s, b_s, np_tok // t_s, state_delta, j)
            ob_p = _ctx_attn(attn_sink[j], qb, kb, vb, t_p, b_p)
            ob_s = _win_attn(attn_sink[j], qb, kb, vb, ck, cv, j, t_s, b_s, np_tok // t_s)
            dn_row = jnp.tile(dn_norm_g[j], H_A).reshape(1, H_A * DV_A)
            y = _mix_out(y, mods, l, row_of_tile, np_tiles, oa_p, oa_s, ob_p, ob_s, gate, dn_row, ab_w_out[j])
            new_dn.append(s_fin)
            new_k.append(kb[:np_tok].reshape(b_p, t_p, KV_B, HD_B))
            new_v.append(vb[:np_tok].reshape(b_p, t_p, KV_B, HD_B))
        else:
            wl, bl, lm = _lru_cols(lru_w_a[j], lru_b_a[j], lru_w_x[j], lru_b_x[j], lru_lambda[j])
            nseg = TM // t_p
            h0 = jnp.zeros((nt, nseg, 2, D_RNN), f32).at[np_tiles:, 0].set(state_lru[:, j])
            mixed, fin = _lru(y, mods, l, row_of_tile, np_tiles, t_p, t_s, n1, c_w_in[j], c_conv_w[j],
                              c_conv_b[j].reshape(1, D_RNN), wl, bl, lm, h0)
            y = _proj_out(y, mods, l, row_of_tile, mixed, c_w_out[j])
            new_lru.append(fin[:np_tiles].reshape(b_p, 2, D_RNN))
        y = _mlp(y, mods, l, row_of_tile, norm2_g[l].reshape(1, D_MODEL), ff_w1, ff_w2)

    yp = y[:np_tok].reshape(b_p, t_p, D_MODEL)
    ys = y[np_tok:].reshape(b_s, t_s, D_MODEL)
    return (yp, ys, jnp.stack(new_dn, axis=1), jnp.stack(new_k, axis=1), jnp.stack(new_v, axis=1),
            jnp.stack(new_lru, axis=1))
```

```python
import functools
import math

import jax
import jax.numpy as jnp
from jax import lax
from jax.experimental import pallas as pl
from jax.experimental.pallas import tpu as pltpu

f32 = jnp.float32
bf16 = jnp.bfloat16

D_MODEL = 1024
DEPTH = 4
GRID_W = 64
H_A = 8
DK_A = 64
DV_A = 64
CHUNK = 64
CONV_W = 4
H_B = 8
KV_B = 2
G_B = H_B // KV_B
HD_B = 64
WINDOW = 128
QBLK = 128
ROPE_AXIS = HD_B // 2
ROPE_BASE = 10000.0
D_RNN = D_MODEL
LRU_BLOCKS = 16
LRU_BW = D_RNN // LRU_BLOCKS
RG_C = 8.0
D_FF = 4 * D_MODEL
EPS = 1e-6
NEG_INF = -1e30

TM = 1024
LANES = 128
SUBLANES = 8
MXU_DIM = 256
N_QKV = 3 * H_A * DK_A
N_ATTN = H_A * DV_A + (H_B + 2 * KV_B) * HD_B + LANES
VMEM_LIMIT = 56 * 1024 * 1024


def _cparams(sem):
    return pltpu.CompilerParams(dimension_semantics=sem, vmem_limit_bytes=VMEM_LIMIT)


def _dot(a, b):
    return jnp.dot(a.astype(bf16), b.astype(bf16), preferred_element_type=f32)


def _dot_nt(a, b):
    return lax.dot_general(a.astype(bf16), b.astype(bf16), (((1,), (1,)), ((), ())),
                           preferred_element_type=f32)


def _dot_tn(a, b):
    return lax.dot_general(a.astype(bf16), b.astype(bf16), (((0,), (0,)), ((), ())),
                           preferred_element_type=f32)


def _sigmoid(x):
    return 1.0 / (1.0 + jnp.exp(-x))


def _softplus(x):
    return jnp.maximum(x, 0.0) + jnp.log1p(jnp.exp(-jnp.abs(x)))


def _modulate(x, g, shift, scale):
    ms = jnp.mean(x * x, axis=-1, keepdims=True)
    y = x * lax.rsqrt(ms + EPS) * g
    return y * (1.0 + scale) + shift


def _split2(x):
    hi = x.astype(bf16)
    lo = (x - hi.astype(f32)).astype(bf16)
    return hi, lo


def _split3(x):
    p1 = x.astype(bf16)
    r1 = x - p1.astype(f32)
    p2 = r1.astype(bf16)
    p3 = (r1 - p2.astype(f32)).astype(bf16)
    return p1, p2, p3


def _group_sum64(v):
    w = v.shape[-1]
    r = lax.broadcasted_iota(jnp.int32, (w, w), 0) >> 6
    c = lax.broadcasted_iota(jnp.int32, (w, w), 1) >> 6
    ones_bd = jnp.where(r == c, 1.0, 0.0).astype(bf16)
    hi, lo = _split2(v)
    return (jnp.dot(hi, ones_bd, preferred_element_type=f32)
            + jnp.dot(lo, ones_bd, preferred_element_type=f32))


def _seq_conv(z, cw_ref, cb_ref, pos, tseq):
    n = z.shape[0]
    acc = jnp.zeros_like(z) + cb_ref[...]
    left = CONV_W // 2
    for j in range(CONV_W):
        o = j - left
        zs = z if o == 0 else pltpu.roll(z, (-o) % n, axis=0)
        valid = (pos + o >= 0) & (pos + o < tseq)
        acc = acc + jnp.where(valid, zs, 0.0) * cw_ref[j:j + 1, :]
    return acc


def _adaln_kernel(c_ref, w_ref, b_ref, o_ref):
    c = c_ref[...]
    a = c * _sigmoid(c)
    o_ref[...] = _dot(a, w_ref[...]) + b_ref[...]


def _adaln(cond, ada_w, ada_b):
    rows = cond.shape[0]
    out = pl.pallas_call(
        _adaln_kernel,
        grid=(DEPTH, 6),
        in_specs=[
            pl.BlockSpec((rows, D_MODEL), lambda l, k: (0, 0)),
            pl.BlockSpec((None, D_MODEL, D_MODEL), lambda l, k: (l, 0, k)),
            pl.BlockSpec((None, None, 1, D_MODEL), lambda l, k: (l, k, 0, 0)),
        ],
        out_specs=pl.BlockSpec((None, None, rows, D_MODEL), lambda l, k: (l, k, 0, 0)),
        out_shape=jax.ShapeDtypeStruct((DEPTH, 6, rows, D_MODEL), f32),
        compiler_params=_cparams(("arbitrary", "arbitrary")),
        name="adaln",
    )(cond, ada_w, ada_b.reshape(DEPTH, 6, 1, D_MODEL))
    return out.reshape(DEPTH, 6, rows, 1, D_MODEL)


def _stream_specs(np_tiles, ngrid, split):
    def spec(block_of_tile):
        if ngrid == 1:
            return pl.BlockSpec((TM, D_MODEL), lambda i: (block_of_tile(i), 0))
        return pl.BlockSpec((TM, D_MODEL), lambda i, j: (block_of_tile(i), 0))

    if not split:
        return [spec(lambda i: i)]
    return [spec(lambda i: jnp.minimum(i, np_tiles - 1)), spec(lambda i: jnp.maximum(i - np_tiles, 0))]


def _stream_load(np_tiles, refs):
    if len(refs) == 1:
        return refs[0][...]
    return jnp.where(pl.program_id(0) < np_tiles, refs[0][...], refs[1][...])


def _stream_store(np_tiles, refs, val):
    if len(refs) == 1:
        refs[0][...] = val
        return
    i = pl.program_id(0)

    @pl.when(i < np_tiles)
    def _():
        refs[0][...] = val

    @pl.when(i >= np_tiles)
    def _():
        refs[1][...] = val


def _stream_shapes(np_tiles, nt, split):
    if not split:
        return [jax.ShapeDtypeStruct((nt * TM, D_MODEL), f32)]
    return [jax.ShapeDtypeStruct((np_tiles * TM, D_MODEL), f32),
            jax.ShapeDtypeStruct(((nt - np_tiles) * TM, D_MODEL), f32)]


def _stream_tiles(ys):
    return sum(y.shape[0] for y in ys) // TM


def _mod_spec(layer, k, row_of_tile, ngrid):
    if ngrid == 1:
        return pl.BlockSpec((None, None, None, 1, D_MODEL), lambda i: (layer, k, row_of_tile(i), 0, 0))
    return pl.BlockSpec((None, None, None, 1, D_MODEL), lambda i, j: (layer, k, row_of_tile(i), 0, 0))


QKV_BLK = H_A * DK_A


def _delta_proj_kernel(np_tiles, t_p, t_s, n_y, *refs):
    y_refs = refs[:n_y]
    g_ref, sh_ref, sc_ref, w_ref, cw_ref, cb_ref, o_ref, h_s = refs[n_y:]
    i = pl.program_id(0)
    n = pl.program_id(1)

    @pl.when(n == 0)
    def _():
        x = _stream_load(np_tiles, y_refs)
        h_s[...] = _modulate(x, g_ref[...], sh_ref[...], sc_ref[...]).astype(bf16)

    z = jnp.dot(h_s[...], w_ref[...].astype(bf16), preferred_element_type=f32)
    tseq = jnp.where(i < np_tiles, t_p, t_s)
    pos = lax.broadcasted_iota(jnp.int32, (TM, 1), 0) & (tseq - 1)
    y = _seq_conv(z, cw_ref, cb_ref, pos, tseq)
    y = y * _sigmoid(y)

    @pl.when(n < 2)
    def _():
        o_ref[...] = y * lax.rsqrt(_group_sum64(y * y) + EPS)

    @pl.when(n == 2)
    def _():
        o_ref[...] = y


def _delta_proj(ys, mods, layer, row_of_tile, np_tiles, t_p, t_s, norm_g, w_in, conv_w, conv_b):
    nt = _stream_tiles(ys)
    ntok = nt * TM
    kern = functools.partial(_delta_proj_kernel, np_tiles, t_p, t_s, len(ys))
    return pl.pallas_call(
        kern,
        grid=(nt, N_QKV // QKV_BLK),
        in_specs=_stream_specs(np_tiles, 2, len(ys) == 2) + [
            pl.BlockSpec((1, D_MODEL), lambda i, n: (0, 0)),
            _mod_spec(layer, 0, row_of_tile, 2),
            _mod_spec(layer, 1, row_of_tile, 2),
            pl.BlockSpec((D_MODEL, QKV_BLK), lambda i, n: (0, n)),
            pl.BlockSpec((CONV_W, QKV_BLK), lambda i, n: (0, n)),
            pl.BlockSpec((1, QKV_BLK), lambda i, n: (0, n)),
        ],
        out_specs=pl.BlockSpec((TM, QKV_BLK), lambda i, n: (i, n)),
        out_shape=jax.ShapeDtypeStruct((ntok, N_QKV), f32),
        scratch_shapes=[pltpu.VMEM((TM, D_MODEL), bf16)],
        compiler_params=_cparams(("arbitrary", "arbitrary")),
        name="delta_proj",
    )(*ys, norm_g, mods, mods, w_in, conv_w, conv_b)


def _rope_swap(x):
    w = x.shape[1]
    lane = lax.broadcasted_iota(jnp.int32, (1, w), 1)
    first = (lane & (ROPE_AXIS - 1)) < ROPE_AXIS // 2
    return jnp.where(first, pltpu.roll(x, w - ROPE_AXIS // 2, axis=1), pltpu.roll(x, ROPE_AXIS // 2, axis=1))


def _attn_proj_kernel(np_tiles, n_y, *refs):
    y_refs = refs[:n_y]
    (g_ref, sh_ref, sc_ref, w_ref, qn_ref, kn_ref, cos_ref, sin_ref, al_ref, dt_ref,
     gate_ref, q_ref, k_ref, v_ref, gb_ref, kc_ref, vc_ref) = refs[n_y:]
    x = _stream_load(np_tiles, y_refs)
    h = _modulate(x, g_ref[...], sh_ref[...], sc_ref[...]).astype(bf16)
    z = jnp.dot(h, w_ref[...].astype(bf16), preferred_element_type=f32)
    nq = H_B * HD_B
    nkv = KV_B * HD_B
    o1 = H_A * DV_A
    o2 = o1 + nq + 2 * nkv
    gz = z[:, :o1]
    gate_ref[...] = gz * _sigmoid(gz)
    q = z[:, o1:o1 + nq]
    k = z[:, o1 + nq:o1 + nq + nkv]
    v = z[:, o1 + nq + nkv:o2]
    v_ref[...] = v
    qn = q * lax.rsqrt(_group_sum64(q * q) * (1.0 / HD_B) + EPS) * qn_ref[...]
    kn = k * lax.rsqrt(_group_sum64(k * k) * (1.0 / HD_B) + EPS) * kn_ref[...]
    cos = cos_ref[...]
    sin = sin_ref[...]
    q_ref[...] = qn * cos + _rope_swap(qn) * sin
    k_ref[...] = kn * cos[:, :nkv] + _rope_swap(kn) * sin[:, :nkv]

    @pl.when(pl.program_id(0) < np_tiles)
    def _():
        kc_ref[...] = kn
        vc_ref[...] = v
    zs = z[:, o2:]
    lane = lax.broadcasted_iota(jnp.int32, (1, LANES), 1)
    g_val = -jnp.exp(al_ref[...]) * _softplus(zs + dt_ref[...])
    gb_ref[...] = jnp.where(lane < 2 * H_A, g_val, jnp.where(lane < 4 * H_A, _sigmoid(zs), 0.0))


def _attn_proj(ys, mods, layer, row_of_tile, np_tiles, norm_g, w_b, qn_row, kn_row, cos_tab, sin_tab,
               al_row, dt_row):
    nt = _stream_tiles(ys)
    ntok = nt * TM
    nq = H_B * HD_B
    nkv = KV_B * HD_B
    tab_idx = lambda i: (jnp.where(i < np_tiles, 0, 1), 0)
    cache_idx = lambda i: (jnp.minimum(i, np_tiles - 1), 0)
    return pl.pallas_call(
        functools.partial(_attn_proj_kernel, np_tiles, len(ys)),
        grid=(nt,),
        in_specs=_stream_specs(np_tiles, 1, len(ys) == 2) + [
            pl.BlockSpec((1, D_MODEL), lambda i: (0, 0)),
            _mod_spec(layer, 0, row_of_tile, 1),
            _mod_spec(layer, 1, row_of_tile, 1),
            pl.BlockSpec((D_MODEL, N_ATTN), lambda i: (0, 0)),
            pl.BlockSpec((1, nq), lambda i: (0, 0)),
            pl.BlockSpec((1, nkv), lambda i: (0, 0)),
            pl.BlockSpec((TM, nq), tab_idx),
            pl.BlockSpec((TM, nq), tab_idx),
            pl.BlockSpec((1, LANES), lambda i: (0, 0)),
            pl.BlockSpec((1, LANES), lambda i: (0, 0)),
        ],
        out_specs=[
            pl.BlockSpec((TM, H_A * DV_A), lambda i: (i, 0)),
            pl.BlockSpec((TM, nq), lambda i: (i, 0)),
            pl.BlockSpec((TM, nkv), lambda i: (i, 0)),
            pl.BlockSpec((TM, nkv), lambda i: (i, 0)),
            pl.BlockSpec((TM, LANES), lambda i: (i, 0)),
            pl.BlockSpec((TM, nkv), cache_idx),
            pl.BlockSpec((TM, nkv), cache_idx),
        ],
        out_shape=[jax.ShapeDtypeStruct((ntok, H_A * DV_A), f32),
                   jax.ShapeDtypeStruct((ntok, nq), f32),
                   jax.ShapeDtypeStruct((ntok, nkv), f32),
                   jax.ShapeDtypeStruct((ntok, nkv), f32),
                   jax.ShapeDtypeStruct((ntok, LANES), f32),
                   jax.ShapeDtypeStruct((np_tiles * TM, nkv), f32),
                   jax.ShapeDtypeStruct((np_tiles * TM, nkv), f32)],
        compiler_params=_cparams(("arbitrary",)),
        name="attn_proj",
    )(*ys, norm_g, mods, mods, w_b, qn_row, kn_row, cos_tab, sin_tab, al_row, dt_row)


INV_BASE = 8
HPG = MXU_DIM // DK_A
NLG = H_A // HPG
W_ALL = H_A * DK_A


def _chunk_scan(x, reverse):
    t = x.shape[0]
    r = lax.broadcasted_iota(jnp.int32, (t, 1), 0) & (CHUNK - 1)
    s = 1
    while s < CHUNK:
        if reverse:
            x = x + jnp.where(r + s < CHUNK, pltpu.roll(x, t - s, axis=0), 0.0)
        else:
            x = x + jnp.where(r >= s, pltpu.roll(x, s, axis=0), 0.0)
        s *= 2
    return x


def _delta_kernel(t, has_s0, *refs):
    if has_s0:
        q_ref, k_ref, v_ref, gb_ref, s0_ref = refs[:5]
        rest = refs[5:]
    else:
        q_ref, k_ref, v_ref, gb_ref = refs[:4]
        s0_ref = None
        rest = refs[4:]
    o_ref, sf_ref, exp_s, kbg_s, qg_s, kd_s, vb_s, egt_s, ti_s, in_s, st_s = rest
    n = t // CHUNK
    scale = DK_A ** -0.5

    gb = gb_ref[...]
    lane = lax.broadcasted_iota(jnp.int32, (1, LANES), 1)
    sc = jnp.where(lane < H_A, _chunk_scan(gb, False), jnp.where(lane < 2 * H_A, _chunk_scan(gb, True), gb))
    er = lax.broadcasted_iota(jnp.int32, (LANES, 4 * W_ALL), 0)
    ec = lax.broadcasted_iota(jnp.int32, (LANES, 4 * W_ALL), 1) >> 6
    expand = jnp.where(er == ec, 1.0, 0.0).astype(bf16)
    rb = min(t, 256)
    for r0 in range(0, t, rb):
        p1, p2, p3 = _split3(sc[r0:r0 + rb])
        exp_s[r0:r0 + rb, :] = ((jnp.dot(p1, expand, preferred_element_type=f32)
                                 + jnp.dot(p2, expand, preferred_element_type=f32))
                                + jnp.dot(p3, expand, preferred_element_type=f32))

    ri = lax.broadcasted_iota(jnp.int32, (CHUNK, MXU_DIM), 0)
    ci = lax.broadcasted_iota(jnp.int32, (CHUNK, MXU_DIM), 1) & (CHUNK - 1)
    eye_ls = ri == ci
    bd_mask = ((lax.broadcasted_iota(jnp.int32, (MXU_DIM, MXU_DIM), 0) >> 6)
               == (lax.broadcasted_iota(jnp.int32, (MXU_DIM, MXU_DIM), 1) >> 6))
    ones_c = jnp.ones((CHUNK, CHUNK), bf16)

    def bd(x):
        xb = x.astype(bf16)
        return jnp.where(bd_mask, jnp.concatenate([xb] * HPG, axis=0), jnp.zeros((), bf16))

    def blk(b):
        sh = b.bit_length() - 1
        return (ri >> sh) == (ci >> sh)

    def mm3(xh, xl, yh, yl):
        r1 = jnp.dot(jnp.concatenate([xh, xl], axis=0), bd(yh), preferred_element_type=f32)
        r2 = jnp.dot(xh, bd(yl), preferred_element_type=f32)
        return (r1[:CHUNK] + r1[CHUNK:]) + r2

    def pre_body(c, carry):
        r0 = pl.multiple_of(c * CHUNK, CHUNK)
        rows = pl.ds(r0, CHUNK)
        q = q_ref[rows, :]
        k = k_ref[rows, :]
        v = v_ref[rows, :]
        gcs, kbs = [], []
        for d in range(2):
            gc = exp_s[rows, d * W_ALL:(d + 1) * W_ALL]
            beta = exp_s[rows, (2 + d) * W_ALL:(3 + d) * W_ALL]
            gtot = gc[CHUNK - 1:CHUNK, :] if d == 0 else gc[0:1, :]
            eg = jnp.exp(gc)
            kb = k * beta
            kbg_s[d, rows, :] = (kb * eg).astype(bf16)
            qg_s[d, rows, :] = (q * (scale * eg)).astype(bf16)
            kd_s[d, rows, :] = (k * jnp.exp(gtot - gc)).astype(bf16)
            vb_s[d, rows, :] = v * beta
            egt_s[d, c] = jnp.zeros((8, W_ALL), f32) + jnp.exp(gtot)
            gcs.append(gc)
            kbs.append(kb)
        units = [(d, lg) for lg in range(NLG) for d in range(2)]
        prods = []
        for lg in range(NLG):
            cols = slice(lg * MXU_DIM, (lg + 1) * MXU_DIM)
            lhs = jnp.concatenate([kbs[0][:, cols], kbs[1][:, cols], q[:, cols] * scale], axis=0)
            prods.append(_dot_nt(lhs, bd(k[:, cols])))
        ps, ys, avs = [], [], []
        for d, lg in units:
            cols = slice(lg * MXU_DIM, (lg + 1) * MXU_DIM)
            gc = gcs[d][:, cols]
            d1, d2, d3 = _split3(jnp.where(eye_ls, gc, 0.0))
            grow = ((jnp.dot(ones_c, d1, preferred_element_type=f32)
                     + jnp.dot(ones_c, d2, preferred_element_type=f32))
                    + jnp.dot(ones_c, d3, preferred_element_type=f32))
            keep = (ri >= ci) if d == 0 else (ri <= ci)
            strict = (ri > ci) if d == 0 else (ri < ci)
            decay = jnp.where(keep, jnp.exp(jnp.where(keep, gc - grow, 0.0)), 0.0)
            a = jnp.where(strict, prods[lg][d * CHUNK:(d + 1) * CHUNK] * decay, 0.0)
            in_s[d, c, :, cols] = jnp.where(keep, prods[lg][2 * CHUNK:] * decay, 0.0).astype(bf16)
            a_d = jnp.where(blk(INV_BASE), a, 0.0)
            avs.append(a)
            ys.append(_split2(-a_d))
            ps.append(jnp.where(eye_ls, 1.0, 0.0) - a_d)
        m = 1
        while m < INV_BASE:
            last = 2 * m >= INV_BASE
            for u in range(len(units)):
                yh, yl = ys[u]
                p = ps[u]
                byh, byl = bd(yh), bd(yl)
                if m == 1:
                    r1 = jnp.dot(jnp.concatenate([yh, yl], axis=0), byh, preferred_element_type=f32)
                    r2 = jnp.dot(yh, byl, preferred_element_type=f32)
                    y2 = (r1[:CHUNK] + r1[CHUNK:]) + r2
                else:
                    ph, plo = _split2(p)
                    if last:
                        r1 = jnp.dot(jnp.concatenate([ph, plo], axis=0), byh, preferred_element_type=f32)
                        r2 = jnp.dot(ph, byl, preferred_element_type=f32)
                        p = p + ((r1[:CHUNK] + r1[CHUNK:]) + r2)
                    else:
                        r1 = jnp.dot(jnp.concatenate([ph, plo, yh, yl], axis=0), byh,
                                     preferred_element_type=f32)
                        r2 = jnp.dot(jnp.concatenate([ph, yh], axis=0), byl, preferred_element_type=f32)
                        p = p + ((r1[:CHUNK] + r1[CHUNK:2 * CHUNK]) + r2[:CHUNK])
                        y2 = (r1[2 * CHUNK:3 * CHUNK] + r1[3 * CHUNK:]) + r2[CHUNK:]
                ps[u] = p
                if not last:
                    ys[u] = _split2(y2)
            m *= 2
        b = INV_BASE
        while b < CHUNK:
            off = blk(2 * b) & jnp.logical_not(blk(b))
            ws = []
            for u in range(len(units)):
                eh, el = _split2(jnp.where(off, avs[u], 0.0))
                ph, plo = _split2(ps[u])
                ws.append(mm3(eh, el, ph, plo))
            for u in range(len(units)):
                ph, plo = _split2(ps[u])
                wh, wl = _split2(ws[u])
                ps[u] = ps[u] - mm3(ph, plo, wh, wl)
            b *= 2
        for u, (d, lg) in enumerate(units):
            ti_s[d, c, :, lg * MXU_DIM:(lg + 1) * MXU_DIM] = ps[u].astype(bf16)
        return carry

    lax.fori_loop(0, n, pre_body, 0)

    for d in range(2):
        for lg in range(NLG):
            if has_s0:
                blocks = []
                for hh in range(HPG):
                    s_h = s0_ref[d, lg * HPG + hh]
                    z_l = jnp.zeros((DK_A, hh * DV_A), f32)
                    z_r = jnp.zeros((DK_A, (HPG - 1 - hh) * DV_A), f32)
                    parts = ([z_l] if hh > 0 else []) + [s_h] + ([z_r] if hh < HPG - 1 else [])
                    blocks.append(jnp.concatenate(parts, axis=1) if len(parts) > 1 else s_h)
                st_s[d, lg] = jnp.concatenate(blocks, axis=0)
            else:
                st_s[d, lg] = jnp.zeros((MXU_DIM, MXU_DIM), f32)
    o_ref[...] = jnp.zeros_like(o_ref)

    def seq_body(j, carry):
        units = [(d, lg) for lg in range(NLG) for d in range(2)]
        cs = [j, n - 1 - j]
        rws = [pl.ds(pl.multiple_of(cc * CHUNK, CHUNK), CHUNK) for cc in cs]
        cls = [slice(lg * MXU_DIM, (lg + 1) * MXU_DIM) for lg in range(NLG)]
        boths, vnbs = [], []
        for d, lg in units:
            lhs = jnp.concatenate([kbg_s[d, rws[d], cls[lg]], qg_s[d, rws[d], cls[lg]]], axis=0)
            boths.append(jnp.dot(lhs, st_s[d, lg].astype(bf16), preferred_element_type=f32))
        for u, (d, lg) in enumerate(units):
            resid = vb_s[d, rws[d], cls[lg]] - boths[u][:CHUNK]
            v_new = jnp.dot(ti_s[d, cs[d], :, cls[lg]], bd(resid), preferred_element_type=f32)
            vnbs.append(v_new.astype(bf16))
        for u, (d, lg) in enumerate(units):
            o = boths[u][CHUNK:] + jnp.dot(in_s[d, cs[d], :, cls[lg]], bd(vnbs[u]), preferred_element_type=f32)
            upd = lax.dot_general(kd_s[d, rws[d], cls[lg]], vnbs[u], (((0,), (0,)), ((), ())),
                                  preferred_element_type=f32)
            st_s[d, lg] = st_s[d, lg] * egt_s[d, cs[d], 0:1, cls[lg]] + jnp.where(bd_mask, upd, 0.0)
            o_ref[rws[d], cls[lg]] = o_ref[rws[d], cls[lg]] + o
        return carry

    lax.fori_loop(0, n, seq_body, 0)
    for d in range(2):
        for lg in range(NLG):
            s_fin = st_s[d, lg]
            for hh in range(HPG):
                sf_ref[d, lg * HPG + hh] = s_fin[hh * DK_A:(hh + 1) * DK_A, hh * DV_A:(hh + 1) * DV_A]


def _delta(qkv, gb, t, nseq, tok_block0, s0, s0_layer):
    n = t // CHUNK
    has_s0 = s0 is not None
    kern = functools.partial(_delta_kernel, t, has_s0)
    in_specs = [
        pl.BlockSpec((t, W_ALL), lambda b: (tok_block0 + b, 0)),
        pl.BlockSpec((t, W_ALL), lambda b: (tok_block0 + b, 1)),
        pl.BlockSpec((t, W_ALL), lambda b: (tok_block0 + b, 2)),
        pl.BlockSpec((t, LANES), lambda b: (tok_block0 + b, 0)),
    ]
    args = [qkv, qkv, qkv, gb]
    if has_s0:
        in_specs.append(pl.BlockSpec((None, None, 2, H_A, DK_A, DV_A), lambda b: (b, s0_layer, 0, 0, 0, 0)))
        args.append(s0)
    return pl.pallas_call(
        kern,
        grid=(nseq,),
        in_specs=in_specs,
        out_specs=[
            pl.BlockSpec((t, W_ALL), lambda b: (b, 0)),
            pl.BlockSpec((None, 2, H_A, DK_A, DV_A), lambda b: (b, 0, 0, 0, 0)),
        ],
        out_shape=[jax.ShapeDtypeStruct((nseq * t, W_ALL), f32),
                   jax.ShapeDtypeStruct((nseq, 2, H_A, DK_A, DV_A), f32)],
        scratch_shapes=[
            pltpu.VMEM((t, 4 * W_ALL), f32),
            pltpu.VMEM((2, t, W_ALL), bf16),
            pltpu.VMEM((2, t, W_ALL), bf16),
            pltpu.VMEM((2, t, W_ALL), bf16),
            pltpu.VMEM((2, t, W_ALL), f32),
            pltpu.VMEM((2, n, 8, W_ALL), f32),
            pltpu.VMEM((2, n, CHUNK, W_ALL), bf16),
            pltpu.VMEM((2, n, CHUNK, W_ALL), bf16),
            pltpu.VMEM((2, NLG, MXU_DIM, MXU_DIM), f32),
        ],
        compiler_params=_cparams(("arbitrary",)),
        name="delta_p" if not has_s0 else "delta_s",
    )(*args)


def _softmax_pv(scores, values, sink):
    m = sink
    for s in scores:
        m = jnp.maximum(m, jnp.max(s, axis=-1, keepdims=True))
    den = jnp.exp(sink - m)
    acc = None
    for s, v in zip(scores, values):
        p = jnp.exp(s - m)
        den = den + jnp.sum(p, axis=-1, keepdims=True)
        pv = _dot(p, v)
        acc = pv if acc is None else acc + pv
    return acc / den


def _ctx_attn_kernel(sink_ref, q_ref, k_ref, v_ref, o_ref):
    scale = HD_B ** -0.5
    q_all = q_ref[...] * scale
    k_all = k_ref[...]
    v_all = v_ref[...]
    outs = []
    for h in range(H_B):
        kv = h // G_B
        q = q_all[:, h * HD_B:(h + 1) * HD_B]
        k = k_all[:, kv * HD_B:(kv + 1) * HD_B]
        v = v_all[:, kv * HD_B:(kv + 1) * HD_B]
        s = _dot_nt(q, k)
        sink = jnp.zeros((q.shape[0], 1), f32) + sink_ref[h]
        outs.append(_softmax_pv([s], [v], sink))
    o_ref[...] = jnp.concatenate(outs, axis=1)


def _ctx_attn(sink, qb, kb, vb, t, nseq):
    nq = H_B * HD_B
    nkv = KV_B * HD_B
    return pl.pallas_call(
        _ctx_attn_kernel,
        grid=(nseq,),
        in_specs=[
            pl.BlockSpec(memory_space=pltpu.SMEM),
            pl.BlockSpec((t, nq), lambda b: (b, 0)),
            pl.BlockSpec((t, nkv), lambda b: (b, 0)),
            pl.BlockSpec((t, nkv), lambda b: (b, 0)),
        ],
        out_specs=pl.BlockSpec((t, nq), lambda b: (b, 0)),
        out_shape=jax.ShapeDtypeStruct((nseq * t, nq), f32),
        compiler_params=_cparams(("arbitrary",)),
        name="ctx_attn",
    )(sink, qb, kb, vb)


def _win_attn_kernel(t, sink_ref, q_ref, k_ref, v_ref, kc_ref, vc_ref, o_ref):
    i = pl.program_id(1)
    scale = HD_B ** -0.5
    span = QBLK + 2 * WINDOW
    start = i * QBLK
    lo = jnp.clip(start - WINDOW, 0, t - span)
    lo = pl.multiple_of(lo, QBLK)
    q_pos = start + lax.broadcasted_iota(jnp.int32, (QBLK, span), 0)
    k_pos = lo + lax.broadcasted_iota(jnp.int32, (QBLK, span), 1)
    valid = jnp.abs(q_pos - k_pos) <= WINDOW
    q_all = q_ref[...] * scale
    kw_all = k_ref[pl.ds(lo, span), :]
    vw_all = v_ref[pl.ds(lo, span), :]
    kc_all = kc_ref[...]
    vc_all = vc_ref[...]
    outs = []
    for h in range(H_B):
        kv = h // G_B
        q = q_all[:, h * HD_B:(h + 1) * HD_B]
        kw = kw_all[:, kv * HD_B:(kv + 1) * HD_B]
        vw = vw_all[:, kv * HD_B:(kv + 1) * HD_B]
        kc = kc_all[:, kv * HD_B:(kv + 1) * HD_B]
        vc = vc_all[:, kv * HD_B:(kv + 1) * HD_B]
        s_win = jnp.where(valid, _dot_nt(q, kw), NEG_INF)
        s_ctx = _dot_nt(q, kc)
        sink = jnp.zeros((QBLK, 1), f32) + sink_ref[h]
        outs.append(_softmax_pv([s_win, s_ctx], [vw, vc], sink))
    o_ref[...] = jnp.concatenate(outs, axis=1)


def _win_attn(sink, qb, kb, vb, cache_k, cache_v, cache_layer, t, nseq, tok_block0):
    nq = H_B * HD_B
    nkv = KV_B * HD_B
    nb = t // QBLK
    past = cache_k.shape[2]
    kern = functools.partial(_win_attn_kernel, t)
    return pl.pallas_call(
        kern,
        grid=(nseq, nb),
        in_specs=[
            pl.BlockSpec(memory_space=pltpu.SMEM),
            pl.BlockSpec((QBLK, nq), lambda b, i: ((tok_block0 + b) * nb + i, 0)),
            pl.BlockSpec((t, nkv), lambda b, i: (tok_block0 + b, 0)),
            pl.BlockSpec((t, nkv), lambda b, i: (tok_block0 + b, 0)),
            pl.BlockSpec((None, None, past, nkv), lambda b, i: (b, cache_layer, 0, 0)),
            pl.BlockSpec((None, None, past, nkv), lambda b, i: (b, cache_layer, 0, 0)),
        ],
        out_specs=pl.BlockSpec((QBLK, nq), lambda b, i: (b * nb + i, 0)),
        out_shape=jax.ShapeDtypeStruct((nseq * t, nq), f32),
        compiler_params=_cparams(("arbitrary", "arbitrary")),
        name="win_attn",
    )(sink, qb, kb, vb, cache_k, cache_v)


def _mix_out_kernel(np_tiles, n_y, *refs):
    y_refs = refs[:n_y]
    oap_ref, oas_ref, obp_ref, obs_ref, gate_ref, dn_ref, w_ref, gm_ref = refs[n_y:n_y + 8]
    o_refs = refs[n_y + 8:]
    i = pl.program_id(0)
    is_p = i < np_tiles
    oa = jnp.where(is_p, oap_ref[...], oas_ref[...])
    ob = jnp.where(is_p, obp_ref[...], obs_ref[...])
    oan = oa * lax.rsqrt(_group_sum64(oa * oa) * (1.0 / DV_A) + EPS) * dn_ref[...] * gate_ref[...]
    ka = H_A * DV_A
    mix = _dot(oan, w_ref[:ka, :]) + _dot(ob, w_ref[ka:, :])
    _stream_store(np_tiles, o_refs, _stream_load(np_tiles, y_refs) + gm_ref[...] * mix)


def _mix_out(ys, mods, layer, row_of_tile, np_tiles, oa_p, oa_s, ob_p, ob_s, gate, dn_row, w_out, split_out):
    nt = _stream_tiles(ys)
    ka = H_A * DV_A
    p_idx = lambda i: (jnp.minimum(i, np_tiles - 1), 0)
    s_idx = lambda i: (jnp.maximum(i - np_tiles, 0), 0)
    kern = functools.partial(_mix_out_kernel, np_tiles, len(ys))
    return pl.pallas_call(
        kern,
        grid=(nt,),
        in_specs=_stream_specs(np_tiles, 1, len(ys) == 2) + [
            pl.BlockSpec((TM, ka), p_idx),
            pl.BlockSpec((TM, ka), s_idx),
            pl.BlockSpec((TM, ka), p_idx),
            pl.BlockSpec((TM, ka), s_idx),
            pl.BlockSpec((TM, ka), lambda i: (i, 0)),
            pl.BlockSpec((1, ka), lambda i: (0, 0)),
            pl.BlockSpec((2 * ka, D_MODEL), lambda i: (0, 0)),
            _mod_spec(layer, 2, row_of_tile, 1),
        ],
        out_specs=_stream_specs(np_tiles, 1, split_out),
        out_shape=_stream_shapes(np_tiles, nt, split_out),
        compiler_params=_cparams(("arbitrary",)),
        name="mix_out",
    )(*ys, oa_p, oa_s, ob_p, ob_s, gate, dn_row, w_out, mods)


def _gelu_tanh(x):
    return 0.5 * x * (1.0 + jnp.tanh(math.sqrt(2.0 / math.pi) * (x + 0.044715 * (x * x * x))))


def _lru_kernel(np_tiles, t_p, t_s, y_ref, g_ref, sh_ref, sc_ref, wx_ref, wg_ref, cw_ref, cb_ref,
                wl_ref, bl_ref, lam_ref, h0_ref, o_ref, fin_ref, h_s, x_s, gt_s, hf_s, hb_s):
    i = pl.program_id(0)

    @pl.when(pl.program_id(1) == 0)
    def _():
        h_s[...] = _modulate(y_ref[...], g_ref[...], sh_ref[...], sc_ref[...]).astype(bf16)

    w = jnp.concatenate([wx_ref[...], wg_ref[...]], axis=1).astype(bf16)
    z = jnp.dot(h_s[...], w, preferred_element_type=f32)
    is_p = i < np_tiles
    tseq = jnp.where(is_p, t_p, t_s)
    pos = lax.broadcasted_iota(jnp.int32, (TM, 1), 0) & (tseq - 1)
    x = _seq_conv(z[:, :LANES], cw_ref, cb_ref, pos, tseq)
    x_s[...] = x
    gt_s[...] = _dot(x, wl_ref[...]) + bl_ref[...]
    decay = [RG_C * _softplus(-lam_ref[:, d * LANES:(d + 1) * LANES]) for d in range(2)]
    sub = lax.broadcasted_iota(jnp.int32, (SUBLANES, 1), 0)
    nblk = TM // SUBLANES
    seg_shift = (t_p // SUBLANES).bit_length() - 1

    def body(kf, carry):
        nxt = []
        for d, kk in ((0, kf), (1, nblk - 1 - kf)):
            r0 = pl.multiple_of(kk * SUBLANES, SUBLANES)
            rows = pl.ds(r0, SUBLANES)
            xb = x_s[rows, :]
            gb = gt_s[rows, :]
            r = _sigmoid(gb[:, (2 * d) * LANES:(2 * d + 1) * LANES])
            ig = _sigmoid(gb[:, (2 * d + 1) * LANES:(2 * d + 2) * LANES])
            log_a = -(r * decay[d])
            a = jnp.exp(log_a)
            u = jnp.sqrt(-jnp.tanh(log_a) * (a * a + 1.0)) * (ig * xb)
            s = 1
            while s < SUBLANES:
                ok = (sub >= s) if d == 0 else (sub + s < SUBLANES)
                sh = s if d == 0 else SUBLANES - s
                u = u + a * jnp.where(ok, pltpu.roll(u, sh, axis=0), 0.0)
                a = a * jnp.where(ok, pltpu.roll(a, sh, axis=0), 1.0)
                s *= 2
            edge = r0 if d == 0 else r0 + SUBLANES
            seg = jnp.where(is_p, kk >> seg_shift, 0)
            c_in = jnp.where((edge & (tseq - 1)) == 0, h0_ref[seg, d:d + 1, :], carry[d])
            hblk = u + a * c_in
            (hf_s if d == 0 else hb_s)[rows, :] = hblk
            nxt.append(hblk[SUBLANES - 1:SUBLANES, :] if d == 0 else hblk[0:1, :])
        return tuple(nxt)

    zero = jnp.zeros((1, LANES), f32)
    lax.fori_loop(0, nblk, body, (zero, zero), unroll=4)
    for sg in range(TM // t_p):
        fin_ref[sg, 0:1, :] = hf_s[sg * t_p + t_p - 1:sg * t_p + t_p, :]
        fin_ref[sg, 1:2, :] = hb_s[sg * t_p:sg * t_p + 1, :]
    o_ref[...] = (hf_s[...] + hb_s[...]) * _gelu_tanh(z[:, LANES:])


def _lru(y, mods, layer, row_of_tile, np_tiles, t_p, t_s, norm_g, w_in, cw, cb, wl, bl, lam, h0):
    ntok = y.shape[0]
    nt = ntok // TM
    ncol = D_RNN // LANES
    nseg = TM // t_p
    kern = functools.partial(_lru_kernel, np_tiles, t_p, t_s)
    return pl.pallas_call(
        kern,
        grid=(nt, ncol),
        in_specs=[
            pl.BlockSpec((TM, D_MODEL), lambda i, j: (i, 0)),
            pl.BlockSpec((1, D_MODEL), lambda i, j: (0, 0)),
            _mod_spec(layer, 0, row_of_tile, 2),
            _mod_spec(layer, 1, row_of_tile, 2),
            pl.BlockSpec((D_MODEL, LANES), lambda i, j: (0, j)),
            pl.BlockSpec((D_MODEL, LANES), lambda i, j: (0, ncol + j)),
            pl.BlockSpec((CONV_W, LANES), lambda i, j: (0, j)),
            pl.BlockSpec((1, LANES), lambda i, j: (0, j)),
            pl.BlockSpec((None, LANES, 4 * LANES), lambda i, j: (j, 0, 0)),
            pl.BlockSpec((None, 1, 4 * LANES), lambda i, j: (j, 0, 0)),
            pl.BlockSpec((None, 1, 2 * LANES), lambda i, j: (j, 0, 0)),
            pl.BlockSpec((None, nseg, 2, LANES), lambda i, j: (i, 0, 0, j)),
        ],
        out_specs=[
            pl.BlockSpec((TM, LANES), lambda i, j: (i, j)),
            pl.BlockSpec((None, nseg, 2, LANES), lambda i, j: (i, 0, 0, j)),
        ],
        out_shape=[jax.ShapeDtypeStruct((ntok, D_RNN), f32),
                   jax.ShapeDtypeStruct((nt, nseg, 2, D_RNN), f32)],
        scratch_shapes=[pltpu.VMEM((TM, D_MODEL), bf16), pltpu.VMEM((TM, LANES), f32),
                        pltpu.VMEM((TM, 4 * LANES), f32), pltpu.VMEM((TM, LANES), f32),
                        pltpu.VMEM((TM, LANES), f32)],
        compiler_params=_cparams(("arbitrary", "arbitrary")),
        name="lru",
    )(y, norm_g, mods, mods, w_in, w_in, cw, cb, wl, bl, lam, h0)


def _proj_out_kernel(y_ref, a_ref, w_ref, gm_ref, o_ref):
    o_ref[...] = y_ref[...] + gm_ref[...] * _dot(a_ref[...], w_ref[...])


def _proj_out(y, mods, layer, row_of_tile, a, w):
    ntok = y.shape[0]
    nt = ntok // TM
    k = a.shape[1]
    return pl.pallas_call(
        _proj_out_kernel,
        grid=(nt,),
        in_specs=[
            pl.BlockSpec((TM, D_MODEL), lambda i: (i, 0)),
            pl.BlockSpec((TM, k), lambda i: (i, 0)),
            pl.BlockSpec((k, D_MODEL), lambda i: (0, 0)),
            _mod_spec(layer, 2, row_of_tile, 1),
        ],
        out_specs=pl.BlockSpec((TM, D_MODEL), lambda i: (i, 0)),
        out_shape=jax.ShapeDtypeStruct((ntok, D_MODEL), f32),
        compiler_params=_cparams(("arbitrary",)),
        name="proj_out",
    )(y, a, w, mods)


FF_BLK = 1024


def _mlp_kernel(np_tiles, n_out, y_ref, g_ref, sh_ref, sc_ref, gm_ref, w1_ref, w2_ref, *refs):
    o_refs = refs[:n_out]
    h_s, acc_s = refs[n_out:]
    k = pl.program_id(1)

    @pl.when(k == 0)
    def _():
        h_s[...] = _modulate(y_ref[...], g_ref[...], sh_ref[...], sc_ref[...]).astype(bf16)
        acc_s[...] = jnp.zeros_like(acc_s)

    a = jnp.dot(h_s[...], w1_ref[...].astype(bf16), preferred_element_type=f32)
    a = jnp.maximum(a, 0.0)
    acc_s[...] += _dot(a * a, w2_ref[...])

    @pl.when(k == pl.num_programs(1) - 1)
    def _():
        _stream_store(np_tiles, o_refs, y_ref[...] + gm_ref[...] * acc_s[...])


def _mlp(y, mods, layer, row_of_tile, np_tiles, norm_g, w1, w2, split_out):
    nt = y.shape[0] // TM
    return pl.pallas_call(
        functools.partial(_mlp_kernel, np_tiles, 2 if split_out else 1),
        grid=(nt, D_FF // FF_BLK),
        in_specs=[
            pl.BlockSpec((TM, D_MODEL), lambda i, k: (i, 0)),
            pl.BlockSpec((1, D_MODEL), lambda i, k: (0, 0)),
            _mod_spec(layer, 3, row_of_tile, 2),
            _mod_spec(layer, 4, row_of_tile, 2),
            _mod_spec(layer, 5, row_of_tile, 2),
            pl.BlockSpec((None, D_MODEL, FF_BLK), lambda i, k: (layer, 0, k)),
            pl.BlockSpec((None, FF_BLK, D_MODEL), lambda i, k: (layer, k, 0)),
        ],
        out_specs=_stream_specs(np_tiles, 2, split_out),
        out_shape=_stream_shapes(np_tiles, nt, split_out),
        scratch_shapes=[pltpu.VMEM((TM, D_MODEL), bf16), pltpu.VMEM((TM, D_MODEL), f32)],
        compiler_params=_cparams(("arbitrary", "arbitrary")),
        name="mlp",
    )(y, norm_g, mods, mods, mods, w1, w2)


def _attn_cols(w_in, a_log, dt_bias):
    s2 = N_QKV + H_A * DV_A
    s4 = s2 + 4 * H_A
    pad = jnp.zeros((D_MODEL, LANES - 4 * H_A), f32)
    w_b = jnp.concatenate([w_in[:, N_QKV:s2], w_in[:, s4:], w_in[:, s2:s4], pad], axis=1)
    row_pad = jnp.zeros((LANES - 2 * H_A,), f32)
    al_row = jnp.concatenate([a_log.reshape(-1), row_pad]).reshape(1, LANES)
    dt_row = jnp.concatenate([dt_bias.reshape(-1), row_pad]).reshape(1, LANES)
    return w_b, al_row, dt_row


def _rope_tables(t):
    rows = t // GRID_W
    r = jnp.repeat(jnp.arange(rows, dtype=f32), GRID_W)
    cc = jnp.tile(jnp.arange(GRID_W, dtype=f32), rows)
    inv = ROPE_BASE ** (-jnp.arange(0, ROPE_AXIS, 2, dtype=f32) / ROPE_AXIS)
    ang_r = r[:, None] * inv
    ang_c = cc[:, None] * inv
    cos = jnp.concatenate([jnp.cos(ang_r), jnp.cos(ang_r), jnp.cos(ang_c), jnp.cos(ang_c)], axis=-1)
    sin = jnp.concatenate([-jnp.sin(ang_r), jnp.sin(ang_r), -jnp.sin(ang_c), jnp.sin(ang_c)], axis=-1)
    cos = jnp.tile(cos, (1, H_B))
    sin = jnp.tile(sin, (1, H_B))
    ident_c = jnp.ones((TM, H_B * HD_B), f32)
    ident_s = jnp.zeros((TM, H_B * HD_B), f32)
    return jnp.concatenate([ident_c, cos], axis=0), jnp.concatenate([ident_s, sin], axis=0)


def _lru_cols(w_a, b_a, w_x, b_x, lam):
    ncol = D_RNN // LANES
    per = LANES // LRU_BW

    def bd(w):
        w = w.reshape(ncol, per, LRU_BW, LRU_BW)
        eye = jnp.eye(per, dtype=f32)
        return jnp.einsum('gpij,pq->gpiqj', w, eye).reshape(ncol, LANES, LANES)

    wl = jnp.concatenate([bd(w_a[0]), bd(w_x[0]), bd(w_a[1]), bd(w_x[1])], axis=-1)
    rows = lambda v: v.reshape(ncol, 1, LANES)
    bl = jnp.concatenate([rows(b_a[0]), rows(b_x[0]), rows(b_a[1]), rows(b_x[1])], axis=-1)
    lm = jnp.concatenate([rows(lam[0]), rows(lam[1])], axis=-1)
    return wl, bl, lm


def kernel(x_prompt, x_sample, state_delta, cache_k, cache_v, state_lru, c, c_ctx, ada_w, ada_b, norm1_g, norm2_g, ff_w1, ff_w2, ab_w_in, ab_conv_w, ab_conv_b, dn_a_log, dn_dt_bias, dn_norm_g, attn_q_norm_g, attn_k_norm_g, attn_sink, ab_w_out, c_w_in, c_conv_w, c_conv_b, lru_w_a, lru_b_a, lru_w_x, lru_b_x, lru_lambda, c_w_out):
    b_p, t_p, _ = x_prompt.shape
    b_s, t_s, _ = x_sample.shape
    assert t_s == TM and TM % t_p == 0 and (b_p * t_p) % TM == 0
    np_tok = b_p * t_p
    np_tiles = np_tok // TM
    nkv = KV_B * HD_B

    ys = (x_prompt.reshape(np_tok, D_MODEL), x_sample.reshape(b_s * t_s, D_MODEL))
    nt = np_tiles + b_s * t_s // TM

    rows = -(-(b_s + 1) // 8) * 8
    cond = jnp.zeros((rows, D_MODEL), f32).at[:b_s].set(c).at[b_s].set(c_ctx)
    mods = _adaln(cond, ada_w, ada_b)
    row_of_tile = lambda i: jnp.where(i < np_tiles, b_s, i - np_tiles)

    cos_tab, sin_tab = _rope_tables(t_s)
    ck = cache_k.reshape(cache_k.shape[0], cache_k.shape[1], cache_k.shape[2], nkv)
    cv = cache_v.reshape(ck.shape)

    new_dn, new_k, new_v, new_lru = [], [], [], []
    for l in range(DEPTH):
        j = l // 2
        n1 = norm1_g[l].reshape(1, D_MODEL)
        if l % 2 == 0:
            qkv = _delta_proj(ys, mods, l, row_of_tile, np_tiles, t_p, t_s, n1, ab_w_in[j], ab_conv_w[j],
                              ab_conv_b[j].reshape(1, N_QKV))
            w_b, al_row, dt_row = _attn_cols(ab_w_in[j], dn_a_log[j], dn_dt_bias[j])
            qn_row = jnp.tile(attn_q_norm_g[j], H_B).reshape(1, H_B * HD_B)
            kn_row = jnp.tile(attn_k_norm_g[j], KV_B).reshape(1, nkv)
            gate, qb, kb, vb, gb, kc, vc = _attn_proj(ys, mods, l, row_of_tile, np_tiles, n1, w_b, qn_row, kn_row,
                                                      cos_tab, sin_tab, al_row, dt_row)
            oa_p, s_fin = _delta(qkv, gb, t_p, b_p, 0, None, 0)
            oa_s, _ = _delta(qkv, gb, t_s, b_s, np_tok // t_s, state_delta, j)
            ob_p = _ctx_attn(attn_sink[j], qb, kb, vb, t_p, b_p)
            ob_s = _win_attn(attn_sink[j], qb, kb, vb, ck, cv, j, t_s, b_s, np_tok // t_s)
            dn_row = jnp.tile(dn_norm_g[j], H_A).reshape(1, H_A * DV_A)
            (y,) = _mix_out(ys, mods, l, row_of_tile, np_tiles, oa_p, oa_s, ob_p, ob_s, gate, dn_row, ab_w_out[j],
                            False)
            new_dn.append(s_fin)
            new_k.append(kc.reshape(b_p, t_p, KV_B, HD_B))
            new_v.append(vc.reshape(b_p, t_p, KV_B, HD_B))
        else:
            wl, bl, lm = _lru_cols(lru_w_a[j], lru_b_a[j], lru_w_x[j], lru_b_x[j], lru_lambda[j])
            nseg = TM // t_p
            h0 = jnp.zeros((nt, nseg, 2, D_RNN), f32).at[np_tiles:, 0].set(state_lru[:, j])
            mixed, fin = _lru(ys[0], mods, l, row_of_tile, np_tiles, t_p, t_s, n1, c_w_in[j], c_conv_w[j],
                              c_conv_b[j].reshape(1, D_RNN), wl, bl, lm, h0)
            y = _proj_out(ys[0], mods, l, row_of_tile, mixed, c_w_out[j])
            new_lru.append(fin[:np_tiles].reshape(b_p, 2, D_RNN))
        ys = tuple(_mlp(y, mods, l, row_of_tile, np_tiles, norm2_g[l].reshape(1, D_MODEL), ff_w1, ff_w2,
                        l == DEPTH - 1))

    yp = ys[0].reshape(b_p, t_p, D_MODEL)
    ysm = ys[1].reshape(b_s, t_s, D_MODEL)
    return (yp, ysm, jnp.stack(new_dn, axis=1), jnp.stack(new_k, axis=1), jnp.stack(new_v, axis=1),
            jnp.stack(new_lru, axis=1))
```

```python
import functools
import math

import jax
import jax.numpy as jnp
from jax import lax
from jax.experimental import pallas as pl
from jax.experimental.pallas import tpu as pltpu

f32 = jnp.float32
bf16 = jnp.bfloat16

D_MODEL = 1024
DEPTH = 4
GRID_W = 64
H_A = 8
DK_A = 64
DV_A = 64
CHUNK = 64
CONV_W = 4
H_B = 8
KV_B = 2
G_B = H_B // KV_B
HD_B = 64
WINDOW = 128
QBLK = 128
ROPE_AXIS = HD_B // 2
ROPE_BASE = 10000.0
D_RNN = D_MODEL
LRU_BLOCKS = 16
LRU_BW = D_RNN // LRU_BLOCKS
RG_C = 8.0
D_FF = 4 * D_MODEL
EPS = 1e-6
NEG_INF = -1e30

TM = 1024
LANES = 128
SUBLANES = 8
MXU_DIM = 256
N_QKV = 3 * H_A * DK_A
N_ATTN = H_A * DV_A + (H_B + 2 * KV_B) * HD_B + LANES
VMEM_LIMIT = 56 * 1024 * 1024


def _cparams(sem):
    return pltpu.CompilerParams(dimension_semantics=sem, vmem_limit_bytes=VMEM_LIMIT)


def _dot(a, b):
    return jnp.dot(a.astype(bf16), b.astype(bf16), preferred_element_type=f32)


def _dot_nt(a, b):
    return lax.dot_general(a.astype(bf16), b.astype(bf16), (((1,), (1,)), ((), ())),
                           preferred_element_type=f32)


def _dot_tn(a, b):
    return lax.dot_general(a.astype(bf16), b.astype(bf16), (((0,), (0,)), ((), ())),
                           preferred_element_type=f32)


def _sigmoid(x):
    return 1.0 / (1.0 + jnp.exp(-x))


def _softplus(x):
    return jnp.maximum(x, 0.0) + jnp.log1p(jnp.exp(-jnp.abs(x)))


def _modulate(x, g, shift, scale):
    ms = jnp.mean(x * x, axis=-1, keepdims=True)
    y = x * lax.rsqrt(ms + EPS) * g
    return y * (1.0 + scale) + shift


def _split2(x):
    hi = x.astype(bf16)
    lo = (x - hi.astype(f32)).astype(bf16)
    return hi, lo


def _split3(x):
    p1 = x.astype(bf16)
    r1 = x - p1.astype(f32)
    p2 = r1.astype(bf16)
    p3 = (r1 - p2.astype(f32)).astype(bf16)
    return p1, p2, p3


def _group_sum64(v):
    w = v.shape[-1]
    r = lax.broadcasted_iota(jnp.int32, (w, w), 0) >> 6
    c = lax.broadcasted_iota(jnp.int32, (w, w), 1) >> 6
    ones_bd = jnp.where(r == c, 1.0, 0.0).astype(bf16)
    hi, lo = _split2(v)
    return (jnp.dot(hi, ones_bd, preferred_element_type=f32)
            + jnp.dot(lo, ones_bd, preferred_element_type=f32))


CONV_LEFT = CONV_W // 2


def _conv_masks(mk_s, tseq):
    n = mk_s.shape[1]
    pos = lax.broadcasted_iota(jnp.int32, (n, LANES), 0) & (tseq - 1)
    t = 0
    for j in range(CONV_W):
        o = j - CONV_LEFT
        if o != 0:
            mk_s[t] = jnp.where((pos + o >= 0) & (pos + o < tseq), 1.0, 0.0)
            t += 1


def _seq_conv(z, cw_ref, cb_ref, mk_s):
    n, c = z.shape
    acc = z * cw_ref[CONV_LEFT:CONV_LEFT + 1, :] + cb_ref[...]
    t = 0
    for j in range(CONV_W):
        o = j - CONV_LEFT
        if o != 0:
            mask = jnp.tile(mk_s[t], (1, c // LANES)) if c != LANES else mk_s[t]
            acc = acc + (pltpu.roll(z, (-o) % n, axis=0) * mask) * cw_ref[j:j + 1, :]
            t += 1
    return acc


def _adaln_kernel(c_ref, w_ref, b_ref, o_ref):
    c = c_ref[...]
    a = c * _sigmoid(c)
    o_ref[...] = _dot(a, w_ref[...]) + b_ref[...]


def _adaln(cond, ada_w, ada_b):
    rows = cond.shape[0]
    out = pl.pallas_call(
        _adaln_kernel,
        grid=(DEPTH, 6),
        in_specs=[
            pl.BlockSpec((rows, D_MODEL), lambda l, k: (0, 0)),
            pl.BlockSpec((None, D_MODEL, D_MODEL), lambda l, k: (l, 0, k)),
            pl.BlockSpec((None, None, 1, D_MODEL), lambda l, k: (l, k, 0, 0)),
        ],
        out_specs=pl.BlockSpec((None, None, rows, D_MODEL), lambda l, k: (l, k, 0, 0)),
        out_shape=jax.ShapeDtypeStruct((DEPTH, 6, rows, D_MODEL), f32),
        compiler_params=_cparams(("arbitrary", "arbitrary")),
        name="adaln",
    )(cond, ada_w, ada_b.reshape(DEPTH, 6, 1, D_MODEL))
    return out.reshape(DEPTH, 6, rows, 1, D_MODEL)


def _stream_specs(np_tiles, ngrid, split):
    def spec(block_of_tile):
        if ngrid == 1:
            return pl.BlockSpec((TM, D_MODEL), lambda i: (block_of_tile(i), 0))
        return pl.BlockSpec((TM, D_MODEL), lambda i, j: (block_of_tile(i), 0))

    if not split:
        return [spec(lambda i: i)]
    return [spec(lambda i: jnp.minimum(i, np_tiles - 1)), spec(lambda i: jnp.maximum(i - np_tiles, 0))]


def _stream_load(np_tiles, refs):
    if len(refs) == 1:
        return refs[0][...]
    return jnp.where(pl.program_id(0) < np_tiles, refs[0][...], refs[1][...])


def _stream_store(np_tiles, refs, val):
    if len(refs) == 1:
        refs[0][...] = val
        return
    i = pl.program_id(0)

    @pl.when(i < np_tiles)
    def _():
        refs[0][...] = val

    @pl.when(i >= np_tiles)
    def _():
        refs[1][...] = val


def _stream_shapes(np_tiles, nt, split):
    if not split:
        return [jax.ShapeDtypeStruct((nt * TM, D_MODEL), f32)]
    return [jax.ShapeDtypeStruct((np_tiles * TM, D_MODEL), f32),
            jax.ShapeDtypeStruct(((nt - np_tiles) * TM, D_MODEL), f32)]


def _stream_tiles(ys):
    return sum(y.shape[0] for y in ys) // TM


def _mod_spec(layer, k, row_of_tile, ngrid):
    if ngrid == 1:
        return pl.BlockSpec((None, None, None, 1, D_MODEL), lambda i: (layer, k, row_of_tile(i), 0, 0))
    return pl.BlockSpec((None, None, None, 1, D_MODEL), lambda i, j: (layer, k, row_of_tile(i), 0, 0))


QKV_BLK = H_A * DK_A


def _delta_proj_kernel(np_tiles, t_p, t_s, n_y, *refs):
    y_refs = refs[:n_y]
    g_ref, sh_ref, sc_ref, w_ref, cw_ref, cb_ref, o_ref, h_s, mk_s = refs[n_y:]
    n = pl.program_id(1)

    @pl.when(n == 0)
    def _():
        x = _stream_load(np_tiles, y_refs)
        h_s[...] = _modulate(x, g_ref[...], sh_ref[...], sc_ref[...]).astype(bf16)
        _conv_masks(mk_s, jnp.where(pl.program_id(0) < np_tiles, t_p, t_s))

    z = jnp.dot(h_s[...], w_ref[...].astype(bf16), preferred_element_type=f32)
    y = _seq_conv(z, cw_ref, cb_ref, mk_s)
    y = y * _sigmoid(y)

    @pl.when(n < 2)
    def _():
        o_ref[...] = y * lax.rsqrt(_group_sum64(y * y) + EPS)

    @pl.when(n == 2)
    def _():
        o_ref[...] = y


def _delta_proj(ys, mods, layer, row_of_tile, np_tiles, t_p, t_s, norm_g, w_in, conv_w, conv_b):
    nt = _stream_tiles(ys)
    ntok = nt * TM
    kern = functools.partial(_delta_proj_kernel, np_tiles, t_p, t_s, len(ys))
    return pl.pallas_call(
        kern,
        grid=(nt, N_QKV // QKV_BLK),
        in_specs=_stream_specs(np_tiles, 2, len(ys) == 2) + [
            pl.BlockSpec((1, D_MODEL), lambda i, n: (0, 0)),
            _mod_spec(layer, 0, row_of_tile, 2),
            _mod_spec(layer, 1, row_of_tile, 2),
            pl.BlockSpec((D_MODEL, QKV_BLK), lambda i, n: (0, n)),
            pl.BlockSpec((CONV_W, QKV_BLK), lambda i, n: (0, n)),
            pl.BlockSpec((1, QKV_BLK), lambda i, n: (0, n)),
        ],
        out_specs=pl.BlockSpec((TM, QKV_BLK), lambda i, n: (i, n)),
        out_shape=jax.ShapeDtypeStruct((ntok, N_QKV), f32),
        scratch_shapes=[pltpu.VMEM((TM, D_MODEL), bf16), pltpu.VMEM((CONV_W - 1, TM, LANES), f32)],
        compiler_params=_cparams(("arbitrary", "arbitrary")),
        name="delta_proj",
    )(*ys, norm_g, mods, mods, w_in, conv_w, conv_b)


def _rope_swap(x):
    w = x.shape[1]
    lane = lax.broadcasted_iota(jnp.int32, (1, w), 1)
    first = (lane & (ROPE_AXIS - 1)) < ROPE_AXIS // 2
    return jnp.where(first, pltpu.roll(x, w - ROPE_AXIS // 2, axis=1), pltpu.roll(x, ROPE_AXIS // 2, axis=1))


def _attn_proj_kernel(np_tiles, n_y, *refs):
    y_refs = refs[:n_y]
    (g_ref, sh_ref, sc_ref, w_ref, qn_ref, kn_ref, cos_ref, sin_ref, al_ref, dt_ref,
     gate_ref, q_ref, k_ref, v_ref, gb_ref, kc_ref, vc_ref) = refs[n_y:]
    x = _stream_load(np_tiles, y_refs)
    h = _modulate(x, g_ref[...], sh_ref[...], sc_ref[...]).astype(bf16)
    z = jnp.dot(h, w_ref[...].astype(bf16), preferred_element_type=f32)
    nq = H_B * HD_B
    nkv = KV_B * HD_B
    o1 = H_A * DV_A
    o2 = o1 + nq + 2 * nkv
    gz = z[:, :o1]
    gate_ref[...] = gz * _sigmoid(gz)
    q = z[:, o1:o1 + nq]
    k = z[:, o1 + nq:o1 + nq + nkv]
    v = z[:, o1 + nq + nkv:o2]
    v_ref[...] = v
    qn = q * lax.rsqrt(_group_sum64(q * q) * (1.0 / HD_B) + EPS) * qn_ref[...]
    kn = k * lax.rsqrt(_group_sum64(k * k) * (1.0 / HD_B) + EPS) * kn_ref[...]
    cos = cos_ref[...]
    sin = sin_ref[...]
    q_ref[...] = qn * cos + _rope_swap(qn) * sin
    k_ref[...] = kn * cos[:, :nkv] + _rope_swap(kn) * sin[:, :nkv]

    @pl.when(pl.program_id(0) < np_tiles)
    def _():
        kc_ref[...] = kn
        vc_ref[...] = v
    zs = z[:, o2:]
    lane = lax.broadcasted_iota(jnp.int32, (1, LANES), 1)
    g_val = -jnp.exp(al_ref[...]) * _softplus(zs + dt_ref[...])
    gb_ref[...] = jnp.where(lane < 2 * H_A, g_val, jnp.where(lane < 4 * H_A, _sigmoid(zs), 0.0))


def _attn_proj(ys, mods, layer, row_of_tile, np_tiles, norm_g, w_b, qn_row, kn_row, cos_tab, sin_tab,
               al_row, dt_row):
    nt = _stream_tiles(ys)
    ntok = nt * TM
    nq = H_B * HD_B
    nkv = KV_B * HD_B
    tab_idx = lambda i: (jnp.where(i < np_tiles, 0, 1), 0)
    cache_idx = lambda i: (jnp.minimum(i, np_tiles - 1), 0)
    return pl.pallas_call(
        functools.partial(_attn_proj_kernel, np_tiles, len(ys)),
        grid=(nt,),
        in_specs=_stream_specs(np_tiles, 1, len(ys) == 2) + [
            pl.BlockSpec((1, D_MODEL), lambda i: (0, 0)),
            _mod_spec(layer, 0, row_of_tile, 1),
            _mod_spec(layer, 1, row_of_tile, 1),
            pl.BlockSpec((D_MODEL, N_ATTN), lambda i: (0, 0)),
            pl.BlockSpec((1, nq), lambda i: (0, 0)),
            pl.BlockSpec((1, nkv), lambda i: (0, 0)),
            pl.BlockSpec((TM, nq), tab_idx),
            pl.BlockSpec((TM, nq), tab_idx),
            pl.BlockSpec((1, LANES), lambda i: (0, 0)),
            pl.BlockSpec((1, LANES), lambda i: (0, 0)),
        ],
        out_specs=[
            pl.BlockSpec((TM, H_A * DV_A), lambda i: (i, 0)),
            pl.BlockSpec((TM, nq), lambda i: (i, 0)),
            pl.BlockSpec((TM, nkv), lambda i: (i, 0)),
            pl.BlockSpec((TM, nkv), lambda i: (i, 0)),
            pl.BlockSpec((TM, LANES), lambda i: (i, 0)),
            pl.BlockSpec((TM, nkv), cache_idx),
            pl.BlockSpec((TM, nkv), cache_idx),
        ],
        out_shape=[jax.ShapeDtypeStruct((ntok, H_A * DV_A), f32),
                   jax.ShapeDtypeStruct((ntok, nq), f32),
                   jax.ShapeDtypeStruct((ntok, nkv), f32),
                   jax.ShapeDtypeStruct((ntok, nkv), f32),
                   jax.ShapeDtypeStruct((ntok, LANES), f32),
                   jax.ShapeDtypeStruct((np_tiles * TM, nkv), f32),
                   jax.ShapeDtypeStruct((np_tiles * TM, nkv), f32)],
        compiler_params=_cparams(("arbitrary",)),
        name="attn_proj",
    )(*ys, norm_g, mods, mods, w_b, qn_row, kn_row, cos_tab, sin_tab, al_row, dt_row)


INV_BASE = 8
HPG = MXU_DIM // DK_A
NLG = H_A // HPG
W_ALL = H_A * DK_A


def _chunk_scan(x, reverse):
    t = x.shape[0]
    r = lax.broadcasted_iota(jnp.int32, (t, 1), 0) & (CHUNK - 1)
    s = 1
    while s < CHUNK:
        if reverse:
            x = x + jnp.where(r + s < CHUNK, pltpu.roll(x, t - s, axis=0), 0.0)
        else:
            x = x + jnp.where(r >= s, pltpu.roll(x, s, axis=0), 0.0)
        s *= 2
    return x


def _delta_kernel(t, has_s0, *refs):
    if has_s0:
        q_ref, k_ref, v_ref, gb_ref, s0_ref = refs[:5]
        rest = refs[5:]
    else:
        q_ref, k_ref, v_ref, gb_ref = refs[:4]
        s0_ref = None
        rest = refs[4:]
    o_ref, sf_ref, exp_s, kbg_s, qg_s, kd_s, vb_s, egt_s, ti_s, in_s, st_s = rest
    n = t // CHUNK
    scale = DK_A ** -0.5

    gb = gb_ref[...]
    lane = lax.broadcasted_iota(jnp.int32, (1, LANES), 1)
    sc = jnp.where(lane < H_A, _chunk_scan(gb, False), jnp.where(lane < 2 * H_A, _chunk_scan(gb, True), gb))
    er = lax.broadcasted_iota(jnp.int32, (LANES, 4 * W_ALL), 0)
    ec = lax.broadcasted_iota(jnp.int32, (LANES, 4 * W_ALL), 1) >> 6
    expand = jnp.where(er == ec, 1.0, 0.0).astype(bf16)
    rb = min(t, 256)
    for r0 in range(0, t, rb):
        p1, p2, p3 = _split3(sc[r0:r0 + rb])
        exp_s[r0:r0 + rb, :] = ((jnp.dot(p1, expand, preferred_element_type=f32)
                                 + jnp.dot(p2, expand, preferred_element_type=f32))
                                + jnp.dot(p3, expand, preferred_element_type=f32))

    ri = lax.broadcasted_iota(jnp.int32, (CHUNK, MXU_DIM), 0)
    ci = lax.broadcasted_iota(jnp.int32, (CHUNK, MXU_DIM), 1) & (CHUNK - 1)
    eye_ls = ri == ci
    bd_mask = ((lax.broadcasted_iota(jnp.int32, (MXU_DIM, MXU_DIM), 0) >> 6)
               == (lax.broadcasted_iota(jnp.int32, (MXU_DIM, MXU_DIM), 1) >> 6))
    ones_c = jnp.ones((CHUNK, CHUNK), bf16)

    def bd(x):
        xb = x.astype(bf16)
        return jnp.where(bd_mask, jnp.concatenate([xb] * HPG, axis=0), jnp.zeros((), bf16))

    def blk(b):
        sh = b.bit_length() - 1
        return (ri >> sh) == (ci >> sh)

    def mm3(xh, xl, yh, yl):
        r1 = jnp.dot(jnp.concatenate([xh, xl], axis=0), bd(yh), preferred_element_type=f32)
        r2 = jnp.dot(xh, bd(yl), preferred_element_type=f32)
        return (r1[:CHUNK] + r1[CHUNK:]) + r2

    def pre_body(c, carry):
        r0 = pl.multiple_of(c * CHUNK, CHUNK)
        rows = pl.ds(r0, CHUNK)
        q = q_ref[rows, :]
        k = k_ref[rows, :]
        v = v_ref[rows, :]
        gcs, kbs = [], []
        for d in range(2):
            gc = exp_s[rows, d * W_ALL:(d + 1) * W_ALL]
            beta = exp_s[rows, (2 + d) * W_ALL:(3 + d) * W_ALL]
            gtot = gc[CHUNK - 1:CHUNK, :] if d == 0 else gc[0:1, :]
            eg = jnp.exp(gc)
            kb = k * beta
            kbg_s[d, rows, :] = (kb * eg).astype(bf16)
            qg_s[d, rows, :] = (q * (scale * eg)).astype(bf16)
            kd_s[d, rows, :] = (k * jnp.exp(gtot - gc)).astype(bf16)
            vb_s[d, rows, :] = v * beta
            egt_s[d, c] = jnp.zeros((8, W_ALL), f32) + jnp.exp(gtot)
            gcs.append(gc)
            kbs.append(kb)
        units = [(d, lg) for lg in range(NLG) for d in range(2)]
        prods = []
        for lg in range(NLG):
            cols = slice(lg * MXU_DIM, (lg + 1) * MXU_DIM)
            lhs = jnp.concatenate([kbs[0][:, cols], kbs[1][:, cols], q[:, cols] * scale], axis=0)
            prods.append(_dot_nt(lhs, bd(k[:, cols])))
        ps, ys, avs = [], [], []
        for d, lg in units:
            cols = slice(lg * MXU_DIM, (lg + 1) * MXU_DIM)
            gc = gcs[d][:, cols]
            d1, d2, d3 = _split3(jnp.where(eye_ls, gc, 0.0))
            grow = ((jnp.dot(ones_c, d1, preferred_element_type=f32)
                     + jnp.dot(ones_c, d2, preferred_element_type=f32))
                    + jnp.dot(ones_c, d3, preferred_element_type=f32))
            keep = (ri >= ci) if d == 0 else (ri <= ci)
            strict = (ri > ci) if d == 0 else (ri < ci)
            decay = jnp.where(keep, jnp.exp(jnp.where(keep, gc - grow, 0.0)), 0.0)
            a = jnp.where(strict, prods[lg][d * CHUNK:(d + 1) * CHUNK] * decay, 0.0)
            in_s[d, c, :, cols] = jnp.where(keep, prods[lg][2 * CHUNK:] * decay, 0.0).astype(bf16)
            ah, al = _split2(a)
            avs.append((ah, al))
            zb = jnp.zeros((), bf16)
            ys.append((jnp.where(blk(INV_BASE), -ah, zb), jnp.where(blk(INV_BASE), -al, zb)))
            ps.append(jnp.where(eye_ls, 1.0, 0.0) - jnp.where(blk(INV_BASE), a, 0.0))
        m = 1
        while m < INV_BASE:
            last = 2 * m >= INV_BASE
            for u in range(len(units)):
                yh, yl = ys[u]
                p = ps[u]
                byh, byl = bd(yh), bd(yl)
                if m == 1:
                    r1 = jnp.dot(jnp.concatenate([yh, yl], axis=0), byh, preferred_element_type=f32)
                    r2 = jnp.dot(yh, byl, preferred_element_type=f32)
                    y2 = (r1[:CHUNK] + r1[CHUNK:]) + r2
                else:
                    ph, plo = _split2(p)
                    if last:
                        r1 = jnp.dot(jnp.concatenate([ph, plo], axis=0), byh, preferred_element_type=f32)
                        r2 = jnp.dot(ph, byl, preferred_element_type=f32)
                        p = p + ((r1[:CHUNK] + r1[CHUNK:]) + r2)
                    else:
                        r1 = jnp.dot(jnp.concatenate([ph, plo, yh, yl], axis=0), byh,
                                     preferred_element_type=f32)
                        r2 = jnp.dot(jnp.concatenate([ph, yh], axis=0), byl, preferred_element_type=f32)
                        p = p + ((r1[:CHUNK] + r1[CHUNK:2 * CHUNK]) + r2[:CHUNK])
                        y2 = (r1[2 * CHUNK:3 * CHUNK] + r1[3 * CHUNK:]) + r2[CHUNK:]
                ps[u] = p
                if not last:
                    ys[u] = _split2(y2)
            m *= 2
        b = INV_BASE
        while b < CHUNK:
            off = blk(2 * b) & jnp.logical_not(blk(b))
            ws, pps = [], []
            zb = jnp.zeros((), bf16)
            for u in range(len(units)):
                pps.append(_split2(ps[u]))
                ws.append(mm3(jnp.where(off, avs[u][0], zb), jnp.where(off, avs[u][1], zb), *pps[u]))
            for u in range(len(units)):
                wh, wl = _split2(ws[u])
                ps[u] = ps[u] - mm3(*pps[u], wh, wl)
            b *= 2
        for u, (d, lg) in enumerate(units):
            ti_s[d, c, :, lg * MXU_DIM:(lg + 1) * MXU_DIM] = ps[u].astype(bf16)
        return carry

    lax.fori_loop(0, n, pre_body, 0)

    for d in range(2):
        for lg in range(NLG):
            if has_s0:
                blocks = []
                for hh in range(HPG):
                    s_h = s0_ref[d, lg * HPG + hh]
                    z_l = jnp.zeros((DK_A, hh * DV_A), f32)
                    z_r = jnp.zeros((DK_A, (HPG - 1 - hh) * DV_A), f32)
                    parts = ([z_l] if hh > 0 else []) + [s_h] + ([z_r] if hh < HPG - 1 else [])
                    blocks.append(jnp.concatenate(parts, axis=1) if len(parts) > 1 else s_h)
                st_s[d, lg] = jnp.concatenate(blocks, axis=0)
            else:
                st_s[d, lg] = jnp.zeros((MXU_DIM, MXU_DIM), f32)
    o_ref[...] = jnp.zeros_like(o_ref)

    def seq_body(j, carry):
        units = [(d, lg) for lg in range(NLG) for d in range(2)]
        cs = [j, n - 1 - j]
        rws = [pl.ds(pl.multiple_of(cc * CHUNK, CHUNK), CHUNK) for cc in cs]
        cls = [slice(lg * MXU_DIM, (lg + 1) * MXU_DIM) for lg in range(NLG)]
        boths, vnbs = [], []
        for d, lg in units:
            lhs = jnp.concatenate([kbg_s[d, rws[d], cls[lg]], qg_s[d, rws[d], cls[lg]]], axis=0)
            boths.append(jnp.dot(lhs, st_s[d, lg].astype(bf16), preferred_element_type=f32))
        for u, (d, lg) in enumerate(units):
            resid = vb_s[d, rws[d], cls[lg]] - boths[u][:CHUNK]
            v_new = jnp.dot(ti_s[d, cs[d], :, cls[lg]], bd(resid), preferred_element_type=f32)
            vnbs.append(v_new.astype(bf16))
        for u, (d, lg) in enumerate(units):
            o = boths[u][CHUNK:] + jnp.dot(in_s[d, cs[d], :, cls[lg]], bd(vnbs[u]), preferred_element_type=f32)
            upd = lax.dot_general(kd_s[d, rws[d], cls[lg]], vnbs[u], (((0,), (0,)), ((), ())),
                                  preferred_element_type=f32)
            st_s[d, lg] = st_s[d, lg] * egt_s[d, cs[d], 0:1, cls[lg]] + jnp.where(bd_mask, upd, 0.0)
            o_ref[rws[d], cls[lg]] = o_ref[rws[d], cls[lg]] + o
        return carry

    lax.fori_loop(0, n, seq_body, 0)
    for d in range(2):
        for lg in range(NLG):
            s_fin = st_s[d, lg]
            for hh in range(HPG):
                sf_ref[d, lg * HPG + hh] = s_fin[hh * DK_A:(hh + 1) * DK_A, hh * DV_A:(hh + 1) * DV_A]


def _delta(qkv, gb, t, nseq, tok_block0, s0, s0_layer):
    n = t // CHUNK
    has_s0 = s0 is not None
    kern = functools.partial(_delta_kernel, t, has_s0)
    in_specs = [
        pl.BlockSpec((t, W_ALL), lambda b: (tok_block0 + b, 0)),
        pl.BlockSpec((t, W_ALL), lambda b: (tok_block0 + b, 1)),
        pl.BlockSpec((t, W_ALL), lambda b: (tok_block0 + b, 2)),
        pl.BlockSpec((t, LANES), lambda b: (tok_block0 + b, 0)),
    ]
    args = [qkv, qkv, qkv, gb]
    if has_s0:
        in_specs.append(pl.BlockSpec((None, None, 2, H_A, DK_A, DV_A), lambda b: (b, s0_layer, 0, 0, 0, 0)))
        args.append(s0)
    return pl.pallas_call(
        kern,
        grid=(nseq,),
        in_specs=in_specs,
        out_specs=[
            pl.BlockSpec((t, W_ALL), lambda b: (b, 0)),
            pl.BlockSpec((None, 2, H_A, DK_A, DV_A), lambda b: (b, 0, 0, 0, 0)),
        ],
        out_shape=[jax.ShapeDtypeStruct((nseq * t, W_ALL), f32),
                   jax.ShapeDtypeStruct((nseq, 2, H_A, DK_A, DV_A), f32)],
        scratch_shapes=[
            pltpu.VMEM((t, 4 * W_ALL), f32),
            pltpu.VMEM((2, t, W_ALL), bf16),
            pltpu.VMEM((2, t, W_ALL), bf16),
            pltpu.VMEM((2, t, W_ALL), bf16),
            pltpu.VMEM((2, t, W_ALL), f32),
            pltpu.VMEM((2, n, 8, W_ALL), f32),
            pltpu.VMEM((2, n, CHUNK, W_ALL), bf16),
            pltpu.VMEM((2, n, CHUNK, W_ALL), bf16),
            pltpu.VMEM((2, NLG, MXU_DIM, MXU_DIM), f32),
        ],
        compiler_params=_cparams(("arbitrary",)),
        name="delta_p" if not has_s0 else "delta_s",
    )(*args)


def _attend(q_all, key_sets, sink_ref):
    nrow = q_all.shape[0]
    scores = []
    for h in range(H_B):
        kv = h // G_B
        q = q_all[:, h * HD_B:(h + 1) * HD_B]
        row = []
        for k_all, _, mask in key_sets:
            s = _dot_nt(q, k_all[:, kv * HD_B:(kv + 1) * HD_B])
            row.append(s if mask is None else jnp.where(mask, s, NEG_INF))
        scores.append(row)
    probs, sink_terms = [], []
    for h in range(H_B):
        sink = jnp.zeros((nrow, 1), f32) + sink_ref[h]
        m = sink
        for s in scores[h]:
            m = jnp.maximum(m, jnp.max(s, axis=-1, keepdims=True))
        probs.append([jnp.exp(s - m).astype(bf16) for s in scores[h]])
        sink_terms.append(jnp.exp(sink - m))
    v_ext = [[jnp.concatenate([v_all[:, kv * HD_B:(kv + 1) * HD_B].astype(bf16),
                               jnp.ones((v_all.shape[0], HD_B), bf16)], axis=1) for kv in range(KV_B)]
             for _, v_all, _ in key_sets]
    outs = []
    for h in range(H_B):
        acc = None
        for p, vs in zip(probs[h], v_ext):
            pv = jnp.dot(p, vs[h // G_B], preferred_element_type=f32)
            acc = pv if acc is None else acc + pv
        outs.append(acc[:, :HD_B] / (acc[:, HD_B:HD_B + 1] + sink_terms[h]))
    return jnp.concatenate(outs, axis=1)


def _ctx_attn_kernel(sink_ref, q_ref, k_ref, v_ref, o_ref):
    scale = HD_B ** -0.5
    o_ref[...] = _attend(q_ref[...] * scale, [(k_ref[...], v_ref[...], None)], sink_ref)


def _ctx_attn(sink, qb, kb, vb, t, nseq):
    nq = H_B * HD_B
    nkv = KV_B * HD_B
    return pl.pallas_call(
        _ctx_attn_kernel,
        grid=(nseq,),
        in_specs=[
            pl.BlockSpec(memory_space=pltpu.SMEM),
            pl.BlockSpec((t, nq), lambda b: (b, 0)),
            pl.BlockSpec((t, nkv), lambda b: (b, 0)),
            pl.BlockSpec((t, nkv), lambda b: (b, 0)),
        ],
        out_specs=pl.BlockSpec((t, nq), lambda b: (b, 0)),
        out_shape=jax.ShapeDtypeStruct((nseq * t, nq), f32),
        compiler_params=_cparams(("arbitrary",)),
        name="ctx_attn",
    )(sink, qb, kb, vb)


def _win_attn_kernel(t, sink_ref, q_ref, k_ref, v_ref, kc_ref, vc_ref, o_ref):
    i = pl.program_id(1)
    scale = HD_B ** -0.5
    span = QBLK + 2 * WINDOW
    start = i * QBLK
    lo = jnp.clip(start - WINDOW, 0, t - span)
    lo = pl.multiple_of(lo, QBLK)
    q_pos = start + lax.broadcasted_iota(jnp.int32, (QBLK, span), 0)
    k_pos = lo + lax.broadcasted_iota(jnp.int32, (QBLK, span), 1)
    valid = jnp.abs(q_pos - k_pos) <= WINDOW
    key_sets = [(k_ref[pl.ds(lo, span), :], v_ref[pl.ds(lo, span), :], valid), (kc_ref[...], vc_ref[...], None)]
    o_ref[...] = _attend(q_ref[...] * scale, key_sets, sink_ref)


def _win_attn(sink, qb, kb, vb, cache_k, cache_v, cache_layer, t, nseq, tok_block0):
    nq = H_B * HD_B
    nkv = KV_B * HD_B
    nb = t // QBLK
    past = cache_k.shape[2]
    kern = functools.partial(_win_attn_kernel, t)
    return pl.pallas_call(
        kern,
        grid=(nseq, nb),
        in_specs=[
            pl.BlockSpec(memory_space=pltpu.SMEM),
            pl.BlockSpec((QBLK, nq), lambda b, i: ((tok_block0 + b) * nb + i, 0)),
            pl.BlockSpec((t, nkv), lambda b, i: (tok_block0 + b, 0)),
            pl.BlockSpec((t, nkv), lambda b, i: (tok_block0 + b, 0)),
            pl.BlockSpec((None, None, past, nkv), lambda b, i: (b, cache_layer, 0, 0)),
            pl.BlockSpec((None, None, past, nkv), lambda b, i: (b, cache_layer, 0, 0)),
        ],
        out_specs=pl.BlockSpec((QBLK, nq), lambda b, i: (b * nb + i, 0)),
        out_shape=jax.ShapeDtypeStruct((nseq * t, nq), f32),
        compiler_params=_cparams(("arbitrary", "arbitrary")),
        name="win_attn",
    )(sink, qb, kb, vb, cache_k, cache_v)


def _mix_out_kernel(np_tiles, n_y, *refs):
    y_refs = refs[:n_y]
    oap_ref, oas_ref, obp_ref, obs_ref, gate_ref, dn_ref, w_ref, gm_ref = refs[n_y:n_y + 8]
    o_refs = refs[n_y + 8:]
    i = pl.program_id(0)
    is_p = i < np_tiles
    oa = jnp.where(is_p, oap_ref[...], oas_ref[...])
    ob = jnp.where(is_p, obp_ref[...], obs_ref[...])
    oan = oa * lax.rsqrt(_group_sum64(oa * oa) * (1.0 / DV_A) + EPS) * dn_ref[...] * gate_ref[...]
    ka = H_A * DV_A
    mix = _dot(oan, w_ref[:ka, :]) + _dot(ob, w_ref[ka:, :])
    _stream_store(np_tiles, o_refs, _stream_load(np_tiles, y_refs) + gm_ref[...] * mix)


def _mix_out(ys, mods, layer, row_of_tile, np_tiles, oa_p, oa_s, ob_p, ob_s, gate, dn_row, w_out, split_out):
    nt = _stream_tiles(ys)
    ka = H_A * DV_A
    p_idx = lambda i: (jnp.minimum(i, np_tiles - 1), 0)
    s_idx = lambda i: (jnp.maximum(i - np_tiles, 0), 0)
    kern = functools.partial(_mix_out_kernel, np_tiles, len(ys))
    return pl.pallas_call(
        kern,
        grid=(nt,),
        in_specs=_stream_specs(np_tiles, 1, len(ys) == 2) + [
            pl.BlockSpec((TM, ka), p_idx),
            pl.BlockSpec((TM, ka), s_idx),
            pl.BlockSpec((TM, ka), p_idx),
            pl.BlockSpec((TM, ka), s_idx),
            pl.BlockSpec((TM, ka), lambda i: (i, 0)),
            pl.BlockSpec((1, ka), lambda i: (0, 0)),
            pl.BlockSpec((2 * ka, D_MODEL), lambda i: (0, 0)),
            _mod_spec(layer, 2, row_of_tile, 1),
        ],
        out_specs=_stream_specs(np_tiles, 1, split_out),
        out_shape=_stream_shapes(np_tiles, nt, split_out),
        compiler_params=_cparams(("arbitrary",)),
        name="mix_out",
    )(*ys, oa_p, oa_s, ob_p, ob_s, gate, dn_row, w_out, mods)


def _gelu_tanh(x):
    return 0.5 * x * (1.0 + jnp.tanh(math.sqrt(2.0 / math.pi) * (x + 0.044715 * (x * x * x))))


def _lru_kernel(np_tiles, t_p, t_s, y_ref, g_ref, sh_ref, sc_ref, wx_ref, wg_ref, cw_ref, cb_ref,
                wl_ref, bl_ref, lam_ref, h0_ref, o_ref, fin_ref, h_s, x_s, gt_s, hf_s, hb_s, mk_s):
    is_p = pl.program_id(0) < np_tiles
    tseq = jnp.where(is_p, t_p, t_s)

    @pl.when(pl.program_id(1) == 0)
    def _():
        h_s[...] = _modulate(y_ref[...], g_ref[...], sh_ref[...], sc_ref[...]).astype(bf16)
        _conv_masks(mk_s, tseq)

    w = jnp.concatenate([wx_ref[...], wg_ref[...]], axis=1).astype(bf16)
    z = jnp.dot(h_s[...], w, preferred_element_type=f32)
    x = _seq_conv(z[:, :LANES], cw_ref, cb_ref, mk_s)
    x_s[...] = x
    gt_s[...] = _dot(x, wl_ref[...]) + bl_ref[...]
    decay = [RG_C * _softplus(-lam_ref[:, d * LANES:(d + 1) * LANES]) for d in range(2)]
    sub = lax.broadcasted_iota(jnp.int32, (SUBLANES, 1), 0)
    nblk = TM // SUBLANES
    seg_shift = (t_p // SUBLANES).bit_length() - 1

    def body(kf, carry):
        nxt = []
        for d, kk in ((0, kf), (1, nblk - 1 - kf)):
            r0 = pl.multiple_of(kk * SUBLANES, SUBLANES)
            rows = pl.ds(r0, SUBLANES)
            xb = x_s[rows, :]
            gb = gt_s[rows, :]
            r = _sigmoid(gb[:, (2 * d) * LANES:(2 * d + 1) * LANES])
            ig = _sigmoid(gb[:, (2 * d + 1) * LANES:(2 * d + 2) * LANES])
            log_a = -(r * decay[d])
            a = jnp.exp(log_a)
            u = jnp.sqrt(-jnp.tanh(log_a) * (a * a + 1.0)) * (ig * xb)
            s = 1
            while s < SUBLANES:
                ok = (sub >= s) if d == 0 else (sub + s < SUBLANES)
                sh = s if d == 0 else SUBLANES - s
                u = u + a * jnp.where(ok, pltpu.roll(u, sh, axis=0), 0.0)
                a = a * jnp.where(ok, pltpu.roll(a, sh, axis=0), 1.0)
                s *= 2
            edge = r0 if d == 0 else r0 + SUBLANES
            seg = jnp.where(is_p, kk >> seg_shift, 0)
            c_in = jnp.where((edge & (tseq - 1)) == 0, h0_ref[seg, d:d + 1, :], carry[d])
            hblk = u + a * c_in
            (hf_s if d == 0 else hb_s)[rows, :] = hblk
            nxt.append(hblk[SUBLANES - 1:SUBLANES, :] if d == 0 else hblk[0:1, :])
        return tuple(nxt)

    zero = jnp.zeros((1, LANES), f32)
    lax.fori_loop(0, nblk, body, (zero, zero), unroll=4)
    for sg in range(TM // t_p):
        fin_ref[sg, 0:1, :] = hf_s[sg * t_p + t_p - 1:sg * t_p + t_p, :]
        fin_ref[sg, 1:2, :] = hb_s[sg * t_p:sg * t_p + 1, :]
    o_ref[...] = (hf_s[...] + hb_s[...]) * _gelu_tanh(z[:, LANES:])


def _lru(y, mods, layer, row_of_tile, np_tiles, t_p, t_s, norm_g, w_in, cw, cb, wl, bl, lam, h0):
    ntok = y.shape[0]
    nt = ntok // TM
    ncol = D_RNN // LANES
    nseg = TM // t_p
    kern = functools.partial(_lru_kernel, np_tiles, t_p, t_s)
    return pl.pallas_call(
        kern,
        grid=(nt, ncol),
        in_specs=[
            pl.BlockSpec((TM, D_MODEL), lambda i, j: (i, 0)),
            pl.BlockSpec((1, D_MODEL), lambda i, j: (0, 0)),
            _mod_spec(layer, 0, row_of_tile, 2),
            _mod_spec(layer, 1, row_of_tile, 2),
            pl.BlockSpec((D_MODEL, LANES), lambda i, j: (0, j)),
            pl.BlockSpec((D_MODEL, LANES), lambda i, j: (0, ncol + j)),
            pl.BlockSpec((CONV_W, LANES), lambda i, j: (0, j)),
            pl.BlockSpec((1, LANES), lambda i, j: (0, j)),
            pl.BlockSpec((None, LANES, 4 * LANES), lambda i, j: (j, 0, 0)),
            pl.BlockSpec((None, 1, 4 * LANES), lambda i, j: (j, 0, 0)),
            pl.BlockSpec((None, 1, 2 * LANES), lambda i, j: (j, 0, 0)),
            pl.BlockSpec((None, nseg, 2, LANES), lambda i, j: (i, 0, 0, j)),
        ],
        out_specs=[
            pl.BlockSpec((TM, LANES), lambda i, j: (i, j)),
            pl.BlockSpec((None, nseg, 2, LANES), lambda i, j: (i, 0, 0, j)),
        ],
        out_shape=[jax.ShapeDtypeStruct((ntok, D_RNN), f32),
                   jax.ShapeDtypeStruct((nt, nseg, 2, D_RNN), f32)],
        scratch_shapes=[pltpu.VMEM((TM, D_MODEL), bf16), pltpu.VMEM((TM, LANES), f32),
                        pltpu.VMEM((TM, 4 * LANES), f32), pltpu.VMEM((TM, LANES), f32),
                        pltpu.VMEM((TM, LANES), f32), pltpu.VMEM((CONV_W - 1, TM, LANES), f32)],
        compiler_params=_cparams(("arbitrary", "arbitrary")),
        name="lru",
    )(y, norm_g, mods, mods, w_in, w_in, cw, cb, wl, bl, lam, h0)


def _proj_out_kernel(y_ref, a_ref, w_ref, gm_ref, o_ref):
    o_ref[...] = y_ref[...] + gm_ref[...] * _dot(a_ref[...], w_ref[...])


def _proj_out(y, mods, layer, row_of_tile, a, w):
    ntok = y.shape[0]
    nt = ntok // TM
    k = a.shape[1]
    return pl.pallas_call(
        _proj_out_kernel,
        grid=(nt,),
        in_specs=[
            pl.BlockSpec((TM, D_MODEL), lambda i: (i, 0)),
            pl.BlockSpec((TM, k), lambda i: (i, 0)),
            pl.BlockSpec((k, D_MODEL), lambda i: (0, 0)),
            _mod_spec(layer, 2, row_of_tile, 1),
        ],
        out_specs=pl.BlockSpec((TM, D_MODEL), lambda i: (i, 0)),
        out_shape=jax.ShapeDtypeStruct((ntok, D_MODEL), f32),
        compiler_params=_cparams(("arbitrary",)),
        name="proj_out",
    )(y, a, w, mods)


FF_BLK = 1024


def _mlp_kernel(np_tiles, n_out, y_ref, g_ref, sh_ref, sc_ref, gm_ref, w1_ref, w2_ref, *refs):
    o_refs = refs[:n_out]
    h_s, acc_s = refs[n_out:]
    k = pl.program_id(1)

    @pl.when(k == 0)
    def _():
        h_s[...] = _modulate(y_ref[...], g_ref[...], sh_ref[...], sc_ref[...]).astype(bf16)
        acc_s[...] = jnp.zeros_like(acc_s)

    a = jnp.dot(h_s[...], w1_ref[...].astype(bf16), preferred_element_type=f32)
    a = jnp.maximum(a, 0.0)
    acc_s[...] += _dot(a * a, w2_ref[...])

    @pl.when(k == pl.num_programs(1) - 1)
    def _():
        _stream_store(np_tiles, o_refs, y_ref[...] + gm_ref[...] * acc_s[...])


def _mlp(y, mods, layer, row_of_tile, np_tiles, norm_g, w1, w2, split_out):
    nt = y.shape[0] // TM
    return pl.pallas_call(
        functools.partial(_mlp_kernel, np_tiles, 2 if split_out else 1),
        grid=(nt, D_FF // FF_BLK),
        in_specs=[
            pl.BlockSpec((TM, D_MODEL), lambda i, k: (i, 0)),
            pl.BlockSpec((1, D_MODEL), lambda i, k: (0, 0)),
            _mod_spec(layer, 3, row_of_tile, 2),
            _mod_spec(layer, 4, row_of_tile, 2),
            _mod_spec(layer, 5, row_of_tile, 2),
            pl.BlockSpec((None, D_MODEL, FF_BLK), lambda i, k: (layer, 0, k)),
            pl.BlockSpec((None, FF_BLK, D_MODEL), lambda i, k: (layer, k, 0)),
        ],
        out_specs=_stream_specs(np_tiles, 2, split_out),
        out_shape=_stream_shapes(np_tiles, nt, split_out),
        scratch_shapes=[pltpu.VMEM((TM, D_MODEL), bf16), pltpu.VMEM((TM, D_MODEL), f32)],
        compiler_params=_cparams(("arbitrary", "arbitrary")),
        name="mlp",
    )(y, norm_g, mods, mods, mods, w1, w2)


def _attn_cols(w_in, a_log, dt_bias):
    s2 = N_QKV + H_A * DV_A
    s4 = s2 + 4 * H_A
    pad = jnp.zeros((D_MODEL, LANES - 4 * H_A), f32)
    w_b = jnp.concatenate([w_in[:, N_QKV:s2], w_in[:, s4:], w_in[:, s2:s4], pad], axis=1)
    row_pad = jnp.zeros((LANES - 2 * H_A,), f32)
    al_row = jnp.concatenate([a_log.reshape(-1), row_pad]).reshape(1, LANES)
    dt_row = jnp.concatenate([dt_bias.reshape(-1), row_pad]).reshape(1, LANES)
    return w_b, al_row, dt_row


def _rope_tables(t):
    rows = t // GRID_W
    r = jnp.repeat(jnp.arange(rows, dtype=f32), GRID_W)
    cc = jnp.tile(jnp.arange(GRID_W, dtype=f32), rows)
    inv = ROPE_BASE ** (-jnp.arange(0, ROPE_AXIS, 2, dtype=f32) / ROPE_AXIS)
    ang_r = r[:, None] * inv
    ang_c = cc[:, None] * inv
    cos = jnp.concatenate([jnp.cos(ang_r), jnp.cos(ang_r), jnp.cos(ang_c), jnp.cos(ang_c)], axis=-1)
    sin = jnp.concatenate([-jnp.sin(ang_r), jnp.sin(ang_r), -jnp.sin(ang_c), jnp.sin(ang_c)], axis=-1)
    cos = jnp.tile(cos, (1, H_B))
    sin = jnp.tile(sin, (1, H_B))
    ident_c = jnp.ones((TM, H_B * HD_B), f32)
    ident_s = jnp.zeros((TM, H_B * HD_B), f32)
    return jnp.concatenate([ident_c, cos], axis=0), jnp.concatenate([ident_s, sin], axis=0)


def _lru_cols(w_a, b_a, w_x, b_x, lam):
    ncol = D_RNN // LANES
    per = LANES // LRU_BW

    def bd(w):
        w = w.reshape(ncol, per, LRU_BW, LRU_BW)
        eye = jnp.eye(per, dtype=f32)
        return jnp.einsum('gpij,pq->gpiqj', w, eye).reshape(ncol, LANES, LANES)

    wl = jnp.concatenate([bd(w_a[0]), bd(w_x[0]), bd(w_a[1]), bd(w_x[1])], axis=-1)
    rows = lambda v: v.reshape(ncol, 1, LANES)
    bl = jnp.concatenate([rows(b_a[0]), rows(b_x[0]), rows(b_a[1]), rows(b_x[1])], axis=-1)
    lm = jnp.concatenate([rows(lam[0]), rows(lam[1])], axis=-1)
    return wl, bl, lm


def kernel(x_prompt, x_sample, state_delta, cache_k, cache_v, state_lru, c, c_ctx, ada_w, ada_b, norm1_g, norm2_g, ff_w1, ff_w2, ab_w_in, ab_conv_w, ab_conv_b, dn_a_log, dn_dt_bias, dn_norm_g, attn_q_norm_g, attn_k_norm_g, attn_sink, ab_w_out, c_w_in, c_conv_w, c_conv_b, lru_w_a, lru_b_a, lru_w_x, lru_b_x, lru_lambda, c_w_out):
    b_p, t_p, _ = x_prompt.shape
    b_s, t_s, _ = x_sample.shape
    assert t_s == TM and TM % t_p == 0 and (b_p * t_p) % TM == 0
    np_tok = b_p * t_p
    np_tiles = np_tok // TM
    nkv = KV_B * HD_B

    ys = (x_prompt.reshape(np_tok, D_MODEL), x_sample.reshape(b_s * t_s, D_MODEL))
    nt = np_tiles + b_s * t_s // TM

    rows = -(-(b_s + 1) // 8) * 8
    cond = jnp.zeros((rows, D_MODEL), f32).at[:b_s].set(c).at[b_s].set(c_ctx)
    mods = _adaln(cond, ada_w, ada_b)
    row_of_tile = lambda i: jnp.where(i < np_tiles, b_s, i - np_tiles)

    cos_tab, sin_tab = _rope_tables(t_s)
    ck = cache_k.reshape(cache_k.shape[0], cache_k.shape[1], cache_k.shape[2], nkv)
    cv = cache_v.reshape(ck.shape)

    new_dn, new_k, new_v, new_lru = [], [], [], []
    for l in range(DEPTH):
        j = l // 2
        n1 = norm1_g[l].reshape(1, D_MODEL)
        if l % 2 == 0:
            qkv = _delta_proj(ys, mods, l, row_of_tile, np_tiles, t_p, t_s, n1, ab_w_in[j], ab_conv_w[j],
                              ab_conv_b[j].reshape(1, N_QKV))
            w_b, al_row, dt_row = _attn_cols(ab_w_in[j], dn_a_log[j], dn_dt_bias[j])
            qn_row = jnp.tile(attn_q_norm_g[j], H_B).reshape(1, H_B * HD_B)
            kn_row = jnp.tile(attn_k_norm_g[j], KV_B).reshape(1, nkv)
            gate, qb, kb, vb, gb, kc, vc = _attn_proj(ys, mods, l, row_of_tile, np_tiles, n1, w_b, qn_row, kn_row,
                                                      cos_tab, sin_tab, al_row, dt_row)
            oa_p, s_fin = _delta(qkv, gb, t_p, b_p, 0, None, 0)
            oa_s, _ = _delta(qkv, gb, t_s, b_s, np_tok // t_s, state_delta, j)
            ob_p = _ctx_attn(attn_sink[j], qb, kb, vb, t_p, b_p)
            ob_s = _win_attn(attn_sink[j], qb, kb, vb, ck, cv, j, t_s, b_s, np_tok // t_s)
            dn_row = jnp.tile(dn_norm_g[j], H_A).reshape(1, H_A * DV_A)
            (y,) = _mix_out(ys, mods, l, row_of_tile, np_tiles, oa_p, oa_s, ob_p, ob_s, gate, dn_row, ab_w_out[j],
                            False)
            new_dn.append(s_fin)
            new_k.append(kc.reshape(b_p, t_p, KV_B, HD_B))
            new_v.append(vc.reshape(b_p, t_p, KV_B, HD_B))
        else:
            wl, bl, lm = _lru_cols(lru_w_a[j], lru_b_a[j], lru_w_x[j], lru_b_x[j], lru_lambda[j])
            nseg = TM // t_p
            h0 = jnp.zeros((nt, nseg, 2, D_RNN), f32).at[np_tiles:, 0].set(state_lru[:, j])
            mixed, fin = _lru(ys[0], mods, l, row_of_tile, np_tiles, t_p, t_s, n1, c_w_in[j], c_conv_w[j],
                              c_conv_b[j].reshape(1, D_RNN), wl, bl, lm, h0)
            y = _proj_out(ys[0], mods, l, row_of_tile, mixed, c_w_out[j])
            new_lru.append(fin[:np_tiles].reshape(b_p, 2, D_RNN))
        ys = tuple(_mlp(y, mods, l, row_of_tile, np_tiles, norm2_g[l].reshape(1, D_MODEL), ff_w1, ff_w2,
                        l == DEPTH - 1))

    yp = ys[0].reshape(b_p, t_p, D_MODEL)
    ysm = ys[1].reshape(b_s, t_s, D_MODEL)
    return (yp, ysm, jnp.stack(new_dn, axis=1), jnp.stack(new_k, axis=1), jnp.stack(new_v, axis=1),
            jnp.stack(new_lru, axis=1))
```

```python
import functools
import math

import jax
import jax.numpy as jnp
from jax import lax
from jax.experimental import pallas as pl
from jax.experimental.pallas import tpu as pltpu

f32 = jnp.float32
bf16 = jnp.bfloat16

D_MODEL = 1024
DEPTH = 4
GRID_W = 64
H_A = 8
DK_A = 64
DV_A = 64
CHUNK = 64
CONV_W = 4
H_B = 8
KV_B = 2
G_B = H_B // KV_B
HD_B = 64
WINDOW = 128
QBLK = 128
ROPE_AXIS = HD_B // 2
ROPE_BASE = 10000.0
D_RNN = D_MODEL
LRU_BLOCKS = 16
LRU_BW = D_RNN // LRU_BLOCKS
RG_C = 8.0
D_FF = 4 * D_MODEL
EPS = 1e-6
NEG_INF = -1e30

TM = 1024
LANES = 128
SUBLANES = 8
MXU_DIM = 256
N_QKV = 3 * H_A * DK_A
N_ATTN = H_A * DV_A + (H_B + 2 * KV_B) * HD_B + LANES
VMEM_LIMIT = 56 * 1024 * 1024


def _cparams(sem):
    return pltpu.CompilerParams(dimension_semantics=sem, vmem_limit_bytes=VMEM_LIMIT)


def _dot(a, b):
    return jnp.dot(a.astype(bf16), b.astype(bf16), preferred_element_type=f32)


def _dot_nt(a, b):
    return lax.dot_general(a.astype(bf16), b.astype(bf16), (((1,), (1,)), ((), ())),
                           preferred_element_type=f32)


def _dot_tn(a, b):
    return lax.dot_general(a.astype(bf16), b.astype(bf16), (((0,), (0,)), ((), ())),
                           preferred_element_type=f32)


def _sigmoid(x):
    return 1.0 / (1.0 + jnp.exp(-x))


def _softplus(x):
    return jnp.maximum(x, 0.0) + jnp.log1p(jnp.exp(-jnp.abs(x)))


def _modulate(x, g, shift, scale):
    ms = jnp.mean(x * x, axis=-1, keepdims=True)
    y = x * lax.rsqrt(ms + EPS) * g
    return y * (1.0 + scale) + shift


def _split2(x):
    hi = x.astype(bf16)
    lo = (x - hi.astype(f32)).astype(bf16)
    return hi, lo


def _split3(x):
    p1 = x.astype(bf16)
    r1 = x - p1.astype(f32)
    p2 = r1.astype(bf16)
    p3 = (r1 - p2.astype(f32)).astype(bf16)
    return p1, p2, p3


def _group_sum64(v):
    w = v.shape[-1]
    r = lax.broadcasted_iota(jnp.int32, (w, w), 0) >> 6
    c = lax.broadcasted_iota(jnp.int32, (w, w), 1) >> 6
    ones_bd = jnp.where(r == c, 1.0, 0.0).astype(bf16)
    hi, lo = _split2(v)
    return (jnp.dot(hi, ones_bd, preferred_element_type=f32)
            + jnp.dot(lo, ones_bd, preferred_element_type=f32))


CONV_LEFT = CONV_W // 2


def _conv_masks(mk_s, tseq):
    n = mk_s.shape[1]
    pos = lax.broadcasted_iota(jnp.int32, (n, LANES), 0) & (tseq - 1)
    t = 0
    for j in range(CONV_W):
        o = j - CONV_LEFT
        if o != 0:
            mk_s[t] = jnp.where((pos + o >= 0) & (pos + o < tseq), 1.0, 0.0)
            t += 1


def _seq_conv(z, cw_ref, cb_ref, mk_s):
    n, c = z.shape
    acc = z * cw_ref[CONV_LEFT:CONV_LEFT + 1, :] + cb_ref[...]
    t = 0
    for j in range(CONV_W):
        o = j - CONV_LEFT
        if o != 0:
            mask = jnp.tile(mk_s[t], (1, c // LANES)) if c != LANES else mk_s[t]
            acc = acc + (pltpu.roll(z, (-o) % n, axis=0) * mask) * cw_ref[j:j + 1, :]
            t += 1
    return acc


def _adaln_kernel(c_ref, w_ref, b_ref, o_ref):
    c = c_ref[...]
    a = c * _sigmoid(c)
    o_ref[...] = _dot(a, w_ref[...]) + b_ref[...]


def _adaln(cond, ada_w, ada_b):
    rows = cond.shape[0]
    out = pl.pallas_call(
        _adaln_kernel,
        grid=(DEPTH, 6),
        in_specs=[
            pl.BlockSpec((rows, D_MODEL), lambda l, k: (0, 0)),
            pl.BlockSpec((None, D_MODEL, D_MODEL), lambda l, k: (l, 0, k)),
            pl.BlockSpec((None, None, 1, D_MODEL), lambda l, k: (l, k, 0, 0)),
        ],
        out_specs=pl.BlockSpec((None, None, rows, D_MODEL), lambda l, k: (l, k, 0, 0)),
        out_shape=jax.ShapeDtypeStruct((DEPTH, 6, rows, D_MODEL), f32),
        compiler_params=_cparams(("arbitrary", "arbitrary")),
        name="adaln",
    )(cond, ada_w, ada_b.reshape(DEPTH, 6, 1, D_MODEL))
    return out.reshape(DEPTH, 6, rows, 1, D_MODEL)


def _stream_specs(np_tiles, ngrid, split):
    def spec(block_of_tile):
        if ngrid == 1:
            return pl.BlockSpec((TM, D_MODEL), lambda i: (block_of_tile(i), 0))
        return pl.BlockSpec((TM, D_MODEL), lambda i, j: (block_of_tile(i), 0))

    if not split:
        return [spec(lambda i: i)]
    return [spec(lambda i: jnp.minimum(i, np_tiles - 1)), spec(lambda i: jnp.maximum(i - np_tiles, 0))]


def _stream_load(np_tiles, refs):
    if len(refs) == 1:
        return refs[0][...]
    return jnp.where(pl.program_id(0) < np_tiles, refs[0][...], refs[1][...])


def _stream_store(np_tiles, refs, val):
    if len(refs) == 1:
        refs[0][...] = val
        return
    i = pl.program_id(0)

    @pl.when(i < np_tiles)
    def _():
        refs[0][...] = val

    @pl.when(i >= np_tiles)
    def _():
        refs[1][...] = val


def _stream_shapes(np_tiles, nt, split):
    if not split:
        return [jax.ShapeDtypeStruct((nt * TM, D_MODEL), f32)]
    return [jax.ShapeDtypeStruct((np_tiles * TM, D_MODEL), f32),
            jax.ShapeDtypeStruct(((nt - np_tiles) * TM, D_MODEL), f32)]


def _stream_tiles(ys):
    return sum(y.shape[0] for y in ys) // TM


def _mod_spec(layer, k, row_of_tile, ngrid):
    if ngrid == 1:
        return pl.BlockSpec((None, None, None, 1, D_MODEL), lambda i: (layer, k, row_of_tile(i), 0, 0))
    return pl.BlockSpec((None, None, None, 1, D_MODEL), lambda i, j: (layer, k, row_of_tile(i), 0, 0))


QKV_BLK = H_A * DK_A


def _delta_proj_kernel(np_tiles, t_p, t_s, n_y, *refs):
    y_refs = refs[:n_y]
    g_ref, sh_ref, sc_ref, w_ref, cw_ref, cb_ref, o_ref, h_s, mk_s = refs[n_y:]
    n = pl.program_id(1)

    @pl.when(n == 0)
    def _():
        x = _stream_load(np_tiles, y_refs)
        h_s[...] = _modulate(x, g_ref[...], sh_ref[...], sc_ref[...]).astype(bf16)
        _conv_masks(mk_s, jnp.where(pl.program_id(0) < np_tiles, t_p, t_s))

    z = jnp.dot(h_s[...], w_ref[...].astype(bf16), preferred_element_type=f32)
    y = _seq_conv(z, cw_ref, cb_ref, mk_s)
    y = y * _sigmoid(y)

    @pl.when(n < 2)
    def _():
        o_ref[...] = y * lax.rsqrt(_group_sum64(y * y) + EPS)

    @pl.when(n == 2)
    def _():
        o_ref[...] = y


def _delta_proj(ys, mods, layer, row_of_tile, np_tiles, t_p, t_s, norm_g, w_in, conv_w, conv_b):
    nt = _stream_tiles(ys)
    ntok = nt * TM
    kern = functools.partial(_delta_proj_kernel, np_tiles, t_p, t_s, len(ys))
    return pl.pallas_call(
        kern,
        grid=(nt, N_QKV // QKV_BLK),
        in_specs=_stream_specs(np_tiles, 2, len(ys) == 2) + [
            pl.BlockSpec((1, D_MODEL), lambda i, n: (0, 0)),
            _mod_spec(layer, 0, row_of_tile, 2),
            _mod_spec(layer, 1, row_of_tile, 2),
            pl.BlockSpec((D_MODEL, QKV_BLK), lambda i, n: (0, n)),
            pl.BlockSpec((CONV_W, QKV_BLK), lambda i, n: (0, n)),
            pl.BlockSpec((1, QKV_BLK), lambda i, n: (0, n)),
        ],
        out_specs=pl.BlockSpec((TM, QKV_BLK), lambda i, n: (i, n)),
        out_shape=jax.ShapeDtypeStruct((ntok, N_QKV), f32),
        scratch_shapes=[pltpu.VMEM((TM, D_MODEL), bf16), pltpu.VMEM((CONV_W - 1, TM, LANES), f32)],
        compiler_params=_cparams(("arbitrary", "arbitrary")),
        name="delta_proj",
    )(*ys, norm_g, mods, mods, w_in, conv_w, conv_b)


def _rope_swap(x):
    w = x.shape[1]
    lane = lax.broadcasted_iota(jnp.int32, (1, w), 1)
    first = (lane & (ROPE_AXIS - 1)) < ROPE_AXIS // 2
    return jnp.where(first, pltpu.roll(x, w - ROPE_AXIS // 2, axis=1), pltpu.roll(x, ROPE_AXIS // 2, axis=1))


def _attn_proj_kernel(np_tiles, n_y, *refs):
    y_refs = refs[:n_y]
    (g_ref, sh_ref, sc_ref, w_ref, qn_ref, kn_ref, cos_ref, sin_ref, al_ref, dt_ref,
     gate_ref, q_ref, k_ref, v_ref, gb_ref, kc_ref, vc_ref) = refs[n_y:]
    x = _stream_load(np_tiles, y_refs)
    h = _modulate(x, g_ref[...], sh_ref[...], sc_ref[...]).astype(bf16)
    z = jnp.dot(h, w_ref[...].astype(bf16), preferred_element_type=f32)
    nq = H_B * HD_B
    nkv = KV_B * HD_B
    o1 = H_A * DV_A
    o2 = o1 + nq + 2 * nkv
    gz = z[:, :o1]
    gate_ref[...] = gz * _sigmoid(gz)
    q = z[:, o1:o1 + nq]
    k = z[:, o1 + nq:o1 + nq + nkv]
    v = z[:, o1 + nq + nkv:o2]
    v_ref[...] = v
    qn = q * lax.rsqrt(_group_sum64(q * q) * (1.0 / HD_B) + EPS) * qn_ref[...]
    kn = k * lax.rsqrt(_group_sum64(k * k) * (1.0 / HD_B) + EPS) * kn_ref[...]
    cos = cos_ref[...]
    sin = sin_ref[...]
    q_ref[...] = qn * cos + _rope_swap(qn) * sin
    k_ref[...] = kn * cos[:, :nkv] + _rope_swap(kn) * sin[:, :nkv]

    @pl.when(pl.program_id(0) < np_tiles)
    def _():
        kc_ref[...] = kn
        vc_ref[...] = v
    zs = z[:, o2:]
    lane = lax.broadcasted_iota(jnp.int32, (1, LANES), 1)
    g_val = -jnp.exp(al_ref[...]) * _softplus(zs + dt_ref[...])
    gb_ref[...] = jnp.where(lane < 2 * H_A, g_val, jnp.where(lane < 4 * H_A, _sigmoid(zs), 0.0))


def _attn_proj(ys, mods, layer, row_of_tile, np_tiles, norm_g, w_b, qn_row, kn_row, cos_tab, sin_tab,
               al_row, dt_row):
    nt = _stream_tiles(ys)
    ntok = nt * TM
    nq = H_B * HD_B
    nkv = KV_B * HD_B
    tab_idx = lambda i: (jnp.where(i < np_tiles, 0, 1), 0)
    cache_idx = lambda i: (jnp.minimum(i, np_tiles - 1), 0)
    return pl.pallas_call(
        functools.partial(_attn_proj_kernel, np_tiles, len(ys)),
        grid=(nt,),
        in_specs=_stream_specs(np_tiles, 1, len(ys) == 2) + [
            pl.BlockSpec((1, D_MODEL), lambda i: (0, 0)),
            _mod_spec(layer, 0, row_of_tile, 1),
            _mod_spec(layer, 1, row_of_tile, 1),
            pl.BlockSpec((D_MODEL, N_ATTN), lambda i: (0, 0)),
            pl.BlockSpec((1, nq), lambda i: (0, 0)),
            pl.BlockSpec((1, nkv), lambda i: (0, 0)),
            pl.BlockSpec((TM, nq), tab_idx),
            pl.BlockSpec((TM, nq), tab_idx),
            pl.BlockSpec((1, LANES), lambda i: (0, 0)),
            pl.BlockSpec((1, LANES), lambda i: (0, 0)),
        ],
        out_specs=[
            pl.BlockSpec((TM, H_A * DV_A), lambda i: (i, 0)),
            pl.BlockSpec((TM, nq), lambda i: (i, 0)),
            pl.BlockSpec((TM, nkv), lambda i: (i, 0)),
            pl.BlockSpec((TM, nkv), lambda i: (i, 0)),
            pl.BlockSpec((TM, LANES), lambda i: (i, 0)),
            pl.BlockSpec((TM, nkv), cache_idx),
            pl.BlockSpec((TM, nkv), cache_idx),
        ],
        out_shape=[jax.ShapeDtypeStruct((ntok, H_A * DV_A), f32),
                   jax.ShapeDtypeStruct((ntok, nq), f32),
                   jax.ShapeDtypeStruct((ntok, nkv), f32),
                   jax.ShapeDtypeStruct((ntok, nkv), f32),
                   jax.ShapeDtypeStruct((ntok, LANES), f32),
                   jax.ShapeDtypeStruct((np_tiles * TM, nkv), f32),
                   jax.ShapeDtypeStruct((np_tiles * TM, nkv), f32)],
        compiler_params=_cparams(("arbitrary",)),
        name="attn_proj",
    )(*ys, norm_g, mods, mods, w_b, qn_row, kn_row, cos_tab, sin_tab, al_row, dt_row)


INV_BASE = 8
PRE_CHUNKS = 2
DELTA_SEQS_P = 2
HPG = MXU_DIM // DK_A
NLG = H_A // HPG
W_ALL = H_A * DK_A


def _chunk_scan(x, reverse):
    t = x.shape[0]
    r = lax.broadcasted_iota(jnp.int32, (t, 1), 0) & (CHUNK - 1)
    s = 1
    while s < CHUNK:
        if reverse:
            x = x + jnp.where(r + s < CHUNK, pltpu.roll(x, t - s, axis=0), 0.0)
        else:
            x = x + jnp.where(r >= s, pltpu.roll(x, s, axis=0), 0.0)
        s *= 2
    return x


def _delta_kernel(t, nsq, has_s0, *refs):
    if has_s0:
        q_ref, k_ref, v_ref, gb_ref, s0_ref = refs[:5]
        rest = refs[5:]
    else:
        q_ref, k_ref, v_ref, gb_ref = refs[:4]
        s0_ref = None
        rest = refs[4:]
    o_ref, sf_ref, exp_s, kbg_s, qg_s, kd_s, vb_s, egt_s, ti_s, in_s, st_s = rest
    n = t // CHUNK
    scale = DK_A ** -0.5

    gb = gb_ref[...]
    lane = lax.broadcasted_iota(jnp.int32, (1, LANES), 1)
    sc = jnp.where(lane < H_A, _chunk_scan(gb, False), jnp.where(lane < 2 * H_A, _chunk_scan(gb, True), gb))
    er = lax.broadcasted_iota(jnp.int32, (LANES, 4 * W_ALL), 0)
    ec = lax.broadcasted_iota(jnp.int32, (LANES, 4 * W_ALL), 1) >> 6
    expand = jnp.where(er == ec, 1.0, 0.0).astype(bf16)
    rb = min(t, 256)
    for r0 in range(0, nsq * t, rb):
        p1, p2, p3 = _split3(sc[r0:r0 + rb])
        exp_s[r0:r0 + rb, :] = ((jnp.dot(p1, expand, preferred_element_type=f32)
                                 + jnp.dot(p2, expand, preferred_element_type=f32))
                                + jnp.dot(p3, expand, preferred_element_type=f32))

    ri = lax.broadcasted_iota(jnp.int32, (CHUNK, MXU_DIM), 0)
    col_i = lax.broadcasted_iota(jnp.int32, (CHUNK, MXU_DIM), 1) & (CHUNK - 1)
    eye_ls = ri == col_i
    bd_mask = ((lax.broadcasted_iota(jnp.int32, (MXU_DIM, MXU_DIM), 0) >> 6)
               == (lax.broadcasted_iota(jnp.int32, (MXU_DIM, MXU_DIM), 1) >> 6))
    ones_c = jnp.ones((CHUNK, CHUNK), bf16)

    def bd(x):
        xb = x.astype(bf16)
        return jnp.where(bd_mask, jnp.concatenate([xb] * HPG, axis=0), jnp.zeros((), bf16))

    def blk(b):
        sh = b.bit_length() - 1
        return (ri >> sh) == (col_i >> sh)

    def mm3(xh, xl, yh, yl):
        r1 = jnp.dot(jnp.concatenate([xh, xl], axis=0), bd(yh), preferred_element_type=f32)
        r2 = jnp.dot(xh, bd(yl), preferred_element_type=f32)
        return (r1[:CHUNK] + r1[CHUNK:]) + r2

    def pre_body(cp, carry):
        units = [(ci, d, lg) for ci in range(PRE_CHUNKS) for lg in range(NLG) for d in range(2)]
        cidx = [cp * PRE_CHUNKS + ci for ci in range(PRE_CHUNKS)]
        gcs, prods = {}, {}
        for ci, c in enumerate(cidx):
            rows = pl.ds(pl.multiple_of(c * CHUNK, CHUNK), CHUNK)
            q = q_ref[rows, :]
            k = k_ref[rows, :]
            v = v_ref[rows, :]
            kbs = []
            for d in range(2):
                gc = exp_s[rows, d * W_ALL:(d + 1) * W_ALL]
                beta = exp_s[rows, (2 + d) * W_ALL:(3 + d) * W_ALL]
                gtot = gc[CHUNK - 1:CHUNK, :] if d == 0 else gc[0:1, :]
                eg = jnp.exp(gc)
                kb = k * beta
                kbg_s[d, rows, :] = (kb * eg).astype(bf16)
                qg_s[d, rows, :] = (q * (scale * eg)).astype(bf16)
                kd_s[d, rows, :] = (k * jnp.exp(gtot - gc)).astype(bf16)
                vb_s[d, rows, :] = v * beta
                egt_s[d, c] = jnp.zeros((8, W_ALL), f32) + jnp.exp(gtot)
                gcs[ci, d] = gc
                kbs.append(kb)
            for lg in range(NLG):
                cols = slice(lg * MXU_DIM, (lg + 1) * MXU_DIM)
                lhs = jnp.concatenate([kbs[0][:, cols], kbs[1][:, cols], q[:, cols] * scale], axis=0)
                prods[ci, lg] = _dot_nt(lhs, bd(k[:, cols]))
        ps, ys, avs = [], [], []
        for ci, d, lg in units:
            c = cidx[ci]
            cols = slice(lg * MXU_DIM, (lg + 1) * MXU_DIM)
            gc = gcs[ci, d][:, cols]
            prod = prods[ci, lg]
            d1, d2, d3 = _split3(jnp.where(eye_ls, gc, 0.0))
            grow = ((jnp.dot(ones_c, d1, preferred_element_type=f32)
                     + jnp.dot(ones_c, d2, preferred_element_type=f32))
                    + jnp.dot(ones_c, d3, preferred_element_type=f32))
            keep = (ri >= col_i) if d == 0 else (ri <= col_i)
            strict = (ri > col_i) if d == 0 else (ri < col_i)
            decay = jnp.where(keep, jnp.exp(jnp.where(keep, gc - grow, 0.0)), 0.0)
            a = jnp.where(strict, prod[d * CHUNK:(d + 1) * CHUNK] * decay, 0.0)
            in_s[d, c, :, cols] = jnp.where(keep, prod[2 * CHUNK:] * decay, 0.0).astype(bf16)
            ah, al = _split2(a)
            avs.append((ah, al))
            zb = jnp.zeros((), bf16)
            ys.append((jnp.where(blk(INV_BASE), -ah, zb), jnp.where(blk(INV_BASE), -al, zb)))
            ps.append(jnp.where(eye_ls, 1.0, 0.0) - jnp.where(blk(INV_BASE), a, 0.0))
        m = 1
        while m < INV_BASE:
            last = 2 * m >= INV_BASE
            for u in range(len(units)):
                yh, yl = ys[u]
                p = ps[u]
                byh, byl = bd(yh), bd(yl)
                if m == 1:
                    r1 = jnp.dot(jnp.concatenate([yh, yl], axis=0), byh, preferred_element_type=f32)
                    r2 = jnp.dot(yh, byl, preferred_element_type=f32)
                    y2 = (r1[:CHUNK] + r1[CHUNK:]) + r2
                else:
                    ph, plo = _split2(p)
                    if last:
                        r1 = jnp.dot(jnp.concatenate([ph, plo], axis=0), byh, preferred_element_type=f32)
                        r2 = jnp.dot(ph, byl, preferred_element_type=f32)
                        p = p + ((r1[:CHUNK] + r1[CHUNK:]) + r2)
                    else:
                        r1 = jnp.dot(jnp.concatenate([ph, plo, yh, yl], axis=0), byh,
                                     preferred_element_type=f32)
                        r2 = jnp.dot(jnp.concatenate([ph, yh], axis=0), byl, preferred_element_type=f32)
                        p = p + ((r1[:CHUNK] + r1[CHUNK:2 * CHUNK]) + r2[:CHUNK])
                        y2 = (r1[2 * CHUNK:3 * CHUNK] + r1[3 * CHUNK:]) + r2[CHUNK:]
                ps[u] = p
                if not last:
                    ys[u] = _split2(y2)
            m *= 2
        b = INV_BASE
        while b < CHUNK:
            off = blk(2 * b) & jnp.logical_not(blk(b))
            ws, pps = [], []
            zb = jnp.zeros((), bf16)
            for u in range(len(units)):
                pps.append(_split2(ps[u]))
                ws.append(mm3(jnp.where(off, avs[u][0], zb), jnp.where(off, avs[u][1], zb), *pps[u]))
            for u in range(len(units)):
                wh, wl = _split2(ws[u])
                ps[u] = ps[u] - mm3(*pps[u], wh, wl)
            b *= 2
        for u, (ci, d, lg) in enumerate(units):
            ti_s[d, cidx[ci], :, lg * MXU_DIM:(lg + 1) * MXU_DIM] = ps[u].astype(bf16)
        return carry

    lax.fori_loop(0, nsq * n // PRE_CHUNKS, pre_body, 0)

    for sq in range(nsq):
        for d in range(2):
            for lg in range(NLG):
                if has_s0:
                    blocks = []
                    for hh in range(HPG):
                        s_h = s0_ref[sq, d, lg * HPG + hh]
                        z_l = jnp.zeros((DK_A, hh * DV_A), f32)
                        z_r = jnp.zeros((DK_A, (HPG - 1 - hh) * DV_A), f32)
                        parts = ([z_l] if hh > 0 else []) + [s_h] + ([z_r] if hh < HPG - 1 else [])
                        blocks.append(jnp.concatenate(parts, axis=1) if len(parts) > 1 else s_h)
                    st_s[sq, d, lg] = jnp.concatenate(blocks, axis=0)
                else:
                    st_s[sq, d, lg] = jnp.zeros((MXU_DIM, MXU_DIM), f32)
    o_ref[...] = jnp.zeros_like(o_ref)

    def seq_body(j, carry):
        units = [(sq, d, lg) for sq in range(nsq) for lg in range(NLG) for d in range(2)]
        step = [j, n - 1 - j]
        cls = [slice(lg * MXU_DIM, (lg + 1) * MXU_DIM) for lg in range(NLG)]
        chunk = {(sq, d): sq * n + step[d] for sq in range(nsq) for d in range(2)}
        rws = {key: pl.ds(pl.multiple_of(cc * CHUNK, CHUNK), CHUNK) for key, cc in chunk.items()}
        boths, vnbs = [], []
        for sq, d, lg in units:
            r = rws[sq, d]
            lhs = jnp.concatenate([kbg_s[d, r, cls[lg]], qg_s[d, r, cls[lg]]], axis=0)
            boths.append(jnp.dot(lhs, st_s[sq, d, lg].astype(bf16), preferred_element_type=f32))
        for u, (sq, d, lg) in enumerate(units):
            resid = vb_s[d, rws[sq, d], cls[lg]] - boths[u][:CHUNK]
            v_new = jnp.dot(ti_s[d, chunk[sq, d], :, cls[lg]], bd(resid), preferred_element_type=f32)
            vnbs.append(v_new.astype(bf16))
        for u, (sq, d, lg) in enumerate(units):
            r = rws[sq, d]
            c = chunk[sq, d]
            o = boths[u][CHUNK:] + jnp.dot(in_s[d, c, :, cls[lg]], bd(vnbs[u]), preferred_element_type=f32)
            upd = lax.dot_general(kd_s[d, r, cls[lg]], vnbs[u], (((0,), (0,)), ((), ())),
                                  preferred_element_type=f32)
            st_s[sq, d, lg] = st_s[sq, d, lg] * egt_s[d, c, 0:1, cls[lg]] + jnp.where(bd_mask, upd, 0.0)
            o_ref[r, cls[lg]] = o_ref[r, cls[lg]] + o
        return carry

    lax.fori_loop(0, n, seq_body, 0)
    for sq in range(nsq):
        for d in range(2):
            for lg in range(NLG):
                s_fin = st_s[sq, d, lg]
                for hh in range(HPG):
                    sf_ref[sq, d, lg * HPG + hh] = s_fin[hh * DK_A:(hh + 1) * DK_A, hh * DV_A:(hh + 1) * DV_A]


def _delta(qkv, gb, t, nseq, nsq, tok_block0, s0, s0_layer):
    n = t // CHUNK
    tb = nsq * t
    nch = nsq * n
    has_s0 = s0 is not None
    kern = functools.partial(_delta_kernel, t, nsq, has_s0)
    in_specs = [
        pl.BlockSpec((tb, W_ALL), lambda b: (tok_block0 + b, 0)),
        pl.BlockSpec((tb, W_ALL), lambda b: (tok_block0 + b, 1)),
        pl.BlockSpec((tb, W_ALL), lambda b: (tok_block0 + b, 2)),
        pl.BlockSpec((tb, LANES), lambda b: (tok_block0 + b, 0)),
    ]
    args = [qkv, qkv, qkv, gb]
    if has_s0:
        in_specs.append(pl.BlockSpec((nsq, None, 2, H_A, DK_A, DV_A), lambda b: (b, s0_layer, 0, 0, 0, 0)))
        args.append(s0)
    return pl.pallas_call(
        kern,
        grid=(nseq // nsq,),
        in_specs=in_specs,
        out_specs=[
            pl.BlockSpec((tb, W_ALL), lambda b: (b, 0)),
            pl.BlockSpec((nsq, 2, H_A, DK_A, DV_A), lambda b: (b, 0, 0, 0, 0)),
        ],
        out_shape=[jax.ShapeDtypeStruct((nseq * t, W_ALL), f32),
                   jax.ShapeDtypeStruct((nseq, 2, H_A, DK_A, DV_A), f32)],
        scratch_shapes=[
            pltpu.VMEM((tb, 4 * W_ALL), f32),
            pltpu.VMEM((2, tb, W_ALL), bf16),
            pltpu.VMEM((2, tb, W_ALL), bf16),
            pltpu.VMEM((2, tb, W_ALL), bf16),
            pltpu.VMEM((2, tb, W_ALL), f32),
            pltpu.VMEM((2, nch, 8, W_ALL), f32),
            pltpu.VMEM((2, nch, CHUNK, W_ALL), bf16),
            pltpu.VMEM((2, nch, CHUNK, W_ALL), bf16),
            pltpu.VMEM((nsq, 2, NLG, MXU_DIM, MXU_DIM), f32),
        ],
        compiler_params=_cparams(("arbitrary",)),
        name="delta_p" if not has_s0 else "delta_s",
    )(*args)


def _attend(q_all, key_sets, sink_ref):
    nrow = q_all.shape[0]
    scores = []
    for h in range(H_B):
        kv = h // G_B
        q = q_all[:, h * HD_B:(h + 1) * HD_B]
        row = []
        for k_all, _, mask in key_sets:
            s = _dot_nt(q, k_all[:, kv * HD_B:(kv + 1) * HD_B])
            row.append(s if mask is None else jnp.where(mask, s, NEG_INF))
        scores.append(row)
    probs, sink_terms = [], []
    for h in range(H_B):
        sink = jnp.zeros((nrow, 1), f32) + sink_ref[h]
        m = sink
        for s in scores[h]:
            m = jnp.maximum(m, jnp.max(s, axis=-1, keepdims=True))
        probs.append([jnp.exp(s - m).astype(bf16) for s in scores[h]])
        sink_terms.append(jnp.exp(sink - m))
    v_ext = [[jnp.concatenate([v_all[:, kv * HD_B:(kv + 1) * HD_B].astype(bf16),
                               jnp.ones((v_all.shape[0], HD_B), bf16)], axis=1) for kv in range(KV_B)]
             for _, v_all, _ in key_sets]
    outs = []
    for h in range(H_B):
        acc = None
        for p, vs in zip(probs[h], v_ext):
            pv = jnp.dot(p, vs[h // G_B], preferred_element_type=f32)
            acc = pv if acc is None else acc + pv
        outs.append(acc[:, :HD_B] / (acc[:, HD_B:HD_B + 1] + sink_terms[h]))
    return jnp.concatenate(outs, axis=1)


def _ctx_attn_kernel(sink_ref, q_ref, k_ref, v_ref, o_ref):
    scale = HD_B ** -0.5
    o_ref[...] = _attend(q_ref[...] * scale, [(k_ref[...], v_ref[...], None)], sink_ref)


def _ctx_attn(sink, qb, kb, vb, t, nseq):
    nq = H_B * HD_B
    nkv = KV_B * HD_B
    return pl.pallas_call(
        _ctx_attn_kernel,
        grid=(nseq,),
        in_specs=[
            pl.BlockSpec(memory_space=pltpu.SMEM),
            pl.BlockSpec((t, nq), lambda b: (b, 0)),
            pl.BlockSpec((t, nkv), lambda b: (b, 0)),
            pl.BlockSpec((t, nkv), lambda b: (b, 0)),
        ],
        out_specs=pl.BlockSpec((t, nq), lambda b: (b, 0)),
        out_shape=jax.ShapeDtypeStruct((nseq * t, nq), f32),
        compiler_params=_cparams(("arbitrary",)),
        name="ctx_attn",
    )(sink, qb, kb, vb)


def _win_attn_kernel(t, sink_ref, q_ref, k_ref, v_ref, kc_ref, vc_ref, o_ref):
    i = pl.program_id(1)
    scale = HD_B ** -0.5
    span = QBLK + 2 * WINDOW
    start = i * QBLK
    lo = jnp.clip(start - WINDOW, 0, t - span)
    lo = pl.multiple_of(lo, QBLK)
    q_pos = start + lax.broadcasted_iota(jnp.int32, (QBLK, span), 0)
    k_pos = lo + lax.broadcasted_iota(jnp.int32, (QBLK, span), 1)
    valid = jnp.abs(q_pos - k_pos) <= WINDOW
    key_sets = [(k_ref[pl.ds(lo, span), :], v_ref[pl.ds(lo, span), :], valid), (kc_ref[...], vc_ref[...], None)]
    o_ref[...] = _attend(q_ref[...] * scale, key_sets, sink_ref)


def _win_attn(sink, qb, kb, vb, cache_k, cache_v, cache_layer, t, nseq, tok_block0):
    nq = H_B * HD_B
    nkv = KV_B * HD_B
    nb = t // QBLK
    past = cache_k.shape[2]
    kern = functools.partial(_win_attn_kernel, t)
    return pl.pallas_call(
        kern,
        grid=(nseq, nb),
        in_specs=[
            pl.BlockSpec(memory_space=pltpu.SMEM),
            pl.BlockSpec((QBLK, nq), lambda b, i: ((tok_block0 + b) * nb + i, 0)),
            pl.BlockSpec((t, nkv), lambda b, i: (tok_block0 + b, 0)),
            pl.BlockSpec((t, nkv), lambda b, i: (tok_block0 + b, 0)),
            pl.BlockSpec((None, None, past, nkv), lambda b, i: (b, cache_layer, 0, 0)),
            pl.BlockSpec((None, None, past, nkv), lambda b, i: (b, cache_layer, 0, 0)),
        ],
        out_specs=pl.BlockSpec((QBLK, nq), lambda b, i: (b * nb + i, 0)),
        out_shape=jax.ShapeDtypeStruct((nseq * t, nq), f32),
        compiler_params=_cparams(("arbitrary", "arbitrary")),
        name="win_attn",
    )(sink, qb, kb, vb, cache_k, cache_v)


def _mix_out_kernel(np_tiles, n_y, *refs):
    y_refs = refs[:n_y]
    oap_ref, oas_ref, obp_ref, obs_ref, gate_ref, dn_ref, w_ref, gm_ref = refs[n_y:n_y + 8]
    o_refs = refs[n_y + 8:]
    i = pl.program_id(0)
    is_p = i < np_tiles
    oa = jnp.where(is_p, oap_ref[...], oas_ref[...])
    ob = jnp.where(is_p, obp_ref[...], obs_ref[...])
    oan = oa * lax.rsqrt(_group_sum64(oa * oa) * (1.0 / DV_A) + EPS) * dn_ref[...] * gate_ref[...]
    ka = H_A * DV_A
    mix = _dot(oan, w_ref[:ka, :]) + _dot(ob, w_ref[ka:, :])
    _stream_store(np_tiles, o_refs, _stream_load(np_tiles, y_refs) + gm_ref[...] * mix)


def _mix_out(ys, mods, layer, row_of_tile, np_tiles, oa_p, oa_s, ob_p, ob_s, gate, dn_row, w_out, split_out):
    nt = _stream_tiles(ys)
    ka = H_A * DV_A
    p_idx = lambda i: (jnp.minimum(i, np_tiles - 1), 0)
    s_idx = lambda i: (jnp.maximum(i - np_tiles, 0), 0)
    kern = functools.partial(_mix_out_kernel, np_tiles, len(ys))
    return pl.pallas_call(
        kern,
        grid=(nt,),
        in_specs=_stream_specs(np_tiles, 1, len(ys) == 2) + [
            pl.BlockSpec((TM, ka), p_idx),
            pl.BlockSpec((TM, ka), s_idx),
            pl.BlockSpec((TM, ka), p_idx),
            pl.BlockSpec((TM, ka), s_idx),
            pl.BlockSpec((TM, ka), lambda i: (i, 0)),
            pl.BlockSpec((1, ka), lambda i: (0, 0)),
            pl.BlockSpec((2 * ka, D_MODEL), lambda i: (0, 0)),
            _mod_spec(layer, 2, row_of_tile, 1),
        ],
        out_specs=_stream_specs(np_tiles, 1, split_out),
        out_shape=_stream_shapes(np_tiles, nt, split_out),
        compiler_params=_cparams(("arbitrary",)),
        name="mix_out",
    )(*ys, oa_p, oa_s, ob_p, ob_s, gate, dn_row, w_out, mods)


def _gelu_tanh(x):
    return 0.5 * x * (1.0 + jnp.tanh(math.sqrt(2.0 / math.pi) * (x + 0.044715 * (x * x * x))))


def _lru_kernel(np_tiles, t_p, t_s, y_ref, g_ref, sh_ref, sc_ref, wx_ref, wg_ref, cw_ref, cb_ref,
                wl_ref, bl_ref, lam_ref, h0_ref, o_ref, fin_ref, h_s, x_s, gt_s, hf_s, hb_s, mk_s):
    is_p = pl.program_id(0) < np_tiles
    tseq = jnp.where(is_p, t_p, t_s)

    @pl.when(pl.program_id(1) == 0)
    def _():
        h_s[...] = _modulate(y_ref[...], g_ref[...], sh_ref[...], sc_ref[...]).astype(bf16)
        _conv_masks(mk_s, tseq)

    w = jnp.concatenate([wx_ref[...], wg_ref[...]], axis=1).astype(bf16)
    z = jnp.dot(h_s[...], w, preferred_element_type=f32)
    x = _seq_conv(z[:, :LANES], cw_ref, cb_ref, mk_s)
    x_s[...] = x
    gt_s[...] = _dot(x, wl_ref[...]) + bl_ref[...]
    decay = [RG_C * _softplus(-lam_ref[:, d * LANES:(d + 1) * LANES]) for d in range(2)]
    sub = lax.broadcasted_iota(jnp.int32, (SUBLANES, 1), 0)
    nblk = TM // SUBLANES
    seg_shift = (t_p // SUBLANES).bit_length() - 1

    def body(kf, carry):
        nxt = []
        for d, kk in ((0, kf), (1, nblk - 1 - kf)):
            r0 = pl.multiple_of(kk * SUBLANES, SUBLANES)
            rows = pl.ds(r0, SUBLANES)
            xb = x_s[rows, :]
            gb = gt_s[rows, :]
            r = _sigmoid(gb[:, (2 * d) * LANES:(2 * d + 1) * LANES])
            ig = _sigmoid(gb[:, (2 * d + 1) * LANES:(2 * d + 2) * LANES])
            log_a = -(r * decay[d])
            a = jnp.exp(log_a)
            u = jnp.sqrt(-jnp.tanh(log_a) * (a * a + 1.0)) * (ig * xb)
            s = 1
            while s < SUBLANES:
                ok = (sub >= s) if d == 0 else (sub + s < SUBLANES)
                sh = s if d == 0 else SUBLANES - s
                u = u + a * jnp.where(ok, pltpu.roll(u, sh, axis=0), 0.0)
                a = a * jnp.where(ok, pltpu.roll(a, sh, axis=0), 1.0)
                s *= 2
            edge = r0 if d == 0 else r0 + SUBLANES
            seg = jnp.where(is_p, kk >> seg_shift, 0)
            c_in = jnp.where((edge & (tseq - 1)) == 0, h0_ref[seg, d:d + 1, :], carry[d])
            hblk = u + a * c_in
            (hf_s if d == 0 else hb_s)[rows, :] = hblk
            nxt.append(hblk[SUBLANES - 1:SUBLANES, :] if d == 0 else hblk[0:1, :])
        return tuple(nxt)

    zero = jnp.zeros((1, LANES), f32)
    lax.fori_loop(0, nblk, body, (zero, zero), unroll=4)
    for sg in range(TM // t_p):
        fin_ref[sg, 0:1, :] = hf_s[sg * t_p + t_p - 1:sg * t_p + t_p, :]
        fin_ref[sg, 1:2, :] = hb_s[sg * t_p:sg * t_p + 1, :]
    o_ref[...] = (hf_s[...] + hb_s[...]) * _gelu_tanh(z[:, LANES:])


def _lru(y, mods, layer, row_of_tile, np_tiles, t_p, t_s, norm_g, w_in, cw, cb, wl, bl, lam, h0):
    ntok = y.shape[0]
    nt = ntok // TM
    ncol = D_RNN // LANES
    nseg = TM // t_p
    kern = functools.partial(_lru_kernel, np_tiles, t_p, t_s)
    return pl.pallas_call(
        kern,
        grid=(nt, ncol),
        in_specs=[
            pl.BlockSpec((TM, D_MODEL), lambda i, j: (i, 0)),
            pl.BlockSpec((1, D_MODEL), lambda i, j: (0, 0)),
            _mod_spec(layer, 0, row_of_tile, 2),
            _mod_spec(layer, 1, row_of_tile, 2),
            pl.BlockSpec((D_MODEL, LANES), lambda i, j: (0, j)),
            pl.BlockSpec((D_MODEL, LANES), lambda i, j: (0, ncol + j)),
            pl.BlockSpec((CONV_W, LANES), lambda i, j: (0, j)),
            pl.BlockSpec((1, LANES), lambda i, j: (0, j)),
            pl.BlockSpec((None, LANES, 4 * LANES), lambda i, j: (j, 0, 0)),
            pl.BlockSpec((None, 1, 4 * LANES), lambda i, j: (j, 0, 0)),
            pl.BlockSpec((None, 1, 2 * LANES), lambda i, j: (j, 0, 0)),
            pl.BlockSpec((None, nseg, 2, LANES), lambda i, j: (i, 0, 0, j)),
        ],
        out_specs=[
            pl.BlockSpec((TM, LANES), lambda i, j: (i, j)),
            pl.BlockSpec((None, nseg, 2, LANES), lambda i, j: (i, 0, 0, j)),
        ],
        out_shape=[jax.ShapeDtypeStruct((ntok, D_RNN), f32),
                   jax.ShapeDtypeStruct((nt, nseg, 2, D_RNN), f32)],
        scratch_shapes=[pltpu.VMEM((TM, D_MODEL), bf16), pltpu.VMEM((TM, LANES), f32),
                        pltpu.VMEM((TM, 4 * LANES), f32), pltpu.VMEM((TM, LANES), f32),
                        pltpu.VMEM((TM, LANES), f32), pltpu.VMEM((CONV_W - 1, TM, LANES), f32)],
        compiler_params=_cparams(("arbitrary", "arbitrary")),
        name="lru",
    )(y, norm_g, mods, mods, w_in, w_in, cw, cb, wl, bl, lam, h0)


def _proj_out_kernel(y_ref, a_ref, w_ref, gm_ref, o_ref):
    o_ref[...] = y_ref[...] + gm_ref[...] * _dot(a_ref[...], w_ref[...])


def _proj_out(y, mods, layer, row_of_tile, a, w):
    ntok = y.shape[0]
    nt = ntok // TM
    k = a.shape[1]
    return pl.pallas_call(
        _proj_out_kernel,
        grid=(nt,),
        in_specs=[
            pl.BlockSpec((TM, D_MODEL), lambda i: (i, 0)),
            pl.BlockSpec((TM, k), lambda i: (i, 0)),
            pl.BlockSpec((k, D_MODEL), lambda i: (0, 0)),
            _mod_spec(layer, 2, row_of_tile, 1),
        ],
        out_specs=pl.BlockSpec((TM, D_MODEL), lambda i: (i, 0)),
        out_shape=jax.ShapeDtypeStruct((ntok, D_MODEL), f32),
        compiler_params=_cparams(("arbitrary",)),
        name="proj_out",
    )(y, a, w, mods)


FF_BLK = 1024


def _mlp_kernel(np_tiles, n_out, y_ref, g_ref, sh_ref, sc_ref, gm_ref, w1_ref, w2_ref, *refs):
    o_refs = refs[:n_out]
    h_s, acc_s = refs[n_out:]
    k = pl.program_id(1)

    @pl.when(k == 0)
    def _():
        h_s[...] = _modulate(y_ref[...], g_ref[...], sh_ref[...], sc_ref[...]).astype(bf16)
        acc_s[...] = jnp.zeros_like(acc_s)

    a = jnp.dot(h_s[...], w1_ref[...].astype(bf16), preferred_element_type=f32)
    a = jnp.maximum(a, 0.0)
    acc_s[...] += _dot(a * a, w2_ref[...])

    @pl.when(k == pl.num_programs(1) - 1)
    def _():
        _stream_store(np_tiles, o_refs, y_ref[...] + gm_ref[...] * acc_s[...])


def _mlp(y, mods, layer, row_of_tile, np_tiles, norm_g, w1, w2, split_out):
    nt = y.shape[0] // TM
    return pl.pallas_call(
        functools.partial(_mlp_kernel, np_tiles, 2 if split_out else 1),
        grid=(nt, D_FF // FF_BLK),
        in_specs=[
            pl.BlockSpec((TM, D_MODEL), lambda i, k: (i, 0)),
            pl.BlockSpec((1, D_MODEL), lambda i, k: (0, 0)),
            _mod_spec(layer, 3, row_of_tile, 2),
            _mod_spec(layer, 4, row_of_tile, 2),
            _mod_spec(layer, 5, row_of_tile, 2),
            pl.BlockSpec((None, D_MODEL, FF_BLK), lambda i, k: (layer, 0, k)),
            pl.BlockSpec((None, FF_BLK, D_MODEL), lambda i, k: (layer, k, 0)),
        ],
        out_specs=_stream_specs(np_tiles, 2, split_out),
        out_shape=_stream_shapes(np_tiles, nt, split_out),
        scratch_shapes=[pltpu.VMEM((TM, D_MODEL), bf16), pltpu.VMEM((TM, D_MODEL), f32)],
        compiler_params=_cparams(("arbitrary", "arbitrary")),
        name="mlp",
    )(y, norm_g, mods, mods, mods, w1, w2)


def _attn_cols(w_in, a_log, dt_bias):
    s2 = N_QKV + H_A * DV_A
    s4 = s2 + 4 * H_A
    pad = jnp.zeros((D_MODEL, LANES - 4 * H_A), f32)
    w_b = jnp.concatenate([w_in[:, N_QKV:s2], w_in[:, s4:], w_in[:, s2:s4], pad], axis=1)
    row_pad = jnp.zeros((LANES - 2 * H_A,), f32)
    al_row = jnp.concatenate([a_log.reshape(-1), row_pad]).reshape(1, LANES)
    dt_row = jnp.concatenate([dt_bias.reshape(-1), row_pad]).reshape(1, LANES)
    return w_b, al_row, dt_row


def _rope_tables(t):
    rows = t // GRID_W
    r = jnp.repeat(jnp.arange(rows, dtype=f32), GRID_W)
    cc = jnp.tile(jnp.arange(GRID_W, dtype=f32), rows)
    inv = ROPE_BASE ** (-jnp.arange(0, ROPE_AXIS, 2, dtype=f32) / ROPE_AXIS)
    ang_r = r[:, None] * inv
    ang_c = cc[:, None] * inv
    cos = jnp.concatenate([jnp.cos(ang_r), jnp.cos(ang_r), jnp.cos(ang_c), jnp.cos(ang_c)], axis=-1)
    sin = jnp.concatenate([-jnp.sin(ang_r), jnp.sin(ang_r), -jnp.sin(ang_c), jnp.sin(ang_c)], axis=-1)
    cos = jnp.tile(cos, (1, H_B))
    sin = jnp.tile(sin, (1, H_B))
    ident_c = jnp.ones((TM, H_B * HD_B), f32)
    ident_s = jnp.zeros((TM, H_B * HD_B), f32)
    return jnp.concatenate([ident_c, cos], axis=0), jnp.concatenate([ident_s, sin], axis=0)


def _lru_cols(w_a, b_a, w_x, b_x, lam):
    ncol = D_RNN // LANES
    per = LANES // LRU_BW

    def bd(w):
        w = w.reshape(ncol, per, LRU_BW, LRU_BW)
        eye = jnp.eye(per, dtype=f32)
        return jnp.einsum('gpij,pq->gpiqj', w, eye).reshape(ncol, LANES, LANES)

    wl = jnp.concatenate([bd(w_a[0]), bd(w_x[0]), bd(w_a[1]), bd(w_x[1])], axis=-1)
    rows = lambda v: v.reshape(ncol, 1, LANES)
    bl = jnp.concatenate([rows(b_a[0]), rows(b_x[0]), rows(b_a[1]), rows(b_x[1])], axis=-1)
    lm = jnp.concatenate([rows(lam[0]), rows(lam[1])], axis=-1)
    return wl, bl, lm


def kernel(x_prompt, x_sample, state_delta, cache_k, cache_v, state_lru, c, c_ctx, ada_w, ada_b, norm1_g, norm2_g, ff_w1, ff_w2, ab_w_in, ab_conv_w, ab_conv_b, dn_a_log, dn_dt_bias, dn_norm_g, attn_q_norm_g, attn_k_norm_g, attn_sink, ab_w_out, c_w_in, c_conv_w, c_conv_b, lru_w_a, lru_b_a, lru_w_x, lru_b_x, lru_lambda, c_w_out):
    b_p, t_p, _ = x_prompt.shape
    b_s, t_s, _ = x_sample.shape
    assert t_s == TM and TM % t_p == 0 and (b_p * t_p) % TM == 0
    np_tok = b_p * t_p
    np_tiles = np_tok // TM
    nkv = KV_B * HD_B

    ys = (x_prompt.reshape(np_tok, D_MODEL), x_sample.reshape(b_s * t_s, D_MODEL))
    nt = np_tiles + b_s * t_s // TM

    rows = -(-(b_s + 1) // 8) * 8
    cond = jnp.zeros((rows, D_MODEL), f32).at[:b_s].set(c).at[b_s].set(c_ctx)
    mods = _adaln(cond, ada_w, ada_b)
    row_of_tile = lambda i: jnp.where(i < np_tiles, b_s, i - np_tiles)

    cos_tab, sin_tab = _rope_tables(t_s)
    ck = cache_k.reshape(cache_k.shape[0], cache_k.shape[1], cache_k.shape[2], nkv)
    cv = cache_v.reshape(ck.shape)

    new_dn, new_k, new_v, new_lru = [], [], [], []
    for l in range(DEPTH):
        j = l // 2
        n1 = norm1_g[l].reshape(1, D_MODEL)
        if l % 2 == 0:
            qkv = _delta_proj(ys, mods, l, row_of_tile, np_tiles, t_p, t_s, n1, ab_w_in[j], ab_conv_w[j],
                              ab_conv_b[j].reshape(1, N_QKV))
            w_b, al_row, dt_row = _attn_cols(ab_w_in[j], dn_a_log[j], dn_dt_bias[j])
            qn_row = jnp.tile(attn_q_norm_g[j], H_B).reshape(1, H_B * HD_B)
            kn_row = jnp.tile(attn_k_norm_g[j], KV_B).reshape(1, nkv)
            gate, qb, kb, vb, gb, kc, vc = _attn_proj(ys, mods, l, row_of_tile, np_tiles, n1, w_b, qn_row, kn_row,
                                                      cos_tab, sin_tab, al_row, dt_row)
            oa_p, s_fin = _delta(qkv, gb, t_p, b_p, DELTA_SEQS_P, 0, None, 0)
            oa_s, _ = _delta(qkv, gb, t_s, b_s, 1, np_tok // t_s, state_delta, j)
            ob_p = _ctx_attn(attn_sink[j], qb, kb, vb, t_p, b_p)
            ob_s = _win_attn(attn_sink[j], qb, kb, vb, ck, cv, j, t_s, b_s, np_tok // t_s)
            dn_row = jnp.tile(dn_norm_g[j], H_A).reshape(1, H_A * DV_A)
            (y,) = _mix_out(ys, mods, l, row_of_tile, np_tiles, oa_p, oa_s, ob_p, ob_s, gate, dn_row, ab_w_out[j],
                            False)
            new_dn.append(s_fin)
            new_k.append(kc.reshape(b_p, t_p, KV_B, HD_B))
            new_v.append(vc.reshape(b_p, t_p, KV_B, HD_B))
        else:
            wl, bl, lm = _lru_cols(lru_w_a[j], lru_b_a[j], lru_w_x[j], lru_b_x[j], lru_lambda[j])
            nseg = TM // t_p
            h0 = jnp.zeros((nt, nseg, 2, D_RNN), f32).at[np_tiles:, 0].set(state_lru[:, j])
            mixed, fin = _lru(ys[0], mods, l, row_of_tile, np_tiles, t_p, t_s, n1, c_w_in[j], c_conv_w[j],
                              c_conv_b[j].reshape(1, D_RNN), wl, bl, lm, h0)
            y = _proj_out(ys[0], mods, l, row_of_tile, mixed, c_w_out[j])
            new_lru.append(fin[:np_tiles].reshape(b_p, 2, D_RNN))
        ys = tuple(_mlp(y, mods, l, row_of_tile, np_tiles, norm2_g[l].reshape(1, D_MODEL), ff_w1, ff_w2,
                        l == DEPTH - 1))

    yp = ys[0].reshape(b_p, t_p, D_MODEL)
    ysm = ys[1].reshape(b_s, t_s, D_MODEL)
    return (yp, ysm, jnp.stack(new_dn, axis=1), jnp.stack(new_k, axis=1), jnp.stack(new_v, axis=1),
            jnp.stack(new_lru, axis=1))
```

```python
import functools
import math

import jax
import jax.numpy as jnp
import numpy as np
from jax import lax
from jax.experimental import pallas as pl
from jax.experimental.pallas import tpu as pltpu

f32 = jnp.float32
bf16 = jnp.bfloat16

D_MODEL = 1024
DEPTH = 4
GRID_W = 64
H_A = 8
DK_A = 64
DV_A = 64
CHUNK = 64
CONV_W = 4
H_B = 8
KV_B = 2
G_B = H_B // KV_B
HD_B = 64
WINDOW = 128
QBLK = 128
ROPE_AXIS = HD_B // 2
ROPE_BASE = 10000.0
D_RNN = D_MODEL
LRU_BLOCKS = 16
LRU_BW = D_RNN // LRU_BLOCKS
RG_C = 8.0
D_FF = 4 * D_MODEL
EPS = 1e-6
NEG_INF = -1e30

TM = 1024
LANES = 128
SUBLANES = 8
LRU_W = 256
MXU_DIM = 256
N_QKV = 3 * H_A * DK_A
N_ATTN = H_A * DV_A + (H_B + 2 * KV_B) * HD_B + LANES
VMEM_LIMIT = 56 * 1024 * 1024


def _cparams(sem):
    return pltpu.CompilerParams(dimension_semantics=sem, vmem_limit_bytes=VMEM_LIMIT)


def _dot(a, b):
    return jnp.dot(a.astype(bf16), b.astype(bf16), preferred_element_type=f32)


def _dot_nt(a, b):
    return lax.dot_general(a.astype(bf16), b.astype(bf16), (((1,), (1,)), ((), ())),
                           preferred_element_type=f32)


def _dot_tn(a, b):
    return lax.dot_general(a.astype(bf16), b.astype(bf16), (((0,), (0,)), ((), ())),
                           preferred_element_type=f32)


def _sigmoid(x):
    return 1.0 / (1.0 + jnp.exp(-x))


def _softplus(x):
    return jnp.maximum(x, 0.0) + jnp.log1p(jnp.exp(-jnp.abs(x)))


def _modulate(x, g, shift, scale):
    ms = jnp.mean(x * x, axis=-1, keepdims=True)
    y = x * lax.rsqrt(ms + EPS) * g
    return y * (1.0 + scale) + shift


def _split2(x):
    hi = x.astype(bf16)
    lo = (x - hi.astype(f32)).astype(bf16)
    return hi, lo


def _split3(x):
    p1 = x.astype(bf16)
    r1 = x - p1.astype(f32)
    p2 = r1.astype(bf16)
    p3 = (r1 - p2.astype(f32)).astype(bf16)
    return p1, p2, p3


def _group_sum64(v):
    w = v.shape[-1]
    r = lax.broadcasted_iota(jnp.int32, (w, w), 0) >> 6
    c = lax.broadcasted_iota(jnp.int32, (w, w), 1) >> 6
    ones_bd = jnp.where(r == c, 1.0, 0.0).astype(bf16)
    return jnp.dot(v.astype(bf16), ones_bd, preferred_element_type=f32)


CONV_LEFT = CONV_W // 2


def _conv_masks(mk_s, tseq):
    n = mk_s.shape[1]
    pos = lax.broadcasted_iota(jnp.int32, (n, LANES), 0) & (tseq - 1)
    t = 0
    for j in range(CONV_W):
        o = j - CONV_LEFT
        if o != 0:
            mk_s[t] = jnp.where((pos + o >= 0) & (pos + o < tseq), 1.0, 0.0)
            t += 1


def _seq_conv(z, cw_ref, cb_ref, mk_s):
    n, c = z.shape
    acc = z * cw_ref[CONV_LEFT:CONV_LEFT + 1, :] + cb_ref[...]
    t = 0
    for j in range(CONV_W):
        o = j - CONV_LEFT
        if o != 0:
            mask = jnp.tile(mk_s[t], (1, c // LANES)) if c != LANES else mk_s[t]
            acc = acc + (pltpu.roll(z, (-o) % n, axis=0) * mask) * cw_ref[j:j + 1, :]
            t += 1
    return acc


def _adaln_kernel(c_ref, w_ref, b_ref, o_ref):
    c = c_ref[...]
    a = c * _sigmoid(c)
    o_ref[...] = _dot(a, w_ref[...]) + b_ref[...]


def _adaln(cond, ada_w, ada_b):
    rows = cond.shape[0]
    out = pl.pallas_call(
        _adaln_kernel,
        grid=(DEPTH, 6),
        in_specs=[
            pl.BlockSpec((rows, D_MODEL), lambda l, k: (0, 0)),
            pl.BlockSpec((None, D_MODEL, D_MODEL), lambda l, k: (l, 0, k)),
            pl.BlockSpec((None, None, 1, D_MODEL), lambda l, k: (l, k, 0, 0)),
        ],
        out_specs=pl.BlockSpec((None, None, rows, D_MODEL), lambda l, k: (l, k, 0, 0)),
        out_shape=jax.ShapeDtypeStruct((DEPTH, 6, rows, D_MODEL), f32),
        compiler_params=_cparams(("arbitrary", "arbitrary")),
        name="adaln",
    )(cond, ada_w, ada_b.reshape(DEPTH, 6, 1, D_MODEL))
    return out.reshape(DEPTH, 6, rows, 1, D_MODEL)


def _stream_specs(np_tiles, ngrid, split):
    def spec(block_of_tile):
        if ngrid == 1:
            return pl.BlockSpec((TM, D_MODEL), lambda i: (block_of_tile(i), 0))
        return pl.BlockSpec((TM, D_MODEL), lambda i, j: (block_of_tile(i), 0))

    if not split:
        return [spec(lambda i: i)]
    return [spec(lambda i: jnp.minimum(i, np_tiles - 1)), spec(lambda i: jnp.maximum(i - np_tiles, 0))]


def _stream_load(np_tiles, refs):
    if len(refs) == 1:
        return refs[0][...]
    return jnp.where(pl.program_id(0) < np_tiles, refs[0][...], refs[1][...])


def _stream_store(np_tiles, refs, val):
    if len(refs) == 1:
        refs[0][...] = val
        return
    i = pl.program_id(0)

    @pl.when(i < np_tiles)
    def _():
        refs[0][...] = val

    @pl.when(i >= np_tiles)
    def _():
        refs[1][...] = val


def _stream_shapes(np_tiles, nt, split):
    if not split:
        return [jax.ShapeDtypeStruct((nt * TM, D_MODEL), f32)]
    return [jax.ShapeDtypeStruct((np_tiles * TM, D_MODEL), f32),
            jax.ShapeDtypeStruct(((nt - np_tiles) * TM, D_MODEL), f32)]


def _stream_tiles(ys):
    return sum(y.shape[0] for y in ys) // TM


def _mod_spec(layer, k, row_of_tile, ngrid):
    if ngrid == 1:
        return pl.BlockSpec((None, None, None, 1, D_MODEL), lambda i: (layer, k, row_of_tile(i), 0, 0))
    return pl.BlockSpec((None, None, None, 1, D_MODEL), lambda i, j: (layer, k, row_of_tile(i), 0, 0))


QKV_BLK = H_A * DK_A


def _delta_proj_kernel(np_tiles, t_p, t_s, n_y, *refs):
    y_refs = refs[:n_y]
    g_ref, sh_ref, sc_ref, w_ref, cw_ref, cb_ref, o_ref, h_s, mk_s = refs[n_y:]
    n = pl.program_id(1)

    @pl.when(n == 0)
    def _():
        x = _stream_load(np_tiles, y_refs)
        h_s[...] = _modulate(x, g_ref[...], sh_ref[...], sc_ref[...]).astype(bf16)
        _conv_masks(mk_s, jnp.where(pl.program_id(0) < np_tiles, t_p, t_s))

    z = jnp.dot(h_s[...], w_ref[...].astype(bf16), preferred_element_type=f32)
    y = _seq_conv(z, cw_ref, cb_ref, mk_s)
    y = y * _sigmoid(y)

    @pl.when(n < 2)
    def _():
        o_ref[...] = y * lax.rsqrt(_group_sum64(y * y) + EPS)

    @pl.when(n == 2)
    def _():
        o_ref[...] = y


def _delta_proj(ys, mods, layer, row_of_tile, np_tiles, t_p, t_s, norm_g, w_in, conv_w, conv_b):
    nt = _stream_tiles(ys)
    ntok = nt * TM
    kern = functools.partial(_delta_proj_kernel, np_tiles, t_p, t_s, len(ys))
    return pl.pallas_call(
        kern,
        grid=(nt, N_QKV // QKV_BLK),
        in_specs=_stream_specs(np_tiles, 2, len(ys) == 2) + [
            pl.BlockSpec((1, D_MODEL), lambda i, n: (0, 0)),
            _mod_spec(layer, 0, row_of_tile, 2),
            _mod_spec(layer, 1, row_of_tile, 2),
            pl.BlockSpec((D_MODEL, QKV_BLK), lambda i, n: (0, n)),
            pl.BlockSpec((CONV_W, QKV_BLK), lambda i, n: (0, n)),
            pl.BlockSpec((1, QKV_BLK), lambda i, n: (0, n)),
        ],
        out_specs=pl.BlockSpec((TM, QKV_BLK), lambda i, n: (i, n)),
        out_shape=jax.ShapeDtypeStruct((ntok, N_QKV), f32),
        scratch_shapes=[pltpu.VMEM((TM, D_MODEL), bf16), pltpu.VMEM((CONV_W - 1, TM, LANES), f32)],
        compiler_params=_cparams(("arbitrary", "arbitrary")),
        name="delta_proj",
    )(*ys, norm_g, mods, mods, w_in, conv_w, conv_b)


def _rope_swap(x):
    w = x.shape[1]
    lane = lax.broadcasted_iota(jnp.int32, (1, w), 1)
    first = (lane & (ROPE_AXIS - 1)) < ROPE_AXIS // 2
    return jnp.where(first, pltpu.roll(x, w - ROPE_AXIS // 2, axis=1), pltpu.roll(x, ROPE_AXIS // 2, axis=1))


def _attn_proj_kernel(np_tiles, n_y, *refs):
    y_refs = refs[:n_y]
    (g_ref, sh_ref, sc_ref, w_ref, qn_ref, kn_ref, cos_ref, sin_ref, al_ref, dt_ref,
     gate_ref, q_ref, k_ref, v_ref, gb_ref, kc_ref, vc_ref) = refs[n_y:]
    x = _stream_load(np_tiles, y_refs)
    h = _modulate(x, g_ref[...], sh_ref[...], sc_ref[...]).astype(bf16)
    z = jnp.dot(h, w_ref[...].astype(bf16), preferred_element_type=f32)
    nq = H_B * HD_B
    nkv = KV_B * HD_B
    o1 = H_A * DV_A
    o2 = o1 + nq + 2 * nkv
    gz = z[:, :o1]
    gate_ref[...] = gz * _sigmoid(gz)
    q = z[:, o1:o1 + nq]
    k = z[:, o1 + nq:o1 + nq + nkv]
    v = z[:, o1 + nq + nkv:o2]
    v_ref[...] = v
    qn = q * lax.rsqrt(_group_sum64(q * q) * (1.0 / HD_B) + EPS) * qn_ref[...]
    kn = k * lax.rsqrt(_group_sum64(k * k) * (1.0 / HD_B) + EPS) * kn_ref[...]
    cos = cos_ref[...]
    sin = sin_ref[...]
    q_ref[...] = qn * cos + _rope_swap(qn) * sin
    k_ref[...] = kn * cos[:, :nkv] + _rope_swap(kn) * sin[:, :nkv]

    @pl.when(pl.program_id(0) < np_tiles)
    def _():
        kc_ref[...] = kn
        vc_ref[...] = v
    zs = z[:, o2:]
    lane = lax.broadcasted_iota(jnp.int32, (1, LANES), 1)
    g_val = -jnp.exp(al_ref[...]) * _softplus(zs + dt_ref[...])
    gb_ref[...] = jnp.where(lane < 2 * H_A, g_val, jnp.where(lane < 4 * H_A, _sigmoid(zs), 0.0))


def _attn_proj(ys, mods, layer, row_of_tile, np_tiles, norm_g, w_b, qn_row, kn_row, cos_tab, sin_tab,
               al_row, dt_row):
    nt = _stream_tiles(ys)
    ntok = nt * TM
    nq = H_B * HD_B
    nkv = KV_B * HD_B
    tab_idx = lambda i: (jnp.where(i < np_tiles, 0, 1), 0)
    cache_idx = lambda i: (jnp.minimum(i, np_tiles - 1), 0)
    return pl.pallas_call(
        functools.partial(_attn_proj_kernel, np_tiles, len(ys)),
        grid=(nt,),
        in_specs=_stream_specs(np_tiles, 1, len(ys) == 2) + [
            pl.BlockSpec((1, D_MODEL), lambda i: (0, 0)),
            _mod_spec(layer, 0, row_of_tile, 1),
            _mod_spec(layer, 1, row_of_tile, 1),
            pl.BlockSpec((D_MODEL, N_ATTN), lambda i: (0, 0)),
            pl.BlockSpec((1, nq), lambda i: (0, 0)),
            pl.BlockSpec((1, nkv), lambda i: (0, 0)),
            pl.BlockSpec((TM, nq), tab_idx),
            pl.BlockSpec((TM, nq), tab_idx),
            pl.BlockSpec((1, LANES), lambda i: (0, 0)),
            pl.BlockSpec((1, LANES), lambda i: (0, 0)),
        ],
        out_specs=[
            pl.BlockSpec((TM, H_A * DV_A), lambda i: (i, 0)),
            pl.BlockSpec((TM, nq), lambda i: (i, 0)),
            pl.BlockSpec((TM, nkv), lambda i: (i, 0)),
            pl.BlockSpec((TM, nkv), lambda i: (i, 0)),
            pl.BlockSpec((TM, LANES), lambda i: (i, 0)),
            pl.BlockSpec((TM, nkv), cache_idx),
            pl.BlockSpec((TM, nkv), cache_idx),
        ],
        out_shape=[jax.ShapeDtypeStruct((ntok, H_A * DV_A), f32),
                   jax.ShapeDtypeStruct((ntok, nq), f32),
                   jax.ShapeDtypeStruct((ntok, nkv), f32),
                   jax.ShapeDtypeStruct((ntok, nkv), f32),
                   jax.ShapeDtypeStruct((ntok, LANES), f32),
                   jax.ShapeDtypeStruct((np_tiles * TM, nkv), f32),
                   jax.ShapeDtypeStruct((np_tiles * TM, nkv), f32)],
        compiler_params=_cparams(("arbitrary",)),
        name="attn_proj",
    )(*ys, norm_g, mods, mods, w_b, qn_row, kn_row, cos_tab, sin_tab, al_row, dt_row)


INV_BASE = 8
PRE_CHUNKS = 2
DELTA_SEQS_P = 2
HPG = MXU_DIM // DK_A
NLG = H_A // HPG
W_ALL = H_A * DK_A


def _chunk_scan(x, reverse):
    t = x.shape[0]
    r = lax.broadcasted_iota(jnp.int32, (t, 1), 0) & (CHUNK - 1)
    s = 1
    while s < CHUNK:
        if reverse:
            x = x + jnp.where(r + s < CHUNK, pltpu.roll(x, t - s, axis=0), 0.0)
        else:
            x = x + jnp.where(r >= s, pltpu.roll(x, s, axis=0), 0.0)
        s *= 2
    return x


def _delta_kernel(t, nsq, has_s0, *refs):
    if has_s0:
        q_ref, k_ref, v_ref, gb_ref, s0_ref = refs[:5]
        rest = refs[5:]
    else:
        q_ref, k_ref, v_ref, gb_ref = refs[:4]
        s0_ref = None
        rest = refs[4:]
    o_ref, sf_ref, exp_s, kbg_s, qg_s, kd_s, vb_s, egt_s, ti_s, in_s, st_s = rest
    n = t // CHUNK
    scale = DK_A ** -0.5

    gb = gb_ref[...]
    lane = lax.broadcasted_iota(jnp.int32, (1, LANES), 1)
    sc = jnp.where(lane < H_A, _chunk_scan(gb, False), jnp.where(lane < 2 * H_A, _chunk_scan(gb, True), gb))
    er = lax.broadcasted_iota(jnp.int32, (LANES, 4 * W_ALL), 0)
    ec = lax.broadcasted_iota(jnp.int32, (LANES, 4 * W_ALL), 1) >> 6
    expand = jnp.where(er == ec, 1.0, 0.0).astype(bf16)
    rb = min(t, 256)
    for r0 in range(0, nsq * t, rb):
        p1, p2, p3 = _split3(sc[r0:r0 + rb])
        exp_s[r0:r0 + rb, :] = ((jnp.dot(p1, expand, preferred_element_type=f32)
                                 + jnp.dot(p2, expand, preferred_element_type=f32))
                                + jnp.dot(p3, expand, preferred_element_type=f32))

    ri = lax.broadcasted_iota(jnp.int32, (CHUNK, MXU_DIM), 0)
    col_i = lax.broadcasted_iota(jnp.int32, (CHUNK, MXU_DIM), 1) & (CHUNK - 1)
    eye_ls = ri == col_i
    bd_mask = ((lax.broadcasted_iota(jnp.int32, (MXU_DIM, MXU_DIM), 0) >> 6)
               == (lax.broadcasted_iota(jnp.int32, (MXU_DIM, MXU_DIM), 1) >> 6))
    ones_c = jnp.ones((CHUNK, CHUNK), bf16)

    def bd(x):
        xb = x.astype(bf16)
        return jnp.where(bd_mask, jnp.concatenate([xb] * HPG, axis=0), jnp.zeros((), bf16))

    def blk(b):
        sh = b.bit_length() - 1
        return (ri >> sh) == (col_i >> sh)

    def mm3(xh, xl, yh, yl):
        r1 = jnp.dot(jnp.concatenate([xh, xl], axis=0), bd(yh), preferred_element_type=f32)
        r2 = jnp.dot(xh, bd(yl), preferred_element_type=f32)
        return (r1[:CHUNK] + r1[CHUNK:]) + r2

    def pre_body(cp, carry):
        units = [(ci, d, lg) for ci in range(PRE_CHUNKS) for lg in range(NLG) for d in range(2)]
        cidx = [cp * PRE_CHUNKS + ci for ci in range(PRE_CHUNKS)]
        gcs, prods = {}, {}
        for ci, c in enumerate(cidx):
            rows = pl.ds(pl.multiple_of(c * CHUNK, CHUNK), CHUNK)
            q = q_ref[rows, :]
            k = k_ref[rows, :]
            v = v_ref[rows, :]
            kbs = []
            for d in range(2):
                gc = exp_s[rows, d * W_ALL:(d + 1) * W_ALL]
                beta = exp_s[rows, (2 + d) * W_ALL:(3 + d) * W_ALL]
                gtot = gc[CHUNK - 1:CHUNK, :] if d == 0 else gc[0:1, :]
                eg = jnp.exp(gc)
                kb = k * beta
                kbg_s[d, rows, :] = (kb * eg).astype(bf16)
                qg_s[d, rows, :] = (q * (scale * eg)).astype(bf16)
                kd_s[d, rows, :] = (k * jnp.exp(gtot - gc)).astype(bf16)
                vb_s[d, rows, :] = v * beta
                egt_s[d, c] = jnp.zeros((8, W_ALL), f32) + jnp.exp(gtot)
                gcs[ci, d] = gc
                kbs.append(kb)
            for lg in range(NLG):
                cols = slice(lg * MXU_DIM, (lg + 1) * MXU_DIM)
                lhs = jnp.concatenate([kbs[0][:, cols], kbs[1][:, cols], q[:, cols] * scale], axis=0)
                prods[ci, lg] = _dot_nt(lhs, bd(k[:, cols]))
        ps, ys, avs = [], [], []
        for ci, d, lg in units:
            c = cidx[ci]
            cols = slice(lg * MXU_DIM, (lg + 1) * MXU_DIM)
            gc = gcs[ci, d][:, cols]
            prod = prods[ci, lg]
            d1, d2, d3 = _split3(jnp.where(eye_ls, gc, 0.0))
            grow = ((jnp.dot(ones_c, d1, preferred_element_type=f32)
                     + jnp.dot(ones_c, d2, preferred_element_type=f32))
                    + jnp.dot(ones_c, d3, preferred_element_type=f32))
            keep = (ri >= col_i) if d == 0 else (ri <= col_i)
            strict = (ri > col_i) if d == 0 else (ri < col_i)
            decay = jnp.where(keep, jnp.exp(jnp.where(keep, gc - grow, 0.0)), 0.0)
            a = jnp.where(strict, prod[d * CHUNK:(d + 1) * CHUNK] * decay, 0.0)
            in_s[d, c, :, cols] = jnp.where(keep, prod[2 * CHUNK:] * decay, 0.0).astype(bf16)
            ah, al = _split2(a)
            avs.append((ah, al))
            zb = jnp.zeros((), bf16)
            ys.append((jnp.where(blk(INV_BASE), -ah, zb), jnp.where(blk(INV_BASE), -al, zb)))
            ps.append(jnp.where(eye_ls, 1.0, 0.0) - jnp.where(blk(INV_BASE), a, 0.0))
        m = 1
        while m < INV_BASE:
            last = 2 * m >= INV_BASE
            for u in range(len(units)):
                yh, yl = ys[u]
                p = ps[u]
                byh, byl = bd(yh), bd(yl)
                if m == 1:
                    r1 = jnp.dot(jnp.concatenate([yh, yl], axis=0), byh, preferred_element_type=f32)
                    r2 = jnp.dot(yh, byl, preferred_element_type=f32)
                    y2 = (r1[:CHUNK] + r1[CHUNK:]) + r2
                else:
                    ph, plo = _split2(p)
                    if last:
                        r1 = jnp.dot(jnp.concatenate([ph, plo], axis=0), byh, preferred_element_type=f32)
                        r2 = jnp.dot(ph, byl, preferred_element_type=f32)
                        p = p + ((r1[:CHUNK] + r1[CHUNK:]) + r2)
                    else:
                        r1 = jnp.dot(jnp.concatenate([ph, plo, yh, yl], axis=0), byh,
                                     preferred_element_type=f32)
                        r2 = jnp.dot(jnp.concatenate([ph, yh], axis=0), byl, preferred_element_type=f32)
                        p = p + ((r1[:CHUNK] + r1[CHUNK:2 * CHUNK]) + r2[:CHUNK])
                        y2 = (r1[2 * CHUNK:3 * CHUNK] + r1[3 * CHUNK:]) + r2[CHUNK:]
                ps[u] = p
                if not last:
                    ys[u] = _split2(y2)
            m *= 2
        b = INV_BASE
        while b < CHUNK:
            off = blk(2 * b) & jnp.logical_not(blk(b))
            ws, pps = [], []
            zb = jnp.zeros((), bf16)
            for u in range(len(units)):
                pps.append(_split2(ps[u]))
                ws.append(mm3(jnp.where(off, avs[u][0], zb), jnp.where(off, avs[u][1], zb), *pps[u]))
            for u in range(len(units)):
                wh, wl = _split2(ws[u])
                ps[u] = ps[u] - mm3(*pps[u], wh, wl)
            b *= 2
        for u, (ci, d, lg) in enumerate(units):
            ti_s[d, cidx[ci], :, lg * MXU_DIM:(lg + 1) * MXU_DIM] = ps[u].astype(bf16)
        return carry

    lax.fori_loop(0, nsq * n // PRE_CHUNKS, pre_body, 0)

    for sq in range(nsq):
        for d in range(2):
            for lg in range(NLG):
                if has_s0:
                    blocks = []
                    for hh in range(HPG):
                        s_h = s0_ref[sq, d, lg * HPG + hh]
                        z_l = jnp.zeros((DK_A, hh * DV_A), f32)
                        z_r = jnp.zeros((DK_A, (HPG - 1 - hh) * DV_A), f32)
                        parts = ([z_l] if hh > 0 else []) + [s_h] + ([z_r] if hh < HPG - 1 else [])
                        blocks.append(jnp.concatenate(parts, axis=1) if len(parts) > 1 else s_h)
                    st_s[sq, d, lg] = jnp.concatenate(blocks, axis=0)
                else:
                    st_s[sq, d, lg] = jnp.zeros((MXU_DIM, MXU_DIM), f32)
    o_ref[...] = jnp.zeros_like(o_ref)

    def seq_body(j, carry):
        units = [(sq, d, lg) for sq in range(nsq) for lg in range(NLG) for d in range(2)]
        step = [j, n - 1 - j]
        cls = [slice(lg * MXU_DIM, (lg + 1) * MXU_DIM) for lg in range(NLG)]
        chunk = {(sq, d): sq * n + step[d] for sq in range(nsq) for d in range(2)}
        rws = {key: pl.ds(pl.multiple_of(cc * CHUNK, CHUNK), CHUNK) for key, cc in chunk.items()}
        boths, vnbs = [], []
        for sq, d, lg in units:
            r = rws[sq, d]
            lhs = jnp.concatenate([kbg_s[d, r, cls[lg]], qg_s[d, r, cls[lg]]], axis=0)
            boths.append(jnp.dot(lhs, st_s[sq, d, lg].astype(bf16), preferred_element_type=f32))
        for u, (sq, d, lg) in enumerate(units):
            resid = vb_s[d, rws[sq, d], cls[lg]] - boths[u][:CHUNK]
            v_new = jnp.dot(ti_s[d, chunk[sq, d], :, cls[lg]], bd(resid), preferred_element_type=f32)
            vnbs.append(v_new.astype(bf16))
        for u, (sq, d, lg) in enumerate(units):
            r = rws[sq, d]
            c = chunk[sq, d]
            o = boths[u][CHUNK:] + jnp.dot(in_s[d, c, :, cls[lg]], bd(vnbs[u]), preferred_element_type=f32)
            upd = lax.dot_general(kd_s[d, r, cls[lg]], vnbs[u], (((0,), (0,)), ((), ())),
                                  preferred_element_type=f32)
            st_s[sq, d, lg] = st_s[sq, d, lg] * egt_s[d, c, 0:1, cls[lg]] + jnp.where(bd_mask, upd, 0.0)
            o_ref[r, cls[lg]] = o_ref[r, cls[lg]] + o
        return carry

    lax.fori_loop(0, n, seq_body, 0)
    for sq in range(nsq):
        for d in range(2):
            for lg in range(NLG):
                s_fin = st_s[sq, d, lg]
                for hh in range(HPG):
                    sf_ref[sq, d, lg * HPG + hh] = s_fin[hh * DK_A:(hh + 1) * DK_A, hh * DV_A:(hh + 1) * DV_A]


def _delta(qkv, gb, t, nseq, nsq, tok_block0, s0, s0_layer):
    n = t // CHUNK
    tb = nsq * t
    nch = nsq * n
    has_s0 = s0 is not None
    kern = functools.partial(_delta_kernel, t, nsq, has_s0)
    in_specs = [
        pl.BlockSpec((tb, W_ALL), lambda b: (tok_block0 + b, 0)),
        pl.BlockSpec((tb, W_ALL), lambda b: (tok_block0 + b, 1)),
        pl.BlockSpec((tb, W_ALL), lambda b: (tok_block0 + b, 2)),
        pl.BlockSpec((tb, LANES), lambda b: (tok_block0 + b, 0)),
    ]
    args = [qkv, qkv, qkv, gb]
    if has_s0:
        in_specs.append(pl.BlockSpec((nsq, None, 2, H_A, DK_A, DV_A), lambda b: (b, s0_layer, 0, 0, 0, 0)))
        args.append(s0)
    return pl.pallas_call(
        kern,
        grid=(nseq // nsq,),
        in_specs=in_specs,
        out_specs=[
            pl.BlockSpec((tb, W_ALL), lambda b: (b, 0)),
            pl.BlockSpec((nsq, 2, H_A, DK_A, DV_A), lambda b: (b, 0, 0, 0, 0)),
        ],
        out_shape=[jax.ShapeDtypeStruct((nseq * t, W_ALL), f32),
                   jax.ShapeDtypeStruct((nseq, 2, H_A, DK_A, DV_A), f32)],
        scratch_shapes=[
            pltpu.VMEM((tb, 4 * W_ALL), f32),
            pltpu.VMEM((2, tb, W_ALL), bf16),
            pltpu.VMEM((2, tb, W_ALL), bf16),
            pltpu.VMEM((2, tb, W_ALL), bf16),
            pltpu.VMEM((2, tb, W_ALL), f32),
            pltpu.VMEM((2, nch, 8, W_ALL), f32),
            pltpu.VMEM((2, nch, CHUNK, W_ALL), bf16),
            pltpu.VMEM((2, nch, CHUNK, W_ALL), bf16),
            pltpu.VMEM((nsq, 2, NLG, MXU_DIM, MXU_DIM), f32),
        ],
        compiler_params=_cparams(("arbitrary",)),
        name="delta_p" if not has_s0 else "delta_s",
    )(*args)


def _attend(q_all, key_sets, sink_ref):
    nrow = q_all.shape[0]
    scores = []
    for h in range(H_B):
        kv = h // G_B
        q = q_all[:, h * HD_B:(h + 1) * HD_B]
        row = []
        for k_all, _, mask in key_sets:
            s = _dot_nt(q, k_all[:, kv * HD_B:(kv + 1) * HD_B])
            row.append(s if mask is None else jnp.where(mask, s, NEG_INF))
        scores.append(row)
    probs, sink_terms = [], []
    for h in range(H_B):
        sink = jnp.zeros((nrow, 1), f32) + sink_ref[h]
        m = sink
        for s in scores[h]:
            m = jnp.maximum(m, jnp.max(s, axis=-1, keepdims=True))
        probs.append([jnp.exp(s - m).astype(bf16) for s in scores[h]])
        sink_terms.append(jnp.exp(sink - m))
    v_ext = [[jnp.concatenate([v_all[:, kv * HD_B:(kv + 1) * HD_B].astype(bf16),
                               jnp.ones((v_all.shape[0], HD_B), bf16)], axis=1) for kv in range(KV_B)]
             for _, v_all, _ in key_sets]
    outs = []
    for h in range(H_B):
        acc = None
        for p, vs in zip(probs[h], v_ext):
            pv = jnp.dot(p, vs[h // G_B], preferred_element_type=f32)
            acc = pv if acc is None else acc + pv
        outs.append(acc[:, :HD_B] / (acc[:, HD_B:HD_B + 1] + sink_terms[h]))
    return jnp.concatenate(outs, axis=1)


def _ctx_attn_kernel(sink_ref, q_ref, k_ref, v_ref, o_ref):
    scale = HD_B ** -0.5
    o_ref[...] = _attend(q_ref[...] * scale, [(k_ref[...], v_ref[...], None)], sink_ref)


def _ctx_attn(sink, qb, kb, vb, t, nseq):
    nq = H_B * HD_B
    nkv = KV_B * HD_B
    return pl.pallas_call(
        _ctx_attn_kernel,
        grid=(nseq,),
        in_specs=[
            pl.BlockSpec(memory_space=pltpu.SMEM),
            pl.BlockSpec((t, nq), lambda b: (b, 0)),
            pl.BlockSpec((t, nkv), lambda b: (b, 0)),
            pl.BlockSpec((t, nkv), lambda b: (b, 0)),
        ],
        out_specs=pl.BlockSpec((t, nq), lambda b: (b, 0)),
        out_shape=jax.ShapeDtypeStruct((nseq * t, nq), f32),
        compiler_params=_cparams(("arbitrary",)),
        name="ctx_attn",
    )(sink, qb, kb, vb)


def _win_attn_kernel(t, sink_ref, q_ref, k_ref, v_ref, kc_ref, vc_ref, o_ref):
    i = pl.program_id(1)
    scale = HD_B ** -0.5
    span = QBLK + 2 * WINDOW
    start = i * QBLK
    lo = jnp.clip(start - WINDOW, 0, t - span)
    lo = pl.multiple_of(lo, QBLK)
    q_pos = start + lax.broadcasted_iota(jnp.int32, (QBLK, span), 0)
    k_pos = lo + lax.broadcasted_iota(jnp.int32, (QBLK, span), 1)
    valid = jnp.abs(q_pos - k_pos) <= WINDOW
    key_sets = [(k_ref[pl.ds(lo, span), :], v_ref[pl.ds(lo, span), :], valid), (kc_ref[...], vc_ref[...], None)]
    o_ref[...] = _attend(q_ref[...] * scale, key_sets, sink_ref)


def _win_attn(sink, qb, kb, vb, cache_k, cache_v, cache_layer, t, nseq, tok_block0):
    nq = H_B * HD_B
    nkv = KV_B * HD_B
    nb = t // QBLK
    past = cache_k.shape[2]
    kern = functools.partial(_win_attn_kernel, t)
    return pl.pallas_call(
        kern,
        grid=(nseq, nb),
        in_specs=[
            pl.BlockSpec(memory_space=pltpu.SMEM),
            pl.BlockSpec((QBLK, nq), lambda b, i: ((tok_block0 + b) * nb + i, 0)),
            pl.BlockSpec((t, nkv), lambda b, i: (tok_block0 + b, 0)),
            pl.BlockSpec((t, nkv), lambda b, i: (tok_block0 + b, 0)),
            pl.BlockSpec((None, None, past, nkv), lambda b, i: (b, cache_layer, 0, 0)),
            pl.BlockSpec((None, None, past, nkv), lambda b, i: (b, cache_layer, 0, 0)),
        ],
        out_specs=pl.BlockSpec((QBLK, nq), lambda b, i: (b * nb + i, 0)),
        out_shape=jax.ShapeDtypeStruct((nseq * t, nq), f32),
        compiler_params=_cparams(("arbitrary", "arbitrary")),
        name="win_attn",
    )(sink, qb, kb, vb, cache_k, cache_v)


def _mix_out_kernel(np_tiles, n_y, *refs):
    y_refs = refs[:n_y]
    oap_ref, oas_ref, obp_ref, obs_ref, gate_ref, dn_ref, w_ref, gm_ref = refs[n_y:n_y + 8]
    o_refs = refs[n_y + 8:]
    i = pl.program_id(0)
    is_p = i < np_tiles
    oa = jnp.where(is_p, oap_ref[...], oas_ref[...])
    ob = jnp.where(is_p, obp_ref[...], obs_ref[...])
    oan = oa * lax.rsqrt(_group_sum64(oa * oa) * (1.0 / DV_A) + EPS) * dn_ref[...] * gate_ref[...]
    ka = H_A * DV_A
    mix = _dot(oan, w_ref[:ka, :]) + _dot(ob, w_ref[ka:, :])
    _stream_store(np_tiles, o_refs, _stream_load(np_tiles, y_refs) + gm_ref[...] * mix)


def _mix_out(ys, mods, layer, row_of_tile, np_tiles, oa_p, oa_s, ob_p, ob_s, gate, dn_row, w_out, split_out):
    nt = _stream_tiles(ys)
    ka = H_A * DV_A
    p_idx = lambda i: (jnp.minimum(i, np_tiles - 1), 0)
    s_idx = lambda i: (jnp.maximum(i - np_tiles, 0), 0)
    kern = functools.partial(_mix_out_kernel, np_tiles, len(ys))
    return pl.pallas_call(
        kern,
        grid=(nt,),
        in_specs=_stream_specs(np_tiles, 1, len(ys) == 2) + [
            pl.BlockSpec((TM, ka), p_idx),
            pl.BlockSpec((TM, ka), s_idx),
            pl.BlockSpec((TM, ka), p_idx),
            pl.BlockSpec((TM, ka), s_idx),
            pl.BlockSpec((TM, ka), lambda i: (i, 0)),
            pl.BlockSpec((1, ka), lambda i: (0, 0)),
            pl.BlockSpec((2 * ka, D_MODEL), lambda i: (0, 0)),
            _mod_spec(layer, 2, row_of_tile, 1),
        ],
        out_specs=_stream_specs(np_tiles, 1, split_out),
        out_shape=_stream_shapes(np_tiles, nt, split_out),
        compiler_params=_cparams(("arbitrary",)),
        name="mix_out",
    )(*ys, oa_p, oa_s, ob_p, ob_s, gate, dn_row, w_out, mods)


def _gelu_tanh(x):
    return 0.5 * x * (1.0 + jnp.tanh(math.sqrt(2.0 / math.pi) * (x + 0.044715 * (x * x * x))))


def _lru_kernel(np_tiles, t_p, t_s, y_ref, g_ref, sh_ref, sc_ref, wx_ref, wg_ref, cw_ref, cb_ref,
                wl_ref, bl_ref, lam_ref, h0_ref, o_ref, fin_ref, h_s, x_s, gt_s, hf_s, hb_s, mk_s):
    is_p = pl.program_id(0) < np_tiles
    tseq = jnp.where(is_p, t_p, t_s)

    @pl.when(pl.program_id(1) == 0)
    def _():
        h_s[...] = _modulate(y_ref[...], g_ref[...], sh_ref[...], sc_ref[...]).astype(bf16)
        _conv_masks(mk_s, tseq)

    w = jnp.concatenate([wx_ref[...], wg_ref[...]], axis=1).astype(bf16)
    z = jnp.dot(h_s[...], w, preferred_element_type=f32)
    x = _seq_conv(z[:, :LRU_W], cw_ref, cb_ref, mk_s)
    x_s[...] = x
    gt_s[...] = _dot(x, wl_ref[...]) + bl_ref[...]
    decay = [RG_C * _softplus(-lam_ref[:, d * LRU_W:(d + 1) * LRU_W]) for d in range(2)]
    sub = lax.broadcasted_iota(jnp.int32, (SUBLANES, 1), 0)
    nblk = TM // SUBLANES
    seg_shift = (t_p // SUBLANES).bit_length() - 1

    def body(kf, carry):
        nxt = []
        for d, kk in ((0, kf), (1, nblk - 1 - kf)):
            r0 = pl.multiple_of(kk * SUBLANES, SUBLANES)
            rows = pl.ds(r0, SUBLANES)
            xb = x_s[rows, :]
            gb = gt_s[rows, :]
            r = _sigmoid(gb[:, (2 * d) * LRU_W:(2 * d + 1) * LRU_W])
            ig = _sigmoid(gb[:, (2 * d + 1) * LRU_W:(2 * d + 2) * LRU_W])
            log_a = -(r * decay[d])
            a = jnp.exp(log_a)
            u = jnp.sqrt(-jnp.tanh(log_a) * (a * a + 1.0)) * (ig * xb)
            s = 1
            while s < SUBLANES:
                ok = (sub >= s) if d == 0 else (sub + s < SUBLANES)
                sh = s if d == 0 else SUBLANES - s
                u = u + a * jnp.where(ok, pltpu.roll(u, sh, axis=0), 0.0)
                a = a * jnp.where(ok, pltpu.roll(a, sh, axis=0), 1.0)
                s *= 2
            edge = r0 if d == 0 else r0 + SUBLANES
            seg = jnp.where(is_p, kk >> seg_shift, 0)
            c_in = jnp.where((edge & (tseq - 1)) == 0, h0_ref[seg, d:d + 1, :], carry[d])
            hblk = u + a * c_in
            (hf_s if d == 0 else hb_s)[rows, :] = hblk
            nxt.append(hblk[SUBLANES - 1:SUBLANES, :] if d == 0 else hblk[0:1, :])
        return tuple(nxt)

    zero = jnp.zeros((1, LRU_W), f32)
    lax.fori_loop(0, nblk, body, (zero, zero), unroll=4)
    for sg in range(TM // t_p):
        fin_ref[sg, 0:1, :] = hf_s[sg * t_p + t_p - 1:sg * t_p + t_p, :]
        fin_ref[sg, 1:2, :] = hb_s[sg * t_p:sg * t_p + 1, :]
    o_ref[...] = (hf_s[...] + hb_s[...]) * _gelu_tanh(z[:, LRU_W:])


def _lru(y, mods, layer, row_of_tile, np_tiles, t_p, t_s, norm_g, w_in, cw, cb, wl, bl, lam, h0):
    ntok = y.shape[0]
    nt = ntok // TM
    ncol = D_RNN // LRU_W
    nseg = TM // t_p
    kern = functools.partial(_lru_kernel, np_tiles, t_p, t_s)
    return pl.pallas_call(
        kern,
        grid=(nt, ncol),
        in_specs=[
            pl.BlockSpec((TM, D_MODEL), lambda i, j: (i, 0)),
            pl.BlockSpec((1, D_MODEL), lambda i, j: (0, 0)),
            _mod_spec(layer, 0, row_of_tile, 2),
            _mod_spec(layer, 1, row_of_tile, 2),
            pl.BlockSpec((D_MODEL, LRU_W), lambda i, j: (0, j)),
            pl.BlockSpec((D_MODEL, LRU_W), lambda i, j: (0, ncol + j)),
            pl.BlockSpec((CONV_W, LRU_W), lambda i, j: (0, j)),
            pl.BlockSpec((1, LRU_W), lambda i, j: (0, j)),
            pl.BlockSpec((None, LRU_W, 4 * LRU_W), lambda i, j: (j, 0, 0)),
            pl.BlockSpec((None, 1, 4 * LRU_W), lambda i, j: (j, 0, 0)),
            pl.BlockSpec((None, 1, 2 * LRU_W), lambda i, j: (j, 0, 0)),
            pl.BlockSpec((None, nseg, 2, LRU_W), lambda i, j: (i, 0, 0, j)),
        ],
        out_specs=[
            pl.BlockSpec((TM, LRU_W), lambda i, j: (i, j)),
            pl.BlockSpec((None, nseg, 2, LRU_W), lambda i, j: (i, 0, 0, j)),
        ],
        out_shape=[jax.ShapeDtypeStruct((ntok, D_RNN), f32),
                   jax.ShapeDtypeStruct((nt, nseg, 2, D_RNN), f32)],
        scratch_shapes=[pltpu.VMEM((TM, D_MODEL), bf16), pltpu.VMEM((TM, LRU_W), f32),
                        pltpu.VMEM((TM, 4 * LRU_W), f32), pltpu.VMEM((TM, LRU_W), f32),
                        pltpu.VMEM((TM, LRU_W), f32), pltpu.VMEM((CONV_W - 1, TM, LANES), f32)],
        compiler_params=_cparams(("arbitrary", "arbitrary")),
        name="lru",
    )(y, norm_g, mods, mods, w_in, w_in, cw, cb, wl, bl, lam, h0)


def _proj_out_kernel(y_ref, a_ref, w_ref, gm_ref, o_ref):
    o_ref[...] = y_ref[...] + gm_ref[...] * _dot(a_ref[...], w_ref[...])


def _proj_out(y, mods, layer, row_of_tile, a, w):
    ntok = y.shape[0]
    nt = ntok // TM
    k = a.shape[1]
    return pl.pallas_call(
        _proj_out_kernel,
        grid=(nt,),
        in_specs=[
            pl.BlockSpec((TM, D_MODEL), lambda i: (i, 0)),
            pl.BlockSpec((TM, k), lambda i: (i, 0)),
            pl.BlockSpec((k, D_MODEL), lambda i: (0, 0)),
            _mod_spec(layer, 2, row_of_tile, 1),
        ],
        out_specs=pl.BlockSpec((TM, D_MODEL), lambda i: (i, 0)),
        out_shape=jax.ShapeDtypeStruct((ntok, D_MODEL), f32),
        compiler_params=_cparams(("arbitrary",)),
        name="proj_out",
    )(y, a, w, mods)


FF_BLK = 1024


def _mlp_kernel(np_tiles, n_out, y_ref, g_ref, sh_ref, sc_ref, gm_ref, w1_ref, w2_ref, *refs):
    o_refs = refs[:n_out]
    h_s, acc_s = refs[n_out:]
    k = pl.program_id(1)

    @pl.when(k == 0)
    def _():
        h_s[...] = _modulate(y_ref[...], g_ref[...], sh_ref[...], sc_ref[...]).astype(bf16)
        acc_s[...] = jnp.zeros_like(acc_s)

    a = jnp.dot(h_s[...], w1_ref[...].astype(bf16), preferred_element_type=f32)
    a = jnp.maximum(a, 0.0)
    acc_s[...] += _dot(a * a, w2_ref[...])

    @pl.when(k == pl.num_programs(1) - 1)
    def _():
        _stream_store(np_tiles, o_refs, y_ref[...] + gm_ref[...] * acc_s[...])


def _mlp(y, mods, layer, row_of_tile, np_tiles, norm_g, w1, w2, split_out):
    nt = y.shape[0] // TM
    return pl.pallas_call(
        functools.partial(_mlp_kernel, np_tiles, 2 if split_out else 1),
        grid=(nt, D_FF // FF_BLK),
        in_specs=[
            pl.BlockSpec((TM, D_MODEL), lambda i, k: (i, 0)),
            pl.BlockSpec((1, D_MODEL), lambda i, k: (0, 0)),
            _mod_spec(layer, 3, row_of_tile, 2),
            _mod_spec(layer, 4, row_of_tile, 2),
            _mod_spec(layer, 5, row_of_tile, 2),
            pl.BlockSpec((None, D_MODEL, FF_BLK), lambda i, k: (layer, 0, k)),
            pl.BlockSpec((None, FF_BLK, D_MODEL), lambda i, k: (layer, k, 0)),
        ],
        out_specs=_stream_specs(np_tiles, 2, split_out),
        out_shape=_stream_shapes(np_tiles, nt, split_out),
        scratch_shapes=[pltpu.VMEM((TM, D_MODEL), bf16), pltpu.VMEM((TM, D_MODEL), f32)],
        compiler_params=_cparams(("arbitrary", "arbitrary")),
        name="mlp",
    )(y, norm_g, mods, mods, mods, w1, w2)


def _attn_cols(w_in, a_log, dt_bias):
    s2 = N_QKV + H_A * DV_A
    s4 = s2 + 4 * H_A
    pad = jnp.zeros((D_MODEL, LANES - 4 * H_A), f32)
    w_b = jnp.concatenate([w_in[:, N_QKV:s2], w_in[:, s4:], w_in[:, s2:s4], pad], axis=1)
    row_pad = jnp.zeros((LANES - 2 * H_A,), f32)
    al_row = jnp.concatenate([a_log.reshape(-1), row_pad]).reshape(1, LANES)
    dt_row = jnp.concatenate([dt_bias.reshape(-1), row_pad]).reshape(1, LANES)
    return w_b, al_row, dt_row


def _rope_tables(t):
    rows = t // GRID_W
    r = np.repeat(np.arange(rows, dtype=np.float32), GRID_W)
    cc = np.tile(np.arange(GRID_W, dtype=np.float32), rows)
    inv = np.float32(ROPE_BASE) ** (-np.arange(0, ROPE_AXIS, 2, dtype=np.float32) / np.float32(ROPE_AXIS))
    ang_r = (r[:, None] * inv).astype(np.float32)
    ang_c = (cc[:, None] * inv).astype(np.float32)
    cos = np.concatenate([np.cos(ang_r), np.cos(ang_r), np.cos(ang_c), np.cos(ang_c)], axis=-1)
    sin = np.concatenate([-np.sin(ang_r), np.sin(ang_r), -np.sin(ang_c), np.sin(ang_c)], axis=-1)
    cos = np.tile(cos, (1, H_B)).astype(np.float32)
    sin = np.tile(sin, (1, H_B)).astype(np.float32)
    ident_c = np.ones((TM, H_B * HD_B), np.float32)
    ident_s = np.zeros((TM, H_B * HD_B), np.float32)
    return (jnp.asarray(np.concatenate([ident_c, cos], axis=0)),
            jnp.asarray(np.concatenate([ident_s, sin], axis=0)))


def _lru_cols(w_a, b_a, w_x, b_x, lam):
    ncol = D_RNN // LRU_W
    per = LRU_W // LRU_BW

    def bd(w):
        w = w.reshape(ncol, per, LRU_BW, LRU_BW)
        eye = jnp.eye(per, dtype=f32)
        return jnp.einsum('gpij,pq->gpiqj', w, eye).reshape(ncol, LRU_W, LRU_W)

    wl = jnp.concatenate([bd(w_a[0]), bd(w_x[0]), bd(w_a[1]), bd(w_x[1])], axis=-1)
    rows = lambda v: v.reshape(ncol, 1, LRU_W)
    bl = jnp.concatenate([rows(b_a[0]), rows(b_x[0]), rows(b_a[1]), rows(b_x[1])], axis=-1)
    lm = jnp.concatenate([rows(lam[0]), rows(lam[1])], axis=-1)
    return wl, bl, lm


def kernel(x_prompt, x_sample, state_delta, cache_k, cache_v, state_lru, c, c_ctx, ada_w, ada_b, norm1_g, norm2_g, ff_w1, ff_w2, ab_w_in, ab_conv_w, ab_conv_b, dn_a_log, dn_dt_bias, dn_norm_g, attn_q_norm_g, attn_k_norm_g, attn_sink, ab_w_out, c_w_in, c_conv_w, c_conv_b, lru_w_a, lru_b_a, lru_w_x, lru_b_x, lru_lambda, c_w_out):
    b_p, t_p, _ = x_prompt.shape
    b_s, t_s, _ = x_sample.shape
    assert t_s == TM and TM % t_p == 0 and (b_p * t_p) % TM == 0
    np_tok = b_p * t_p
    np_tiles = np_tok // TM
    nkv = KV_B * HD_B

    ys = (x_prompt.reshape(np_tok, D_MODEL), x_sample.reshape(b_s * t_s, D_MODEL))
    nt = np_tiles + b_s * t_s // TM

    rows = -(-(b_s + 1) // 8) * 8
    cond = jnp.zeros((rows, D_MODEL), f32).at[:b_s].set(c).at[b_s].set(c_ctx)
    mods = _adaln(cond, ada_w, ada_b)
    row_of_tile = lambda i: jnp.where(i < np_tiles, b_s, i - np_tiles)

    cos_tab, sin_tab = _rope_tables(t_s)
    ck = cache_k.reshape(cache_k.shape[0], cache_k.shape[1], cache_k.shape[2], nkv)
    cv = cache_v.reshape(ck.shape)

    new_dn, new_k, new_v, new_lru = [], [], [], []
    for l in range(DEPTH):
        j = l // 2
        n1 = norm1_g[l].reshape(1, D_MODEL)
        if l % 2 == 0:
            qkv = _delta_proj(ys, mods, l, row_of_tile, np_tiles, t_p, t_s, n1, ab_w_in[j], ab_conv_w[j],
                              ab_conv_b[j].reshape(1, N_QKV))
            w_b, al_row, dt_row = _attn_cols(ab_w_in[j], dn_a_log[j], dn_dt_bias[j])
            qn_row = jnp.tile(attn_q_norm_g[j], H_B).reshape(1, H_B * HD_B)
            kn_row = jnp.tile(attn_k_norm_g[j], KV_B).reshape(1, nkv)
            gate, qb, kb, vb, gb, kc, vc = _attn_proj(ys, mods, l, row_of_tile, np_tiles, n1, w_b, qn_row, kn_row,
                                                      cos_tab, sin_tab, al_row, dt_row)
            oa_p, s_fin = _delta(qkv, gb, t_p, b_p, DELTA_SEQS_P, 0, None, 0)
            oa_s, _ = _delta(qkv, gb, t_s, b_s, 1, np_tok // t_s, state_delta, j)
            ob_p = _ctx_attn(attn_sink[j], qb, kb, vb, t_p, b_p)
            ob_s = _win_attn(attn_sink[j], qb, kb, vb, ck, cv, j, t_s, b_s, np_tok // t_s)
            dn_row = jnp.tile(dn_norm_g[j], H_A).reshape(1, H_A * DV_A)
            (y,) = _mix_out(ys, mods, l, row_of_tile, np_tiles, oa_p, oa_s, ob_p, ob_s, gate, dn_row, ab_w_out[j],
                            False)
            new_dn.append(s_fin)
            new_k.append(kc.reshape(b_p, t_p, KV_B, HD_B))
            new_v.append(vc.reshape(b_p, t_p, KV_B, HD_B))
        else:
            wl, bl, lm = _lru_cols(lru_w_a[j], lru_b_a[j], lru_w_x[j], lru_b_x[j], lru_lambda[j])
            nseg = TM // t_p
            h0 = jnp.zeros((nt, nseg, 2, D_RNN), f32).at[np_tiles:, 0].set(state_lru[:, j])
            mixed, fin = _lru(ys[0], mods, l, row_of_tile, np_tiles, t_p, t_s, n1, c_w_in[j], c_conv_w[j],
                              c_conv_b[j].reshape(1, D_RNN), wl, bl, lm, h0)
            y = _proj_out(ys[0], mods, l, row_of_tile, mixed, c_w_out[j])
            new_lru.append(fin[:np_tiles].reshape(b_p, 2, D_RNN))
        ys = tuple(_mlp(y, mods, l, row_of_tile, np_tiles, norm2_g[l].reshape(1, D_MODEL), ff_w1, ff_w2,
                        l == DEPTH - 1))

    yp = ys[0].reshape(b_p, t_p, D_MODEL)
    ysm = ys[1].reshape(b_s, t_s, D_MODEL)
    return (yp, ysm, jnp.stack(new_dn, axis=1), jnp.stack(new_k, axis=1), jnp.stack(new_v, axis=1),
            jnp.stack(new_lru, axis=1))
```

```python
import functools
import math

import jax
import jax.numpy as jnp
import numpy as np
from jax import lax
from jax.experimental import pallas as pl
from jax.experimental.pallas import tpu as pltpu

f32 = jnp.float32
bf16 = jnp.bfloat16

D_MODEL = 1024
DEPTH = 4
GRID_W = 64
H_A = 8
DK_A = 64
DV_A = 64
CHUNK = 64
CONV_W = 4
H_B = 8
KV_B = 2
G_B = H_B // KV_B
HD_B = 64
WINDOW = 128
QBLK = 128
ROPE_AXIS = HD_B // 2
ROPE_BASE = 10000.0
D_RNN = D_MODEL
LRU_BLOCKS = 16
LRU_BW = D_RNN // LRU_BLOCKS
RG_C = 8.0
D_FF = 4 * D_MODEL
EPS = 1e-6
NEG_INF = -1e30

TM = 1024
LANES = 128
SUBLANES = 8
LRU_W = 512
LRU_GW = 256
MXU_DIM = 256
N_QKV = 3 * H_A * DK_A
N_ATTN = H_A * DV_A + (H_B + 2 * KV_B) * HD_B + LANES
VMEM_LIMIT = 56 * 1024 * 1024


def _cparams(sem):
    return pltpu.CompilerParams(dimension_semantics=sem, vmem_limit_bytes=VMEM_LIMIT)


def _dot(a, b):
    return jnp.dot(a.astype(bf16), b.astype(bf16), preferred_element_type=f32)


def _dot_nt(a, b):
    return lax.dot_general(a.astype(bf16), b.astype(bf16), (((1,), (1,)), ((), ())),
                           preferred_element_type=f32)


def _dot_tn(a, b):
    return lax.dot_general(a.astype(bf16), b.astype(bf16), (((0,), (0,)), ((), ())),
                           preferred_element_type=f32)


def _sigmoid(x):
    return 1.0 / (1.0 + jnp.exp(-x))


def _softplus(x):
    return jnp.maximum(x, 0.0) + jnp.log1p(jnp.exp(-jnp.abs(x)))


def _modulate(x, g, shift, scale):
    ms = jnp.mean(x * x, axis=-1, keepdims=True)
    y = x * lax.rsqrt(ms + EPS) * g
    return y * (1.0 + scale) + shift


def _split2(x):
    hi = x.astype(bf16)
    lo = (x - hi.astype(f32)).astype(bf16)
    return hi, lo


def _split3(x):
    p1 = x.astype(bf16)
    r1 = x - p1.astype(f32)
    p2 = r1.astype(bf16)
    p3 = (r1 - p2.astype(f32)).astype(bf16)
    return p1, p2, p3


def _group_sum64(v):
    w = v.shape[-1]
    r = lax.broadcasted_iota(jnp.int32, (w, w), 0) >> 6
    c = lax.broadcasted_iota(jnp.int32, (w, w), 1) >> 6
    ones_bd = jnp.where(r == c, 1.0, 0.0).astype(bf16)
    return jnp.dot(v.astype(bf16), ones_bd, preferred_element_type=f32)


CONV_LEFT = CONV_W // 2


def _conv_masks(mk_s, tseq):
    n = mk_s.shape[1]
    pos = lax.broadcasted_iota(jnp.int32, (n, LANES), 0) & (tseq - 1)
    t = 0
    for j in range(CONV_W):
        o = j - CONV_LEFT
        if o != 0:
            mk_s[t] = jnp.where((pos + o >= 0) & (pos + o < tseq), 1.0, 0.0)
            t += 1


def _seq_conv(z, cw_ref, cb_ref, mk_s):
    n, c = z.shape
    acc = z * cw_ref[CONV_LEFT:CONV_LEFT + 1, :] + cb_ref[...]
    t = 0
    for j in range(CONV_W):
        o = j - CONV_LEFT
        if o != 0:
            mask = jnp.tile(mk_s[t], (1, c // LANES)) if c != LANES else mk_s[t]
            acc = acc + (pltpu.roll(z, (-o) % n, axis=0) * mask) * cw_ref[j:j + 1, :]
            t += 1
    return acc


def _adaln_kernel(c_ref, w_ref, b_ref, o_ref):
    c = c_ref[...]
    a = c * _sigmoid(c)
    o_ref[...] = _dot(a, w_ref[...]) + b_ref[...]


def _adaln(cond, ada_w, ada_b):
    rows = cond.shape[0]
    out = pl.pallas_call(
        _adaln_kernel,
        grid=(DEPTH, 6),
        in_specs=[
            pl.BlockSpec((rows, D_MODEL), lambda l, k: (0, 0)),
            pl.BlockSpec((None, D_MODEL, D_MODEL), lambda l, k: (l, 0, k)),
            pl.BlockSpec((None, None, 1, D_MODEL), lambda l, k: (l, k, 0, 0)),
        ],
        out_specs=pl.BlockSpec((None, None, rows, D_MODEL), lambda l, k: (l, k, 0, 0)),
        out_shape=jax.ShapeDtypeStruct((DEPTH, 6, rows, D_MODEL), f32),
        compiler_params=_cparams(("arbitrary", "arbitrary")),
        name="adaln",
    )(cond, ada_w, ada_b.reshape(DEPTH, 6, 1, D_MODEL))
    return out.reshape(DEPTH, 6, rows, 1, D_MODEL)


def _stream_specs(np_tiles, ngrid, split):
    def spec(block_of_tile):
        if ngrid == 1:
            return pl.BlockSpec((TM, D_MODEL), lambda i: (block_of_tile(i), 0))
        return pl.BlockSpec((TM, D_MODEL), lambda i, j: (block_of_tile(i), 0))

    if not split:
        return [spec(lambda i: i)]
    return [spec(lambda i: jnp.minimum(i, np_tiles - 1)), spec(lambda i: jnp.maximum(i - np_tiles, 0))]


def _stream_load(np_tiles, refs):
    if len(refs) == 1:
        return refs[0][...]
    return jnp.where(pl.program_id(0) < np_tiles, refs[0][...], refs[1][...])


def _stream_store(np_tiles, refs, val):
    if len(refs) == 1:
        refs[0][...] = val
        return
    i = pl.program_id(0)

    @pl.when(i < np_tiles)
    def _():
        refs[0][...] = val

    @pl.when(i >= np_tiles)
    def _():
        refs[1][...] = val


def _stream_shapes(np_tiles, nt, split):
    if not split:
        return [jax.ShapeDtypeStruct((nt * TM, D_MODEL), f32)]
    return [jax.ShapeDtypeStruct((np_tiles * TM, D_MODEL), f32),
            jax.ShapeDtypeStruct(((nt - np_tiles) * TM, D_MODEL), f32)]


def _stream_tiles(ys):
    return sum(y.shape[0] for y in ys) // TM


def _mod_spec(layer, k, row_of_tile, ngrid):
    if ngrid == 1:
        return pl.BlockSpec((None, None, None, 1, D_MODEL), lambda i: (layer, k, row_of_tile(i), 0, 0))
    return pl.BlockSpec((None, None, None, 1, D_MODEL), lambda i, j: (layer, k, row_of_tile(i), 0, 0))


QKV_BLK = H_A * DK_A


def _delta_proj_kernel(np_tiles, t_p, t_s, n_y, *refs):
    y_refs = refs[:n_y]
    g_ref, sh_ref, sc_ref, w_ref, cw_ref, cb_ref, o_ref, h_s, mk_s = refs[n_y:]
    n = pl.program_id(1)

    @pl.when(n == 0)
    def _():
        x = _stream_load(np_tiles, y_refs)
        h_s[...] = _modulate(x, g_ref[...], sh_ref[...], sc_ref[...]).astype(bf16)
        _conv_masks(mk_s, jnp.where(pl.program_id(0) < np_tiles, t_p, t_s))

    z = jnp.dot(h_s[...], w_ref[...].astype(bf16), preferred_element_type=f32)
    y = _seq_conv(z, cw_ref, cb_ref, mk_s)
    y = y * _sigmoid(y)

    @pl.when(n < 2)
    def _():
        o_ref[...] = y * lax.rsqrt(_group_sum64(y * y) + EPS)

    @pl.when(n == 2)
    def _():
        o_ref[...] = y


def _delta_proj(ys, mods, layer, row_of_tile, np_tiles, t_p, t_s, norm_g, w_in, conv_w, conv_b):
    nt = _stream_tiles(ys)
    ntok = nt * TM
    kern = functools.partial(_delta_proj_kernel, np_tiles, t_p, t_s, len(ys))
    return pl.pallas_call(
        kern,
        grid=(nt, N_QKV // QKV_BLK),
        in_specs=_stream_specs(np_tiles, 2, len(ys) == 2) + [
            pl.BlockSpec((1, D_MODEL), lambda i, n: (0, 0)),
            _mod_spec(layer, 0, row_of_tile, 2),
            _mod_spec(layer, 1, row_of_tile, 2),
            pl.BlockSpec((None, D_MODEL, QKV_BLK), lambda i, n: (layer // 2, 0, n)),
            pl.BlockSpec((None, CONV_W, QKV_BLK), lambda i, n: (layer // 2, 0, n)),
            pl.BlockSpec((1, QKV_BLK), lambda i, n: (0, n)),
        ],
        out_specs=pl.BlockSpec((TM, QKV_BLK), lambda i, n: (i, n)),
        out_shape=jax.ShapeDtypeStruct((ntok, N_QKV), f32),
        scratch_shapes=[pltpu.VMEM((TM, D_MODEL), bf16), pltpu.VMEM((CONV_W - 1, TM, LANES), f32)],
        compiler_params=_cparams(("arbitrary", "arbitrary")),
        name="delta_proj",
    )(*ys, norm_g, mods, mods, w_in, conv_w, conv_b)


def _rope_swap(x):
    w = x.shape[1]
    lane = lax.broadcasted_iota(jnp.int32, (1, w), 1)
    first = (lane & (ROPE_AXIS - 1)) < ROPE_AXIS // 2
    return jnp.where(first, pltpu.roll(x, w - ROPE_AXIS // 2, axis=1), pltpu.roll(x, ROPE_AXIS // 2, axis=1))


def _attn_proj_kernel(np_tiles, n_y, *refs):
    y_refs = refs[:n_y]
    (g_ref, sh_ref, sc_ref, w_ref, qn_ref, kn_ref, cos_ref, sin_ref, al_ref, dt_ref,
     gate_ref, q_ref, k_ref, v_ref, gb_ref, kc_ref, vc_ref) = refs[n_y:]
    x = _stream_load(np_tiles, y_refs)
    h = _modulate(x, g_ref[...], sh_ref[...], sc_ref[...]).astype(bf16)
    z = jnp.dot(h, w_ref[...].astype(bf16), preferred_element_type=f32)
    nq = H_B * HD_B
    nkv = KV_B * HD_B
    o1 = H_A * DV_A
    o2 = o1 + nq + 2 * nkv
    gz = z[:, :o1]
    gate_ref[...] = gz * _sigmoid(gz)
    q = z[:, o1:o1 + nq]
    k = z[:, o1 + nq:o1 + nq + nkv]
    v = z[:, o1 + nq + nkv:o2]
    v_ref[...] = v
    qn = q * lax.rsqrt(_group_sum64(q * q) * (1.0 / HD_B) + EPS) * qn_ref[...]
    kn = k * lax.rsqrt(_group_sum64(k * k) * (1.0 / HD_B) + EPS) * kn_ref[...]
    cos = cos_ref[...]
    sin = sin_ref[...]
    q_ref[...] = qn * cos + _rope_swap(qn) * sin
    k_ref[...] = kn * cos[:, :nkv] + _rope_swap(kn) * sin[:, :nkv]

    @pl.when(pl.program_id(0) < np_tiles)
    def _():
        kc_ref[...] = kn
        vc_ref[...] = v
    zs = z[:, o2:]
    lane = lax.broadcasted_iota(jnp.int32, (1, LANES), 1)
    g_val = -jnp.exp(al_ref[...]) * _softplus(zs + dt_ref[...])
    gb_ref[...] = jnp.where(lane < 2 * H_A, g_val, jnp.where(lane < 4 * H_A, _sigmoid(zs), 0.0))


def _attn_proj(ys, mods, layer, row_of_tile, np_tiles, norm_g, w_b, qn_row, kn_row, cos_tab, sin_tab,
               al_row, dt_row):
    nt = _stream_tiles(ys)
    ntok = nt * TM
    nq = H_B * HD_B
    nkv = KV_B * HD_B
    tab_idx = lambda i: (jnp.where(i < np_tiles, 0, 1), 0)
    cache_idx = lambda i: (jnp.minimum(i, np_tiles - 1), 0)
    return pl.pallas_call(
        functools.partial(_attn_proj_kernel, np_tiles, len(ys)),
        grid=(nt,),
        in_specs=_stream_specs(np_tiles, 1, len(ys) == 2) + [
            pl.BlockSpec((1, D_MODEL), lambda i: (0, 0)),
            _mod_spec(layer, 0, row_of_tile, 1),
            _mod_spec(layer, 1, row_of_tile, 1),
            pl.BlockSpec((D_MODEL, N_ATTN), lambda i: (0, 0)),
            pl.BlockSpec((1, nq), lambda i: (0, 0)),
            pl.BlockSpec((1, nkv), lambda i: (0, 0)),
            pl.BlockSpec((TM, nq), tab_idx),
            pl.BlockSpec((TM, nq), tab_idx),
            pl.BlockSpec((1, LANES), lambda i: (0, 0)),
            pl.BlockSpec((1, LANES), lambda i: (0, 0)),
        ],
        out_specs=[
            pl.BlockSpec((TM, H_A * DV_A), lambda i: (i, 0)),
            pl.BlockSpec((TM, nq), lambda i: (i, 0)),
            pl.BlockSpec((TM, nkv), lambda i: (i, 0)),
            pl.BlockSpec((TM, nkv), lambda i: (i, 0)),
            pl.BlockSpec((TM, LANES), lambda i: (i, 0)),
            pl.BlockSpec((TM, nkv), cache_idx),
            pl.BlockSpec((TM, nkv), cache_idx),
        ],
        out_shape=[jax.ShapeDtypeStruct((ntok, H_A * DV_A), f32),
                   jax.ShapeDtypeStruct((ntok, nq), f32),
                   jax.ShapeDtypeStruct((ntok, nkv), f32),
                   jax.ShapeDtypeStruct((ntok, nkv), f32),
                   jax.ShapeDtypeStruct((ntok, LANES), f32),
                   jax.ShapeDtypeStruct((np_tiles * TM, nkv), f32),
                   jax.ShapeDtypeStruct((np_tiles * TM, nkv), f32)],
        compiler_params=_cparams(("arbitrary",)),
        name="attn_proj",
    )(*ys, norm_g, mods, mods, w_b, qn_row, kn_row, cos_tab, sin_tab, al_row, dt_row)


INV_BASE = 8
PRE_CHUNKS = 2
DELTA_SEQS_P = 4
HPG = MXU_DIM // DK_A
NLG = H_A // HPG
W_ALL = H_A * DK_A


def _chunk_scan(x, reverse):
    t = x.shape[0]
    r = lax.broadcasted_iota(jnp.int32, (t, 1), 0) & (CHUNK - 1)
    s = 1
    while s < CHUNK:
        if reverse:
            x = x + jnp.where(r + s < CHUNK, pltpu.roll(x, t - s, axis=0), 0.0)
        else:
            x = x + jnp.where(r >= s, pltpu.roll(x, s, axis=0), 0.0)
        s *= 2
    return x


def _delta_kernel(t, nsq, has_s0, *refs):
    if has_s0:
        q_ref, k_ref, v_ref, gb_ref, s0_ref = refs[:5]
        rest = refs[5:]
    else:
        q_ref, k_ref, v_ref, gb_ref = refs[:4]
        s0_ref = None
        rest = refs[4:]
    o_ref, sf_ref, exp_s, kbg_s, qg_s, kd_s, vb_s, egt_s, ti_s, in_s, st_s = rest
    n = t // CHUNK
    scale = DK_A ** -0.5

    gb = gb_ref[...]
    lane = lax.broadcasted_iota(jnp.int32, (1, LANES), 1)
    sc = jnp.where(lane < H_A, _chunk_scan(gb, False), jnp.where(lane < 2 * H_A, _chunk_scan(gb, True), gb))
    er = lax.broadcasted_iota(jnp.int32, (LANES, 4 * W_ALL), 0)
    ec = lax.broadcasted_iota(jnp.int32, (LANES, 4 * W_ALL), 1) >> 6
    expand = jnp.where(er == ec, 1.0, 0.0).astype(bf16)
    rb = min(t, 256)
    for r0 in range(0, nsq * t, rb):
        p1, p2, p3 = _split3(sc[r0:r0 + rb])
        exp_s[r0:r0 + rb, :] = ((jnp.dot(p1, expand, preferred_element_type=f32)
                                 + jnp.dot(p2, expand, preferred_element_type=f32))
                                + jnp.dot(p3, expand, preferred_element_type=f32))

    ri = lax.broadcasted_iota(jnp.int32, (CHUNK, MXU_DIM), 0)
    col_i = lax.broadcasted_iota(jnp.int32, (CHUNK, MXU_DIM), 1) & (CHUNK - 1)
    eye_ls = ri == col_i
    bd_mask = ((lax.broadcasted_iota(jnp.int32, (MXU_DIM, MXU_DIM), 0) >> 6)
               == (lax.broadcasted_iota(jnp.int32, (MXU_DIM, MXU_DIM), 1) >> 6))
    ones_c = jnp.ones((CHUNK, CHUNK), bf16)

    def bd(x):
        xb = x.astype(bf16)
        return jnp.where(bd_mask, jnp.concatenate([xb] * HPG, axis=0), jnp.zeros((), bf16))

    def blk(b):
        sh = b.bit_length() - 1
        return (ri >> sh) == (col_i >> sh)

    def mm3(xh, xl, yh, yl):
        r1 = jnp.dot(jnp.concatenate([xh, xl], axis=0), bd(yh), preferred_element_type=f32)
        r2 = jnp.dot(xh, bd(yl), preferred_element_type=f32)
        return (r1[:CHUNK] + r1[CHUNK:]) + r2

    def pre_body(cp, carry):
        units = [(ci, d, lg) for ci in range(PRE_CHUNKS) for lg in range(NLG) for d in range(2)]
        cidx = [cp * PRE_CHUNKS + ci for ci in range(PRE_CHUNKS)]
        gcs, prods = {}, {}
        for ci, c in enumerate(cidx):
            rows = pl.ds(pl.multiple_of(c * CHUNK, CHUNK), CHUNK)
            q = q_ref[rows, :]
            k = k_ref[rows, :]
            v = v_ref[rows, :]
            kbs = []
            for d in range(2):
                gc = exp_s[rows, d * W_ALL:(d + 1) * W_ALL]
                beta = exp_s[rows, (2 + d) * W_ALL:(3 + d) * W_ALL]
                gtot = gc[CHUNK - 1:CHUNK, :] if d == 0 else gc[0:1, :]
                eg = jnp.exp(gc)
                kb = k * beta
                kbg_s[d, rows, :] = (kb * eg).astype(bf16)
                qg_s[d, rows, :] = (q * (scale * eg)).astype(bf16)
                kd_s[d, rows, :] = (k * jnp.exp(gtot - gc)).astype(bf16)
                vb_s[d, rows, :] = v * beta
                egt_s[d, c] = jnp.zeros((8, W_ALL), f32) + jnp.exp(gtot)
                gcs[ci, d] = gc
                kbs.append(kb)
            for lg in range(NLG):
                cols = slice(lg * MXU_DIM, (lg + 1) * MXU_DIM)
                lhs = jnp.concatenate([kbs[0][:, cols], kbs[1][:, cols], q[:, cols] * scale], axis=0)
                prods[ci, lg] = _dot_nt(lhs, bd(k[:, cols]))
        ps, ys, avs = [], [], []
        for ci, d, lg in units:
            c = cidx[ci]
            cols = slice(lg * MXU_DIM, (lg + 1) * MXU_DIM)
            gc = gcs[ci, d][:, cols]
            prod = prods[ci, lg]
            d1, d2, d3 = _split3(jnp.where(eye_ls, gc, 0.0))
            grow = ((jnp.dot(ones_c, d1, preferred_element_type=f32)
                     + jnp.dot(ones_c, d2, preferred_element_type=f32))
                    + jnp.dot(ones_c, d3, preferred_element_type=f32))
            keep = (ri >= col_i) if d == 0 else (ri <= col_i)
            strict = (ri > col_i) if d == 0 else (ri < col_i)
            decay = jnp.where(keep, jnp.exp(jnp.where(keep, gc - grow, 0.0)), 0.0)
            a = jnp.where(strict, prod[d * CHUNK:(d + 1) * CHUNK] * decay, 0.0)
            in_s[d, c, :, cols] = jnp.where(keep, prod[2 * CHUNK:] * decay, 0.0).astype(bf16)
            ah, al = _split2(a)
            avs.append((ah, al))
            zb = jnp.zeros((), bf16)
            ys.append((jnp.where(blk(INV_BASE), -ah, zb), jnp.where(blk(INV_BASE), -al, zb)))
            ps.append(jnp.where(eye_ls, 1.0, 0.0) - jnp.where(blk(INV_BASE), a, 0.0))
        m = 1
        while m < INV_BASE:
            last = 2 * m >= INV_BASE
            for u in range(len(units)):
                yh, yl = ys[u]
                p = ps[u]
                byh, byl = bd(yh), bd(yl)
                if m == 1:
                    r1 = jnp.dot(jnp.concatenate([yh, yl], axis=0), byh, preferred_element_type=f32)
                    r2 = jnp.dot(yh, byl, preferred_element_type=f32)
                    y2 = (r1[:CHUNK] + r1[CHUNK:]) + r2
                else:
                    ph, plo = _split2(p)
                    if last:
                        r1 = jnp.dot(jnp.concatenate([ph, plo], axis=0), byh, preferred_element_type=f32)
                        r2 = jnp.dot(ph, byl, preferred_element_type=f32)
                        p = p + ((r1[:CHUNK] + r1[CHUNK:]) + r2)
                    else:
                        r1 = jnp.dot(jnp.concatenate([ph, plo, yh, yl], axis=0), byh,
                                     preferred_element_type=f32)
                        r2 = jnp.dot(jnp.concatenate([ph, yh], axis=0), byl, preferred_element_type=f32)
                        p = p + ((r1[:CHUNK] + r1[CHUNK:2 * CHUNK]) + r2[:CHUNK])
                        y2 = (r1[2 * CHUNK:3 * CHUNK] + r1[3 * CHUNK:]) + r2[CHUNK:]
                ps[u] = p
                if not last:
                    ys[u] = _split2(y2)
            m *= 2
        b = INV_BASE
        while b < CHUNK:
            off = blk(2 * b) & jnp.logical_not(blk(b))
            ws, pps = [], []
            zb = jnp.zeros((), bf16)
            for u in range(len(units)):
                pps.append(_split2(ps[u]))
                ws.append(mm3(jnp.where(off, avs[u][0], zb), jnp.where(off, avs[u][1], zb), *pps[u]))
            for u in range(len(units)):
                wh, wl = _split2(ws[u])
                ps[u] = ps[u] - mm3(*pps[u], wh, wl)
            b *= 2
        for u, (ci, d, lg) in enumerate(units):
            ti_s[d, cidx[ci], :, lg * MXU_DIM:(lg + 1) * MXU_DIM] = ps[u].astype(bf16)
        return carry

    lax.fori_loop(0, nsq * n // PRE_CHUNKS, pre_body, 0)

    for sq in range(nsq):
        for d in range(2):
            for lg in range(NLG):
                if has_s0:
                    blocks = []
                    for hh in range(HPG):
                        s_h = s0_ref[sq, d, lg * HPG + hh]
                        z_l = jnp.zeros((DK_A, hh * DV_A), f32)
                        z_r = jnp.zeros((DK_A, (HPG - 1 - hh) * DV_A), f32)
                        parts = ([z_l] if hh > 0 else []) + [s_h] + ([z_r] if hh < HPG - 1 else [])
                        blocks.append(jnp.concatenate(parts, axis=1) if len(parts) > 1 else s_h)
                    st_s[sq, d, lg] = jnp.concatenate(blocks, axis=0)
                else:
                    st_s[sq, d, lg] = jnp.zeros((MXU_DIM, MXU_DIM), f32)
    o_ref[...] = jnp.zeros_like(o_ref)

    def seq_body(j, carry):
        units = [(sq, d, lg) for sq in range(nsq) for lg in range(NLG) for d in range(2)]
        step = [j, n - 1 - j]
        cls = [slice(lg * MXU_DIM, (lg + 1) * MXU_DIM) for lg in range(NLG)]
        chunk = {(sq, d): sq * n + step[d] for sq in range(nsq) for d in range(2)}
        rws = {key: pl.ds(pl.multiple_of(cc * CHUNK, CHUNK), CHUNK) for key, cc in chunk.items()}
        boths, vnbs = [], []
        for sq, d, lg in units:
            r = rws[sq, d]
            lhs = jnp.concatenate([kbg_s[d, r, cls[lg]], qg_s[d, r, cls[lg]]], axis=0)
            boths.append(jnp.dot(lhs, st_s[sq, d, lg].astype(bf16), preferred_element_type=f32))
        for u, (sq, d, lg) in enumerate(units):
            resid = vb_s[d, rws[sq, d], cls[lg]] - boths[u][:CHUNK]
            v_new = jnp.dot(ti_s[d, chunk[sq, d], :, cls[lg]], bd(resid), preferred_element_type=f32)
            vnbs.append(v_new.astype(bf16))
        for u, (sq, d, lg) in enumerate(units):
            r = rws[sq, d]
            c = chunk[sq, d]
            o = boths[u][CHUNK:] + jnp.dot(in_s[d, c, :, cls[lg]], bd(vnbs[u]), preferred_element_type=f32)
            upd = lax.dot_general(kd_s[d, r, cls[lg]], vnbs[u], (((0,), (0,)), ((), ())),
                                  preferred_element_type=f32)
            st_s[sq, d, lg] = st_s[sq, d, lg] * egt_s[d, c, 0:1, cls[lg]] + jnp.where(bd_mask, upd, 0.0)
            o_ref[r, cls[lg]] = o_ref[r, cls[lg]] + o
        return carry

    lax.fori_loop(0, n, seq_body, 0)
    for sq in range(nsq):
        for d in range(2):
            for lg in range(NLG):
                s_fin = st_s[sq, d, lg]
                for hh in range(HPG):
                    sf_ref[sq, d, lg * HPG + hh] = s_fin[hh * DK_A:(hh + 1) * DK_A, hh * DV_A:(hh + 1) * DV_A]


def _delta(qkv, gb, t, nseq, nsq, tok_block0, s0, s0_layer):
    n = t // CHUNK
    tb = nsq * t
    nch = nsq * n
    has_s0 = s0 is not None
    kern = functools.partial(_delta_kernel, t, nsq, has_s0)
    in_specs = [
        pl.BlockSpec((tb, W_ALL), lambda b: (tok_block0 + b, 0)),
        pl.BlockSpec((tb, W_ALL), lambda b: (tok_block0 + b, 1)),
        pl.BlockSpec((tb, W_ALL), lambda b: (tok_block0 + b, 2)),
        pl.BlockSpec((tb, LANES), lambda b: (tok_block0 + b, 0)),
    ]
    args = [qkv, qkv, qkv, gb]
    if has_s0:
        in_specs.append(pl.BlockSpec((nsq, None, 2, H_A, DK_A, DV_A), lambda b: (b, s0_layer, 0, 0, 0, 0)))
        args.append(s0)
    return pl.pallas_call(
        kern,
        grid=(nseq // nsq,),
        in_specs=in_specs,
        out_specs=[
            pl.BlockSpec((tb, W_ALL), lambda b: (b, 0)),
            pl.BlockSpec((nsq, 2, H_A, DK_A, DV_A), lambda b: (b, 0, 0, 0, 0)),
        ],
        out_shape=[jax.ShapeDtypeStruct((nseq * t, W_ALL), f32),
                   jax.ShapeDtypeStruct((nseq, 2, H_A, DK_A, DV_A), f32)],
        scratch_shapes=[
            pltpu.VMEM((tb, 4 * W_ALL), f32),
            pltpu.VMEM((2, tb, W_ALL), bf16),
            pltpu.VMEM((2, tb, W_ALL), bf16),
            pltpu.VMEM((2, tb, W_ALL), bf16),
            pltpu.VMEM((2, tb, W_ALL), f32),
            pltpu.VMEM((2, nch, 8, W_ALL), f32),
            pltpu.VMEM((2, nch, CHUNK, W_ALL), bf16),
            pltpu.VMEM((2, nch, CHUNK, W_ALL), bf16),
            pltpu.VMEM((nsq, 2, NLG, MXU_DIM, MXU_DIM), f32),
        ],
        compiler_params=_cparams(("arbitrary",)),
        name="delta_p" if not has_s0 else "delta_s",
    )(*args)


def _attend(q_all, key_sets, sink_ref):
    nrow = q_all.shape[0]
    scores = []
    for h in range(H_B):
        kv = h // G_B
        q = q_all[:, h * HD_B:(h + 1) * HD_B]
        row = []
        for k_all, _, mask in key_sets:
            s = _dot_nt(q, k_all[:, kv * HD_B:(kv + 1) * HD_B])
            row.append(s if mask is None else jnp.where(mask, s, NEG_INF))
        scores.append(row)
    probs, sink_terms = [], []
    for h in range(H_B):
        sink = jnp.zeros((nrow, 1), f32) + sink_ref[h]
        m = sink
        for s in scores[h]:
            m = jnp.maximum(m, jnp.max(s, axis=-1, keepdims=True))
        probs.append([jnp.exp(s - m).astype(bf16) for s in scores[h]])
        sink_terms.append(jnp.exp(sink - m))
    v_ext = [[jnp.concatenate([v_all[:, kv * HD_B:(kv + 1) * HD_B].astype(bf16),
                               jnp.ones((v_all.shape[0], HD_B), bf16)], axis=1) for kv in range(KV_B)]
             for _, v_all, _ in key_sets]
    outs = []
    for h in range(H_B):
        acc = None
        for p, vs in zip(probs[h], v_ext):
            pv = jnp.dot(p, vs[h // G_B], preferred_element_type=f32)
            acc = pv if acc is None else acc + pv
        outs.append(acc[:, :HD_B] / (acc[:, HD_B:HD_B + 1] + sink_terms[h]))
    return jnp.concatenate(outs, axis=1)


def _ctx_attn_kernel(sink_ref, q_ref, k_ref, v_ref, o_ref):
    scale = HD_B ** -0.5
    o_ref[...] = _attend(q_ref[...] * scale, [(k_ref[...], v_ref[...], None)], sink_ref)


def _ctx_attn(sink, qb, kb, vb, t, nseq):
    nq = H_B * HD_B
    nkv = KV_B * HD_B
    return pl.pallas_call(
        _ctx_attn_kernel,
        grid=(nseq,),
        in_specs=[
            pl.BlockSpec(memory_space=pltpu.SMEM),
            pl.BlockSpec((t, nq), lambda b: (b, 0)),
            pl.BlockSpec((t, nkv), lambda b: (b, 0)),
            pl.BlockSpec((t, nkv), lambda b: (b, 0)),
        ],
        out_specs=pl.BlockSpec((t, nq), lambda b: (b, 0)),
        out_shape=jax.ShapeDtypeStruct((nseq * t, nq), f32),
        compiler_params=_cparams(("arbitrary",)),
        name="ctx_attn",
    )(sink, qb, kb, vb)


def _win_attn_kernel(t, sink_ref, q_ref, k_ref, v_ref, kc_ref, vc_ref, o_ref):
    i = pl.program_id(1)
    scale = HD_B ** -0.5
    span = QBLK + 2 * WINDOW
    start = i * QBLK
    lo = jnp.clip(start - WINDOW, 0, t - span)
    lo = pl.multiple_of(lo, QBLK)
    q_pos = start + lax.broadcasted_iota(jnp.int32, (QBLK, span), 0)
    k_pos = lo + lax.broadcasted_iota(jnp.int32, (QBLK, span), 1)
    valid = jnp.abs(q_pos - k_pos) <= WINDOW
    key_sets = [(k_ref[pl.ds(lo, span), :], v_ref[pl.ds(lo, span), :], valid), (kc_ref[...], vc_ref[...], None)]
    o_ref[...] = _attend(q_ref[...] * scale, key_sets, sink_ref)


def _win_attn(sink, qb, kb, vb, cache_k, cache_v, cache_layer, t, nseq, tok_block0):
    nq = H_B * HD_B
    nkv = KV_B * HD_B
    nb = t // QBLK
    past = cache_k.shape[2]
    kern = functools.partial(_win_attn_kernel, t)
    return pl.pallas_call(
        kern,
        grid=(nseq, nb),
        in_specs=[
            pl.BlockSpec(memory_space=pltpu.SMEM),
            pl.BlockSpec((QBLK, nq), lambda b, i: ((tok_block0 + b) * nb + i, 0)),
            pl.BlockSpec((t, nkv), lambda b, i: (tok_block0 + b, 0)),
            pl.BlockSpec((t, nkv), lambda b, i: (tok_block0 + b, 0)),
            pl.BlockSpec((None, None, past, nkv), lambda b, i: (b, cache_layer, 0, 0)),
            pl.BlockSpec((None, None, past, nkv), lambda b, i: (b, cache_layer, 0, 0)),
        ],
        out_specs=pl.BlockSpec((QBLK, nq), lambda b, i: (b * nb + i, 0)),
        out_shape=jax.ShapeDtypeStruct((nseq * t, nq), f32),
        compiler_params=_cparams(("arbitrary", "arbitrary")),
        name="win_attn",
    )(sink, qb, kb, vb, cache_k, cache_v)


def _mix_out_kernel(np_tiles, n_y, *refs):
    y_refs = refs[:n_y]
    oap_ref, oas_ref, obp_ref, obs_ref, gate_ref, dn_ref, w_ref, gm_ref = refs[n_y:n_y + 8]
    o_refs = refs[n_y + 8:]
    i = pl.program_id(0)
    is_p = i < np_tiles
    oa = jnp.where(is_p, oap_ref[...], oas_ref[...])
    ob = jnp.where(is_p, obp_ref[...], obs_ref[...])
    oan = oa * lax.rsqrt(_group_sum64(oa * oa) * (1.0 / DV_A) + EPS) * dn_ref[...] * gate_ref[...]
    ka = H_A * DV_A
    mix = _dot(oan, w_ref[:ka, :]) + _dot(ob, w_ref[ka:, :])
    _stream_store(np_tiles, o_refs, _stream_load(np_tiles, y_refs) + gm_ref[...] * mix)


def _mix_out(ys, mods, layer, row_of_tile, np_tiles, oa_p, oa_s, ob_p, ob_s, gate, dn_row, w_out, split_out):
    nt = _stream_tiles(ys)
    ka = H_A * DV_A
    p_idx = lambda i: (jnp.minimum(i, np_tiles - 1), 0)
    s_idx = lambda i: (jnp.maximum(i - np_tiles, 0), 0)
    kern = functools.partial(_mix_out_kernel, np_tiles, len(ys))
    return pl.pallas_call(
        kern,
        grid=(nt,),
        in_specs=_stream_specs(np_tiles, 1, len(ys) == 2) + [
            pl.BlockSpec((TM, ka), p_idx),
            pl.BlockSpec((TM, ka), s_idx),
            pl.BlockSpec((TM, ka), p_idx),
            pl.BlockSpec((TM, ka), s_idx),
            pl.BlockSpec((TM, ka), lambda i: (i, 0)),
            pl.BlockSpec((1, ka), lambda i: (0, 0)),
            pl.BlockSpec((None, 2 * ka, D_MODEL), lambda i: (layer // 2, 0, 0)),
            _mod_spec(layer, 2, row_of_tile, 1),
        ],
        out_specs=_stream_specs(np_tiles, 1, split_out),
        out_shape=_stream_shapes(np_tiles, nt, split_out),
        compiler_params=_cparams(("arbitrary",)),
        name="mix_out",
    )(*ys, oa_p, oa_s, ob_p, ob_s, gate, dn_row, w_out, mods)


def _gelu_tanh(x):
    return 0.5 * x * (1.0 + jnp.tanh(math.sqrt(2.0 / math.pi) * (x + 0.044715 * (x * x * x))))


def _lru_kernel(np_tiles, t_p, t_s, y_ref, g_ref, sh_ref, sc_ref, wx_ref, wg_ref, cw_ref, cb_ref,
                wl_ref, bl_ref, lam_ref, h0_ref, o_ref, fin_ref, h_s, x_s, gt_s, hf_s, hb_s, mk_s):
    is_p = pl.program_id(0) < np_tiles
    tseq = jnp.where(is_p, t_p, t_s)

    @pl.when(pl.program_id(1) == 0)
    def _():
        h_s[...] = _modulate(y_ref[...], g_ref[...], sh_ref[...], sc_ref[...]).astype(bf16)
        _conv_masks(mk_s, tseq)

    w = jnp.concatenate([wx_ref[...], wg_ref[...]], axis=1).astype(bf16)
    z = jnp.dot(h_s[...], w, preferred_element_type=f32)
    x = _seq_conv(z[:, :LRU_W], cw_ref, cb_ref, mk_s)
    x_s[...] = x
    ngrp = LRU_W // LRU_GW
    gcols = [slice(g * LRU_GW, (g + 1) * LRU_GW) for g in range(ngrp)]
    for g in range(ngrp):
        gt_s[g] = _dot(x[:, gcols[g]], wl_ref[g]) + bl_ref[g]
    decay = [[RG_C * _softplus(-lam_ref[g, :, d * LRU_GW:(d + 1) * LRU_GW]) for g in range(ngrp)]
             for d in range(2)]
    sub = lax.broadcasted_iota(jnp.int32, (SUBLANES, 1), 0)
    nblk = TM // SUBLANES
    seg_shift = (t_p // SUBLANES).bit_length() - 1

    def body(kf, carry):
        nxt = []
        for d, kk in ((0, kf), (1, nblk - 1 - kf)):
            r0 = pl.multiple_of(kk * SUBLANES, SUBLANES)
            rows = pl.ds(r0, SUBLANES)
            edge = r0 if d == 0 else r0 + SUBLANES
            at_edge = (edge & (tseq - 1)) == 0
            seg = jnp.where(is_p, kk >> seg_shift, 0)
            for g in range(ngrp):
                xb = x_s[rows, gcols[g]]
                gb = gt_s[g, rows, :]
                r = _sigmoid(gb[:, (2 * d) * LRU_GW:(2 * d + 1) * LRU_GW])
                ig = _sigmoid(gb[:, (2 * d + 1) * LRU_GW:(2 * d + 2) * LRU_GW])
                log_a = -(r * decay[d][g])
                a = jnp.exp(log_a)
                u = jnp.sqrt(-jnp.tanh(log_a) * (a * a + 1.0)) * (ig * xb)
                s = 1
                while s < SUBLANES:
                    ok = (sub >= s) if d == 0 else (sub + s < SUBLANES)
                    sh = s if d == 0 else SUBLANES - s
                    u = u + a * jnp.where(ok, pltpu.roll(u, sh, axis=0), 0.0)
                    a = a * jnp.where(ok, pltpu.roll(a, sh, axis=0), 1.0)
                    s *= 2
                c_in = jnp.where(at_edge, h0_ref[seg, d:d + 1, gcols[g]], carry[d * ngrp + g])
                hblk = u + a * c_in
                (hf_s if d == 0 else hb_s)[rows, gcols[g]] = hblk
                nxt.append(hblk[SUBLANES - 1:SUBLANES, :] if d == 0 else hblk[0:1, :])
        return tuple(nxt)

    zero = jnp.zeros((1, LRU_GW), f32)
    lax.fori_loop(0, nblk, body, (zero,) * (2 * ngrp), unroll=4)
    for sg in range(TM // t_p):
        fin_ref[sg, 0:1, :] = hf_s[sg * t_p + t_p - 1:sg * t_p + t_p, :]
        fin_ref[sg, 1:2, :] = hb_s[sg * t_p:sg * t_p + 1, :]
    o_ref[...] = (hf_s[...] + hb_s[...]) * _gelu_tanh(z[:, LRU_W:])


def _lru(y, mods, layer, row_of_tile, np_tiles, t_p, t_s, norm_g, w_in, cw, cb, wl, bl, lam, h0):
    ntok = y.shape[0]
    nt = ntok // TM
    ncol = D_RNN // LRU_W
    ngrp = LRU_W // LRU_GW
    nseg = TM // t_p
    kern = functools.partial(_lru_kernel, np_tiles, t_p, t_s)
    return pl.pallas_call(
        kern,
        grid=(nt, ncol),
        in_specs=[
            pl.BlockSpec((TM, D_MODEL), lambda i, j: (i, 0)),
            pl.BlockSpec((1, D_MODEL), lambda i, j: (0, 0)),
            _mod_spec(layer, 0, row_of_tile, 2),
            _mod_spec(layer, 1, row_of_tile, 2),
            pl.BlockSpec((None, D_MODEL, LRU_W), lambda i, j: (layer // 2, 0, j)),
            pl.BlockSpec((None, D_MODEL, LRU_W), lambda i, j: (layer // 2, 0, ncol + j)),
            pl.BlockSpec((None, CONV_W, LRU_W), lambda i, j: (layer // 2, 0, j)),
            pl.BlockSpec((1, LRU_W), lambda i, j: (0, j)),
            pl.BlockSpec((ngrp, LRU_GW, 4 * LRU_GW), lambda i, j: (j, 0, 0)),
            pl.BlockSpec((ngrp, 1, 4 * LRU_GW), lambda i, j: (j, 0, 0)),
            pl.BlockSpec((ngrp, 1, 2 * LRU_GW), lambda i, j: (j, 0, 0)),
            pl.BlockSpec((None, nseg, 2, LRU_W), lambda i, j: (i, 0, 0, j)),
        ],
        out_specs=[
            pl.BlockSpec((TM, LRU_W), lambda i, j: (i, j)),
            pl.BlockSpec((None, nseg, 2, LRU_W), lambda i, j: (i, 0, 0, j)),
        ],
        out_shape=[jax.ShapeDtypeStruct((ntok, D_RNN), f32),
                   jax.ShapeDtypeStruct((nt, nseg, 2, D_RNN), f32)],
        scratch_shapes=[pltpu.VMEM((TM, D_MODEL), bf16), pltpu.VMEM((TM, LRU_W), f32),
                        pltpu.VMEM((ngrp, TM, 4 * LRU_GW), f32), pltpu.VMEM((TM, LRU_W), f32),
                        pltpu.VMEM((TM, LRU_W), f32), pltpu.VMEM((CONV_W - 1, TM, LANES), f32)],
        compiler_params=_cparams(("arbitrary", "arbitrary")),
        name="lru",
    )(y, norm_g, mods, mods, w_in, w_in, cw, cb, wl, bl, lam, h0)


def _proj_out_kernel(y_ref, a_ref, w_ref, gm_ref, o_ref):
    o_ref[...] = y_ref[...] + gm_ref[...] * _dot(a_ref[...], w_ref[...])


def _proj_out(y, mods, layer, row_of_tile, a, w):
    ntok = y.shape[0]
    nt = ntok // TM
    k = a.shape[1]
    return pl.pallas_call(
        _proj_out_kernel,
        grid=(nt,),
        in_specs=[
            pl.BlockSpec((TM, D_MODEL), lambda i: (i, 0)),
            pl.BlockSpec((TM, k), lambda i: (i, 0)),
            pl.BlockSpec((None, k, D_MODEL), lambda i: (layer // 2, 0, 0)),
            _mod_spec(layer, 2, row_of_tile, 1),
        ],
        out_specs=pl.BlockSpec((TM, D_MODEL), lambda i: (i, 0)),
        out_shape=jax.ShapeDtypeStruct((ntok, D_MODEL), f32),
        compiler_params=_cparams(("arbitrary",)),
        name="proj_out",
    )(y, a, w, mods)


FF_BLK = 1024


def _mlp_kernel(np_tiles, n_out, y_ref, g_ref, sh_ref, sc_ref, gm_ref, w1_ref, w2_ref, *refs):
    o_refs = refs[:n_out]
    h_s, acc_s = refs[n_out:]
    k = pl.program_id(1)

    @pl.when(k == 0)
    def _():
        h_s[...] = _modulate(y_ref[...], g_ref[...], sh_ref[...], sc_ref[...]).astype(bf16)
        acc_s[...] = jnp.zeros_like(acc_s)

    a = jnp.dot(h_s[...], w1_ref[...].astype(bf16), preferred_element_type=f32)
    a = jnp.maximum(a, 0.0)
    acc_s[...] += _dot(a * a, w2_ref[...])

    @pl.when(k == pl.num_programs(1) - 1)
    def _():
        _stream_store(np_tiles, o_refs, y_ref[...] + gm_ref[...] * acc_s[...])


def _mlp(y, mods, layer, row_of_tile, np_tiles, norm_g, w1, w2, split_out):
    nt = y.shape[0] // TM
    return pl.pallas_call(
        functools.partial(_mlp_kernel, np_tiles, 2 if split_out else 1),
        grid=(nt, D_FF // FF_BLK),
        in_specs=[
            pl.BlockSpec((TM, D_MODEL), lambda i, k: (i, 0)),
            pl.BlockSpec((1, D_MODEL), lambda i, k: (0, 0)),
            _mod_spec(layer, 3, row_of_tile, 2),
            _mod_spec(layer, 4, row_of_tile, 2),
            _mod_spec(layer, 5, row_of_tile, 2),
            pl.BlockSpec((None, D_MODEL, FF_BLK), lambda i, k: (layer, 0, k)),
            pl.BlockSpec((None, FF_BLK, D_MODEL), lambda i, k: (layer, k, 0)),
        ],
        out_specs=_stream_specs(np_tiles, 2, split_out),
        out_shape=_stream_shapes(np_tiles, nt, split_out),
        scratch_shapes=[pltpu.VMEM((TM, D_MODEL), bf16), pltpu.VMEM((TM, D_MODEL), f32)],
        compiler_params=_cparams(("arbitrary", "arbitrary")),
        name="mlp",
    )(y, norm_g, mods, mods, mods, w1, w2)


def _attn_cols(w_in, a_log, dt_bias):
    s2 = N_QKV + H_A * DV_A
    s4 = s2 + 4 * H_A
    pad = jnp.zeros((D_MODEL, LANES - 4 * H_A), f32)
    w_b = jnp.concatenate([w_in[:, N_QKV:s2], w_in[:, s4:], w_in[:, s2:s4], pad], axis=1)
    row_pad = jnp.zeros((LANES - 2 * H_A,), f32)
    al_row = jnp.concatenate([a_log.reshape(-1), row_pad]).reshape(1, LANES)
    dt_row = jnp.concatenate([dt_bias.reshape(-1), row_pad]).reshape(1, LANES)
    return w_b, al_row, dt_row


def _rope_tables(t):
    rows = t // GRID_W
    r = np.repeat(np.arange(rows, dtype=np.float32), GRID_W)
    cc = np.tile(np.arange(GRID_W, dtype=np.float32), rows)
    inv = np.float32(ROPE_BASE) ** (-np.arange(0, ROPE_AXIS, 2, dtype=np.float32) / np.float32(ROPE_AXIS))
    ang_r = (r[:, None] * inv).astype(np.float32)
    ang_c = (cc[:, None] * inv).astype(np.float32)
    cos = np.concatenate([np.cos(ang_r), np.cos(ang_r), np.cos(ang_c), np.cos(ang_c)], axis=-1)
    sin = np.concatenate([-np.sin(ang_r), np.sin(ang_r), -np.sin(ang_c), np.sin(ang_c)], axis=-1)
    cos = np.tile(cos, (1, H_B)).astype(np.float32)
    sin = np.tile(sin, (1, H_B)).astype(np.float32)
    ident_c = np.ones((TM, H_B * HD_B), np.float32)
    ident_s = np.zeros((TM, H_B * HD_B), np.float32)
    return (jnp.asarray(np.concatenate([ident_c, cos], axis=0)),
            jnp.asarray(np.concatenate([ident_s, sin], axis=0)))


def _lru_cols(w_a, b_a, w_x, b_x, lam):
    ncol = D_RNN // LRU_GW
    per = LRU_GW // LRU_BW

    def bd(w):
        w = w.reshape(ncol, per, LRU_BW, LRU_BW)
        eye = jnp.eye(per, dtype=f32)
        return jnp.einsum('gpij,pq->gpiqj', w, eye).reshape(ncol, LRU_GW, LRU_GW)

    wl = jnp.concatenate([bd(w_a[0]), bd(w_x[0]), bd(w_a[1]), bd(w_x[1])], axis=-1)
    rows = lambda v: v.reshape(ncol, 1, LRU_GW)
    bl = jnp.concatenate([rows(b_a[0]), rows(b_x[0]), rows(b_a[1]), rows(b_x[1])], axis=-1)
    lm = jnp.concatenate([rows(lam[0]), rows(lam[1])], axis=-1)
    return wl, bl, lm


def kernel(x_prompt, x_sample, state_delta, cache_k, cache_v, state_lru, c, c_ctx, ada_w, ada_b, norm1_g, norm2_g, ff_w1, ff_w2, ab_w_in, ab_conv_w, ab_conv_b, dn_a_log, dn_dt_bias, dn_norm_g, attn_q_norm_g, attn_k_norm_g, attn_sink, ab_w_out, c_w_in, c_conv_w, c_conv_b, lru_w_a, lru_b_a, lru_w_x, lru_b_x, lru_lambda, c_w_out):
    b_p, t_p, _ = x_prompt.shape
    b_s, t_s, _ = x_sample.shape
    assert t_s == TM and TM % t_p == 0 and (b_p * t_p) % TM == 0
    np_tok = b_p * t_p
    np_tiles = np_tok // TM
    nkv = KV_B * HD_B

    ys = (x_prompt.reshape(np_tok, D_MODEL), x_sample.reshape(b_s * t_s, D_MODEL))
    nt = np_tiles + b_s * t_s // TM

    rows = -(-(b_s + 1) // 8) * 8
    cond = jnp.zeros((rows, D_MODEL), f32).at[:b_s].set(c).at[b_s].set(c_ctx)
    mods = _adaln(cond, ada_w, ada_b)
    row_of_tile = lambda i: jnp.where(i < np_tiles, b_s, i - np_tiles)

    cos_tab, sin_tab = _rope_tables(t_s)
    ck = cache_k.reshape(cache_k.shape[0], cache_k.shape[1], cache_k.shape[2], nkv)
    cv = cache_v.reshape(ck.shape)

    new_dn, new_k, new_v, new_lru = [], [], [], []
    for l in range(DEPTH):
        j = l // 2
        n1 = norm1_g[l].reshape(1, D_MODEL)
        if l % 2 == 0:
            qkv = _delta_proj(ys, mods, l, row_of_tile, np_tiles, t_p, t_s, n1, ab_w_in, ab_conv_w,
                              ab_conv_b[j].reshape(1, N_QKV))
            w_b, al_row, dt_row = _attn_cols(ab_w_in[j], dn_a_log[j], dn_dt_bias[j])
            qn_row = jnp.tile(attn_q_norm_g[j], H_B).reshape(1, H_B * HD_B)
            kn_row = jnp.tile(attn_k_norm_g[j], KV_B).reshape(1, nkv)
            gate, qb, kb, vb, gb, kc, vc = _attn_proj(ys, mods, l, row_of_tile, np_tiles, n1, w_b, qn_row, kn_row,
                                                      cos_tab, sin_tab, al_row, dt_row)
            oa_p, s_fin = _delta(qkv, gb, t_p, b_p, DELTA_SEQS_P, 0, None, 0)
            oa_s, _ = _delta(qkv, gb, t_s, b_s, 1, np_tok // t_s, state_delta, j)
            ob_p = _ctx_attn(attn_sink[j], qb, kb, vb, t_p, b_p)
            ob_s = _win_attn(attn_sink[j], qb, kb, vb, ck, cv, j, t_s, b_s, np_tok // t_s)
            dn_row = jnp.tile(dn_norm_g[j], H_A).reshape(1, H_A * DV_A)
            (y,) = _mix_out(ys, mods, l, row_of_tile, np_tiles, oa_p, oa_s, ob_p, ob_s, gate, dn_row, ab_w_out,
                            False)
            new_dn.append(s_fin)
            new_k.append(kc.reshape(b_p, t_p, KV_B, HD_B))
            new_v.append(vc.reshape(b_p, t_p, KV_B, HD_B))
        else:
            wl, bl, lm = _lru_cols(lru_w_a[j], lru_b_a[j], lru_w_x[j], lru_b_x[j], lru_lambda[j])
            nseg = TM // t_p
            h0 = jnp.zeros((nt, nseg, 2, D_RNN), f32).at[np_tiles:, 0].set(state_lru[:, j])
            mixed, fin = _lru(ys[0], mods, l, row_of_tile, np_tiles, t_p, t_s, n1, c_w_in, c_conv_w,
                              c_conv_b[j].reshape(1, D_RNN), wl, bl, lm, h0)
            y = _proj_out(ys[0], mods, l, row_of_tile, mixed, c_w_out)
            new_lru.append(fin[:np_tiles].reshape(b_p, 2, D_RNN))
        ys = tuple(_mlp(y, mods, l, row_of_tile, np_tiles, norm2_g[l].reshape(1, D_MODEL), ff_w1, ff_w2,
                        l == DEPTH - 1))

    yp = ys[0].reshape(b_p, t_p, D_MODEL)
    ysm = ys[1].reshape(b_s, t_s, D_MODEL)
    return (yp, ysm, jnp.stack(new_dn, axis=1), jnp.stack(new_k, axis=1), jnp.stack(new_v, axis=1),
            jnp.stack(new_lru, axis=1))
```

```python
import functools
import math

import jax
import jax.numpy as jnp
import numpy as np
from jax import lax
from jax.experimental import pallas as pl
from jax.experimental.pallas import tpu as pltpu

f32 = jnp.float32
bf16 = jnp.bfloat16

D_MODEL = 1024
DEPTH = 4
GRID_W = 64
H_A = 8
DK_A = 64
DV_A = 64
CHUNK = 64
CONV_W = 4
H_B = 8
KV_B = 2
G_B = H_B // KV_B
HD_B = 64
WINDOW = 128
QBLK = 128
ROPE_AXIS = HD_B // 2
ROPE_BASE = 10000.0
D_RNN = D_MODEL
LRU_BLOCKS = 16
LRU_BW = D_RNN // LRU_BLOCKS
RG_C = 8.0
D_FF = 4 * D_MODEL
EPS = 1e-6
NEG_INF = -1e30

TM = 1024
LANES = 128
SUBLANES = 8
LRU_W = 512
LRU_GW = 256
MXU_DIM = 256
N_QKV = 3 * H_A * DK_A
N_ATTN = H_A * DV_A + (H_B + 2 * KV_B) * HD_B + LANES
VMEM_LIMIT = 56 * 1024 * 1024


def _cparams(sem):
    return pltpu.CompilerParams(dimension_semantics=sem, vmem_limit_bytes=VMEM_LIMIT)


def _dot(a, b):
    return jnp.dot(a.astype(bf16), b.astype(bf16), preferred_element_type=f32)


def _dot_nt(a, b):
    return lax.dot_general(a.astype(bf16), b.astype(bf16), (((1,), (1,)), ((), ())),
                           preferred_element_type=f32)


def _dot_tn(a, b):
    return lax.dot_general(a.astype(bf16), b.astype(bf16), (((0,), (0,)), ((), ())),
                           preferred_element_type=f32)


def _sigmoid(x):
    return 1.0 / (1.0 + jnp.exp(-x))


def _softplus(x):
    return jnp.maximum(x, 0.0) + jnp.log1p(jnp.exp(-jnp.abs(x)))


def _modulate(x, g, shift, scale):
    ms = jnp.mean(x * x, axis=-1, keepdims=True)
    y = x * lax.rsqrt(ms + EPS) * g
    return y * (1.0 + scale) + shift


def _split2(x):
    hi = x.astype(bf16)
    lo = (x - hi.astype(f32)).astype(bf16)
    return hi, lo


def _split3(x):
    p1 = x.astype(bf16)
    r1 = x - p1.astype(f32)
    p2 = r1.astype(bf16)
    p3 = (r1 - p2.astype(f32)).astype(bf16)
    return p1, p2, p3


def _group_sum64(v):
    w = v.shape[-1]
    r = lax.broadcasted_iota(jnp.int32, (w, w), 0) >> 6
    c = lax.broadcasted_iota(jnp.int32, (w, w), 1) >> 6
    ones_bd = jnp.where(r == c, 1.0, 0.0).astype(bf16)
    return jnp.dot(v.astype(bf16), ones_bd, preferred_element_type=f32)


CONV_LEFT = CONV_W // 2


def _conv_masks(mk_s, tseq):
    n = mk_s.shape[1]
    pos = lax.broadcasted_iota(jnp.int32, (n, LANES), 0) & (tseq - 1)
    t = 0
    for j in range(CONV_W):
        o = j - CONV_LEFT
        if o != 0:
            mk_s[t] = jnp.where((pos + o >= 0) & (pos + o < tseq), 1.0, 0.0)
            t += 1


def _seq_conv(z, cw_ref, cb_ref, mk_s):
    n, c = z.shape
    acc = z * cw_ref[CONV_LEFT:CONV_LEFT + 1, :] + cb_ref[...]
    t = 0
    for j in range(CONV_W):
        o = j - CONV_LEFT
        if o != 0:
            mask = jnp.tile(mk_s[t], (1, c // LANES)) if c != LANES else mk_s[t]
            acc = acc + (pltpu.roll(z, (-o) % n, axis=0) * mask) * cw_ref[j:j + 1, :]
            t += 1
    return acc


def _adaln_kernel(c_ref, w_ref, b_ref, o_ref):
    c = c_ref[...]
    a = c * _sigmoid(c)
    o_ref[...] = _dot(a, w_ref[...]) + b_ref[...]


def _adaln(cond, ada_w, ada_b):
    rows = cond.shape[0]
    out = pl.pallas_call(
        _adaln_kernel,
        grid=(DEPTH, 6),
        in_specs=[
            pl.BlockSpec((rows, D_MODEL), lambda l, k: (0, 0)),
            pl.BlockSpec((None, D_MODEL, D_MODEL), lambda l, k: (l, 0, k)),
            pl.BlockSpec((None, None, 1, D_MODEL), lambda l, k: (l, k, 0, 0)),
        ],
        out_specs=pl.BlockSpec((None, None, rows, D_MODEL), lambda l, k: (l, k, 0, 0)),
        out_shape=jax.ShapeDtypeStruct((DEPTH, 6, rows, D_MODEL), f32),
        compiler_params=_cparams(("arbitrary", "arbitrary")),
        name="adaln",
    )(cond, ada_w, ada_b.reshape(DEPTH, 6, 1, D_MODEL))
    return out.reshape(DEPTH, 6, rows, 1, D_MODEL)


def _stream_specs(np_tiles, ngrid, split):
    def spec(block_of_tile):
        if ngrid == 1:
            return pl.BlockSpec((TM, D_MODEL), lambda i: (block_of_tile(i), 0))
        return pl.BlockSpec((TM, D_MODEL), lambda i, j: (block_of_tile(i), 0))

    if not split:
        return [spec(lambda i: i)]
    return [spec(lambda i: jnp.minimum(i, np_tiles - 1)), spec(lambda i: jnp.maximum(i - np_tiles, 0))]


def _stream_load(np_tiles, refs):
    if len(refs) == 1:
        return refs[0][...]
    return jnp.where(pl.program_id(0) < np_tiles, refs[0][...], refs[1][...])


def _stream_store(np_tiles, refs, val):
    if len(refs) == 1:
        refs[0][...] = val
        return
    i = pl.program_id(0)

    @pl.when(i < np_tiles)
    def _():
        refs[0][...] = val

    @pl.when(i >= np_tiles)
    def _():
        refs[1][...] = val


def _stream_shapes(np_tiles, nt, split):
    if not split:
        return [jax.ShapeDtypeStruct((nt * TM, D_MODEL), f32)]
    return [jax.ShapeDtypeStruct((np_tiles * TM, D_MODEL), f32),
            jax.ShapeDtypeStruct(((nt - np_tiles) * TM, D_MODEL), f32)]


def _stream_tiles(ys):
    return sum(y.shape[0] for y in ys) // TM


def _mod_spec(layer, k, row_of_tile, ngrid):
    if ngrid == 1:
        return pl.BlockSpec((None, None, None, 1, D_MODEL), lambda i: (layer, k, row_of_tile(i), 0, 0))
    return pl.BlockSpec((None, None, None, 1, D_MODEL), lambda i, j: (layer, k, row_of_tile(i), 0, 0))


QKV_BLK = H_A * DK_A


def _delta_proj_kernel(np_tiles, t_p, t_s, n_y, *refs):
    y_refs = refs[:n_y]
    g_ref, sh_ref, sc_ref, w_ref, cw_ref, cb_ref, o_ref, h_s, mk_s = refs[n_y:]
    n = pl.program_id(1)

    @pl.when(n == 0)
    def _():
        x = _stream_load(np_tiles, y_refs)
        h_s[...] = _modulate(x, g_ref[...], sh_ref[...], sc_ref[...]).astype(bf16)
        _conv_masks(mk_s, jnp.where(pl.program_id(0) < np_tiles, t_p, t_s))

    w = w_ref[...].astype(bf16)
    rp = TM // 4
    z = jnp.concatenate([jnp.dot(h_s[r0:r0 + rp, :], w, preferred_element_type=f32)
                         for r0 in range(0, TM, rp)], axis=0)
    y = _seq_conv(z, cw_ref, cb_ref, mk_s)
    y = y * _sigmoid(y)

    @pl.when(n < 2)
    def _():
        o_ref[...] = y * lax.rsqrt(_group_sum64(y * y) + EPS)

    @pl.when(n == 2)
    def _():
        o_ref[...] = y


def _delta_proj(ys, mods, layer, row_of_tile, np_tiles, t_p, t_s, norm_g, w_in, conv_w, conv_b):
    nt = _stream_tiles(ys)
    ntok = nt * TM
    kern = functools.partial(_delta_proj_kernel, np_tiles, t_p, t_s, len(ys))
    return pl.pallas_call(
        kern,
        grid=(nt, N_QKV // QKV_BLK),
        in_specs=_stream_specs(np_tiles, 2, len(ys) == 2) + [
            pl.BlockSpec((1, D_MODEL), lambda i, n: (0, 0)),
            _mod_spec(layer, 0, row_of_tile, 2),
            _mod_spec(layer, 1, row_of_tile, 2),
            pl.BlockSpec((None, D_MODEL, QKV_BLK), lambda i, n: (layer // 2, 0, n)),
            pl.BlockSpec((None, CONV_W, QKV_BLK), lambda i, n: (layer // 2, 0, n)),
            pl.BlockSpec((1, QKV_BLK), lambda i, n: (0, n)),
        ],
        out_specs=pl.BlockSpec((TM, QKV_BLK), lambda i, n: (i, n)),
        out_shape=jax.ShapeDtypeStruct((ntok, N_QKV), f32),
        scratch_shapes=[pltpu.VMEM((TM, D_MODEL), bf16), pltpu.VMEM((CONV_W - 1, TM, LANES), f32)],
        compiler_params=_cparams(("arbitrary", "arbitrary")),
        name="delta_proj",
    )(*ys, norm_g, mods, mods, w_in, conv_w, conv_b)


def _rope_swap(x):
    w = x.shape[1]
    lane = lax.broadcasted_iota(jnp.int32, (1, w), 1)
    first = (lane & (ROPE_AXIS - 1)) < ROPE_AXIS // 2
    return jnp.where(first, pltpu.roll(x, w - ROPE_AXIS // 2, axis=1), pltpu.roll(x, ROPE_AXIS // 2, axis=1))


def _attn_proj_kernel(np_tiles, n_y, *refs):
    y_refs = refs[:n_y]
    (g_ref, sh_ref, sc_ref, w_ref, qn_ref, kn_ref, cos_ref, sin_ref, al_ref, dt_ref,
     gate_ref, q_ref, k_ref, v_ref, gb_ref, kc_ref, vc_ref) = refs[n_y:]
    x = _stream_load(np_tiles, y_refs)
    h = _modulate(x, g_ref[...], sh_ref[...], sc_ref[...]).astype(bf16)
    z = jnp.dot(h, w_ref[...].astype(bf16), preferred_element_type=f32)
    nq = H_B * HD_B
    nkv = KV_B * HD_B
    o1 = H_A * DV_A
    o2 = o1 + nq + 2 * nkv
    gz = z[:, :o1]
    gate_ref[...] = gz * _sigmoid(gz)
    q = z[:, o1:o1 + nq]
    k = z[:, o1 + nq:o1 + nq + nkv]
    v = z[:, o1 + nq + nkv:o2]
    v_ref[...] = v
    qn = q * lax.rsqrt(_group_sum64(q * q) * (1.0 / HD_B) + EPS) * qn_ref[...]
    kn = k * lax.rsqrt(_group_sum64(k * k) * (1.0 / HD_B) + EPS) * kn_ref[...]
    cos = cos_ref[...]
    sin = sin_ref[...]
    q_ref[...] = qn * cos + _rope_swap(qn) * sin
    k_ref[...] = kn * cos[:, :nkv] + _rope_swap(kn) * sin[:, :nkv]

    @pl.when(pl.program_id(0) < np_tiles)
    def _():
        kc_ref[...] = kn
        vc_ref[...] = v
    zs = z[:, o2:]
    lane = lax.broadcasted_iota(jnp.int32, (1, LANES), 1)
    g_val = -jnp.exp(al_ref[...]) * _softplus(zs + dt_ref[...])
    gb_ref[...] = jnp.where(lane < 2 * H_A, g_val, jnp.where(lane < 4 * H_A, _sigmoid(zs), 0.0))


def _attn_proj(ys, mods, layer, row_of_tile, np_tiles, norm_g, w_b, qn_row, kn_row, cos_tab, sin_tab,
               al_row, dt_row):
    nt = _stream_tiles(ys)
    ntok = nt * TM
    nq = H_B * HD_B
    nkv = KV_B * HD_B
    tab_idx = lambda i: (jnp.where(i < np_tiles, 0, 1), 0)
    cache_idx = lambda i: (jnp.minimum(i, np_tiles - 1), 0)
    return pl.pallas_call(
        functools.partial(_attn_proj_kernel, np_tiles, len(ys)),
        grid=(nt,),
        in_specs=_stream_specs(np_tiles, 1, len(ys) == 2) + [
            pl.BlockSpec((1, D_MODEL), lambda i: (0, 0)),
            _mod_spec(layer, 0, row_of_tile, 1),
            _mod_spec(layer, 1, row_of_tile, 1),
            pl.BlockSpec((D_MODEL, N_ATTN), lambda i: (0, 0)),
            pl.BlockSpec((1, nq), lambda i: (0, 0)),
            pl.BlockSpec((1, nkv), lambda i: (0, 0)),
            pl.BlockSpec((TM, nq), tab_idx),
            pl.BlockSpec((TM, nq), tab_idx),
            pl.BlockSpec((1, LANES), lambda i: (0, 0)),
            pl.BlockSpec((1, LANES), lambda i: (0, 0)),
        ],
        out_specs=[
            pl.BlockSpec((TM, H_A * DV_A), lambda i: (i, 0)),
            pl.BlockSpec((TM, nq), lambda i: (i, 0)),
            pl.BlockSpec((TM, nkv), lambda i: (i, 0)),
            pl.BlockSpec((TM, nkv), lambda i: (i, 0)),
            pl.BlockSpec((TM, LANES), lambda i: (i, 0)),
            pl.BlockSpec((TM, nkv), cache_idx),
            pl.BlockSpec((TM, nkv), cache_idx),
        ],
        out_shape=[jax.ShapeDtypeStruct((ntok, H_A * DV_A), f32),
                   jax.ShapeDtypeStruct((ntok, nq), f32),
                   jax.ShapeDtypeStruct((ntok, nkv), f32),
                   jax.ShapeDtypeStruct((ntok, nkv), f32),
                   jax.ShapeDtypeStruct((ntok, LANES), f32),
                   jax.ShapeDtypeStruct((np_tiles * TM, nkv), f32),
                   jax.ShapeDtypeStruct((np_tiles * TM, nkv), f32)],
        compiler_params=_cparams(("arbitrary",)),
        name="attn_proj",
    )(*ys, norm_g, mods, mods, w_b, qn_row, kn_row, cos_tab, sin_tab, al_row, dt_row)


INV_BASE = 8
PRE_CHUNKS = 2
DELTA_SEQS_P = 4
HPG = MXU_DIM // DK_A
NLG = H_A // HPG
W_ALL = H_A * DK_A


def _chunk_scan(x, reverse):
    t = x.shape[0]
    r = lax.broadcasted_iota(jnp.int32, (t, 1), 0) & (CHUNK - 1)
    s = 1
    while s < CHUNK:
        if reverse:
            x = x + jnp.where(r + s < CHUNK, pltpu.roll(x, t - s, axis=0), 0.0)
        else:
            x = x + jnp.where(r >= s, pltpu.roll(x, s, axis=0), 0.0)
        s *= 2
    return x


def _delta_kernel(t, nsq, has_s0, *refs):
    if has_s0:
        q_ref, k_ref, v_ref, gb_ref, s0_ref = refs[:5]
        rest = refs[5:]
    else:
        q_ref, k_ref, v_ref, gb_ref = refs[:4]
        s0_ref = None
        rest = refs[4:]
    o_ref, sf_ref, exp_s, kbg_s, qg_s, kd_s, vb_s, egt_s, ti_s, in_s, st_s = rest
    n = t // CHUNK
    scale = DK_A ** -0.5

    gb = gb_ref[...]
    lane = lax.broadcasted_iota(jnp.int32, (1, LANES), 1)
    sc = jnp.where(lane < H_A, _chunk_scan(gb, False), jnp.where(lane < 2 * H_A, _chunk_scan(gb, True), gb))
    er = lax.broadcasted_iota(jnp.int32, (LANES, 4 * W_ALL), 0)
    ec = lax.broadcasted_iota(jnp.int32, (LANES, 4 * W_ALL), 1) >> 6
    expand = jnp.where(er == ec, 1.0, 0.0).astype(bf16)
    rb = min(t, 256)
    for r0 in range(0, nsq * t, rb):
        p1, p2, p3 = _split3(sc[r0:r0 + rb])
        exp_s[r0:r0 + rb, :] = ((jnp.dot(p1, expand, preferred_element_type=f32)
                                 + jnp.dot(p2, expand, preferred_element_type=f32))
                                + jnp.dot(p3, expand, preferred_element_type=f32))

    ri = lax.broadcasted_iota(jnp.int32, (CHUNK, MXU_DIM), 0)
    col_i = lax.broadcasted_iota(jnp.int32, (CHUNK, MXU_DIM), 1) & (CHUNK - 1)
    eye_ls = ri == col_i
    bd_mask = ((lax.broadcasted_iota(jnp.int32, (MXU_DIM, MXU_DIM), 0) >> 6)
               == (lax.broadcasted_iota(jnp.int32, (MXU_DIM, MXU_DIM), 1) >> 6))
    ones_c = jnp.ones((CHUNK, CHUNK), bf16)

    def bd(x):
        xb = x.astype(bf16)
        return jnp.where(bd_mask, jnp.concatenate([xb] * HPG, axis=0), jnp.zeros((), bf16))

    def blk(b):
        sh = b.bit_length() - 1
        return (ri >> sh) == (col_i >> sh)

    def mm3(xh, xl, yh, yl):
        r1 = jnp.dot(jnp.concatenate([xh, xl], axis=0), bd(yh), preferred_element_type=f32)
        r2 = jnp.dot(xh, bd(yl), preferred_element_type=f32)
        return (r1[:CHUNK] + r1[CHUNK:]) + r2

    def mm2(xh, xl, yh):
        r1 = jnp.dot(jnp.concatenate([xh, xl], axis=0), bd(yh), preferred_element_type=f32)
        return r1[:CHUNK] + r1[CHUNK:]

    def pre_body(cp, carry):
        units = [(ci, d, lg) for ci in range(PRE_CHUNKS) for lg in range(NLG) for d in range(2)]
        cidx = [cp * PRE_CHUNKS + ci for ci in range(PRE_CHUNKS)]
        gcs, prods = {}, {}
        for ci, c in enumerate(cidx):
            rows = pl.ds(pl.multiple_of(c * CHUNK, CHUNK), CHUNK)
            q = q_ref[rows, :]
            k = k_ref[rows, :]
            v = v_ref[rows, :]
            kbs = []
            for d in range(2):
                gc = exp_s[rows, d * W_ALL:(d + 1) * W_ALL]
                beta = exp_s[rows, (2 + d) * W_ALL:(3 + d) * W_ALL]
                gtot = gc[CHUNK - 1:CHUNK, :] if d == 0 else gc[0:1, :]
                eg = jnp.exp(gc)
                kb = k * beta
                kbg_s[d, rows, :] = (kb * eg).astype(bf16)
                qg_s[d, rows, :] = (q * (scale * eg)).astype(bf16)
                kd_s[d, rows, :] = (k * jnp.exp(gtot - gc)).astype(bf16)
                vb_s[d, rows, :] = v * beta
                egt_s[d, c] = jnp.zeros((8, W_ALL), f32) + jnp.exp(gtot)
                gcs[ci, d] = gc
                kbs.append(kb)
            for lg in range(NLG):
                cols = slice(lg * MXU_DIM, (lg + 1) * MXU_DIM)
                lhs = jnp.concatenate([kbs[0][:, cols], kbs[1][:, cols], q[:, cols] * scale], axis=0)
                prods[ci, lg] = _dot_nt(lhs, bd(k[:, cols]))
        ps, ys, avs = [], [], []
        for ci, d, lg in units:
            c = cidx[ci]
            cols = slice(lg * MXU_DIM, (lg + 1) * MXU_DIM)
            gc = gcs[ci, d][:, cols]
            prod = prods[ci, lg]
            d1, d2, d3 = _split3(jnp.where(eye_ls, gc, 0.0))
            grow = ((jnp.dot(ones_c, d1, preferred_element_type=f32)
                     + jnp.dot(ones_c, d2, preferred_element_type=f32))
                    + jnp.dot(ones_c, d3, preferred_element_type=f32))
            keep = (ri >= col_i) if d == 0 else (ri <= col_i)
            strict = (ri > col_i) if d == 0 else (ri < col_i)
            decay = jnp.where(keep, jnp.exp(jnp.where(keep, gc - grow, 0.0)), 0.0)
            a = jnp.where(strict, prod[d * CHUNK:(d + 1) * CHUNK] * decay, 0.0)
            in_s[d, c, :, cols] = jnp.where(keep, prod[2 * CHUNK:] * decay, 0.0).astype(bf16)
            ah, al = _split2(a)
            avs.append((ah, al))
            zb = jnp.zeros((), bf16)
            ys.append((jnp.where(blk(INV_BASE), -ah, zb), jnp.where(blk(INV_BASE), -al, zb)))
            ps.append(jnp.where(eye_ls, 1.0, 0.0) - jnp.where(blk(INV_BASE), a, 0.0))
        m = 1
        while m < INV_BASE:
            last = 2 * m >= INV_BASE
            for u in range(len(units)):
                yh, yl = ys[u]
                p = ps[u]
                byh = bd(yh)
                if m == 1:
                    y2 = mm2(yh, yl, yh)
                else:
                    ph, plo = _split2(p)
                    if last:
                        p = p + mm2(ph, plo, yh)
                    else:
                        r1 = jnp.dot(jnp.concatenate([ph, plo, yh, yl], axis=0), byh,
                                     preferred_element_type=f32)
                        p = p + (r1[:CHUNK] + r1[CHUNK:2 * CHUNK])
                        y2 = r1[2 * CHUNK:3 * CHUNK] + r1[3 * CHUNK:]
                ps[u] = p
                if not last:
                    ys[u] = _split2(y2)
            m *= 2
        b = INV_BASE
        while b < CHUNK:
            off = blk(2 * b) & jnp.logical_not(blk(b))
            ws, pps = [], []
            zb = jnp.zeros((), bf16)
            for u in range(len(units)):
                pps.append(_split2(ps[u]))
                ws.append(mm2(jnp.where(off, avs[u][0], zb), jnp.where(off, avs[u][1], zb), pps[u][0]))
            for u in range(len(units)):
                ps[u] = ps[u] - mm2(*pps[u], ws[u].astype(bf16))
            b *= 2
        for u, (ci, d, lg) in enumerate(units):
            ti_s[d, cidx[ci], :, lg * MXU_DIM:(lg + 1) * MXU_DIM] = ps[u].astype(bf16)
        return carry

    lax.fori_loop(0, nsq * n // PRE_CHUNKS, pre_body, 0)

    for sq in range(nsq):
        for d in range(2):
            for lg in range(NLG):
                if has_s0:
                    blocks = []
                    for hh in range(HPG):
                        s_h = s0_ref[sq, d, lg * HPG + hh]
                        z_l = jnp.zeros((DK_A, hh * DV_A), f32)
                        z_r = jnp.zeros((DK_A, (HPG - 1 - hh) * DV_A), f32)
                        parts = ([z_l] if hh > 0 else []) + [s_h] + ([z_r] if hh < HPG - 1 else [])
                        blocks.append(jnp.concatenate(parts, axis=1) if len(parts) > 1 else s_h)
                    st_s[sq, d, lg] = jnp.concatenate(blocks, axis=0)
                else:
                    st_s[sq, d, lg] = jnp.zeros((MXU_DIM, MXU_DIM), f32)
    o_ref[...] = jnp.zeros_like(o_ref)

    def seq_body(j, carry):
        units = [(sq, d, lg) for sq in range(nsq) for lg in range(NLG) for d in range(2)]
        step = [j, n - 1 - j]
        cls = [slice(lg * MXU_DIM, (lg + 1) * MXU_DIM) for lg in range(NLG)]
        chunk = {(sq, d): sq * n + step[d] for sq in range(nsq) for d in range(2)}
        rws = {key: pl.ds(pl.multiple_of(cc * CHUNK, CHUNK), CHUNK) for key, cc in chunk.items()}
        boths, vnbs = [], []
        for sq, d, lg in units:
            r = rws[sq, d]
            lhs = jnp.concatenate([kbg_s[d, r, cls[lg]], qg_s[d, r, cls[lg]]], axis=0)
            boths.append(jnp.dot(lhs, st_s[sq, d, lg].astype(bf16), preferred_element_type=f32))
        for u, (sq, d, lg) in enumerate(units):
            resid = vb_s[d, rws[sq, d], cls[lg]] - boths[u][:CHUNK]
            v_new = jnp.dot(ti_s[d, chunk[sq, d], :, cls[lg]], bd(resid), preferred_element_type=f32)
            vnbs.append(v_new.astype(bf16))
        for u, (sq, d, lg) in enumerate(units):
            r = rws[sq, d]
            c = chunk[sq, d]
            o = boths[u][CHUNK:] + jnp.dot(in_s[d, c, :, cls[lg]], bd(vnbs[u]), preferred_element_type=f32)
            upd = lax.dot_general(kd_s[d, r, cls[lg]], vnbs[u], (((0,), (0,)), ((), ())),
                                  preferred_element_type=f32)
            st_s[sq, d, lg] = st_s[sq, d, lg] * egt_s[d, c, 0:1, cls[lg]] + jnp.where(bd_mask, upd, 0.0)
            o_ref[r, cls[lg]] = o_ref[r, cls[lg]] + o
        return carry

    lax.fori_loop(0, n, seq_body, 0)
    for sq in range(nsq):
        for d in range(2):
            for lg in range(NLG):
                s_fin = st_s[sq, d, lg]
                for hh in range(HPG):
                    sf_ref[sq, d, lg * HPG + hh] = s_fin[hh * DK_A:(hh + 1) * DK_A, hh * DV_A:(hh + 1) * DV_A]


def _delta(qkv, gb, t, nseq, nsq, tok_block0, s0, s0_layer):
    n = t // CHUNK
    tb = nsq * t
    nch = nsq * n
    has_s0 = s0 is not None
    kern = functools.partial(_delta_kernel, t, nsq, has_s0)
    in_specs = [
        pl.BlockSpec((tb, W_ALL), lambda b: (tok_block0 + b, 0)),
        pl.BlockSpec((tb, W_ALL), lambda b: (tok_block0 + b, 1)),
        pl.BlockSpec((tb, W_ALL), lambda b: (tok_block0 + b, 2)),
        pl.BlockSpec((tb, LANES), lambda b: (tok_block0 + b, 0)),
    ]
    args = [qkv, qkv, qkv, gb]
    if has_s0:
        in_specs.append(pl.BlockSpec((nsq, None, 2, H_A, DK_A, DV_A), lambda b: (b, s0_layer, 0, 0, 0, 0)))
        args.append(s0)
    return pl.pallas_call(
        kern,
        grid=(nseq // nsq,),
        in_specs=in_specs,
        out_specs=[
            pl.BlockSpec((tb, W_ALL), lambda b: (b, 0)),
            pl.BlockSpec((nsq, 2, H_A, DK_A, DV_A), lambda b: (b, 0, 0, 0, 0)),
        ],
        out_shape=[jax.ShapeDtypeStruct((nseq * t, W_ALL), f32),
                   jax.ShapeDtypeStruct((nseq, 2, H_A, DK_A, DV_A), f32)],
        scratch_shapes=[
            pltpu.VMEM((tb, 4 * W_ALL), f32),
            pltpu.VMEM((2, tb, W_ALL), bf16),
            pltpu.VMEM((2, tb, W_ALL), bf16),
            pltpu.VMEM((2, tb, W_ALL), bf16),
            pltpu.VMEM((2, tb, W_ALL), f32),
            pltpu.VMEM((2, nch, 8, W_ALL), f32),
            pltpu.VMEM((2, nch, CHUNK, W_ALL), bf16),
            pltpu.VMEM((2, nch, CHUNK, W_ALL), bf16),
            pltpu.VMEM((nsq, 2, NLG, MXU_DIM, MXU_DIM), f32),
        ],
        compiler_params=_cparams(("arbitrary",)),
        name="delta_p" if not has_s0 else "delta_s",
    )(*args)


def _attend(q_all, key_sets, sink_ref):
    nrow = q_all.shape[0]
    scores = []
    for h in range(H_B):
        kv = h // G_B
        q = q_all[:, h * HD_B:(h + 1) * HD_B]
        row = []
        for k_all, _, mask in key_sets:
            s = _dot_nt(q, k_all[:, kv * HD_B:(kv + 1) * HD_B])
            row.append(s if mask is None else jnp.where(mask, s, NEG_INF))
        scores.append(row)
    probs, sink_terms = [], []
    for h in range(H_B):
        sink = jnp.zeros((nrow, 1), f32) + sink_ref[h]
        m = sink
        for s in scores[h]:
            m = jnp.maximum(m, jnp.max(s, axis=-1, keepdims=True))
        probs.append([jnp.exp(s - m).astype(bf16) for s in scores[h]])
        sink_terms.append(jnp.exp(sink - m))
    v_ext = [[jnp.concatenate([v_all[:, kv * HD_B:(kv + 1) * HD_B].astype(bf16),
                               jnp.ones((v_all.shape[0], HD_B), bf16)], axis=1) for kv in range(KV_B)]
             for _, v_all, _ in key_sets]
    outs = []
    for h in range(H_B):
        acc = None
        for p, vs in zip(probs[h], v_ext):
            pv = jnp.dot(p, vs[h // G_B], preferred_element_type=f32)
            acc = pv if acc is None else acc + pv
        outs.append(acc[:, :HD_B] / (acc[:, HD_B:HD_B + 1] + sink_terms[h]))
    return jnp.concatenate(outs, axis=1)


def _ctx_attn_kernel(sink_ref, q_ref, k_ref, v_ref, o_ref):
    scale = HD_B ** -0.5
    o_ref[...] = _attend(q_ref[...] * scale, [(k_ref[...], v_ref[...], None)], sink_ref)


def _ctx_attn(sink, qb, kb, vb, t, nseq):
    nq = H_B * HD_B
    nkv = KV_B * HD_B
    return pl.pallas_call(
        _ctx_attn_kernel,
        grid=(nseq,),
        in_specs=[
            pl.BlockSpec(memory_space=pltpu.SMEM),
            pl.BlockSpec((t, nq), lambda b: (b, 0)),
            pl.BlockSpec((t, nkv), lambda b: (b, 0)),
            pl.BlockSpec((t, nkv), lambda b: (b, 0)),
        ],
        out_specs=pl.BlockSpec((t, nq), lambda b: (b, 0)),
        out_shape=jax.ShapeDtypeStruct((nseq * t, nq), f32),
        compiler_params=_cparams(("arbitrary",)),
        name="ctx_attn",
    )(sink, qb, kb, vb)


def _win_attn_kernel(t, sink_ref, q_ref, k_ref, v_ref, kc_ref, vc_ref, o_ref):
    i = pl.program_id(1)
    scale = HD_B ** -0.5
    span = QBLK + 2 * WINDOW
    start = i * QBLK
    lo = jnp.clip(start - WINDOW, 0, t - span)
    lo = pl.multiple_of(lo, QBLK)
    q_pos = start + lax.broadcasted_iota(jnp.int32, (QBLK, span), 0)
    k_pos = lo + lax.broadcasted_iota(jnp.int32, (QBLK, span), 1)
    valid = jnp.abs(q_pos - k_pos) <= WINDOW
    key_sets = [(k_ref[pl.ds(lo, span), :], v_ref[pl.ds(lo, span), :], valid), (kc_ref[...], vc_ref[...], None)]
    o_ref[...] = _attend(q_ref[...] * scale, key_sets, sink_ref)


def _win_attn(sink, qb, kb, vb, cache_k, cache_v, cache_layer, t, nseq, tok_block0):
    nq = H_B * HD_B
    nkv = KV_B * HD_B
    nb = t // QBLK
    past = cache_k.shape[2]
    kern = functools.partial(_win_attn_kernel, t)
    return pl.pallas_call(
        kern,
        grid=(nseq, nb),
        in_specs=[
            pl.BlockSpec(memory_space=pltpu.SMEM),
            pl.BlockSpec((QBLK, nq), lambda b, i: ((tok_block0 + b) * nb + i, 0)),
            pl.BlockSpec((t, nkv), lambda b, i: (tok_block0 + b, 0)),
            pl.BlockSpec((t, nkv), lambda b, i: (tok_block0 + b, 0)),
            pl.BlockSpec((None, None, past, nkv), lambda b, i: (b, cache_layer, 0, 0)),
            pl.BlockSpec((None, None, past, nkv), lambda b, i: (b, cache_layer, 0, 0)),
        ],
        out_specs=pl.BlockSpec((QBLK, nq), lambda b, i: (b * nb + i, 0)),
        out_shape=jax.ShapeDtypeStruct((nseq * t, nq), f32),
        compiler_params=_cparams(("arbitrary", "arbitrary")),
        name="win_attn",
    )(sink, qb, kb, vb, cache_k, cache_v)


def _mix_out_kernel(np_tiles, n_y, *refs):
    y_refs = refs[:n_y]
    oap_ref, oas_ref, obp_ref, obs_ref, gate_ref, dn_ref, w_ref, gm_ref = refs[n_y:n_y + 8]
    o_refs = refs[n_y + 8:]
    i = pl.program_id(0)
    is_p = i < np_tiles
    oa = jnp.where(is_p, oap_ref[...], oas_ref[...])
    ob = jnp.where(is_p, obp_ref[...], obs_ref[...])
    oan = oa * lax.rsqrt(_group_sum64(oa * oa) * (1.0 / DV_A) + EPS) * dn_ref[...] * gate_ref[...]
    ka = H_A * DV_A
    mix = _dot(oan, w_ref[:ka, :]) + _dot(ob, w_ref[ka:, :])
    _stream_store(np_tiles, o_refs, _stream_load(np_tiles, y_refs) + gm_ref[...] * mix)


def _mix_out(ys, mods, layer, row_of_tile, np_tiles, oa_p, oa_s, ob_p, ob_s, gate, dn_row, w_out, split_out):
    nt = _stream_tiles(ys)
    ka = H_A * DV_A
    p_idx = lambda i: (jnp.minimum(i, np_tiles - 1), 0)
    s_idx = lambda i: (jnp.maximum(i - np_tiles, 0), 0)
    kern = functools.partial(_mix_out_kernel, np_tiles, len(ys))
    return pl.pallas_call(
        kern,
        grid=(nt,),
        in_specs=_stream_specs(np_tiles, 1, len(ys) == 2) + [
            pl.BlockSpec((TM, ka), p_idx),
            pl.BlockSpec((TM, ka), s_idx),
            pl.BlockSpec((TM, ka), p_idx),
            pl.BlockSpec((TM, ka), s_idx),
            pl.BlockSpec((TM, ka), lambda i: (i, 0)),
            pl.BlockSpec((1, ka), lambda i: (0, 0)),
            pl.BlockSpec((None, 2 * ka, D_MODEL), lambda i: (layer // 2, 0, 0)),
            _mod_spec(layer, 2, row_of_tile, 1),
        ],
        out_specs=_stream_specs(np_tiles, 1, split_out),
        out_shape=_stream_shapes(np_tiles, nt, split_out),
        compiler_params=_cparams(("arbitrary",)),
        name="mix_out",
    )(*ys, oa_p, oa_s, ob_p, ob_s, gate, dn_row, w_out, mods)


def _gelu_tanh(x):
    return 0.5 * x * (1.0 + jnp.tanh(math.sqrt(2.0 / math.pi) * (x + 0.044715 * (x * x * x))))


def _lru_kernel(np_tiles, t_p, t_s, y_ref, g_ref, sh_ref, sc_ref, wx_ref, wg_ref, cw_ref, cb_ref,
                wl_ref, bl_ref, lam_ref, h0_ref, o_ref, fin_ref, h_s, x_s, gt_s, hf_s, hb_s, mk_s):
    is_p = pl.program_id(0) < np_tiles
    tseq = jnp.where(is_p, t_p, t_s)

    @pl.when(pl.program_id(1) == 0)
    def _():
        h_s[...] = _modulate(y_ref[...], g_ref[...], sh_ref[...], sc_ref[...]).astype(bf16)
        _conv_masks(mk_s, tseq)

    w = jnp.concatenate([wx_ref[...], wg_ref[...]], axis=1).astype(bf16)
    z = jnp.dot(h_s[...], w, preferred_element_type=f32)
    x = _seq_conv(z[:, :LRU_W], cw_ref, cb_ref, mk_s)
    x_s[...] = x
    ngrp = LRU_W // LRU_GW
    gcols = [slice(g * LRU_GW, (g + 1) * LRU_GW) for g in range(ngrp)]
    for g in range(ngrp):
        gt_s[g] = _dot(x[:, gcols[g]], wl_ref[g]) + bl_ref[g]
    decay = [[RG_C * _softplus(-lam_ref[g, :, d * LRU_GW:(d + 1) * LRU_GW]) for g in range(ngrp)]
             for d in range(2)]
    sub = lax.broadcasted_iota(jnp.int32, (SUBLANES, 1), 0)
    nblk = TM // SUBLANES
    seg_shift = (t_p // SUBLANES).bit_length() - 1

    def body(kf, carry):
        nxt = []
        for d, kk in ((0, kf), (1, nblk - 1 - kf)):
            r0 = pl.multiple_of(kk * SUBLANES, SUBLANES)
            rows = pl.ds(r0, SUBLANES)
            edge = r0 if d == 0 else r0 + SUBLANES
            at_edge = (edge & (tseq - 1)) == 0
            seg = jnp.where(is_p, kk >> seg_shift, 0)
            for g in range(ngrp):
                xb = x_s[rows, gcols[g]]
                gb = gt_s[g, rows, :]
                r = _sigmoid(gb[:, (2 * d) * LRU_GW:(2 * d + 1) * LRU_GW])
                ig = _sigmoid(gb[:, (2 * d + 1) * LRU_GW:(2 * d + 2) * LRU_GW])
                log_a = -(r * decay[d][g])
                a = jnp.exp(log_a)
                u = jnp.sqrt(-jnp.tanh(log_a) * (a * a + 1.0)) * (ig * xb)
                s = 1
                while s < SUBLANES:
                    ok = (sub >= s) if d == 0 else (sub + s < SUBLANES)
                    sh = s if d == 0 else SUBLANES - s
                    u = u + a * jnp.where(ok, pltpu.roll(u, sh, axis=0), 0.0)
                    a = a * jnp.where(ok, pltpu.roll(a, sh, axis=0), 1.0)
                    s *= 2
                c_in = jnp.where(at_edge, h0_ref[seg, d:d + 1, gcols[g]], carry[d * ngrp + g])
                hblk = u + a * c_in
                (hf_s if d == 0 else hb_s)[rows, gcols[g]] = hblk
                nxt.append(hblk[SUBLANES - 1:SUBLANES, :] if d == 0 else hblk[0:1, :])
        return tuple(nxt)

    zero = jnp.zeros((1, LRU_GW), f32)
    lax.fori_loop(0, nblk, body, (zero,) * (2 * ngrp), unroll=4)
    for sg in range(TM // t_p):
        fin_ref[sg, 0:1, :] = hf_s[sg * t_p + t_p - 1:sg * t_p + t_p, :]
        fin_ref[sg, 1:2, :] = hb_s[sg * t_p:sg * t_p + 1, :]
    o_ref[...] = (hf_s[...] + hb_s[...]) * _gelu_tanh(z[:, LRU_W:])


def _lru(y, mods, layer, row_of_tile, np_tiles, t_p, t_s, norm_g, w_in, cw, cb, wl, bl, lam, h0):
    ntok = y.shape[0]
    nt = ntok // TM
    ncol = D_RNN // LRU_W
    ngrp = LRU_W // LRU_GW
    nseg = TM // t_p
    kern = functools.partial(_lru_kernel, np_tiles, t_p, t_s)
    return pl.pallas_call(
        kern,
        grid=(nt, ncol),
        in_specs=[
            pl.BlockSpec((TM, D_MODEL), lambda i, j: (i, 0)),
            pl.BlockSpec((1, D_MODEL), lambda i, j: (0, 0)),
            _mod_spec(layer, 0, row_of_tile, 2),
            _mod_spec(layer, 1, row_of_tile, 2),
            pl.BlockSpec((None, D_MODEL, LRU_W), lambda i, j: (layer // 2, 0, j)),
            pl.BlockSpec((None, D_MODEL, LRU_W), lambda i, j: (layer // 2, 0, ncol + j)),
            pl.BlockSpec((None, CONV_W, LRU_W), lambda i, j: (layer // 2, 0, j)),
            pl.BlockSpec((1, LRU_W), lambda i, j: (0, j)),
            pl.BlockSpec((ngrp, LRU_GW, 4 * LRU_GW), lambda i, j: (j, 0, 0)),
            pl.BlockSpec((ngrp, 1, 4 * LRU_GW), lambda i, j: (j, 0, 0)),
            pl.BlockSpec((ngrp, 1, 2 * LRU_GW), lambda i, j: (j, 0, 0)),
            pl.BlockSpec((None, nseg, 2, LRU_W), lambda i, j: (i, 0, 0, j)),
        ],
        out_specs=[
            pl.BlockSpec((TM, LRU_W), lambda i, j: (i, j)),
            pl.BlockSpec((None, nseg, 2, LRU_W), lambda i, j: (i, 0, 0, j)),
        ],
        out_shape=[jax.ShapeDtypeStruct((ntok, D_RNN), f32),
                   jax.ShapeDtypeStruct((nt, nseg, 2, D_RNN), f32)],
        scratch_shapes=[pltpu.VMEM((TM, D_MODEL), bf16), pltpu.VMEM((TM, LRU_W), f32),
                        pltpu.VMEM((ngrp, TM, 4 * LRU_GW), f32), pltpu.VMEM((TM, LRU_W), f32),
                        pltpu.VMEM((TM, LRU_W), f32), pltpu.VMEM((CONV_W - 1, TM, LANES), f32)],
        compiler_params=_cparams(("arbitrary", "arbitrary")),
        name="lru",
    )(y, norm_g, mods, mods, w_in, w_in, cw, cb, wl, bl, lam, h0)


def _proj_out_kernel(y_ref, a_ref, w_ref, gm_ref, o_ref):
    o_ref[...] = y_ref[...] + gm_ref[...] * _dot(a_ref[...], w_ref[...])


def _proj_out(y, mods, layer, row_of_tile, a, w):
    ntok = y.shape[0]
    nt = ntok // TM
    k = a.shape[1]
    return pl.pallas_call(
        _proj_out_kernel,
        grid=(nt,),
        in_specs=[
            pl.BlockSpec((TM, D_MODEL), lambda i: (i, 0)),
            pl.BlockSpec((TM, k), lambda i: (i, 0)),
            pl.BlockSpec((None, k, D_MODEL), lambda i: (layer // 2, 0, 0)),
            _mod_spec(layer, 2, row_of_tile, 1),
        ],
        out_specs=pl.BlockSpec((TM, D_MODEL), lambda i: (i, 0)),
        out_shape=jax.ShapeDtypeStruct((ntok, D_MODEL), f32),
        compiler_params=_cparams(("arbitrary",)),
        name="proj_out",
    )(y, a, w, mods)


FF_BLK = 1024


def _mlp_kernel(np_tiles, n_out, y_ref, g_ref, sh_ref, sc_ref, gm_ref, w1_ref, w2_ref, *refs):
    o_refs = refs[:n_out]
    h_s, acc_s = refs[n_out:]
    k = pl.program_id(1)

    @pl.when(k == 0)
    def _():
        h_s[...] = _modulate(y_ref[...], g_ref[...], sh_ref[...], sc_ref[...]).astype(bf16)
        acc_s[...] = jnp.zeros_like(acc_s)

    a = jnp.dot(h_s[...], w1_ref[...].astype(bf16), preferred_element_type=f32)
    a = jnp.maximum(a, 0.0)
    acc_s[...] += _dot(a * a, w2_ref[...])

    @pl.when(k == pl.num_programs(1) - 1)
    def _():
        _stream_store(np_tiles, o_refs, y_ref[...] + gm_ref[...] * acc_s[...])


def _mlp(y, mods, layer, row_of_tile, np_tiles, norm_g, w1, w2, split_out):
    nt = y.shape[0] // TM
    return pl.pallas_call(
        functools.partial(_mlp_kernel, np_tiles, 2 if split_out else 1),
        grid=(nt, D_FF // FF_BLK),
        in_specs=[
            pl.BlockSpec((TM, D_MODEL), lambda i, k: (i, 0)),
            pl.BlockSpec((1, D_MODEL), lambda i, k: (0, 0)),
            _mod_spec(layer, 3, row_of_tile, 2),
            _mod_spec(layer, 4, row_of_tile, 2),
            _mod_spec(layer, 5, row_of_tile, 2),
            pl.BlockSpec((None, D_MODEL, FF_BLK), lambda i, k: (layer, 0, k)),
            pl.BlockSpec((None, FF_BLK, D_MODEL), lambda i, k: (layer, k, 0)),
        ],
        out_specs=_stream_specs(np_tiles, 2, split_out),
        out_shape=_stream_shapes(np_tiles, nt, split_out),
        scratch_shapes=[pltpu.VMEM((TM, D_MODEL), bf16), pltpu.VMEM((TM, D_MODEL), f32)],
        compiler_params=_cparams(("arbitrary", "arbitrary")),
        name="mlp",
    )(y, norm_g, mods, mods, mods, w1, w2)


def _attn_cols(w_in, a_log, dt_bias):
    s2 = N_QKV + H_A * DV_A
    s4 = s2 + 4 * H_A
    pad = jnp.zeros((D_MODEL, LANES - 4 * H_A), f32)
    w_b = jnp.concatenate([w_in[:, N_QKV:s2], w_in[:, s4:], w_in[:, s2:s4], pad], axis=1)
    row_pad = jnp.zeros((LANES - 2 * H_A,), f32)
    al_row = jnp.concatenate([a_log.reshape(-1), row_pad]).reshape(1, LANES)
    dt_row = jnp.concatenate([dt_bias.reshape(-1), row_pad]).reshape(1, LANES)
    return w_b, al_row, dt_row


def _rope_tables(t):
    rows = t // GRID_W
    r = np.repeat(np.arange(rows, dtype=np.float32), GRID_W)
    cc = np.tile(np.arange(GRID_W, dtype=np.float32), rows)
    inv = np.float32(ROPE_BASE) ** (-np.arange(0, ROPE_AXIS, 2, dtype=np.float32) / np.float32(ROPE_AXIS))
    ang_r = (r[:, None] * inv).astype(np.float32)
    ang_c = (cc[:, None] * inv).astype(np.float32)
    cos = np.concatenate([np.cos(ang_r), np.cos(ang_r), np.cos(ang_c), np.cos(ang_c)], axis=-1)
    sin = np.concatenate([-np.sin(ang_r), np.sin(ang_r), -np.sin(ang_c), np.sin(ang_c)], axis=-1)
    cos = np.tile(cos, (1, H_B)).astype(np.float32)
    sin = np.tile(sin, (1, H_B)).astype(np.float32)
    ident_c = np.ones((TM, H_B * HD_B), np.float32)
    ident_s = np.zeros((TM, H_B * HD_B), np.float32)
    return (jnp.asarray(np.concatenate([ident_c, cos], axis=0)),
            jnp.asarray(np.concatenate([ident_s, sin], axis=0)))


def _lru_cols(w_a, b_a, w_x, b_x, lam):
    ncol = D_RNN // LRU_GW
    per = LRU_GW // LRU_BW

    def bd(w):
        w = w.reshape(ncol, per, LRU_BW, LRU_BW)
        eye = jnp.eye(per, dtype=f32)
        return jnp.einsum('gpij,pq->gpiqj', w, eye).reshape(ncol, LRU_GW, LRU_GW)

    wl = jnp.concatenate([bd(w_a[0]), bd(w_x[0]), bd(w_a[1]), bd(w_x[1])], axis=-1)
    rows = lambda v: v.reshape(ncol, 1, LRU_GW)
    bl = jnp.concatenate([rows(b_a[0]), rows(b_x[0]), rows(b_a[1]), rows(b_x[1])], axis=-1)
    lm = jnp.concatenate([rows(lam[0]), rows(lam[1])], axis=-1)
    return wl, bl, lm


def kernel(x_prompt, x_sample, state_delta, cache_k, cache_v, state_lru, c, c_ctx, ada_w, ada_b, norm1_g, norm2_g, ff_w1, ff_w2, ab_w_in, ab_conv_w, ab_conv_b, dn_a_log, dn_dt_bias, dn_norm_g, attn_q_norm_g, attn_k_norm_g, attn_sink, ab_w_out, c_w_in, c_conv_w, c_conv_b, lru_w_a, lru_b_a, lru_w_x, lru_b_x, lru_lambda, c_w_out):
    b_p, t_p, _ = x_prompt.shape
    b_s, t_s, _ = x_sample.shape
    assert t_s == TM and TM % t_p == 0 and (b_p * t_p) % TM == 0
    np_tok = b_p * t_p
    np_tiles = np_tok // TM
    nkv = KV_B * HD_B

    ys = (x_prompt.reshape(np_tok, D_MODEL), x_sample.reshape(b_s * t_s, D_MODEL))
    nt = np_tiles + b_s * t_s // TM

    rows = -(-(b_s + 1) // 8) * 8
    cond = jnp.zeros((rows, D_MODEL), f32).at[:b_s].set(c).at[b_s].set(c_ctx)
    mods = _adaln(cond, ada_w, ada_b)
    row_of_tile = lambda i: jnp.where(i < np_tiles, b_s, i - np_tiles)

    cos_tab, sin_tab = _rope_tables(t_s)
    ck = cache_k.reshape(cache_k.shape[0], cache_k.shape[1], cache_k.shape[2], nkv)
    cv = cache_v.reshape(ck.shape)

    new_dn, new_k, new_v, new_lru = [], [], [], []
    for l in range(DEPTH):
        j = l // 2
        n1 = norm1_g[l].reshape(1, D_MODEL)
        if l % 2 == 0:
            qkv = _delta_proj(ys, mods, l, row_of_tile, np_tiles, t_p, t_s, n1, ab_w_in, ab_conv_w,
                              ab_conv_b[j].reshape(1, N_QKV))
            w_b, al_row, dt_row = _attn_cols(ab_w_in[j], dn_a_log[j], dn_dt_bias[j])
            qn_row = jnp.tile(attn_q_norm_g[j], H_B).reshape(1, H_B * HD_B)
            kn_row = jnp.tile(attn_k_norm_g[j], KV_B).reshape(1, nkv)
            gate, qb, kb, vb, gb, kc, vc = _attn_proj(ys, mods, l, row_of_tile, np_tiles, n1, w_b, qn_row, kn_row,
                                                      cos_tab, sin_tab, al_row, dt_row)
            oa_p, s_fin = _delta(qkv, gb, t_p, b_p, DELTA_SEQS_P, 0, None, 0)
            oa_s, _ = _delta(qkv, gb, t_s, b_s, 1, np_tok // t_s, state_delta, j)
            ob_p = _ctx_attn(attn_sink[j], qb, kb, vb, t_p, b_p)
            ob_s = _win_attn(attn_sink[j], qb, kb, vb, ck, cv, j, t_s, b_s, np_tok // t_s)
            dn_row = jnp.tile(dn_norm_g[j], H_A).reshape(1, H_A * DV_A)
            (y,) = _mix_out(ys, mods, l, row_of_tile, np_tiles, oa_p, oa_s, ob_p, ob_s, gate, dn_row, ab_w_out,
                            False)
            new_dn.append(s_fin)
            new_k.append(kc.reshape(b_p, t_p, KV_B, HD_B))
            new_v.append(vc.reshape(b_p, t_p, KV_B, HD_B))
        else:
            wl, bl, lm = _lru_cols(lru_w_a[j], lru_b_a[j], lru_w_x[j], lru_b_x[j], lru_lambda[j])
            nseg = TM // t_p
            h0 = jnp.zeros((nt, nseg, 2, D_RNN), f32).at[np_tiles:, 0].set(state_lru[:, j])
            mixed, fin = _lru(ys[0], mods, l, row_of_tile, np_tiles, t_p, t_s, n1, c_w_in, c_conv_w,
                              c_conv_b[j].reshape(1, D_RNN), wl, bl, lm, h0)
            y = _proj_out(ys[0], mods, l, row_of_tile, mixed, c_w_out)
            new_lru.append(fin[:np_tiles].reshape(b_p, 2, D_RNN))
        ys = tuple(_mlp(y, mods, l, row_of_tile, np_tiles, norm2_g[l].reshape(1, D_MODEL), ff_w1, ff_w2,
                        l == DEPTH - 1))

    yp = ys[0].reshape(b_p, t_p, D_MODEL)
    ysm = ys[1].reshape(b_s, t_s, D_MODEL)
    return (yp, ysm, jnp.stack(new_dn, axis=1), jnp.stack(new_k, axis=1), jnp.stack(new_v, axis=1),
            jnp.stack(new_lru, axis=1))
```

```python
import functools
import math

import jax
import jax.numpy as jnp
import numpy as np
from jax import lax
from jax.experimental import pallas as pl
from jax.experimental.pallas import tpu as pltpu

f32 = jnp.float32
bf16 = jnp.bfloat16

D_MODEL = 1024
DEPTH = 4
GRID_W = 64
H_A = 8
DK_A = 64
DV_A = 64
CHUNK = 64
CONV_W = 4
H_B = 8
KV_B = 2
G_B = H_B // KV_B
HD_B = 64
WINDOW = 128
QBLK = 128
ROPE_AXIS = HD_B // 2
ROPE_BASE = 10000.0
D_RNN = D_MODEL
LRU_BLOCKS = 16
LRU_BW = D_RNN // LRU_BLOCKS
RG_C = 8.0
D_FF = 4 * D_MODEL
EPS = 1e-6
NEG_INF = -1e30

TM = 1024
LANES = 128
SUBLANES = 8
LRU_W = 512
LRU_GW = 256
MXU_DIM = 256
N_QKV = 3 * H_A * DK_A
N_ATTN = H_A * DV_A + (H_B + 2 * KV_B) * HD_B + LANES
VMEM_LIMIT = 56 * 1024 * 1024


def _cparams(sem):
    return pltpu.CompilerParams(dimension_semantics=sem, vmem_limit_bytes=VMEM_LIMIT)


def _dot(a, b):
    return jnp.dot(a.astype(bf16), b.astype(bf16), preferred_element_type=f32)


def _dot_nt(a, b):
    return lax.dot_general(a.astype(bf16), b.astype(bf16), (((1,), (1,)), ((), ())),
                           preferred_element_type=f32)


def _dot_tn(a, b):
    return lax.dot_general(a.astype(bf16), b.astype(bf16), (((0,), (0,)), ((), ())),
                           preferred_element_type=f32)


def _sigmoid(x):
    return 1.0 / (1.0 + jnp.exp(-x))


def _softplus(x):
    return jnp.maximum(x, 0.0) + jnp.log1p(jnp.exp(-jnp.abs(x)))


def _modulate(x, g, shift, scale):
    ms = jnp.mean(x * x, axis=-1, keepdims=True)
    y = x * lax.rsqrt(ms + EPS) * g
    return y * (1.0 + scale) + shift


def _split3(x):
    p1 = x.astype(bf16)
    r1 = x - p1.astype(f32)
    p2 = r1.astype(bf16)
    p3 = (r1 - p2.astype(f32)).astype(bf16)
    return p1, p2, p3


def _group_sum64(v):
    w = v.shape[-1]
    r = lax.broadcasted_iota(jnp.int32, (w, w), 0) >> 6
    c = lax.broadcasted_iota(jnp.int32, (w, w), 1) >> 6
    ones_bd = jnp.where(r == c, 1.0, 0.0).astype(bf16)
    return jnp.dot(v.astype(bf16), ones_bd, preferred_element_type=f32)


CONV_LEFT = CONV_W // 2


def _conv_masks(mk_s, tseq):
    n = mk_s.shape[1]
    pos = lax.broadcasted_iota(jnp.int32, (n, LANES), 0) & (tseq - 1)
    t = 0
    for j in range(CONV_W):
        o = j - CONV_LEFT
        if o != 0:
            mk_s[t] = jnp.where((pos + o >= 0) & (pos + o < tseq), 1.0, 0.0)
            t += 1


def _seq_conv(z, cw_ref, cb_ref, mk_s):
    n, c = z.shape
    acc = z * cw_ref[CONV_LEFT:CONV_LEFT + 1, :] + cb_ref[...]
    t = 0
    for j in range(CONV_W):
        o = j - CONV_LEFT
        if o != 0:
            mask = jnp.tile(mk_s[t], (1, c // LANES)) if c != LANES else mk_s[t]
            acc = acc + (pltpu.roll(z, (-o) % n, axis=0) * mask) * cw_ref[j:j + 1, :]
            t += 1
    return acc


def _adaln_kernel(c_ref, w_ref, b_ref, o_ref):
    c = c_ref[...]
    a = c * _sigmoid(c)
    o_ref[...] = _dot(a, w_ref[...]) + b_ref[...]


def _adaln(cond, ada_w, ada_b):
    rows = cond.shape[0]
    out = pl.pallas_call(
        _adaln_kernel,
        grid=(DEPTH, 6),
        in_specs=[
            pl.BlockSpec((rows, D_MODEL), lambda l, k: (0, 0)),
            pl.BlockSpec((None, D_MODEL, D_MODEL), lambda l, k: (l, 0, k)),
            pl.BlockSpec((None, None, 1, D_MODEL), lambda l, k: (l, k, 0, 0)),
        ],
        out_specs=pl.BlockSpec((None, None, rows, D_MODEL), lambda l, k: (l, k, 0, 0)),
        out_shape=jax.ShapeDtypeStruct((DEPTH, 6, rows, D_MODEL), f32),
        compiler_params=_cparams(("arbitrary", "arbitrary")),
        name="adaln",
    )(cond, ada_w, ada_b.reshape(DEPTH, 6, 1, D_MODEL))
    return out.reshape(DEPTH, 6, rows, 1, D_MODEL)


def _stream_specs(np_tiles, ngrid, split):
    def spec(block_of_tile):
        if ngrid == 1:
            return pl.BlockSpec((TM, D_MODEL), lambda i: (block_of_tile(i), 0))
        return pl.BlockSpec((TM, D_MODEL), lambda i, j: (block_of_tile(i), 0))

    if not split:
        return [spec(lambda i: i)]
    return [spec(lambda i: jnp.minimum(i, np_tiles - 1)), spec(lambda i: jnp.maximum(i - np_tiles, 0))]


def _stream_load(np_tiles, refs):
    if len(refs) == 1:
        return refs[0][...]
    return jnp.where(pl.program_id(0) < np_tiles, refs[0][...], refs[1][...])


def _stream_store(np_tiles, refs, val):
    if len(refs) == 1:
        refs[0][...] = val
        return
    i = pl.program_id(0)

    @pl.when(i < np_tiles)
    def _():
        refs[0][...] = val

    @pl.when(i >= np_tiles)
    def _():
        refs[1][...] = val


def _stream_shapes(np_tiles, nt, split):
    if not split:
        return [jax.ShapeDtypeStruct((nt * TM, D_MODEL), f32)]
    return [jax.ShapeDtypeStruct((np_tiles * TM, D_MODEL), f32),
            jax.ShapeDtypeStruct(((nt - np_tiles) * TM, D_MODEL), f32)]


def _stream_tiles(ys):
    return sum(y.shape[0] for y in ys) // TM


def _mod_spec(layer, k, row_of_tile, ngrid):
    if ngrid == 1:
        return pl.BlockSpec((None, None, None, 1, D_MODEL), lambda i: (layer, k, row_of_tile(i), 0, 0))
    return pl.BlockSpec((None, None, None, 1, D_MODEL), lambda i, j: (layer, k, row_of_tile(i), 0, 0))


QKV_BLK = H_A * DK_A


def _delta_proj_kernel(np_tiles, t_p, t_s, n_y, *refs):
    y_refs = refs[:n_y]
    g_ref, sh_ref, sc_ref, w_ref, cw_ref, cb_ref, o_ref, h_s, mk_s = refs[n_y:]
    n = pl.program_id(1)

    @pl.when(n == 0)
    def _():
        x = _stream_load(np_tiles, y_refs)
        h_s[...] = _modulate(x, g_ref[...], sh_ref[...], sc_ref[...]).astype(bf16)
        _conv_masks(mk_s, jnp.where(pl.program_id(0) < np_tiles, t_p, t_s))

    z = jnp.dot(h_s[...], w_ref[...].astype(bf16), preferred_element_type=f32)
    y = _seq_conv(z, cw_ref, cb_ref, mk_s)
    y = y * _sigmoid(y)

    @pl.when(n < 2)
    def _():
        o_ref[...] = y * lax.rsqrt(_group_sum64(y * y) + EPS)

    @pl.when(n == 2)
    def _():
        o_ref[...] = y


def _delta_proj(ys, mods, layer, row_of_tile, np_tiles, t_p, t_s, norm_g, w_in, conv_w, conv_b):
    nt = _stream_tiles(ys)
    ntok = nt * TM
    kern = functools.partial(_delta_proj_kernel, np_tiles, t_p, t_s, len(ys))
    return pl.pallas_call(
        kern,
        grid=(nt, N_QKV // QKV_BLK),
        in_specs=_stream_specs(np_tiles, 2, len(ys) == 2) + [
            pl.BlockSpec((1, D_MODEL), lambda i, n: (0, 0)),
            _mod_spec(layer, 0, row_of_tile, 2),
            _mod_spec(layer, 1, row_of_tile, 2),
            pl.BlockSpec((None, D_MODEL, QKV_BLK), lambda i, n: (layer // 2, 0, n)),
            pl.BlockSpec((None, CONV_W, QKV_BLK), lambda i, n: (layer // 2, 0, n)),
            pl.BlockSpec((1, QKV_BLK), lambda i, n: (0, n)),
        ],
        out_specs=pl.BlockSpec((TM, QKV_BLK), lambda i, n: (i, n)),
        out_shape=jax.ShapeDtypeStruct((ntok, N_QKV), f32),
        scratch_shapes=[pltpu.VMEM((TM, D_MODEL), bf16), pltpu.VMEM((CONV_W - 1, TM, LANES), f32)],
        compiler_params=_cparams(("arbitrary", "arbitrary")),
        name="delta_proj",
    )(*ys, norm_g, mods, mods, w_in, conv_w, conv_b)


def _rope_swap(x):
    w = x.shape[1]
    lane = lax.broadcasted_iota(jnp.int32, (1, w), 1)
    first = (lane & (ROPE_AXIS - 1)) < ROPE_AXIS // 2
    return jnp.where(first, pltpu.roll(x, w - ROPE_AXIS // 2, axis=1), pltpu.roll(x, ROPE_AXIS // 2, axis=1))


def _attn_proj_kernel(np_tiles, n_y, *refs):
    y_refs = refs[:n_y]
    (g_ref, sh_ref, sc_ref, w_ref, qn_ref, kn_ref, cos_ref, sin_ref, al_ref, dt_ref,
     gate_ref, q_ref, k_ref, v_ref, gb_ref, kc_ref, vc_ref) = refs[n_y:]
    x = _stream_load(np_tiles, y_refs)
    h = _modulate(x, g_ref[...], sh_ref[...], sc_ref[...]).astype(bf16)
    z = jnp.dot(h, w_ref[...].astype(bf16), preferred_element_type=f32)
    nq = H_B * HD_B
    nkv = KV_B * HD_B
    o1 = H_A * DV_A
    o2 = o1 + nq + 2 * nkv
    gz = z[:, :o1]
    gate_ref[...] = gz * _sigmoid(gz)
    q = z[:, o1:o1 + nq]
    k = z[:, o1 + nq:o1 + nq + nkv]
    v = z[:, o1 + nq + nkv:o2]
    v_ref[...] = v
    qn = q * lax.rsqrt(_group_sum64(q * q) * (1.0 / HD_B) + EPS) * qn_ref[...]
    kn = k * lax.rsqrt(_group_sum64(k * k) * (1.0 / HD_B) + EPS) * kn_ref[...]
    cos = cos_ref[...]
    sin = sin_ref[...]
    q_ref[...] = qn * cos + _rope_swap(qn) * sin
    k_ref[...] = kn * cos[:, :nkv] + _rope_swap(kn) * sin[:, :nkv]

    @pl.when(pl.program_id(0) < np_tiles)
    def _():
        kc_ref[...] = kn
        vc_ref[...] = v
    zs = z[:, o2:]
    lane = lax.broadcasted_iota(jnp.int32, (1, LANES), 1)
    g_val = -jnp.exp(al_ref[...]) * _softplus(zs + dt_ref[...])
    gb_ref[...] = jnp.where(lane < 2 * H_A, g_val, jnp.where(lane < 4 * H_A, _sigmoid(zs), 0.0))


def _attn_proj(ys, mods, layer, row_of_tile, np_tiles, norm_g, w_b, qn_row, kn_row, cos_tab, sin_tab,
               al_row, dt_row):
    nt = _stream_tiles(ys)
    ntok = nt * TM
    nq = H_B * HD_B
    nkv = KV_B * HD_B
    tab_idx = lambda i: (jnp.where(i < np_tiles, 0, 1), 0)
    cache_idx = lambda i: (jnp.minimum(i, np_tiles - 1), 0)
    return pl.pallas_call(
        functools.partial(_attn_proj_kernel, np_tiles, len(ys)),
        grid=(nt,),
        in_specs=_stream_specs(np_tiles, 1, len(ys) == 2) + [
            pl.BlockSpec((1, D_MODEL), lambda i: (0, 0)),
            _mod_spec(layer, 0, row_of_tile, 1),
            _mod_spec(layer, 1, row_of_tile, 1),
            pl.BlockSpec((D_MODEL, N_ATTN), lambda i: (0, 0)),
            pl.BlockSpec((1, nq), lambda i: (0, 0)),
            pl.BlockSpec((1, nkv), lambda i: (0, 0)),
            pl.BlockSpec((TM, nq), tab_idx),
            pl.BlockSpec((TM, nq), tab_idx),
            pl.BlockSpec((1, LANES), lambda i: (0, 0)),
            pl.BlockSpec((1, LANES), lambda i: (0, 0)),
        ],
        out_specs=[
            pl.BlockSpec((TM, H_A * DV_A), lambda i: (i, 0)),
            pl.BlockSpec((TM, nq), lambda i: (i, 0)),
            pl.BlockSpec((TM, nkv), lambda i: (i, 0)),
            pl.BlockSpec((TM, nkv), lambda i: (i, 0)),
            pl.BlockSpec((TM, LANES), lambda i: (i, 0)),
            pl.BlockSpec((TM, nkv), cache_idx),
            pl.BlockSpec((TM, nkv), cache_idx),
        ],
        out_shape=[jax.ShapeDtypeStruct((ntok, H_A * DV_A), f32),
                   jax.ShapeDtypeStruct((ntok, nq), f32),
                   jax.ShapeDtypeStruct((ntok, nkv), f32),
                   jax.ShapeDtypeStruct((ntok, nkv), f32),
                   jax.ShapeDtypeStruct((ntok, LANES), f32),
                   jax.ShapeDtypeStruct((np_tiles * TM, nkv), f32),
                   jax.ShapeDtypeStruct((np_tiles * TM, nkv), f32)],
        compiler_params=_cparams(("arbitrary",)),
        name="attn_proj",
    )(*ys, norm_g, mods, mods, w_b, qn_row, kn_row, cos_tab, sin_tab, al_row, dt_row)


INV_BASE = 8
PRE_CHUNKS = 4
DELTA_SEQS_P = 4
HPG = MXU_DIM // DK_A
NLG = H_A // HPG
W_ALL = H_A * DK_A


def _chunk_scan(x, reverse):
    t = x.shape[0]
    r = lax.broadcasted_iota(jnp.int32, (t, 1), 0) & (CHUNK - 1)
    s = 1
    while s < CHUNK:
        if reverse:
            x = x + jnp.where(r + s < CHUNK, pltpu.roll(x, t - s, axis=0), 0.0)
        else:
            x = x + jnp.where(r >= s, pltpu.roll(x, s, axis=0), 0.0)
        s *= 2
    return x


def _delta_kernel(t, nsq, has_s0, *refs):
    if has_s0:
        q_ref, k_ref, v_ref, gb_ref, s0_ref = refs[:5]
        rest = refs[5:]
    else:
        q_ref, k_ref, v_ref, gb_ref = refs[:4]
        s0_ref = None
        rest = refs[4:]
    o_ref, sf_ref, exp_s, kbg_s, qg_s, kd_s, vb_s, egt_s, ti_s, in_s, st_s = rest
    n = t // CHUNK
    scale = DK_A ** -0.5

    gb = gb_ref[...]
    lane = lax.broadcasted_iota(jnp.int32, (1, LANES), 1)
    sc = jnp.where(lane < H_A, _chunk_scan(gb, False), jnp.where(lane < 2 * H_A, _chunk_scan(gb, True), gb))
    er = lax.broadcasted_iota(jnp.int32, (LANES, 4 * W_ALL), 0)
    ec = lax.broadcasted_iota(jnp.int32, (LANES, 4 * W_ALL), 1) >> 6
    expand = jnp.where(er == ec, 1.0, 0.0).astype(bf16)
    rb = min(t, 256)
    for r0 in range(0, nsq * t, rb):
        p1, p2, p3 = _split3(sc[r0:r0 + rb])
        exp_s[r0:r0 + rb, :] = ((jnp.dot(p1, expand, preferred_element_type=f32)
                                 + jnp.dot(p2, expand, preferred_element_type=f32))
                                + jnp.dot(p3, expand, preferred_element_type=f32))

    ri = lax.broadcasted_iota(jnp.int32, (CHUNK, MXU_DIM), 0)
    col_i = lax.broadcasted_iota(jnp.int32, (CHUNK, MXU_DIM), 1) & (CHUNK - 1)
    eye_ls = ri == col_i
    bd_mask = ((lax.broadcasted_iota(jnp.int32, (MXU_DIM, MXU_DIM), 0) >> 6)
               == (lax.broadcasted_iota(jnp.int32, (MXU_DIM, MXU_DIM), 1) >> 6))
    ones_c = jnp.ones((CHUNK, CHUNK), bf16)

    def bd(x):
        xb = x.astype(bf16)
        return jnp.where(bd_mask, jnp.concatenate([xb] * HPG, axis=0), jnp.zeros((), bf16))

    def blk(b):
        sh = b.bit_length() - 1
        return (ri >> sh) == (col_i >> sh)

    def hmm(x, y):
        return jnp.dot(x.astype(bf16), bd(y), preferred_element_type=f32)

    def pre_body(cp, carry):
        units = [(ci, d, lg) for ci in range(PRE_CHUNKS) for lg in range(NLG) for d in range(2)]
        cidx = [cp * PRE_CHUNKS + ci for ci in range(PRE_CHUNKS)]
        gcs, prods = {}, {}
        for ci, c in enumerate(cidx):
            rows = pl.ds(pl.multiple_of(c * CHUNK, CHUNK), CHUNK)
            q = q_ref[rows, :]
            k = k_ref[rows, :]
            v = v_ref[rows, :]
            kbs = []
            for d in range(2):
                gc = exp_s[rows, d * W_ALL:(d + 1) * W_ALL]
                beta = exp_s[rows, (2 + d) * W_ALL:(3 + d) * W_ALL]
                gtot = gc[CHUNK - 1:CHUNK, :] if d == 0 else gc[0:1, :]
                eg = jnp.exp(gc)
                kb = k * beta
                kbg_s[d, rows, :] = (kb * eg).astype(bf16)
                qg_s[d, rows, :] = (q * (scale * eg)).astype(bf16)
                kd_s[d, rows, :] = (k * jnp.exp(gtot - gc)).astype(bf16)
                vb_s[d, rows, :] = v * beta
                egt_s[d, c] = jnp.zeros((8, W_ALL), f32) + jnp.exp(gtot)
                gcs[ci, d] = gc
                kbs.append(kb)
            for lg in range(NLG):
                cols = slice(lg * MXU_DIM, (lg + 1) * MXU_DIM)
                lhs = jnp.concatenate([kbs[0][:, cols], kbs[1][:, cols], q[:, cols] * scale], axis=0)
                prods[ci, lg] = _dot_nt(lhs, bd(k[:, cols]))
        ps, ys, avs = [], [], []
        for ci, d, lg in units:
            c = cidx[ci]
            cols = slice(lg * MXU_DIM, (lg + 1) * MXU_DIM)
            gc = gcs[ci, d][:, cols]
            prod = prods[ci, lg]
            d1, d2, d3 = _split3(jnp.where(eye_ls, gc, 0.0))
            grow = ((jnp.dot(ones_c, d1, preferred_element_type=f32)
                     + jnp.dot(ones_c, d2, preferred_element_type=f32))
                    + jnp.dot(ones_c, d3, preferred_element_type=f32))
            keep = (ri >= col_i) if d == 0 else (ri <= col_i)
            strict = (ri > col_i) if d == 0 else (ri < col_i)
            decay = jnp.where(keep, jnp.exp(jnp.where(keep, gc - grow, 0.0)), 0.0)
            a = jnp.where(strict, prod[d * CHUNK:(d + 1) * CHUNK] * decay, 0.0)
            in_s[d, c, :, cols] = jnp.where(keep, prod[2 * CHUNK:] * decay, 0.0).astype(bf16)
            ab = a.astype(bf16)
            avs.append(ab)
            ys.append(jnp.where(blk(INV_BASE), -ab, jnp.zeros((), bf16)))
            ps.append(jnp.where(eye_ls, 1.0, 0.0) - jnp.where(blk(INV_BASE), a, 0.0))
        m = 1
        while m < INV_BASE:
            last = 2 * m >= INV_BASE
            for u in range(len(units)):
                y = ys[u]
                if m == 1:
                    ys[u] = hmm(y, y).astype(bf16)
                elif last:
                    ps[u] = ps[u] + hmm(ps[u], y)
                else:
                    r = hmm(jnp.concatenate([ps[u].astype(bf16), y], axis=0), y)
                    ps[u] = ps[u] + r[:CHUNK]
                    ys[u] = r[CHUNK:].astype(bf16)
            m *= 2
        b = INV_BASE
        while b < CHUNK:
            off = blk(2 * b) & jnp.logical_not(blk(b))
            ws = [hmm(jnp.where(off, avs[u], jnp.zeros((), bf16)), ps[u]) for u in range(len(units))]
            for u in range(len(units)):
                ps[u] = ps[u] - hmm(ps[u], ws[u])
            b *= 2
        for u, (ci, d, lg) in enumerate(units):
            ti_s[d, cidx[ci], :, lg * MXU_DIM:(lg + 1) * MXU_DIM] = ps[u].astype(bf16)
        return carry

    lax.fori_loop(0, nsq * n // PRE_CHUNKS, pre_body, 0)

    for sq in range(nsq):
        for d in range(2):
            for lg in range(NLG):
                if has_s0:
                    blocks = []
                    for hh in range(HPG):
                        s_h = s0_ref[sq, d, lg * HPG + hh]
                        z_l = jnp.zeros((DK_A, hh * DV_A), f32)
                        z_r = jnp.zeros((DK_A, (HPG - 1 - hh) * DV_A), f32)
                        parts = ([z_l] if hh > 0 else []) + [s_h] + ([z_r] if hh < HPG - 1 else [])
                        blocks.append(jnp.concatenate(parts, axis=1) if len(parts) > 1 else s_h)
                    st_s[sq, d, lg] = jnp.concatenate(blocks, axis=0)
                else:
                    st_s[sq, d, lg] = jnp.zeros((MXU_DIM, MXU_DIM), f32)
    o_ref[...] = jnp.zeros_like(o_ref)

    def seq_body(j, carry):
        units = [(sq, d, lg) for sq in range(nsq) for lg in range(NLG) for d in range(2)]
        step = [j, n - 1 - j]
        cls = [slice(lg * MXU_DIM, (lg + 1) * MXU_DIM) for lg in range(NLG)]
        chunk = {(sq, d): sq * n + step[d] for sq in range(nsq) for d in range(2)}
        rws = {key: pl.ds(pl.multiple_of(cc * CHUNK, CHUNK), CHUNK) for key, cc in chunk.items()}
        boths, vnbs = [], []
        for sq, d, lg in units:
            r = rws[sq, d]
            lhs = jnp.concatenate([kbg_s[d, r, cls[lg]], qg_s[d, r, cls[lg]]], axis=0)
            boths.append(jnp.dot(lhs, st_s[sq, d, lg].astype(bf16), preferred_element_type=f32))
        for u, (sq, d, lg) in enumerate(units):
            resid = vb_s[d, rws[sq, d], cls[lg]] - boths[u][:CHUNK]
            v_new = jnp.dot(ti_s[d, chunk[sq, d], :, cls[lg]], bd(resid), preferred_element_type=f32)
            vnbs.append(v_new.astype(bf16))
        for u, (sq, d, lg) in enumerate(units):
            r = rws[sq, d]
            c = chunk[sq, d]
            o = boths[u][CHUNK:] + jnp.dot(in_s[d, c, :, cls[lg]], bd(vnbs[u]), preferred_element_type=f32)
            upd = lax.dot_general(kd_s[d, r, cls[lg]], vnbs[u], (((0,), (0,)), ((), ())),
                                  preferred_element_type=f32)
            st_s[sq, d, lg] = st_s[sq, d, lg] * egt_s[d, c, 0:1, cls[lg]] + jnp.where(bd_mask, upd, 0.0)
            o_ref[r, cls[lg]] = o_ref[r, cls[lg]] + o
        return carry

    lax.fori_loop(0, n, seq_body, 0)
    for sq in range(nsq):
        for d in range(2):
            for lg in range(NLG):
                s_fin = st_s[sq, d, lg]
                for hh in range(HPG):
                    sf_ref[sq, d, lg * HPG + hh] = s_fin[hh * DK_A:(hh + 1) * DK_A, hh * DV_A:(hh + 1) * DV_A]


def _delta(qkv, gb, t, nseq, nsq, tok_block0, s0, s0_layer):
    n = t // CHUNK
    tb = nsq * t
    nch = nsq * n
    has_s0 = s0 is not None
    kern = functools.partial(_delta_kernel, t, nsq, has_s0)
    in_specs = [
        pl.BlockSpec((tb, W_ALL), lambda b: (tok_block0 + b, 0)),
        pl.BlockSpec((tb, W_ALL), lambda b: (tok_block0 + b, 1)),
        pl.BlockSpec((tb, W_ALL), lambda b: (tok_block0 + b, 2)),
        pl.BlockSpec((tb, LANES), lambda b: (tok_block0 + b, 0)),
    ]
    args = [qkv, qkv, qkv, gb]
    if has_s0:
        in_specs.append(pl.BlockSpec((nsq, None, 2, H_A, DK_A, DV_A), lambda b: (b, s0_layer, 0, 0, 0, 0)))
        args.append(s0)
    return pl.pallas_call(
        kern,
        grid=(nseq // nsq,),
        in_specs=in_specs,
        out_specs=[
            pl.BlockSpec((tb, W_ALL), lambda b: (b, 0)),
            pl.BlockSpec((nsq, 2, H_A, DK_A, DV_A), lambda b: (b, 0, 0, 0, 0)),
        ],
        out_shape=[jax.ShapeDtypeStruct((nseq * t, W_ALL), f32),
                   jax.ShapeDtypeStruct((nseq, 2, H_A, DK_A, DV_A), f32)],
        scratch_shapes=[
            pltpu.VMEM((tb, 4 * W_ALL), f32),
            pltpu.VMEM((2, tb, W_ALL), bf16),
            pltpu.VMEM((2, tb, W_ALL), bf16),
            pltpu.VMEM((2, tb, W_ALL), bf16),
            pltpu.VMEM((2, tb, W_ALL), f32),
            pltpu.VMEM((2, nch, 8, W_ALL), f32),
            pltpu.VMEM((2, nch, CHUNK, W_ALL), bf16),
            pltpu.VMEM((2, nch, CHUNK, W_ALL), bf16),
            pltpu.VMEM((nsq, 2, NLG, MXU_DIM, MXU_DIM), f32),
        ],
        compiler_params=_cparams(("arbitrary",)),
        name="delta_p" if not has_s0 else "delta_s",
    )(*args)


def _attend(q_all, key_sets, sink_ref):
    nrow = q_all.shape[0]
    scores = []
    for h in range(H_B):
        kv = h // G_B
        q = q_all[:, h * HD_B:(h + 1) * HD_B]
        row = []
        for k_all, _, mask in key_sets:
            s = _dot_nt(q, k_all[:, kv * HD_B:(kv + 1) * HD_B])
            row.append(s if mask is None else jnp.where(mask, s, NEG_INF))
        scores.append(row)
    probs, sink_terms = [], []
    for h in range(H_B):
        sink = jnp.zeros((nrow, 1), f32) + sink_ref[h]
        m = sink
        for s in scores[h]:
            m = jnp.maximum(m, jnp.max(s, axis=-1, keepdims=True))
        probs.append([jnp.exp(s - m).astype(bf16) for s in scores[h]])
        sink_terms.append(jnp.exp(sink - m))
    v_ext = [[jnp.concatenate([v_all[:, kv * HD_B:(kv + 1) * HD_B].astype(bf16),
                               jnp.ones((v_all.shape[0], HD_B), bf16)], axis=1) for kv in range(KV_B)]
             for _, v_all, _ in key_sets]
    outs = []
    for h in range(H_B):
        acc = None
        for p, vs in zip(probs[h], v_ext):
            pv = jnp.dot(p, vs[h // G_B], preferred_element_type=f32)
            acc = pv if acc is None else acc + pv
        outs.append(acc[:, :HD_B] / (acc[:, HD_B:HD_B + 1] + sink_terms[h]))
    return jnp.concatenate(outs, axis=1)


def _ctx_attn_kernel(sink_ref, q_ref, k_ref, v_ref, o_ref):
    scale = HD_B ** -0.5
    o_ref[...] = _attend(q_ref[...] * scale, [(k_ref[...], v_ref[...], None)], sink_ref)


def _ctx_attn(sink, qb, kb, vb, t, nseq):
    nq = H_B * HD_B
    nkv = KV_B * HD_B
    return pl.pallas_call(
        _ctx_attn_kernel,
        grid=(nseq,),
        in_specs=[
            pl.BlockSpec(memory_space=pltpu.SMEM),
            pl.BlockSpec((t, nq), lambda b: (b, 0)),
            pl.BlockSpec((t, nkv), lambda b: (b, 0)),
            pl.BlockSpec((t, nkv), lambda b: (b, 0)),
        ],
        out_specs=pl.BlockSpec((t, nq), lambda b: (b, 0)),
        out_shape=jax.ShapeDtypeStruct((nseq * t, nq), f32),
        compiler_params=_cparams(("arbitrary",)),
        name="ctx_attn",
    )(sink, qb, kb, vb)


def _win_attn_kernel(t, sink_ref, q_ref, k_ref, v_ref, kc_ref, vc_ref, o_ref):
    i = pl.program_id(1)
    scale = HD_B ** -0.5
    span = QBLK + 2 * WINDOW
    start = i * QBLK
    lo = jnp.clip(start - WINDOW, 0, t - span)
    lo = pl.multiple_of(lo, QBLK)
    q_pos = start + lax.broadcasted_iota(jnp.int32, (QBLK, span), 0)
    k_pos = lo + lax.broadcasted_iota(jnp.int32, (QBLK, span), 1)
    valid = jnp.abs(q_pos - k_pos) <= WINDOW
    key_sets = [(k_ref[pl.ds(lo, span), :], v_ref[pl.ds(lo, span), :], valid), (kc_ref[...], vc_ref[...], None)]
    o_ref[...] = _attend(q_ref[...] * scale, key_sets, sink_ref)


def _win_attn(sink, qb, kb, vb, cache_k, cache_v, cache_layer, t, nseq, tok_block0):
    nq = H_B * HD_B
    nkv = KV_B * HD_B
    nb = t // QBLK
    past = cache_k.shape[2]
    kern = functools.partial(_win_attn_kernel, t)
    return pl.pallas_call(
        kern,
        grid=(nseq, nb),
        in_specs=[
            pl.BlockSpec(memory_space=pltpu.SMEM),
            pl.BlockSpec((QBLK, nq), lambda b, i: ((tok_block0 + b) * nb + i, 0)),
            pl.BlockSpec((t, nkv), lambda b, i: (tok_block0 + b, 0)),
            pl.BlockSpec((t, nkv), lambda b, i: (tok_block0 + b, 0)),
            pl.BlockSpec((None, None, past, nkv), lambda b, i: (b, cache_layer, 0, 0)),
            pl.BlockSpec((None, None, past, nkv), lambda b, i: (b, cache_layer, 0, 0)),
        ],
        out_specs=pl.BlockSpec((QBLK, nq), lambda b, i: (b * nb + i, 0)),
        out_shape=jax.ShapeDtypeStruct((nseq * t, nq), f32),
        compiler_params=_cparams(("arbitrary", "arbitrary")),
        name="win_attn",
    )(sink, qb, kb, vb, cache_k, cache_v)


def _mix_out_kernel(np_tiles, n_y, *refs):
    y_refs = refs[:n_y]
    oap_ref, oas_ref, obp_ref, obs_ref, gate_ref, dn_ref, w_ref, gm_ref = refs[n_y:n_y + 8]
    o_refs = refs[n_y + 8:]
    i = pl.program_id(0)
    is_p = i < np_tiles
    oa = jnp.where(is_p, oap_ref[...], oas_ref[...])
    ob = jnp.where(is_p, obp_ref[...], obs_ref[...])
    oan = oa * lax.rsqrt(_group_sum64(oa * oa) * (1.0 / DV_A) + EPS) * dn_ref[...] * gate_ref[...]
    ka = H_A * DV_A
    mix = _dot(oan, w_ref[:ka, :]) + _dot(ob, w_ref[ka:, :])
    _stream_store(np_tiles, o_refs, _stream_load(np_tiles, y_refs) + gm_ref[...] * mix)


def _mix_out(ys, mods, layer, row_of_tile, np_tiles, oa_p, oa_s, ob_p, ob_s, gate, dn_row, w_out, split_out):
    nt = _stream_tiles(ys)
    ka = H_A * DV_A
    p_idx = lambda i: (jnp.minimum(i, np_tiles - 1), 0)
    s_idx = lambda i: (jnp.maximum(i - np_tiles, 0), 0)
    kern = functools.partial(_mix_out_kernel, np_tiles, len(ys))
    return pl.pallas_call(
        kern,
        grid=(nt,),
        in_specs=_stream_specs(np_tiles, 1, len(ys) == 2) + [
            pl.BlockSpec((TM, ka), p_idx),
            pl.BlockSpec((TM, ka), s_idx),
            pl.BlockSpec((TM, ka), p_idx),
            pl.BlockSpec((TM, ka), s_idx),
            pl.BlockSpec((TM, ka), lambda i: (i, 0)),
            pl.BlockSpec((1, ka), lambda i: (0, 0)),
            pl.BlockSpec((None, 2 * ka, D_MODEL), lambda i: (layer // 2, 0, 0)),
            _mod_spec(layer, 2, row_of_tile, 1),
        ],
        out_specs=_stream_specs(np_tiles, 1, split_out),
        out_shape=_stream_shapes(np_tiles, nt, split_out),
        compiler_params=_cparams(("arbitrary",)),
        name="mix_out",
    )(*ys, oa_p, oa_s, ob_p, ob_s, gate, dn_row, w_out, mods)


def _gelu_tanh(x):
    return 0.5 * x * (1.0 + jnp.tanh(math.sqrt(2.0 / math.pi) * (x + 0.044715 * (x * x * x))))


def _lru_kernel(np_tiles, t_p, t_s, y_ref, g_ref, sh_ref, sc_ref, wx_ref, wg_ref, cw_ref, cb_ref,
                wl_ref, bl_ref, lam_ref, h0_ref, o_ref, fin_ref, h_s, x_s, gt_s, hf_s, hb_s, mk_s):
    is_p = pl.program_id(0) < np_tiles
    tseq = jnp.where(is_p, t_p, t_s)

    @pl.when(pl.program_id(1) == 0)
    def _():
        h_s[...] = _modulate(y_ref[...], g_ref[...], sh_ref[...], sc_ref[...]).astype(bf16)
        _conv_masks(mk_s, tseq)

    w = jnp.concatenate([wx_ref[...], wg_ref[...]], axis=1).astype(bf16)
    z = jnp.dot(h_s[...], w, preferred_element_type=f32)
    x = _seq_conv(z[:, :LRU_W], cw_ref, cb_ref, mk_s)
    x_s[...] = x
    ngrp = LRU_W // LRU_GW
    gcols = [slice(g * LRU_GW, (g + 1) * LRU_GW) for g in range(ngrp)]
    for g in range(ngrp):
        gt_s[g] = _dot(x[:, gcols[g]], wl_ref[g]) + bl_ref[g]
    decay = [[RG_C * _softplus(-lam_ref[g, :, d * LRU_GW:(d + 1) * LRU_GW]) for g in range(ngrp)]
             for d in range(2)]
    sub = lax.broadcasted_iota(jnp.int32, (SUBLANES, 1), 0)
    nblk = TM // SUBLANES
    seg_shift = (t_p // SUBLANES).bit_length() - 1

    def body(kf, carry):
        nxt = []
        for d, kk in ((0, kf), (1, nblk - 1 - kf)):
            r0 = pl.multiple_of(kk * SUBLANES, SUBLANES)
            rows = pl.ds(r0, SUBLANES)
            edge = r0 if d == 0 else r0 + SUBLANES
            at_edge = (edge & (tseq - 1)) == 0
            seg = jnp.where(is_p, kk >> seg_shift, 0)
            for g in range(ngrp):
                xb = x_s[rows, gcols[g]]
                gb = gt_s[g, rows, :]
                r = _sigmoid(gb[:, (2 * d) * LRU_GW:(2 * d + 1) * LRU_GW])
                ig = _sigmoid(gb[:, (2 * d + 1) * LRU_GW:(2 * d + 2) * LRU_GW])
                log_a = -(r * decay[d][g])
                a = jnp.exp(log_a)
                u = jnp.sqrt(-jnp.tanh(log_a) * (a * a + 1.0)) * (ig * xb)
                s = 1
                while s < SUBLANES:
                    ok = (sub >= s) if d == 0 else (sub + s < SUBLANES)
                    sh = s if d == 0 else SUBLANES - s
                    u = u + a * jnp.where(ok, pltpu.roll(u, sh, axis=0), 0.0)
                    a = a * jnp.where(ok, pltpu.roll(a, sh, axis=0), 1.0)
                    s *= 2
                c_in = jnp.where(at_edge, h0_ref[seg, d:d + 1, gcols[g]], carry[d * ngrp + g])
                hblk = u + a * c_in
                (hf_s if d == 0 else hb_s)[rows, gcols[g]] = hblk
                nxt.append(hblk[SUBLANES - 1:SUBLANES, :] if d == 0 else hblk[0:1, :])
        return tuple(nxt)

    zero = jnp.zeros((1, LRU_GW), f32)
    lax.fori_loop(0, nblk, body, (zero,) * (2 * ngrp), unroll=4)
    for sg in range(TM // t_p):
        fin_ref[sg, 0:1, :] = hf_s[sg * t_p + t_p - 1:sg * t_p + t_p, :]
        fin_ref[sg, 1:2, :] = hb_s[sg * t_p:sg * t_p + 1, :]
    o_ref[...] = (hf_s[...] + hb_s[...]) * _gelu_tanh(z[:, LRU_W:])


def _lru(y, mods, layer, row_of_tile, np_tiles, t_p, t_s, norm_g, w_in, cw, cb, wl, bl, lam, h0):
    ntok = y.shape[0]
    nt = ntok // TM
    ncol = D_RNN // LRU_W
    ngrp = LRU_W // LRU_GW
    nseg = TM // t_p
    kern = functools.partial(_lru_kernel, np_tiles, t_p, t_s)
    return pl.pallas_call(
        kern,
        grid=(nt, ncol),
        in_specs=[
            pl.BlockSpec((TM, D_MODEL), lambda i, j: (i, 0)),
            pl.BlockSpec((1, D_MODEL), lambda i, j: (0, 0)),
            _mod_spec(layer, 0, row_of_tile, 2),
            _mod_spec(layer, 1, row_of_tile, 2),
            pl.BlockSpec((None, D_MODEL, LRU_W), lambda i, j: (layer // 2, 0, j)),
            pl.BlockSpec((None, D_MODEL, LRU_W), lambda i, j: (layer // 2, 0, ncol + j)),
            pl.BlockSpec((None, CONV_W, LRU_W), lambda i, j: (layer // 2, 0, j)),
            pl.BlockSpec((1, LRU_W), lambda i, j: (0, j)),
            pl.BlockSpec((ngrp, LRU_GW, 4 * LRU_GW), lambda i, j: (j, 0, 0)),
            pl.BlockSpec((ngrp, 1, 4 * LRU_GW), lambda i, j: (j, 0, 0)),
            pl.BlockSpec((ngrp, 1, 2 * LRU_GW), lambda i, j: (j, 0, 0)),
            pl.BlockSpec((None, nseg, 2, LRU_W), lambda i, j: (i, 0, 0, j)),
        ],
        out_specs=[
            pl.BlockSpec((TM, LRU_W), lambda i, j: (i, j)),
            pl.BlockSpec((None, nseg, 2, LRU_W), lambda i, j: (i, 0, 0, j)),
        ],
        out_shape=[jax.ShapeDtypeStruct((ntok, D_RNN), f32),
                   jax.ShapeDtypeStruct((nt, nseg, 2, D_RNN), f32)],
        scratch_shapes=[pltpu.VMEM((TM, D_MODEL), bf16), pltpu.VMEM((TM, LRU_W), f32),
                        pltpu.VMEM((ngrp, TM, 4 * LRU_GW), f32), pltpu.VMEM((TM, LRU_W), f32),
                        pltpu.VMEM((TM, LRU_W), f32), pltpu.VMEM((CONV_W - 1, TM, LANES), f32)],
        compiler_params=_cparams(("arbitrary", "arbitrary")),
        name="lru",
    )(y, norm_g, mods, mods, w_in, w_in, cw, cb, wl, bl, lam, h0)


def _proj_out_kernel(y_ref, a_ref, w_ref, gm_ref, o_ref):
    o_ref[...] = y_ref[...] + gm_ref[...] * _dot(a_ref[...], w_ref[...])


def _proj_out(y, mods, layer, row_of_tile, a, w):
    ntok = y.shape[0]
    nt = ntok // TM
    k = a.shape[1]
    return pl.pallas_call(
        _proj_out_kernel,
        grid=(nt,),
        in_specs=[
            pl.BlockSpec((TM, D_MODEL), lambda i: (i, 0)),
            pl.BlockSpec((TM, k), lambda i: (i, 0)),
            pl.BlockSpec((None, k, D_MODEL), lambda i: (layer // 2, 0, 0)),
            _mod_spec(layer, 2, row_of_tile, 1),
        ],
        out_specs=pl.BlockSpec((TM, D_MODEL), lambda i: (i, 0)),
        out_shape=jax.ShapeDtypeStruct((ntok, D_MODEL), f32),
        compiler_params=_cparams(("arbitrary",)),
        name="proj_out",
    )(y, a, w, mods)


FF_BLK = 1024


def _mlp_kernel(np_tiles, n_out, y_ref, g_ref, sh_ref, sc_ref, gm_ref, w1_ref, w2_ref, *refs):
    o_refs = refs[:n_out]
    h_s, acc_s = refs[n_out:]
    k = pl.program_id(1)

    @pl.when(k == 0)
    def _():
        h_s[...] = _modulate(y_ref[...], g_ref[...], sh_ref[...], sc_ref[...]).astype(bf16)
        acc_s[...] = jnp.zeros_like(acc_s)

    a = jnp.dot(h_s[...], w1_ref[...].astype(bf16), preferred_element_type=f32)
    a = jnp.maximum(a, 0.0)
    acc_s[...] += _dot(a * a, w2_ref[...])

    @pl.when(k == pl.num_programs(1) - 1)
    def _():
        _stream_store(np_tiles, o_refs, y_ref[...] + gm_ref[...] * acc_s[...])


def _mlp(y, mods, layer, row_of_tile, np_tiles, norm_g, w1, w2, split_out):
    nt = y.shape[0] // TM
    return pl.pallas_call(
        functools.partial(_mlp_kernel, np_tiles, 2 if split_out else 1),
        grid=(nt, D_FF // FF_BLK),
        in_specs=[
            pl.BlockSpec((TM, D_MODEL), lambda i, k: (i, 0)),
            pl.BlockSpec((1, D_MODEL), lambda i, k: (0, 0)),
            _mod_spec(layer, 3, row_of_tile, 2),
            _mod_spec(layer, 4, row_of_tile, 2),
            _mod_spec(layer, 5, row_of_tile, 2),
            pl.BlockSpec((None, D_MODEL, FF_BLK), lambda i, k: (layer, 0, k)),
            pl.BlockSpec((None, FF_BLK, D_MODEL), lambda i, k: (layer, k, 0)),
        ],
        out_specs=_stream_specs(np_tiles, 2, split_out),
        out_shape=_stream_shapes(np_tiles, nt, split_out),
        scratch_shapes=[pltpu.VMEM((TM, D_MODEL), bf16), pltpu.VMEM((TM, D_MODEL), f32)],
        compiler_params=_cparams(("arbitrary", "arbitrary")),
        name="mlp",
    )(y, norm_g, mods, mods, mods, w1, w2)


def _attn_cols(w_in, a_log, dt_bias):
    s2 = N_QKV + H_A * DV_A
    s4 = s2 + 4 * H_A
    pad = jnp.zeros((D_MODEL, LANES - 4 * H_A), f32)
    w_b = jnp.concatenate([w_in[:, N_QKV:s2], w_in[:, s4:], w_in[:, s2:s4], pad], axis=1)
    row_pad = jnp.zeros((LANES - 2 * H_A,), f32)
    al_row = jnp.concatenate([a_log.reshape(-1), row_pad]).reshape(1, LANES)
    dt_row = jnp.concatenate([dt_bias.reshape(-1), row_pad]).reshape(1, LANES)
    return w_b, al_row, dt_row


def _rope_tables(t):
    rows = t // GRID_W
    r = np.repeat(np.arange(rows, dtype=np.float32), GRID_W)
    cc = np.tile(np.arange(GRID_W, dtype=np.float32), rows)
    inv = np.float32(ROPE_BASE) ** (-np.arange(0, ROPE_AXIS, 2, dtype=np.float32) / np.float32(ROPE_AXIS))
    ang_r = (r[:, None] * inv).astype(np.float32)
    ang_c = (cc[:, None] * inv).astype(np.float32)
    cos = np.concatenate([np.cos(ang_r), np.cos(ang_r), np.cos(ang_c), np.cos(ang_c)], axis=-1)
    sin = np.concatenate([-np.sin(ang_r), np.sin(ang_r), -np.sin(ang_c), np.sin(ang_c)], axis=-1)
    cos = np.tile(cos, (1, H_B)).astype(np.float32)
    sin = np.tile(sin, (1, H_B)).astype(np.float32)
    ident_c = np.ones((TM, H_B * HD_B), np.float32)
    ident_s = np.zeros((TM, H_B * HD_B), np.float32)
    return (jnp.asarray(np.concatenate([ident_c, cos], axis=0)),
            jnp.asarray(np.concatenate([ident_s, sin], axis=0)))


def _lru_cols(w_a, b_a, w_x, b_x, lam):
    ncol = D_RNN // LRU_GW
    per = LRU_GW // LRU_BW

    def bd(w):
        w = w.reshape(ncol, per, LRU_BW, LRU_BW)
        eye = jnp.eye(per, dtype=f32)
        return jnp.einsum('gpij,pq->gpiqj', w, eye).reshape(ncol, LRU_GW, LRU_GW)

    wl = jnp.concatenate([bd(w_a[0]), bd(w_x[0]), bd(w_a[1]), bd(w_x[1])], axis=-1)
    rows = lambda v: v.reshape(ncol, 1, LRU_GW)
    bl = jnp.concatenate([rows(b_a[0]), rows(b_x[0]), rows(b_a[1]), rows(b_x[1])], axis=-1)
    lm = jnp.concatenate([rows(lam[0]), rows(lam[1])], axis=-1)
    return wl, bl, lm


def kernel(x_prompt, x_sample, state_delta, cache_k, cache_v, state_lru, c, c_ctx, ada_w, ada_b, norm1_g, norm2_g, ff_w1, ff_w2, ab_w_in, ab_conv_w, ab_conv_b, dn_a_log, dn_dt_bias, dn_norm_g, attn_q_norm_g, attn_k_norm_g, attn_sink, ab_w_out, c_w_in, c_conv_w, c_conv_b, lru_w_a, lru_b_a, lru_w_x, lru_b_x, lru_lambda, c_w_out):
    b_p, t_p, _ = x_prompt.shape
    b_s, t_s, _ = x_sample.shape
    assert t_s == TM and TM % t_p == 0 and (b_p * t_p) % TM == 0
    np_tok = b_p * t_p
    np_tiles = np_tok // TM
    nkv = KV_B * HD_B

    ys = (x_prompt.reshape(np_tok, D_MODEL), x_sample.reshape(b_s * t_s, D_MODEL))
    nt = np_tiles + b_s * t_s // TM

    rows = -(-(b_s + 1) // 8) * 8
    cond = jnp.zeros((rows, D_MODEL), f32).at[:b_s].set(c).at[b_s].set(c_ctx)
    mods = _adaln(cond, ada_w, ada_b)
    row_of_tile = lambda i: jnp.where(i < np_tiles, b_s, i - np_tiles)

    cos_tab, sin_tab = _rope_tables(t_s)
    ck = cache_k.reshape(cache_k.shape[0], cache_k.shape[1], cache_k.shape[2], nkv)
    cv = cache_v.reshape(ck.shape)

    new_dn, new_k, new_v, new_lru = [], [], [], []
    for l in range(DEPTH):
        j = l // 2
        n1 = norm1_g[l].reshape(1, D_MODEL)
        if l % 2 == 0:
            qkv = _delta_proj(ys, mods, l, row_of_tile, np_tiles, t_p, t_s, n1, ab_w_in, ab_conv_w,
                              ab_conv_b[j].reshape(1, N_QKV))
            w_b, al_row, dt_row = _attn_cols(ab_w_in[j], dn_a_log[j], dn_dt_bias[j])
            qn_row = jnp.tile(attn_q_norm_g[j], H_B).reshape(1, H_B * HD_B)
            kn_row = jnp.tile(attn_k_norm_g[j], KV_B).reshape(1, nkv)
            gate, qb, kb, vb, gb, kc, vc = _attn_proj(ys, mods, l, row_of_tile, np_tiles, n1, w_b, qn_row, kn_row,
                                                      cos_tab, sin_tab, al_row, dt_row)
            oa_p, s_fin = _delta(qkv, gb, t_p, b_p, DELTA_SEQS_P, 0, None, 0)
            oa_s, _ = _delta(qkv, gb, t_s, b_s, 1, np_tok // t_s, state_delta, j)
            ob_p = _ctx_attn(attn_sink[j], qb, kb, vb, t_p, b_p)
            ob_s = _win_attn(attn_sink[j], qb, kb, vb, ck, cv, j, t_s, b_s, np_tok // t_s)
            dn_row = jnp.tile(dn_norm_g[j], H_A).reshape(1, H_A * DV_A)
            (y,) = _mix_out(ys, mods, l, row_of_tile, np_tiles, oa_p, oa_s, ob_p, ob_s, gate, dn_row, ab_w_out,
                            False)
            new_dn.append(s_fin)
            new_k.append(kc.reshape(b_p, t_p, KV_B, HD_B))
            new_v.append(vc.reshape(b_p, t_p, KV_B, HD_B))
        else:
            wl, bl, lm = _lru_cols(lru_w_a[j], lru_b_a[j], lru_w_x[j], lru_b_x[j], lru_lambda[j])
            nseg = TM // t_p
            h0 = jnp.zeros((nt, nseg, 2, D_RNN), f32).at[np_tiles:, 0].set(state_lru[:, j])
            mixed, fin = _lru(ys[0], mods, l, row_of_tile, np_tiles, t_p, t_s, n1, c_w_in, c_conv_w,
                              c_conv_b[j].reshape(1, D_RNN), wl, bl, lm, h0)
            y = _proj_out(ys[0], mods, l, row_of_tile, mixed, c_w_out)
            new_lru.append(fin[:np_tiles].reshape(b_p, 2, D_RNN))
        ys = tuple(_mlp(y, mods, l, row_of_tile, np_tiles, norm2_g[l].reshape(1, D_MODEL), ff_w1, ff_w2,
                        l == DEPTH - 1))

    yp = ys[0].reshape(b_p, t_p, D_MODEL)
    ysm = ys[1].reshape(b_s, t_s, D_MODEL)
    return (yp, ysm, jnp.stack(new_dn, axis=1), jnp.stack(new_k, axis=1), jnp.stack(new_v, axis=1),
            jnp.stack(new_lru, axis=1))
```

```python
import functools
import math

import jax
import jax.numpy as jnp
import numpy as np
from jax import lax
from jax.experimental import pallas as pl
from jax.experimental.pallas import tpu as pltpu

f32 = jnp.float32
bf16 = jnp.bfloat16

D_MODEL = 1024
DEPTH = 4
GRID_W = 64
H_A = 8
DK_A = 64
DV_A = 64
CHUNK = 64
CONV_W = 4
H_B = 8
KV_B = 2
G_B = H_B // KV_B
HD_B = 64
WINDOW = 128
QBLK = 128
ROPE_AXIS = HD_B // 2
ROPE_BASE = 10000.0
D_RNN = D_MODEL
LRU_BLOCKS = 16
LRU_BW = D_RNN // LRU_BLOCKS
RG_C = 8.0
D_FF = 4 * D_MODEL
EPS = 1e-6
NEG_INF = -1e30

TM = 1024
LANES = 128
SUBLANES = 8
LRU_W = 512
LRU_GW = 256
MXU_DIM = 256
N_QKV = 3 * H_A * DK_A
N_ATTN = H_A * DV_A + (H_B + 2 * KV_B) * HD_B + LANES
VMEM_LIMIT = 56 * 1024 * 1024


def _cparams(sem):
    return pltpu.CompilerParams(dimension_semantics=sem, vmem_limit_bytes=VMEM_LIMIT)


def _dot(a, b):
    return jnp.dot(a.astype(bf16), b.astype(bf16), preferred_element_type=f32)


def _dot_nt(a, b):
    return lax.dot_general(a.astype(bf16), b.astype(bf16), (((1,), (1,)), ((), ())),
                           preferred_element_type=f32)


def _dot_tn(a, b):
    return lax.dot_general(a.astype(bf16), b.astype(bf16), (((0,), (0,)), ((), ())),
                           preferred_element_type=f32)


def _sigmoid(x):
    return 1.0 / (1.0 + jnp.exp(-x))


def _softplus(x):
    return jnp.maximum(x, 0.0) + jnp.log1p(jnp.exp(-jnp.abs(x)))


def _modulate(x, g, shift, scale):
    ms = jnp.mean(x * x, axis=-1, keepdims=True)
    return (x * lax.rsqrt(ms + EPS)) * (g * (1.0 + scale)) + shift


def _split3(x):
    p1 = x.astype(bf16)
    r1 = x - p1.astype(f32)
    p2 = r1.astype(bf16)
    p3 = (r1 - p2.astype(f32)).astype(bf16)
    return p1, p2, p3


def _group_sum64(v):
    w = v.shape[-1]
    r = lax.broadcasted_iota(jnp.int32, (w, w), 0) >> 6
    c = lax.broadcasted_iota(jnp.int32, (w, w), 1) >> 6
    ones_bd = jnp.where(r == c, 1.0, 0.0).astype(bf16)
    return jnp.dot(v.astype(bf16), ones_bd, preferred_element_type=f32)


CONV_LEFT = CONV_W // 2


def _conv_masks(mk_s, tseq):
    n = mk_s.shape[1]
    pos = lax.broadcasted_iota(jnp.int32, (n, LANES), 0) & (tseq - 1)
    t = 0
    for j in range(CONV_W):
        o = j - CONV_LEFT
        if o != 0:
            mk_s[t] = jnp.where((pos + o >= 0) & (pos + o < tseq), 1.0, 0.0)
            t += 1


def _seq_conv(z, cw_ref, cb_ref, mk_s):
    n, c = z.shape
    acc = z * cw_ref[CONV_LEFT:CONV_LEFT + 1, :] + cb_ref[...]
    t = 0
    for j in range(CONV_W):
        o = j - CONV_LEFT
        if o != 0:
            mask = jnp.tile(mk_s[t], (1, c // LANES)) if c != LANES else mk_s[t]
            acc = acc + (pltpu.roll(z, (-o) % n, axis=0) * mask) * cw_ref[j:j + 1, :]
            t += 1
    return acc


def _adaln_kernel(c_ref, w_ref, b_ref, o_ref):
    c = c_ref[...]
    a = c * _sigmoid(c)
    o_ref[...] = _dot(a, w_ref[...]) + b_ref[...]


def _adaln(cond, ada_w, ada_b):
    rows = cond.shape[0]
    out = pl.pallas_call(
        _adaln_kernel,
        grid=(DEPTH, 6),
        in_specs=[
            pl.BlockSpec((rows, D_MODEL), lambda l, k: (0, 0)),
            pl.BlockSpec((None, D_MODEL, D_MODEL), lambda l, k: (l, 0, k)),
            pl.BlockSpec((None, None, 1, D_MODEL), lambda l, k: (l, k, 0, 0)),
        ],
        out_specs=pl.BlockSpec((None, None, rows, D_MODEL), lambda l, k: (l, k, 0, 0)),
        out_shape=jax.ShapeDtypeStruct((DEPTH, 6, rows, D_MODEL), f32),
        compiler_params=_cparams(("arbitrary", "arbitrary")),
        name="adaln",
    )(cond, ada_w, ada_b.reshape(DEPTH, 6, 1, D_MODEL))
    return out.reshape(DEPTH, 6, rows, 1, D_MODEL)


def _stream_specs(np_tiles, ngrid, split):
    def spec(block_of_tile):
        if ngrid == 1:
            return pl.BlockSpec((TM, D_MODEL), lambda i: (block_of_tile(i), 0))
        return pl.BlockSpec((TM, D_MODEL), lambda i, j: (block_of_tile(i), 0))

    if not split:
        return [spec(lambda i: i)]
    return [spec(lambda i: jnp.minimum(i, np_tiles - 1)), spec(lambda i: jnp.maximum(i - np_tiles, 0))]


def _stream_load(np_tiles, refs):
    if len(refs) == 1:
        return refs[0][...]
    return jnp.where(pl.program_id(0) < np_tiles, refs[0][...], refs[1][...])


def _stream_store(np_tiles, refs, val):
    if len(refs) == 1:
        refs[0][...] = val
        return
    i = pl.program_id(0)

    @pl.when(i < np_tiles)
    def _():
        refs[0][...] = val

    @pl.when(i >= np_tiles)
    def _():
        refs[1][...] = val


def _stream_shapes(np_tiles, nt, split):
    if not split:
        return [jax.ShapeDtypeStruct((nt * TM, D_MODEL), f32)]
    return [jax.ShapeDtypeStruct((np_tiles * TM, D_MODEL), f32),
            jax.ShapeDtypeStruct(((nt - np_tiles) * TM, D_MODEL), f32)]


def _stream_tiles(ys):
    return sum(y.shape[0] for y in ys) // TM


def _mod_spec(layer, k, row_of_tile, ngrid):
    if ngrid == 1:
        return pl.BlockSpec((None, None, None, 1, D_MODEL), lambda i: (layer, k, row_of_tile(i), 0, 0))
    return pl.BlockSpec((None, None, None, 1, D_MODEL), lambda i, j: (layer, k, row_of_tile(i), 0, 0))


QKV_BLK = H_A * DK_A


def _delta_proj_kernel(np_tiles, t_p, t_s, n_y, *refs):
    y_refs = refs[:n_y]
    g_ref, sh_ref, sc_ref, w_ref, cw_ref, cb_ref, o_ref, h_s, mk_s = refs[n_y:]
    n = pl.program_id(1)

    @pl.when(n == 0)
    def _():
        x = _stream_load(np_tiles, y_refs)
        h_s[...] = _modulate(x, g_ref[...], sh_ref[...], sc_ref[...]).astype(bf16)
        _conv_masks(mk_s, jnp.where(pl.program_id(0) < np_tiles, t_p, t_s))

    z = jnp.dot(h_s[...], w_ref[...].astype(bf16), preferred_element_type=f32)
    y = _seq_conv(z, cw_ref, cb_ref, mk_s)
    y = y * _sigmoid(y)

    @pl.when(n < 2)
    def _():
        o_ref[...] = y * lax.rsqrt(_group_sum64(y * y) + EPS)

    @pl.when(n == 2)
    def _():
        o_ref[...] = y


def _delta_proj(ys, mods, layer, row_of_tile, np_tiles, t_p, t_s, norm_g, w_in, conv_w, conv_b):
    nt = _stream_tiles(ys)
    ntok = nt * TM
    kern = functools.partial(_delta_proj_kernel, np_tiles, t_p, t_s, len(ys))
    return pl.pallas_call(
        kern,
        grid=(nt, N_QKV // QKV_BLK),
        in_specs=_stream_specs(np_tiles, 2, len(ys) == 2) + [
            pl.BlockSpec((1, D_MODEL), lambda i, n: (0, 0)),
            _mod_spec(layer, 0, row_of_tile, 2),
            _mod_spec(layer, 1, row_of_tile, 2),
            pl.BlockSpec((None, D_MODEL, QKV_BLK), lambda i, n: (layer // 2, 0, n)),
            pl.BlockSpec((None, CONV_W, QKV_BLK), lambda i, n: (layer // 2, 0, n)),
            pl.BlockSpec((1, QKV_BLK), lambda i, n: (0, n)),
        ],
        out_specs=pl.BlockSpec((TM, QKV_BLK), lambda i, n: (i, n)),
        out_shape=jax.ShapeDtypeStruct((ntok, N_QKV), f32),
        scratch_shapes=[pltpu.VMEM((TM, D_MODEL), bf16), pltpu.VMEM((CONV_W - 1, TM, LANES), f32)],
        compiler_params=_cparams(("arbitrary", "arbitrary")),
        name="delta_proj",
    )(*ys, norm_g, mods, mods, w_in, conv_w, conv_b)


def _rope_swap(x):
    w = x.shape[1]
    lane = lax.broadcasted_iota(jnp.int32, (1, w), 1)
    first = (lane & (ROPE_AXIS - 1)) < ROPE_AXIS // 2
    return jnp.where(first, pltpu.roll(x, w - ROPE_AXIS // 2, axis=1), pltpu.roll(x, ROPE_AXIS // 2, axis=1))


def _attn_proj_kernel(np_tiles, n_y, *refs):
    y_refs = refs[:n_y]
    (g_ref, sh_ref, sc_ref, w_ref, qn_ref, kn_ref, cos_ref, sin_ref, al_ref, dt_ref,
     gate_ref, q_ref, k_ref, v_ref, gb_ref, kc_ref, vc_ref) = refs[n_y:]
    x = _stream_load(np_tiles, y_refs)
    h = _modulate(x, g_ref[...], sh_ref[...], sc_ref[...]).astype(bf16)
    z = jnp.dot(h, w_ref[...].astype(bf16), preferred_element_type=f32)
    nq = H_B * HD_B
    nkv = KV_B * HD_B
    o1 = H_A * DV_A
    o2 = o1 + nq + 2 * nkv
    gz = z[:, :o1]
    gate_ref[...] = gz * _sigmoid(gz)
    q = z[:, o1:o1 + nq]
    k = z[:, o1 + nq:o1 + nq + nkv]
    v = z[:, o1 + nq + nkv:o2]
    v_ref[...] = v
    qn = q * lax.rsqrt(_group_sum64(q * q) * (1.0 / HD_B) + EPS) * qn_ref[...]
    kn = k * lax.rsqrt(_group_sum64(k * k) * (1.0 / HD_B) + EPS) * kn_ref[...]
    cos = cos_ref[...]
    sin = sin_ref[...]
    q_ref[...] = qn * cos + _rope_swap(qn) * sin
    k_ref[...] = kn * cos[:, :nkv] + _rope_swap(kn) * sin[:, :nkv]

    @pl.when(pl.program_id(0) < np_tiles)
    def _():
        kc_ref[...] = kn
        vc_ref[...] = v
    zs = z[:, o2:]
    lane = lax.broadcasted_iota(jnp.int32, (1, LANES), 1)
    g_val = -jnp.exp(al_ref[...]) * _softplus(zs + dt_ref[...])
    gb_ref[...] = jnp.where(lane < 2 * H_A, g_val, jnp.where(lane < 4 * H_A, _sigmoid(zs), 0.0))


def _attn_proj(ys, mods, layer, row_of_tile, np_tiles, norm_g, w_b, qn_row, kn_row, cos_tab, sin_tab,
               al_row, dt_row):
    nt = _stream_tiles(ys)
    ntok = nt * TM
    nq = H_B * HD_B
    nkv = KV_B * HD_B
    tab_idx = lambda i: (jnp.where(i < np_tiles, 0, 1), 0)
    cache_idx = lambda i: (jnp.minimum(i, np_tiles - 1), 0)
    return pl.pallas_call(
        functools.partial(_attn_proj_kernel, np_tiles, len(ys)),
        grid=(nt,),
        in_specs=_stream_specs(np_tiles, 1, len(ys) == 2) + [
            pl.BlockSpec((1, D_MODEL), lambda i: (0, 0)),
            _mod_spec(layer, 0, row_of_tile, 1),
            _mod_spec(layer, 1, row_of_tile, 1),
            pl.BlockSpec((D_MODEL, N_ATTN), lambda i: (0, 0)),
            pl.BlockSpec((1, nq), lambda i: (0, 0)),
            pl.BlockSpec((1, nkv), lambda i: (0, 0)),
            pl.BlockSpec((TM, nq), tab_idx),
            pl.BlockSpec((TM, nq), tab_idx),
            pl.BlockSpec((1, LANES), lambda i: (0, 0)),
            pl.BlockSpec((1, LANES), lambda i: (0, 0)),
        ],
        out_specs=[
            pl.BlockSpec((TM, H_A * DV_A), lambda i: (i, 0)),
            pl.BlockSpec((TM, nq), lambda i: (i, 0)),
            pl.BlockSpec((TM, nkv), lambda i: (i, 0)),
            pl.BlockSpec((TM, nkv), lambda i: (i, 0)),
            pl.BlockSpec((TM, LANES), lambda i: (i, 0)),
            pl.BlockSpec((TM, nkv), cache_idx),
            pl.BlockSpec((TM, nkv), cache_idx),
        ],
        out_shape=[jax.ShapeDtypeStruct((ntok, H_A * DV_A), f32),
                   jax.ShapeDtypeStruct((ntok, nq), f32),
                   jax.ShapeDtypeStruct((ntok, nkv), f32),
                   jax.ShapeDtypeStruct((ntok, nkv), f32),
                   jax.ShapeDtypeStruct((ntok, LANES), f32),
                   jax.ShapeDtypeStruct((np_tiles * TM, nkv), f32),
                   jax.ShapeDtypeStruct((np_tiles * TM, nkv), f32)],
        compiler_params=_cparams(("arbitrary",)),
        name="attn_proj",
    )(*ys, norm_g, mods, mods, w_b, qn_row, kn_row, cos_tab, sin_tab, al_row, dt_row)


INV_BASE = 8
PRE_CHUNKS = 4
DELTA_SEQS_P = 4
HPG = MXU_DIM // DK_A
NLG = H_A // HPG
W_ALL = H_A * DK_A


def _chunk_scan(x, reverse):
    t = x.shape[0]
    r = lax.broadcasted_iota(jnp.int32, (t, 1), 0) & (CHUNK - 1)
    s = 1
    while s < CHUNK:
        if reverse:
            x = x + jnp.where(r + s < CHUNK, pltpu.roll(x, t - s, axis=0), 0.0)
        else:
            x = x + jnp.where(r >= s, pltpu.roll(x, s, axis=0), 0.0)
        s *= 2
    return x


def _delta_kernel(t, nsq, has_s0, *refs):
    if has_s0:
        q_ref, k_ref, v_ref, gb_ref, s0_ref = refs[:5]
        rest = refs[5:]
    else:
        q_ref, k_ref, v_ref, gb_ref = refs[:4]
        s0_ref = None
        rest = refs[4:]
    o_ref, sf_ref, exp_s, kbg_s, qg_s, kd_s, vb_s, egt_s, ti_s, in_s, st_s = rest
    n = t // CHUNK
    scale = DK_A ** -0.5

    gb = gb_ref[...]
    lane = lax.broadcasted_iota(jnp.int32, (1, LANES), 1)
    sc = jnp.where(lane < H_A, _chunk_scan(gb, False), jnp.where(lane < 2 * H_A, _chunk_scan(gb, True), gb))
    er = lax.broadcasted_iota(jnp.int32, (LANES, 4 * W_ALL), 0)
    ec = lax.broadcasted_iota(jnp.int32, (LANES, 4 * W_ALL), 1) >> 6
    expand = jnp.where(er == ec, 1.0, 0.0).astype(bf16)
    rb = min(t, 256)
    for r0 in range(0, nsq * t, rb):
        p1, p2, p3 = _split3(sc[r0:r0 + rb])
        exp_s[r0:r0 + rb, :] = ((jnp.dot(p1, expand, preferred_element_type=f32)
                                 + jnp.dot(p2, expand, preferred_element_type=f32))
                                + jnp.dot(p3, expand, preferred_element_type=f32))

    ri = lax.broadcasted_iota(jnp.int32, (CHUNK, MXU_DIM), 0)
    col_i = lax.broadcasted_iota(jnp.int32, (CHUNK, MXU_DIM), 1) & (CHUNK - 1)
    eye_ls = ri == col_i
    bd_mask = ((lax.broadcasted_iota(jnp.int32, (MXU_DIM, MXU_DIM), 0) >> 6)
               == (lax.broadcasted_iota(jnp.int32, (MXU_DIM, MXU_DIM), 1) >> 6))
    ones_c = jnp.ones((CHUNK, CHUNK), bf16)

    def bd(x):
        xb = x.astype(bf16)
        return jnp.where(bd_mask, jnp.concatenate([xb] * HPG, axis=0), jnp.zeros((), bf16))

    def blk(b):
        sh = b.bit_length() - 1
        return (ri >> sh) == (col_i >> sh)

    def hmm(x, y):
        return jnp.dot(x.astype(bf16), bd(y), preferred_element_type=f32)

    def pre_body(cp, carry):
        units = [(ci, d, lg) for ci in range(PRE_CHUNKS) for lg in range(NLG) for d in range(2)]
        cidx = [cp * PRE_CHUNKS + ci for ci in range(PRE_CHUNKS)]
        gcs, prods = {}, {}
        for ci, c in enumerate(cidx):
            rows = pl.ds(pl.multiple_of(c * CHUNK, CHUNK), CHUNK)
            q = q_ref[rows, :]
            k = k_ref[rows, :]
            v = v_ref[rows, :]
            kbs = []
            for d in range(2):
                gc = exp_s[rows, d * W_ALL:(d + 1) * W_ALL]
                beta = exp_s[rows, (2 + d) * W_ALL:(3 + d) * W_ALL]
                gtot = gc[CHUNK - 1:CHUNK, :] if d == 0 else gc[0:1, :]
                eg = jnp.exp(gc)
                kb = k * beta
                kbg_s[d, rows, :] = (kb * eg).astype(bf16)
                qg_s[d, rows, :] = (q * (scale * eg)).astype(bf16)
                kd_s[d, rows, :] = (k * jnp.exp(gtot - gc)).astype(bf16)
                vb_s[d, rows, :] = v * beta
                egt_s[d, c] = jnp.zeros((8, W_ALL), f32) + jnp.exp(gtot)
                gcs[ci, d] = gc
                kbs.append(kb)
            for lg in range(NLG):
                cols = slice(lg * MXU_DIM, (lg + 1) * MXU_DIM)
                lhs = jnp.concatenate([kbs[0][:, cols], kbs[1][:, cols], q[:, cols] * scale], axis=0)
                prods[ci, lg] = _dot_nt(lhs, bd(k[:, cols]))
        ps, ys, avs = [], [], []
        for ci, d, lg in units:
            c = cidx[ci]
            cols = slice(lg * MXU_DIM, (lg + 1) * MXU_DIM)
            gc = gcs[ci, d][:, cols]
            prod = prods[ci, lg]
            d1, d2, d3 = _split3(jnp.where(eye_ls, gc, 0.0))
            grow = ((jnp.dot(ones_c, d1, preferred_element_type=f32)
                     + jnp.dot(ones_c, d2, preferred_element_type=f32))
                    + jnp.dot(ones_c, d3, preferred_element_type=f32))
            keep = (ri >= col_i) if d == 0 else (ri <= col_i)
            strict = (ri > col_i) if d == 0 else (ri < col_i)
            decay = jnp.where(keep, jnp.exp(jnp.where(keep, gc - grow, 0.0)), 0.0)
            a = jnp.where(strict, prod[d * CHUNK:(d + 1) * CHUNK] * decay, 0.0)
            in_s[d, c, :, cols] = jnp.where(keep, prod[2 * CHUNK:] * decay, 0.0).astype(bf16)
            ab = a.astype(bf16)
            avs.append(ab)
            ys.append(jnp.where(blk(INV_BASE), -ab, jnp.zeros((), bf16)))
            ps.append(jnp.where(eye_ls, 1.0, 0.0) - jnp.where(blk(INV_BASE), a, 0.0))
        m = 1
        while m < INV_BASE:
            last = 2 * m >= INV_BASE
            for u in range(len(units)):
                y = ys[u]
                if m == 1:
                    ys[u] = hmm(y, y).astype(bf16)
                elif last:
                    ps[u] = ps[u] + hmm(ps[u], y)
                else:
                    r = hmm(jnp.concatenate([ps[u].astype(bf16), y], axis=0), y)
                    ps[u] = ps[u] + r[:CHUNK]
                    ys[u] = r[CHUNK:].astype(bf16)
            m *= 2
        b = INV_BASE
        while b < CHUNK:
            off = blk(2 * b) & jnp.logical_not(blk(b))
            ws = [hmm(jnp.where(off, avs[u], jnp.zeros((), bf16)), ps[u]) for u in range(len(units))]
            for u in range(len(units)):
                ps[u] = ps[u] - hmm(ps[u], ws[u])
            b *= 2
        for u, (ci, d, lg) in enumerate(units):
            ti_s[d, cidx[ci], :, lg * MXU_DIM:(lg + 1) * MXU_DIM] = ps[u].astype(bf16)
        return carry

    lax.fori_loop(0, nsq * n // PRE_CHUNKS, pre_body, 0)

    for sq in range(nsq):
        for d in range(2):
            for lg in range(NLG):
                if has_s0:
                    blocks = []
                    for hh in range(HPG):
                        s_h = s0_ref[sq, d, lg * HPG + hh]
                        z_l = jnp.zeros((DK_A, hh * DV_A), f32)
                        z_r = jnp.zeros((DK_A, (HPG - 1 - hh) * DV_A), f32)
                        parts = ([z_l] if hh > 0 else []) + [s_h] + ([z_r] if hh < HPG - 1 else [])
                        blocks.append(jnp.concatenate(parts, axis=1) if len(parts) > 1 else s_h)
                    st_s[sq, d, lg] = jnp.concatenate(blocks, axis=0)
                else:
                    st_s[sq, d, lg] = jnp.zeros((MXU_DIM, MXU_DIM), f32)
    o_ref[...] = jnp.zeros_like(o_ref)

    def seq_body(j, carry):
        units = [(sq, d, lg) for sq in range(nsq) for lg in range(NLG) for d in range(2)]
        step = [j, n - 1 - j]
        cls = [slice(lg * MXU_DIM, (lg + 1) * MXU_DIM) for lg in range(NLG)]
        chunk = {(sq, d): sq * n + step[d] for sq in range(nsq) for d in range(2)}
        rws = {key: pl.ds(pl.multiple_of(cc * CHUNK, CHUNK), CHUNK) for key, cc in chunk.items()}
        boths, vnbs = [], []
        for sq, d, lg in units:
            r = rws[sq, d]
            lhs = jnp.concatenate([kbg_s[d, r, cls[lg]], qg_s[d, r, cls[lg]]], axis=0)
            boths.append(jnp.dot(lhs, st_s[sq, d, lg].astype(bf16), preferred_element_type=f32))
        for u, (sq, d, lg) in enumerate(units):
            resid = vb_s[d, rws[sq, d], cls[lg]] - boths[u][:CHUNK]
            v_new = jnp.dot(ti_s[d, chunk[sq, d], :, cls[lg]], bd(resid), preferred_element_type=f32)
            vnbs.append(v_new.astype(bf16))
        for u, (sq, d, lg) in enumerate(units):
            r = rws[sq, d]
            c = chunk[sq, d]
            o = boths[u][CHUNK:] + jnp.dot(in_s[d, c, :, cls[lg]], bd(vnbs[u]), preferred_element_type=f32)
            upd = lax.dot_general(kd_s[d, r, cls[lg]], vnbs[u], (((0,), (0,)), ((), ())),
                                  preferred_element_type=f32)
            st_s[sq, d, lg] = st_s[sq, d, lg] * egt_s[d, c, 0:1, cls[lg]] + jnp.where(bd_mask, upd, 0.0)
            o_ref[r, cls[lg]] = o_ref[r, cls[lg]] + o
        return carry

    lax.fori_loop(0, n, seq_body, 0)
    for sq in range(nsq):
        for d in range(2):
            for lg in range(NLG):
                s_fin = st_s[sq, d, lg]
                for hh in range(HPG):
                    sf_ref[sq, d, lg * HPG + hh] = s_fin[hh * DK_A:(hh + 1) * DK_A, hh * DV_A:(hh + 1) * DV_A]


def _delta(qkv, gb, t, nseq, nsq, tok_block0, s0, s0_layer):
    n = t // CHUNK
    tb = nsq * t
    nch = nsq * n
    has_s0 = s0 is not None
    kern = functools.partial(_delta_kernel, t, nsq, has_s0)
    in_specs = [
        pl.BlockSpec((tb, W_ALL), lambda b: (tok_block0 + b, 0)),
        pl.BlockSpec((tb, W_ALL), lambda b: (tok_block0 + b, 1)),
        pl.BlockSpec((tb, W_ALL), lambda b: (tok_block0 + b, 2)),
        pl.BlockSpec((tb, LANES), lambda b: (tok_block0 + b, 0)),
    ]
    args = [qkv, qkv, qkv, gb]
    if has_s0:
        in_specs.append(pl.BlockSpec((nsq, None, 2, H_A, DK_A, DV_A), lambda b: (b, s0_layer, 0, 0, 0, 0)))
        args.append(s0)
    return pl.pallas_call(
        kern,
        grid=(nseq // nsq,),
        in_specs=in_specs,
        out_specs=[
            pl.BlockSpec((tb, W_ALL), lambda b: (b, 0)),
            pl.BlockSpec((nsq, 2, H_A, DK_A, DV_A), lambda b: (b, 0, 0, 0, 0)),
        ],
        out_shape=[jax.ShapeDtypeStruct((nseq * t, W_ALL), f32),
                   jax.ShapeDtypeStruct((nseq, 2, H_A, DK_A, DV_A), f32)],
        scratch_shapes=[
            pltpu.VMEM((tb, 4 * W_ALL), f32),
            pltpu.VMEM((2, tb, W_ALL), bf16),
            pltpu.VMEM((2, tb, W_ALL), bf16),
            pltpu.VMEM((2, tb, W_ALL), bf16),
            pltpu.VMEM((2, tb, W_ALL), f32),
            pltpu.VMEM((2, nch, 8, W_ALL), f32),
            pltpu.VMEM((2, nch, CHUNK, W_ALL), bf16),
            pltpu.VMEM((2, nch, CHUNK, W_ALL), bf16),
            pltpu.VMEM((nsq, 2, NLG, MXU_DIM, MXU_DIM), f32),
        ],
        compiler_params=_cparams(("arbitrary",)),
        name="delta_p" if not has_s0 else "delta_s",
    )(*args)


def _attend(q_all, key_sets, sink_ref):
    nrow = q_all.shape[0]
    scores = []
    for h in range(H_B):
        kv = h // G_B
        q = q_all[:, h * HD_B:(h + 1) * HD_B]
        row = []
        for k_all, _, mask in key_sets:
            s = _dot_nt(q, k_all[:, kv * HD_B:(kv + 1) * HD_B])
            row.append(s if mask is None else jnp.where(mask, s, NEG_INF))
        scores.append(row)
    probs, sink_terms = [], []
    for h in range(H_B):
        sink = jnp.zeros((nrow, 1), f32) + sink_ref[h]
        m = sink
        for s in scores[h]:
            m = jnp.maximum(m, jnp.max(s, axis=-1, keepdims=True))
        probs.append([jnp.exp(s - m).astype(bf16) for s in scores[h]])
        sink_terms.append(jnp.exp(sink - m))
    v_ext = [[jnp.concatenate([v_all[:, kv * HD_B:(kv + 1) * HD_B].astype(bf16),
                               jnp.ones((v_all.shape[0], HD_B), bf16)], axis=1) for kv in range(KV_B)]
             for _, v_all, _ in key_sets]
    outs = []
    for h in range(H_B):
        acc = None
        for p, vs in zip(probs[h], v_ext):
            pv = jnp.dot(p, vs[h // G_B], preferred_element_type=f32)
            acc = pv if acc is None else acc + pv
        outs.append(acc[:, :HD_B] / (acc[:, HD_B:HD_B + 1] + sink_terms[h]))
    return jnp.concatenate(outs, axis=1)


def _ctx_attn_kernel(sink_ref, q_ref, k_ref, v_ref, o_ref):
    scale = HD_B ** -0.5
    o_ref[...] = _attend(q_ref[...] * scale, [(k_ref[...], v_ref[...], None)], sink_ref)


def _ctx_attn(sink, qb, kb, vb, t, nseq):
    nq = H_B * HD_B
    nkv = KV_B * HD_B
    return pl.pallas_call(
        _ctx_attn_kernel,
        grid=(nseq,),
        in_specs=[
            pl.BlockSpec(memory_space=pltpu.SMEM),
            pl.BlockSpec((t, nq), lambda b: (b, 0)),
            pl.BlockSpec((t, nkv), lambda b: (b, 0)),
            pl.BlockSpec((t, nkv), lambda b: (b, 0)),
        ],
        out_specs=pl.BlockSpec((t, nq), lambda b: (b, 0)),
        out_shape=jax.ShapeDtypeStruct((nseq * t, nq), f32),
        compiler_params=_cparams(("arbitrary",)),
        name="ctx_attn",
    )(sink, qb, kb, vb)


def _win_attn_kernel(t, sink_ref, q_ref, k_ref, v_ref, kc_ref, vc_ref, o_ref):
    i = pl.program_id(1)
    scale = HD_B ** -0.5
    span = QBLK + 2 * WINDOW
    start = i * QBLK
    lo = jnp.clip(start - WINDOW, 0, t - span)
    lo = pl.multiple_of(lo, QBLK)
    q_pos = start + lax.broadcasted_iota(jnp.int32, (QBLK, span), 0)
    k_pos = lo + lax.broadcasted_iota(jnp.int32, (QBLK, span), 1)
    valid = jnp.abs(q_pos - k_pos) <= WINDOW
    key_sets = [(k_ref[pl.ds(lo, span), :], v_ref[pl.ds(lo, span), :], valid), (kc_ref[...], vc_ref[...], None)]
    o_ref[...] = _attend(q_ref[...] * scale, key_sets, sink_ref)


def _win_attn(sink, qb, kb, vb, cache_k, cache_v, cache_layer, t, nseq, tok_block0):
    nq = H_B * HD_B
    nkv = KV_B * HD_B
    nb = t // QBLK
    past = cache_k.shape[2]
    kern = functools.partial(_win_attn_kernel, t)
    return pl.pallas_call(
        kern,
        grid=(nseq, nb),
        in_specs=[
            pl.BlockSpec(memory_space=pltpu.SMEM),
            pl.BlockSpec((QBLK, nq), lambda b, i: ((tok_block0 + b) * nb + i, 0)),
            pl.BlockSpec((t, nkv), lambda b, i: (tok_block0 + b, 0)),
            pl.BlockSpec((t, nkv), lambda b, i: (tok_block0 + b, 0)),
            pl.BlockSpec((None, None, past, nkv), lambda b, i: (b, cache_layer, 0, 0)),
            pl.BlockSpec((None, None, past, nkv), lambda b, i: (b, cache_layer, 0, 0)),
        ],
        out_specs=pl.BlockSpec((QBLK, nq), lambda b, i: (b * nb + i, 0)),
        out_shape=jax.ShapeDtypeStruct((nseq * t, nq), f32),
        compiler_params=_cparams(("arbitrary", "arbitrary")),
        name="win_attn",
    )(sink, qb, kb, vb, cache_k, cache_v)


def _mix_out_kernel(np_tiles, n_y, *refs):
    y_refs = refs[:n_y]
    oap_ref, oas_ref, obp_ref, obs_ref, gate_ref, dn_ref, w_ref, gm_ref = refs[n_y:n_y + 8]
    o_refs = refs[n_y + 8:]
    i = pl.program_id(0)
    is_p = i < np_tiles
    oa = jnp.where(is_p, oap_ref[...], oas_ref[...])
    ob = jnp.where(is_p, obp_ref[...], obs_ref[...])
    oan = oa * lax.rsqrt(_group_sum64(oa * oa) * (1.0 / DV_A) + EPS) * dn_ref[...] * gate_ref[...]
    ka = H_A * DV_A
    mix = _dot(oan, w_ref[:ka, :]) + _dot(ob, w_ref[ka:, :])
    _stream_store(np_tiles, o_refs, _stream_load(np_tiles, y_refs) + gm_ref[...] * mix)


def _mix_out(ys, mods, layer, row_of_tile, np_tiles, oa_p, oa_s, ob_p, ob_s, gate, dn_row, w_out, split_out):
    nt = _stream_tiles(ys)
    ka = H_A * DV_A
    p_idx = lambda i: (jnp.minimum(i, np_tiles - 1), 0)
    s_idx = lambda i: (jnp.maximum(i - np_tiles, 0), 0)
    kern = functools.partial(_mix_out_kernel, np_tiles, len(ys))
    return pl.pallas_call(
        kern,
        grid=(nt,),
        in_specs=_stream_specs(np_tiles, 1, len(ys) == 2) + [
            pl.BlockSpec((TM, ka), p_idx),
            pl.BlockSpec((TM, ka), s_idx),
            pl.BlockSpec((TM, ka), p_idx),
            pl.BlockSpec((TM, ka), s_idx),
            pl.BlockSpec((TM, ka), lambda i: (i, 0)),
            pl.BlockSpec((1, ka), lambda i: (0, 0)),
            pl.BlockSpec((None, 2 * ka, D_MODEL), lambda i: (layer // 2, 0, 0)),
            _mod_spec(layer, 2, row_of_tile, 1),
        ],
        out_specs=_stream_specs(np_tiles, 1, split_out),
        out_shape=_stream_shapes(np_tiles, nt, split_out),
        compiler_params=_cparams(("arbitrary",)),
        name="mix_out",
    )(*ys, oa_p, oa_s, ob_p, ob_s, gate, dn_row, w_out, mods)


def _gelu_tanh(x):
    return 0.5 * x * (1.0 + jnp.tanh(math.sqrt(2.0 / math.pi) * (x + 0.044715 * (x * x * x))))


def _lru_kernel(np_tiles, t_p, t_s, y_ref, g_ref, sh_ref, sc_ref, wx_ref, wg_ref, cw_ref, cb_ref,
                wl_ref, bl_ref, lam_ref, h0_ref, o_ref, fin_ref, h_s, x_s, gt_s, hf_s, hb_s, mk_s):
    is_p = pl.program_id(0) < np_tiles
    tseq = jnp.where(is_p, t_p, t_s)

    @pl.when(pl.program_id(1) == 0)
    def _():
        h_s[...] = _modulate(y_ref[...], g_ref[...], sh_ref[...], sc_ref[...]).astype(bf16)
        _conv_masks(mk_s, tseq)

    w = jnp.concatenate([wx_ref[...], wg_ref[...]], axis=1).astype(bf16)
    z = jnp.dot(h_s[...], w, preferred_element_type=f32)
    x = _seq_conv(z[:, :LRU_W], cw_ref, cb_ref, mk_s)
    x_s[...] = x
    ngrp = LRU_W // LRU_GW
    gcols = [slice(g * LRU_GW, (g + 1) * LRU_GW) for g in range(ngrp)]
    for g in range(ngrp):
        gt_s[g] = _dot(x[:, gcols[g]], wl_ref[g]) + bl_ref[g]
    decay = [[RG_C * _softplus(-lam_ref[g, :, d * LRU_GW:(d + 1) * LRU_GW]) for g in range(ngrp)]
             for d in range(2)]
    sub = lax.broadcasted_iota(jnp.int32, (SUBLANES, 1), 0)
    nblk = TM // SUBLANES
    seg_shift = (t_p // SUBLANES).bit_length() - 1

    def body(kf, carry):
        nxt = []
        for d, kk in ((0, kf), (1, nblk - 1 - kf)):
            r0 = pl.multiple_of(kk * SUBLANES, SUBLANES)
            rows = pl.ds(r0, SUBLANES)
            edge = r0 if d == 0 else r0 + SUBLANES
            at_edge = (edge & (tseq - 1)) == 0
            seg = jnp.where(is_p, kk >> seg_shift, 0)
            for g in range(ngrp):
                xb = x_s[rows, gcols[g]]
                gb = gt_s[g, rows, :]
                r = _sigmoid(gb[:, (2 * d) * LRU_GW:(2 * d + 1) * LRU_GW])
                ig = _sigmoid(gb[:, (2 * d + 1) * LRU_GW:(2 * d + 2) * LRU_GW])
                log_a = -(r * decay[d][g])
                a = jnp.exp(log_a)
                w2 = -jnp.tanh(log_a) * (a * a + 1.0)
                u = jnp.where(w2 > 0.0, w2 * lax.rsqrt(w2), 0.0) * (ig * xb)
                s = 1
                while s < SUBLANES:
                    ok = (sub >= s) if d == 0 else (sub + s < SUBLANES)
                    sh = s if d == 0 else SUBLANES - s
                    u = u + a * jnp.where(ok, pltpu.roll(u, sh, axis=0), 0.0)
                    a = a * jnp.where(ok, pltpu.roll(a, sh, axis=0), 1.0)
                    s *= 2
                c_in = jnp.where(at_edge, h0_ref[seg, d:d + 1, gcols[g]], carry[d * ngrp + g])
                hblk = u + a * c_in
                (hf_s if d == 0 else hb_s)[rows, gcols[g]] = hblk
                nxt.append(hblk[SUBLANES - 1:SUBLANES, :] if d == 0 else hblk[0:1, :])
        return tuple(nxt)

    zero = jnp.zeros((1, LRU_GW), f32)
    lax.fori_loop(0, nblk, body, (zero,) * (2 * ngrp), unroll=4)
    for sg in range(TM // t_p):
        fin_ref[sg, 0:1, :] = hf_s[sg * t_p + t_p - 1:sg * t_p + t_p, :]
        fin_ref[sg, 1:2, :] = hb_s[sg * t_p:sg * t_p + 1, :]
    o_ref[...] = (hf_s[...] + hb_s[...]) * _gelu_tanh(z[:, LRU_W:])


def _lru(y, mods, layer, row_of_tile, np_tiles, t_p, t_s, norm_g, w_in, cw, cb, wl, bl, lam, h0):
    ntok = y.shape[0]
    nt = ntok // TM
    ncol = D_RNN // LRU_W
    ngrp = LRU_W // LRU_GW
    nseg = TM // t_p
    kern = functools.partial(_lru_kernel, np_tiles, t_p, t_s)
    return pl.pallas_call(
        kern,
        grid=(nt, ncol),
        in_specs=[
            pl.BlockSpec((TM, D_MODEL), lambda i, j: (i, 0)),
            pl.BlockSpec((1, D_MODEL), lambda i, j: (0, 0)),
            _mod_spec(layer, 0, row_of_tile, 2),
            _mod_spec(layer, 1, row_of_tile, 2),
            pl.BlockSpec((None, D_MODEL, LRU_W), lambda i, j: (layer // 2, 0, j)),
            pl.BlockSpec((None, D_MODEL, LRU_W), lambda i, j: (layer // 2, 0, ncol + j)),
            pl.BlockSpec((None, CONV_W, LRU_W), lambda i, j: (layer // 2, 0, j)),
            pl.BlockSpec((1, LRU_W), lambda i, j: (0, j)),
            pl.BlockSpec((ngrp, LRU_GW, 4 * LRU_GW), lambda i, j: (j, 0, 0)),
            pl.BlockSpec((ngrp, 1, 4 * LRU_GW), lambda i, j: (j, 0, 0)),
            pl.BlockSpec((ngrp, 1, 2 * LRU_GW), lambda i, j: (j, 0, 0)),
            pl.BlockSpec((None, nseg, 2, LRU_W), lambda i, j: (i, 0, 0, j)),
        ],
        out_specs=[
            pl.BlockSpec((TM, LRU_W), lambda i, j: (i, j)),
            pl.BlockSpec((None, nseg, 2, LRU_W), lambda i, j: (i, 0, 0, j)),
        ],
        out_shape=[jax.ShapeDtypeStruct((ntok, D_RNN), f32),
                   jax.ShapeDtypeStruct((nt, nseg, 2, D_RNN), f32)],
        scratch_shapes=[pltpu.VMEM((TM, D_MODEL), bf16), pltpu.VMEM((TM, LRU_W), f32),
                        pltpu.VMEM((ngrp, TM, 4 * LRU_GW), f32), pltpu.VMEM((TM, LRU_W), f32),
                        pltpu.VMEM((TM, LRU_W), f32), pltpu.VMEM((CONV_W - 1, TM, LANES), f32)],
        compiler_params=_cparams(("arbitrary", "arbitrary")),
        name="lru",
    )(y, norm_g, mods, mods, w_in, w_in, cw, cb, wl, bl, lam, h0)


def _proj_out_kernel(y_ref, a_ref, w_ref, gm_ref, o_ref):
    o_ref[...] = y_ref[...] + gm_ref[...] * _dot(a_ref[...], w_ref[...])


def _proj_out(y, mods, layer, row_of_tile, a, w):
    ntok = y.shape[0]
    nt = ntok // TM
    k = a.shape[1]
    return pl.pallas_call(
        _proj_out_kernel,
        grid=(nt,),
        in_specs=[
            pl.BlockSpec((TM, D_MODEL), lambda i: (i, 0)),
            pl.BlockSpec((TM, k), lambda i: (i, 0)),
            pl.BlockSpec((None, k, D_MODEL), lambda i: (layer // 2, 0, 0)),
            _mod_spec(layer, 2, row_of_tile, 1),
        ],
        out_specs=pl.BlockSpec((TM, D_MODEL), lambda i: (i, 0)),
        out_shape=jax.ShapeDtypeStruct((ntok, D_MODEL), f32),
        compiler_params=_cparams(("arbitrary",)),
        name="proj_out",
    )(y, a, w, mods)


FF_BLK = 1024


def _mlp_kernel(np_tiles, n_out, y_ref, g_ref, sh_ref, sc_ref, gm_ref, w1_ref, w2_ref, *refs):
    o_refs = refs[:n_out]
    h_s, acc_s = refs[n_out:]
    k = pl.program_id(1)

    @pl.when(k == 0)
    def _():
        h_s[...] = _modulate(y_ref[...], g_ref[...], sh_ref[...], sc_ref[...]).astype(bf16)
        acc_s[...] = jnp.zeros_like(acc_s)

    a = jnp.dot(h_s[...], w1_ref[...].astype(bf16), preferred_element_type=f32)
    a = jnp.maximum(a, 0.0)
    acc_s[...] += _dot(a * a, w2_ref[...])

    @pl.when(k == pl.num_programs(1) - 1)
    def _():
        _stream_store(np_tiles, o_refs, y_ref[...] + gm_ref[...] * acc_s[...])


def _mlp(y, mods, layer, row_of_tile, np_tiles, norm_g, w1, w2, split_out):
    nt = y.shape[0] // TM
    return pl.pallas_call(
        functools.partial(_mlp_kernel, np_tiles, 2 if split_out else 1),
        grid=(nt, D_FF // FF_BLK),
        in_specs=[
            pl.BlockSpec((TM, D_MODEL), lambda i, k: (i, 0)),
            pl.BlockSpec((1, D_MODEL), lambda i, k: (0, 0)),
            _mod_spec(layer, 3, row_of_tile, 2),
            _mod_spec(layer, 4, row_of_tile, 2),
            _mod_spec(layer, 5, row_of_tile, 2),
            pl.BlockSpec((None, D_MODEL, FF_BLK), lambda i, k: (layer, 0, k)),
            pl.BlockSpec((None, FF_BLK, D_MODEL), lambda i, k: (layer, k, 0)),
        ],
        out_specs=_stream_specs(np_tiles, 2, split_out),
        out_shape=_stream_shapes(np_tiles, nt, split_out),
        scratch_shapes=[pltpu.VMEM((TM, D_MODEL), bf16), pltpu.VMEM((TM, D_MODEL), f32)],
        compiler_params=_cparams(("arbitrary", "arbitrary")),
        name="mlp",
    )(y, norm_g, mods, mods, mods, w1, w2)


def _attn_cols(w_in, a_log, dt_bias):
    s2 = N_QKV + H_A * DV_A
    s4 = s2 + 4 * H_A
    pad = jnp.zeros((D_MODEL, LANES - 4 * H_A), f32)
    w_b = jnp.concatenate([w_in[:, N_QKV:s2], w_in[:, s4:], w_in[:, s2:s4], pad], axis=1)
    row_pad = jnp.zeros((LANES - 2 * H_A,), f32)
    al_row = jnp.concatenate([a_log.reshape(-1), row_pad]).reshape(1, LANES)
    dt_row = jnp.concatenate([dt_bias.reshape(-1), row_pad]).reshape(1, LANES)
    return w_b, al_row, dt_row


def _rope_tables(t):
    rows = t // GRID_W
    r = np.repeat(np.arange(rows, dtype=np.float32), GRID_W)
    cc = np.tile(np.arange(GRID_W, dtype=np.float32), rows)
    inv = np.float32(ROPE_BASE) ** (-np.arange(0, ROPE_AXIS, 2, dtype=np.float32) / np.float32(ROPE_AXIS))
    ang_r = (r[:, None] * inv).astype(np.float32)
    ang_c = (cc[:, None] * inv).astype(np.float32)
    cos = np.concatenate([np.cos(ang_r), np.cos(ang_r), np.cos(ang_c), np.cos(ang_c)], axis=-1)
    sin = np.concatenate([-np.sin(ang_r), np.sin(ang_r), -np.sin(ang_c), np.sin(ang_c)], axis=-1)
    cos = np.tile(cos, (1, H_B)).astype(np.float32)
    sin = np.tile(sin, (1, H_B)).astype(np.float32)
    ident_c = np.ones((TM, H_B * HD_B), np.float32)
    ident_s = np.zeros((TM, H_B * HD_B), np.float32)
    return (jnp.asarray(np.concatenate([ident_c, cos], axis=0)),
            jnp.asarray(np.concatenate([ident_s, sin], axis=0)))


def _lru_cols(w_a, b_a, w_x, b_x, lam):
    ncol = D_RNN // LRU_GW
    per = LRU_GW // LRU_BW

    def bd(w):
        w = w.reshape(ncol, per, LRU_BW, LRU_BW)
        eye = jnp.eye(per, dtype=f32)
        return jnp.einsum('gpij,pq->gpiqj', w, eye).reshape(ncol, LRU_GW, LRU_GW)

    wl = jnp.concatenate([bd(w_a[0]), bd(w_x[0]), bd(w_a[1]), bd(w_x[1])], axis=-1)
    rows = lambda v: v.reshape(ncol, 1, LRU_GW)
    bl = jnp.concatenate([rows(b_a[0]), rows(b_x[0]), rows(b_a[1]), rows(b_x[1])], axis=-1)
    lm = jnp.concatenate([rows(lam[0]), rows(lam[1])], axis=-1)
    return wl, bl, lm


def kernel(x_prompt, x_sample, state_delta, cache_k, cache_v, state_lru, c, c_ctx, ada_w, ada_b, norm1_g, norm2_g, ff_w1, ff_w2, ab_w_in, ab_conv_w, ab_conv_b, dn_a_log, dn_dt_bias, dn_norm_g, attn_q_norm_g, attn_k_norm_g, attn_sink, ab_w_out, c_w_in, c_conv_w, c_conv_b, lru_w_a, lru_b_a, lru_w_x, lru_b_x, lru_lambda, c_w_out):
    b_p, t_p, _ = x_prompt.shape
    b_s, t_s, _ = x_sample.shape
    assert t_s == TM and TM % t_p == 0 and (b_p * t_p) % TM == 0
    np_tok = b_p * t_p
    np_tiles = np_tok // TM
    nkv = KV_B * HD_B

    ys = (x_prompt.reshape(np_tok, D_MODEL), x_sample.reshape(b_s * t_s, D_MODEL))
    nt = np_tiles + b_s * t_s // TM

    rows = -(-(b_s + 1) // 8) * 8
    cond = jnp.zeros((rows, D_MODEL), f32).at[:b_s].set(c).at[b_s].set(c_ctx)
    mods = _adaln(cond, ada_w, ada_b)
    row_of_tile = lambda i: jnp.where(i < np_tiles, b_s, i - np_tiles)

    cos_tab, sin_tab = _rope_tables(t_s)
    ck = cache_k.reshape(cache_k.shape[0], cache_k.shape[1], cache_k.shape[2], nkv)
    cv = cache_v.reshape(ck.shape)

    new_dn, new_k, new_v, new_lru = [], [], [], []
    for l in range(DEPTH):
        j = l // 2
        n1 = norm1_g[l].reshape(1, D_MODEL)
        if l % 2 == 0:
            qkv = _delta_proj(ys, mods, l, row_of_tile, np_tiles, t_p, t_s, n1, ab_w_in, ab_conv_w,
                              ab_conv_b[j].reshape(1, N_QKV))
            w_b, al_row, dt_row = _attn_cols(ab_w_in[j], dn_a_log[j], dn_dt_bias[j])
            qn_row = jnp.tile(attn_q_norm_g[j], H_B).reshape(1, H_B * HD_B)
            kn_row = jnp.tile(attn_k_norm_g[j], KV_B).reshape(1, nkv)
            gate, qb, kb, vb, gb, kc, vc = _attn_proj(ys, mods, l, row_of_tile, np_tiles, n1, w_b, qn_row, kn_row,
                                                      cos_tab, sin_tab, al_row, dt_row)
            oa_p, s_fin = _delta(qkv, gb, t_p, b_p, DELTA_SEQS_P, 0, None, 0)
            oa_s, _ = _delta(qkv, gb, t_s, b_s, 1, np_tok // t_s, state_delta, j)
            ob_p = _ctx_attn(attn_sink[j], qb, kb, vb, t_p, b_p)
            ob_s = _win_attn(attn_sink[j], qb, kb, vb, ck, cv, j, t_s, b_s, np_tok // t_s)
            dn_row = jnp.tile(dn_norm_g[j], H_A).reshape(1, H_A * DV_A)
            (y,) = _mix_out(ys, mods, l, row_of_tile, np_tiles, oa_p, oa_s, ob_p, ob_s, gate, dn_row, ab_w_out,
                            False)
            new_dn.append(s_fin)
            new_k.append(kc.reshape(b_p, t_p, KV_B, HD_B))
            new_v.append(vc.reshape(b_p, t_p, KV_B, HD_B))
        else:
            wl, bl, lm = _lru_cols(lru_w_a[j], lru_b_a[j], lru_w_x[j], lru_b_x[j], lru_lambda[j])
            nseg = TM // t_p
            h0 = jnp.zeros((nt, nseg, 2, D_RNN), f32).at[np_tiles:, 0].set(state_lru[:, j])
            mixed, fin = _lru(ys[0], mods, l, row_of_tile, np_tiles, t_p, t_s, n1, c_w_in, c_conv_w,
                              c_conv_b[j].reshape(1, D_RNN), wl, bl, lm, h0)
            y = _proj_out(ys[0], mods, l, row_of_tile, mixed, c_w_out)
            new_lru.append(fin[:np_tiles].reshape(b_p, 2, D_RNN))
        ys = tuple(_mlp(y, mods, l, row_of_tile, np_tiles, norm2_g[l].reshape(1, D_MODEL), ff_w1, ff_w2,
                        l == DEPTH - 1))

    yp = ys[0].reshape(b_p, t_p, D_MODEL)
    ysm = ys[1].reshape(b_s, t_s, D_MODEL)
    return (yp, ysm, jnp.stack(new_dn, axis=1), jnp.stack(new_k, axis=1), jnp.stack(new_v, axis=1),
            jnp.stack(new_lru, axis=1))
```

```python
import functools
import math

import jax
import jax.numpy as jnp
import numpy as np
from jax import lax
from jax.experimental import pallas as pl
from jax.experimental.pallas import tpu as pltpu

f32 = jnp.float32
bf16 = jnp.bfloat16

D_MODEL = 1024
DEPTH = 4
GRID_W = 64
H_A = 8
DK_A = 64
DV_A = 64
CHUNK = 64
CONV_W = 4
H_B = 8
KV_B = 2
G_B = H_B // KV_B
HD_B = 64
WINDOW = 128
QBLK = 128
ROPE_AXIS = HD_B // 2
ROPE_BASE = 10000.0
D_RNN = D_MODEL
LRU_BLOCKS = 16
LRU_BW = D_RNN // LRU_BLOCKS
RG_C = 8.0
D_FF = 4 * D_MODEL
EPS = 1e-6
NEG_INF = -1e30

TM = 1024
LANES = 128
SUBLANES = 8
LRU_W = 512
LRU_GW = 256
MXU_DIM = 256
N_QKV = 3 * H_A * DK_A
N_ATTN = H_A * DV_A + (H_B + 2 * KV_B) * HD_B + LANES
VMEM_LIMIT = 56 * 1024 * 1024


def _cparams(sem):
    return pltpu.CompilerParams(dimension_semantics=sem, vmem_limit_bytes=VMEM_LIMIT)


def _dot(a, b):
    return jnp.dot(a.astype(bf16), b.astype(bf16), preferred_element_type=f32)


def _dot_nt(a, b):
    return lax.dot_general(a.astype(bf16), b.astype(bf16), (((1,), (1,)), ((), ())),
                           preferred_element_type=f32)


def _dot_tn(a, b):
    return lax.dot_general(a.astype(bf16), b.astype(bf16), (((0,), (0,)), ((), ())),
                           preferred_element_type=f32)


def _sigmoid(x):
    return 1.0 / (1.0 + jnp.exp(-x))


def _softplus(x):
    return jnp.maximum(x, 0.0) + jnp.log1p(jnp.exp(-jnp.abs(x)))


def _modulate(x, g, shift, scale):
    ms = jnp.mean(x * x, axis=-1, keepdims=True)
    return (x * lax.rsqrt(ms + EPS)) * (g * (1.0 + scale)) + shift


def _split3(x):
    p1 = x.astype(bf16)
    r1 = x - p1.astype(f32)
    p2 = r1.astype(bf16)
    p3 = (r1 - p2.astype(f32)).astype(bf16)
    return p1, p2, p3


def _group_sum64(v):
    w = v.shape[-1]
    r = lax.broadcasted_iota(jnp.int32, (w, w), 0) >> 6
    c = lax.broadcasted_iota(jnp.int32, (w, w), 1) >> 6
    ones_bd = jnp.where(r == c, 1.0, 0.0).astype(bf16)
    return jnp.dot(v.astype(bf16), ones_bd, preferred_element_type=f32)


CONV_LEFT = CONV_W // 2


def _conv_masks(mk_s, tseq):
    n = mk_s.shape[1]
    pos = lax.broadcasted_iota(jnp.int32, (n, LANES), 0) & (tseq - 1)
    t = 0
    for j in range(CONV_W):
        o = j - CONV_LEFT
        if o != 0:
            mk_s[t] = jnp.where((pos + o >= 0) & (pos + o < tseq), 1.0, 0.0)
            t += 1


def _seq_conv(z, cw_ref, cb_ref, mk_s):
    n, c = z.shape
    acc = z * cw_ref[CONV_LEFT:CONV_LEFT + 1, :] + cb_ref[...]
    t = 0
    for j in range(CONV_W):
        o = j - CONV_LEFT
        if o != 0:
            mask = jnp.tile(mk_s[t], (1, c // LANES)) if c != LANES else mk_s[t]
            acc = acc + (pltpu.roll(z, (-o) % n, axis=0) * mask) * cw_ref[j:j + 1, :]
            t += 1
    return acc


def _adaln_kernel(c_ref, w_ref, b_ref, o_ref):
    c = c_ref[...]
    a = c * _sigmoid(c)
    o_ref[...] = _dot(a, w_ref[...]) + b_ref[...]


def _adaln(cond, ada_w, ada_b):
    rows = cond.shape[0]
    out = pl.pallas_call(
        _adaln_kernel,
        grid=(DEPTH, 6),
        in_specs=[
            pl.BlockSpec((rows, D_MODEL), lambda l, k: (0, 0)),
            pl.BlockSpec((None, D_MODEL, D_MODEL), lambda l, k: (l, 0, k)),
            pl.BlockSpec((None, None, 1, D_MODEL), lambda l, k: (l, k, 0, 0)),
        ],
        out_specs=pl.BlockSpec((None, None, rows, D_MODEL), lambda l, k: (l, k, 0, 0)),
        out_shape=jax.ShapeDtypeStruct((DEPTH, 6, rows, D_MODEL), f32),
        compiler_params=_cparams(("arbitrary", "arbitrary")),
        name="adaln",
    )(cond, ada_w, ada_b.reshape(DEPTH, 6, 1, D_MODEL))
    return out.reshape(DEPTH, 6, rows, 1, D_MODEL)


def _stream_specs(np_tiles, ngrid, split):
    def spec(block_of_tile):
        if ngrid == 1:
            return pl.BlockSpec((TM, D_MODEL), lambda i: (block_of_tile(i), 0))
        return pl.BlockSpec((TM, D_MODEL), lambda i, j: (block_of_tile(i), 0))

    if not split:
        return [spec(lambda i: i)]
    return [spec(lambda i: jnp.minimum(i, np_tiles - 1)), spec(lambda i: jnp.maximum(i - np_tiles, 0))]


def _stream_load(np_tiles, refs):
    if len(refs) == 1:
        return refs[0][...]
    return jnp.where(pl.program_id(0) < np_tiles, refs[0][...], refs[1][...])


def _stream_store(np_tiles, refs, val):
    if len(refs) == 1:
        refs[0][...] = val
        return
    i = pl.program_id(0)

    @pl.when(i < np_tiles)
    def _():
        refs[0][...] = val

    @pl.when(i >= np_tiles)
    def _():
        refs[1][...] = val


def _stream_shapes(np_tiles, nt, split):
    if not split:
        return [jax.ShapeDtypeStruct((nt * TM, D_MODEL), f32)]
    return [jax.ShapeDtypeStruct((np_tiles * TM, D_MODEL), f32),
            jax.ShapeDtypeStruct(((nt - np_tiles) * TM, D_MODEL), f32)]


def _stream_tiles(ys):
    return sum(y.shape[0] for y in ys) // TM


def _mod_spec(layer, k, row_of_tile, ngrid):
    if ngrid == 1:
        return pl.BlockSpec((None, None, None, 1, D_MODEL), lambda i: (layer, k, row_of_tile(i), 0, 0))
    return pl.BlockSpec((None, None, None, 1, D_MODEL), lambda i, j: (layer, k, row_of_tile(i), 0, 0))


QKV_BLK = H_A * DK_A


def _delta_proj_kernel(np_tiles, t_p, t_s, n_y, *refs):
    y_refs = refs[:n_y]
    g_ref, sh_ref, sc_ref, w_ref, cw_ref, cb_ref, o_ref, h_s, mk_s = refs[n_y:]
    n = pl.program_id(1)

    @pl.when(n == 0)
    def _():
        x = _stream_load(np_tiles, y_refs)
        h_s[...] = _modulate(x, g_ref[...], sh_ref[...], sc_ref[...]).astype(bf16)
        _conv_masks(mk_s, jnp.where(pl.program_id(0) < np_tiles, t_p, t_s))

    z = jnp.dot(h_s[...], w_ref[...].astype(bf16), preferred_element_type=f32)
    y = _seq_conv(z, cw_ref, cb_ref, mk_s)
    y = y * _sigmoid(y)

    @pl.when(n < 2)
    def _():
        o_ref[...] = y * lax.rsqrt(_group_sum64(y * y) + EPS)

    @pl.when(n == 2)
    def _():
        o_ref[...] = y


def _delta_proj(ys, mods, layer, row_of_tile, np_tiles, t_p, t_s, norm_g, w_in, conv_w, conv_b):
    nt = _stream_tiles(ys)
    ntok = nt * TM
    kern = functools.partial(_delta_proj_kernel, np_tiles, t_p, t_s, len(ys))
    return pl.pallas_call(
        kern,
        grid=(nt, N_QKV // QKV_BLK),
        in_specs=_stream_specs(np_tiles, 2, len(ys) == 2) + [
            pl.BlockSpec((1, D_MODEL), lambda i, n: (0, 0)),
            _mod_spec(layer, 0, row_of_tile, 2),
            _mod_spec(layer, 1, row_of_tile, 2),
            pl.BlockSpec((None, D_MODEL, QKV_BLK), lambda i, n: (layer // 2, 0, n)),
            pl.BlockSpec((None, CONV_W, QKV_BLK), lambda i, n: (layer // 2, 0, n)),
            pl.BlockSpec((1, QKV_BLK), lambda i, n: (0, n)),
        ],
        out_specs=pl.BlockSpec((TM, QKV_BLK), lambda i, n: (i, n)),
        out_shape=jax.ShapeDtypeStruct((ntok, N_QKV), f32),
        scratch_shapes=[pltpu.VMEM((TM, D_MODEL), bf16), pltpu.VMEM((CONV_W - 1, TM, LANES), f32)],
        compiler_params=_cparams(("arbitrary", "arbitrary")),
        name="delta_proj",
    )(*ys, norm_g, mods, mods, w_in, conv_w, conv_b)


def _rope_swap(x):
    w = x.shape[1]
    lane = lax.broadcasted_iota(jnp.int32, (1, w), 1)
    first = (lane & (ROPE_AXIS - 1)) < ROPE_AXIS // 2
    return jnp.where(first, pltpu.roll(x, w - ROPE_AXIS // 2, axis=1), pltpu.roll(x, ROPE_AXIS // 2, axis=1))


def _attn_proj_kernel(np_tiles, n_y, *refs):
    y_refs = refs[:n_y]
    (g_ref, sh_ref, sc_ref, w_ref, qn_ref, kn_ref, cos_ref, sin_ref, al_ref, dt_ref,
     gate_ref, q_ref, k_ref, v_ref, gb_ref, kc_ref, vc_ref) = refs[n_y:]
    x = _stream_load(np_tiles, y_refs)
    h = _modulate(x, g_ref[...], sh_ref[...], sc_ref[...]).astype(bf16)
    z = jnp.dot(h, w_ref[...].astype(bf16), preferred_element_type=f32)
    nq = H_B * HD_B
    nkv = KV_B * HD_B
    o1 = H_A * DV_A
    o2 = o1 + nq + 2 * nkv
    gz = z[:, :o1]
    gate_ref[...] = gz * _sigmoid(gz)
    q = z[:, o1:o1 + nq]
    k = z[:, o1 + nq:o1 + nq + nkv]
    v = z[:, o1 + nq + nkv:o2]
    v_ref[...] = v
    qn = q * lax.rsqrt(_group_sum64(q * q) * (1.0 / HD_B) + EPS) * qn_ref[...]
    kn = k * lax.rsqrt(_group_sum64(k * k) * (1.0 / HD_B) + EPS) * kn_ref[...]
    cos = cos_ref[...]
    sin = sin_ref[...]
    q_ref[...] = qn * cos + _rope_swap(qn) * sin
    k_ref[...] = kn * cos[:, :nkv] + _rope_swap(kn) * sin[:, :nkv]

    @pl.when(pl.program_id(0) < np_tiles)
    def _():
        kc_ref[...] = kn
        vc_ref[...] = v
    zs = z[:, o2:]
    lane = lax.broadcasted_iota(jnp.int32, (1, LANES), 1)
    g_val = -jnp.exp(al_ref[...]) * _softplus(zs + dt_ref[...])
    gb_ref[...] = jnp.where(lane < 2 * H_A, g_val, jnp.where(lane < 4 * H_A, _sigmoid(zs), 0.0))


def _attn_proj(ys, mods, layer, row_of_tile, np_tiles, norm_g, w_b, qn_row, kn_row, cos_tab, sin_tab,
               al_row, dt_row):
    nt = _stream_tiles(ys)
    ntok = nt * TM
    nq = H_B * HD_B
    nkv = KV_B * HD_B
    tab_idx = lambda i: (jnp.where(i < np_tiles, 0, 1), 0)
    cache_idx = lambda i: (jnp.minimum(i, np_tiles - 1), 0)
    return pl.pallas_call(
        functools.partial(_attn_proj_kernel, np_tiles, len(ys)),
        grid=(nt,),
        in_specs=_stream_specs(np_tiles, 1, len(ys) == 2) + [
            pl.BlockSpec((1, D_MODEL), lambda i: (0, 0)),
            _mod_spec(layer, 0, row_of_tile, 1),
            _mod_spec(layer, 1, row_of_tile, 1),
            pl.BlockSpec((D_MODEL, N_ATTN), lambda i: (0, 0)),
            pl.BlockSpec((1, nq), lambda i: (0, 0)),
            pl.BlockSpec((1, nkv), lambda i: (0, 0)),
            pl.BlockSpec((TM, nq), tab_idx),
            pl.BlockSpec((TM, nq), tab_idx),
            pl.BlockSpec((1, LANES), lambda i: (0, 0)),
            pl.BlockSpec((1, LANES), lambda i: (0, 0)),
        ],
        out_specs=[
            pl.BlockSpec((TM, H_A * DV_A), lambda i: (i, 0)),
            pl.BlockSpec((TM, nq), lambda i: (i, 0)),
            pl.BlockSpec((TM, nkv), lambda i: (i, 0)),
            pl.BlockSpec((TM, nkv), lambda i: (i, 0)),
            pl.BlockSpec((TM, LANES), lambda i: (i, 0)),
            pl.BlockSpec((TM, nkv), cache_idx),
            pl.BlockSpec((TM, nkv), cache_idx),
        ],
        out_shape=[jax.ShapeDtypeStruct((ntok, H_A * DV_A), f32),
                   jax.ShapeDtypeStruct((ntok, nq), f32),
                   jax.ShapeDtypeStruct((ntok, nkv), f32),
                   jax.ShapeDtypeStruct((ntok, nkv), f32),
                   jax.ShapeDtypeStruct((ntok, LANES), f32),
                   jax.ShapeDtypeStruct((np_tiles * TM, nkv), f32),
                   jax.ShapeDtypeStruct((np_tiles * TM, nkv), f32)],
        compiler_params=_cparams(("arbitrary",)),
        name="attn_proj",
    )(*ys, norm_g, mods, mods, w_b, qn_row, kn_row, cos_tab, sin_tab, al_row, dt_row)


INV_BASE = 8
PRE_CHUNKS = 4
DELTA_SEQS_P = 4
HPG = MXU_DIM // DK_A
NLG = H_A // HPG
W_ALL = H_A * DK_A


def _chunk_scan(x, reverse):
    t = x.shape[0]
    r = lax.broadcasted_iota(jnp.int32, (t, 1), 0) & (CHUNK - 1)
    s = 1
    while s < CHUNK:
        if reverse:
            x = x + jnp.where(r + s < CHUNK, pltpu.roll(x, t - s, axis=0), 0.0)
        else:
            x = x + jnp.where(r >= s, pltpu.roll(x, s, axis=0), 0.0)
        s *= 2
    return x


def _delta_kernel(t, nsq, has_s0, *refs):
    if has_s0:
        q_ref, k_ref, v_ref, gb_ref, s0_ref = refs[:5]
        rest = refs[5:]
    else:
        q_ref, k_ref, v_ref, gb_ref = refs[:4]
        s0_ref = None
        rest = refs[4:]
    o_ref, sf_ref, exp_s, kbg_s, qg_s, kd_s, vb_s, egt_s, ti_s, in_s, st_s = rest
    n = t // CHUNK
    scale = DK_A ** -0.5

    gb = gb_ref[...]
    lane = lax.broadcasted_iota(jnp.int32, (1, LANES), 1)
    sc = jnp.where(lane < H_A, _chunk_scan(gb, False), jnp.where(lane < 2 * H_A, _chunk_scan(gb, True), gb))
    er = lax.broadcasted_iota(jnp.int32, (LANES, 4 * W_ALL), 0)
    ec = lax.broadcasted_iota(jnp.int32, (LANES, 4 * W_ALL), 1) >> 6
    expand = jnp.where(er == ec, 1.0, 0.0).astype(bf16)
    rb = min(t, 256)
    for r0 in range(0, nsq * t, rb):
        p1, p2, p3 = _split3(sc[r0:r0 + rb])
        exp_s[r0:r0 + rb, :] = ((jnp.dot(p1, expand, preferred_element_type=f32)
                                 + jnp.dot(p2, expand, preferred_element_type=f32))
                                + jnp.dot(p3, expand, preferred_element_type=f32))

    ri = lax.broadcasted_iota(jnp.int32, (CHUNK, MXU_DIM), 0)
    col_i = lax.broadcasted_iota(jnp.int32, (CHUNK, MXU_DIM), 1) & (CHUNK - 1)
    eye_ls = ri == col_i
    bd_mask = ((lax.broadcasted_iota(jnp.int32, (MXU_DIM, MXU_DIM), 0) >> 6)
               == (lax.broadcasted_iota(jnp.int32, (MXU_DIM, MXU_DIM), 1) >> 6))
    ones_c = jnp.ones((CHUNK, CHUNK), bf16)

    def bd(x):
        xb = x.astype(bf16)
        return jnp.where(bd_mask, jnp.concatenate([xb] * HPG, axis=0), jnp.zeros((), bf16))

    def blk(b):
        sh = b.bit_length() - 1
        return (ri >> sh) == (col_i >> sh)

    def hmm(x, y):
        return jnp.dot(x.astype(bf16), bd(y), preferred_element_type=f32)

    def pre_body(cp, carry):
        units = [(ci, d, lg) for ci in range(PRE_CHUNKS) for lg in range(NLG) for d in range(2)]
        cidx = [cp * PRE_CHUNKS + ci for ci in range(PRE_CHUNKS)]
        gcs, prods = {}, {}
        for ci, c in enumerate(cidx):
            rows = pl.ds(pl.multiple_of(c * CHUNK, CHUNK), CHUNK)
            q = q_ref[rows, :]
            k = k_ref[rows, :]
            v = v_ref[rows, :]
            kbs = []
            for d in range(2):
                gc = exp_s[rows, d * W_ALL:(d + 1) * W_ALL]
                beta = exp_s[rows, (2 + d) * W_ALL:(3 + d) * W_ALL]
                gtot = gc[CHUNK - 1:CHUNK, :] if d == 0 else gc[0:1, :]
                eg = jnp.exp(gc)
                kb = k * beta
                kbg_s[d, rows, :] = (kb * eg).astype(bf16)
                qg_s[d, rows, :] = (q * (scale * eg)).astype(bf16)
                kd_s[d, rows, :] = (k * jnp.exp(gtot - gc)).astype(bf16)
                vb_s[d, rows, :] = v * beta
                egt_s[d, c] = jnp.zeros((8, W_ALL), f32) + jnp.exp(gtot)
                gcs[ci, d] = gc
                kbs.append(kb)
            for lg in range(NLG):
                cols = slice(lg * MXU_DIM, (lg + 1) * MXU_DIM)
                lhs = jnp.concatenate([kbs[0][:, cols], kbs[1][:, cols], q[:, cols] * scale], axis=0)
                prods[ci, lg] = _dot_nt(lhs, bd(k[:, cols]))
        ps, ys, avs = [], [], []
        for ci, d, lg in units:
            c = cidx[ci]
            cols = slice(lg * MXU_DIM, (lg + 1) * MXU_DIM)
            gc = gcs[ci, d][:, cols]
            prod = prods[ci, lg]
            d1, d2, d3 = _split3(jnp.where(eye_ls, gc, 0.0))
            grow = ((jnp.dot(ones_c, d1, preferred_element_type=f32)
                     + jnp.dot(ones_c, d2, preferred_element_type=f32))
                    + jnp.dot(ones_c, d3, preferred_element_type=f32))
            keep = (ri >= col_i) if d == 0 else (ri <= col_i)
            strict = (ri > col_i) if d == 0 else (ri < col_i)
            decay = jnp.where(keep, jnp.exp(jnp.where(keep, gc - grow, 0.0)), 0.0)
            a = jnp.where(strict, prod[d * CHUNK:(d + 1) * CHUNK] * decay, 0.0)
            in_s[d, c, :, cols] = jnp.where(keep, prod[2 * CHUNK:] * decay, 0.0).astype(bf16)
            ab = a.astype(bf16)
            avs.append(ab)
            ys.append(jnp.where(blk(INV_BASE), -ab, jnp.zeros((), bf16)))
            ps.append(jnp.where(eye_ls, 1.0, 0.0) - jnp.where(blk(INV_BASE), a, 0.0))
        m = 1
        while m < INV_BASE:
            last = 2 * m >= INV_BASE
            for u in range(len(units)):
                y = ys[u]
                if m == 1:
                    ys[u] = hmm(y, y).astype(bf16)
                elif last:
                    ps[u] = ps[u] + hmm(ps[u], y)
                else:
                    r = hmm(jnp.concatenate([ps[u].astype(bf16), y], axis=0), y)
                    ps[u] = ps[u] + r[:CHUNK]
                    ys[u] = r[CHUNK:].astype(bf16)
            m *= 2
        b = INV_BASE
        while b < CHUNK:
            off = blk(2 * b) & jnp.logical_not(blk(b))
            ws = [hmm(jnp.where(off, avs[u], jnp.zeros((), bf16)), ps[u]) for u in range(len(units))]
            for u in range(len(units)):
                ps[u] = ps[u] - hmm(ps[u], ws[u])
            b *= 2
        for u, (ci, d, lg) in enumerate(units):
            ti_s[d, cidx[ci], :, lg * MXU_DIM:(lg + 1) * MXU_DIM] = ps[u].astype(bf16)
        return carry

    lax.fori_loop(0, nsq * n // PRE_CHUNKS, pre_body, 0)

    for sq in range(nsq):
        for d in range(2):
            for lg in range(NLG):
                if has_s0:
                    blocks = []
                    for hh in range(HPG):
                        s_h = s0_ref[sq, d, lg * HPG + hh]
                        z_l = jnp.zeros((DK_A, hh * DV_A), f32)
                        z_r = jnp.zeros((DK_A, (HPG - 1 - hh) * DV_A), f32)
                        parts = ([z_l] if hh > 0 else []) + [s_h] + ([z_r] if hh < HPG - 1 else [])
                        blocks.append(jnp.concatenate(parts, axis=1) if len(parts) > 1 else s_h)
                    st_s[sq, d, lg] = jnp.concatenate(blocks, axis=0)
                else:
                    st_s[sq, d, lg] = jnp.zeros((MXU_DIM, MXU_DIM), f32)
    o_ref[...] = jnp.zeros_like(o_ref)

    def seq_body(j, carry):
        units = [(sq, d, lg) for sq in range(nsq) for lg in range(NLG) for d in range(2)]
        step = [j, n - 1 - j]
        cls = [slice(lg * MXU_DIM, (lg + 1) * MXU_DIM) for lg in range(NLG)]
        chunk = {(sq, d): sq * n + step[d] for sq in range(nsq) for d in range(2)}
        rws = {key: pl.ds(pl.multiple_of(cc * CHUNK, CHUNK), CHUNK) for key, cc in chunk.items()}
        boths, vnbs = [], []
        for sq, d, lg in units:
            r = rws[sq, d]
            lhs = jnp.concatenate([kbg_s[d, r, cls[lg]], qg_s[d, r, cls[lg]]], axis=0)
            boths.append(jnp.dot(lhs, st_s[sq, d, lg].astype(bf16), preferred_element_type=f32))
        for u, (sq, d, lg) in enumerate(units):
            resid = vb_s[d, rws[sq, d], cls[lg]] - boths[u][:CHUNK]
            v_new = jnp.dot(ti_s[d, chunk[sq, d], :, cls[lg]], bd(resid), preferred_element_type=f32)
            vnbs.append(v_new.astype(bf16))
        for u, (sq, d, lg) in enumerate(units):
            r = rws[sq, d]
            c = chunk[sq, d]
            o = boths[u][CHUNK:] + jnp.dot(in_s[d, c, :, cls[lg]], bd(vnbs[u]), preferred_element_type=f32)
            upd = lax.dot_general(kd_s[d, r, cls[lg]], vnbs[u], (((0,), (0,)), ((), ())),
                                  preferred_element_type=f32)
            st_s[sq, d, lg] = st_s[sq, d, lg] * egt_s[d, c, 0:1, cls[lg]] + jnp.where(bd_mask, upd, 0.0)
            o_ref[r, cls[lg]] = o_ref[r, cls[lg]] + o
        return carry

    lax.fori_loop(0, n, seq_body, 0)
    for sq in range(nsq):
        for d in range(2):
            for lg in range(NLG):
                s_fin = st_s[sq, d, lg]
                for hh in range(HPG):
                    sf_ref[sq, d, lg * HPG + hh] = s_fin[hh * DK_A:(hh + 1) * DK_A, hh * DV_A:(hh + 1) * DV_A]


def _delta(qkv, gb, t, nseq, nsq, tok_block0, s0, s0_layer):
    n = t // CHUNK
    tb = nsq * t
    nch = nsq * n
    has_s0 = s0 is not None
    kern = functools.partial(_delta_kernel, t, nsq, has_s0)
    in_specs = [
        pl.BlockSpec((tb, W_ALL), lambda b: (tok_block0 + b, 0)),
        pl.BlockSpec((tb, W_ALL), lambda b: (tok_block0 + b, 1)),
        pl.BlockSpec((tb, W_ALL), lambda b: (tok_block0 + b, 2)),
        pl.BlockSpec((tb, LANES), lambda b: (tok_block0 + b, 0)),
    ]
    args = [qkv, qkv, qkv, gb]
    if has_s0:
        in_specs.append(pl.BlockSpec((nsq, None, 2, H_A, DK_A, DV_A), lambda b: (b, s0_layer, 0, 0, 0, 0)))
        args.append(s0)
    return pl.pallas_call(
        kern,
        grid=(nseq // nsq,),
        in_specs=in_specs,
        out_specs=[
            pl.BlockSpec((tb, W_ALL), lambda b: (b, 0)),
            pl.BlockSpec((nsq, 2, H_A, DK_A, DV_A), lambda b: (b, 0, 0, 0, 0)),
        ],
        out_shape=[jax.ShapeDtypeStruct((nseq * t, W_ALL), f32),
                   jax.ShapeDtypeStruct((nseq, 2, H_A, DK_A, DV_A), f32)],
        scratch_shapes=[
            pltpu.VMEM((tb, 4 * W_ALL), f32),
            pltpu.VMEM((2, tb, W_ALL), bf16),
            pltpu.VMEM((2, tb, W_ALL), bf16),
            pltpu.VMEM((2, tb, W_ALL), bf16),
            pltpu.VMEM((2, tb, W_ALL), f32),
            pltpu.VMEM((2, nch, 8, W_ALL), f32),
            pltpu.VMEM((2, nch, CHUNK, W_ALL), bf16),
            pltpu.VMEM((2, nch, CHUNK, W_ALL), bf16),
            pltpu.VMEM((nsq, 2, NLG, MXU_DIM, MXU_DIM), f32),
        ],
        compiler_params=_cparams(("arbitrary",)),
        name="delta_p" if not has_s0 else "delta_s",
    )(*args)


def _attend(q_all, key_sets, sink_ref):
    nrow = q_all.shape[0]
    scores = []
    for h in range(H_B):
        kv = h // G_B
        q = q_all[:, h * HD_B:(h + 1) * HD_B]
        row = []
        for k_all, _, mask in key_sets:
            s = _dot_nt(q, k_all[:, kv * HD_B:(kv + 1) * HD_B])
            row.append(s if mask is None else jnp.where(mask, s, NEG_INF))
        scores.append(row)
    probs, sink_terms = [], []
    for h in range(H_B):
        sink = jnp.zeros((nrow, 1), f32) + sink_ref[h]
        m = sink
        for s in scores[h]:
            m = jnp.maximum(m, jnp.max(s, axis=-1, keepdims=True))
        probs.append([jnp.exp(s - m).astype(bf16) for s in scores[h]])
        sink_terms.append(jnp.exp(sink - m))
    v_ext = [[jnp.concatenate([v_all[:, kv * HD_B:(kv + 1) * HD_B].astype(bf16),
                               jnp.ones((v_all.shape[0], HD_B), bf16)], axis=1) for kv in range(KV_B)]
             for _, v_all, _ in key_sets]
    outs = []
    for h in range(H_B):
        acc = None
        for p, vs in zip(probs[h], v_ext):
            pv = jnp.dot(p, vs[h // G_B], preferred_element_type=f32)
            acc = pv if acc is None else acc + pv
        outs.append(acc[:, :HD_B] / (acc[:, HD_B:HD_B + 1] + sink_terms[h]))
    return jnp.concatenate(outs, axis=1)


ATTN_SEQS_P = 4
WIN_QBLKS = 2


def _ctx_attn_kernel(t, sink_ref, q_ref, k_ref, v_ref, o_ref):
    scale = HD_B ** -0.5
    for sq in range(ATTN_SEQS_P):
        rows = slice(sq * t, (sq + 1) * t)
        o_ref[rows, :] = _attend(q_ref[rows, :] * scale, [(k_ref[rows, :], v_ref[rows, :], None)], sink_ref)


def _ctx_attn(sink, qb, kb, vb, t, nseq):
    nq = H_B * HD_B
    nkv = KV_B * HD_B
    tb = ATTN_SEQS_P * t
    return pl.pallas_call(
        functools.partial(_ctx_attn_kernel, t),
        grid=(nseq // ATTN_SEQS_P,),
        in_specs=[
            pl.BlockSpec(memory_space=pltpu.SMEM),
            pl.BlockSpec((tb, nq), lambda b: (b, 0)),
            pl.BlockSpec((tb, nkv), lambda b: (b, 0)),
            pl.BlockSpec((tb, nkv), lambda b: (b, 0)),
        ],
        out_specs=pl.BlockSpec((tb, nq), lambda b: (b, 0)),
        out_shape=jax.ShapeDtypeStruct((nseq * t, nq), f32),
        compiler_params=_cparams(("arbitrary",)),
        name="ctx_attn",
    )(sink, qb, kb, vb)


def _win_attn_kernel(t, sink_ref, q_ref, k_ref, v_ref, kc_ref, vc_ref, o_ref):
    scale = HD_B ** -0.5
    span = QBLK + 2 * WINDOW
    for qb in range(WIN_QBLKS):
        start = (pl.program_id(1) * WIN_QBLKS + qb) * QBLK
        lo = jnp.clip(start - WINDOW, 0, t - span)
        lo = pl.multiple_of(lo, QBLK)
        q_pos = start + lax.broadcasted_iota(jnp.int32, (QBLK, span), 0)
        k_pos = lo + lax.broadcasted_iota(jnp.int32, (QBLK, span), 1)
        valid = jnp.abs(q_pos - k_pos) <= WINDOW
        key_sets = [(k_ref[pl.ds(lo, span), :], v_ref[pl.ds(lo, span), :], valid),
                    (kc_ref[...], vc_ref[...], None)]
        rows = slice(qb * QBLK, (qb + 1) * QBLK)
        o_ref[rows, :] = _attend(q_ref[rows, :] * scale, key_sets, sink_ref)


def _win_attn(sink, qb, kb, vb, cache_k, cache_v, cache_layer, t, nseq, tok_block0):
    nq = H_B * HD_B
    nkv = KV_B * HD_B
    nb = t // (WIN_QBLKS * QBLK)
    past = cache_k.shape[2]
    kern = functools.partial(_win_attn_kernel, t)
    return pl.pallas_call(
        kern,
        grid=(nseq, nb),
        in_specs=[
            pl.BlockSpec(memory_space=pltpu.SMEM),
            pl.BlockSpec((WIN_QBLKS * QBLK, nq), lambda b, i: ((tok_block0 + b) * nb + i, 0)),
            pl.BlockSpec((t, nkv), lambda b, i: (tok_block0 + b, 0)),
            pl.BlockSpec((t, nkv), lambda b, i: (tok_block0 + b, 0)),
            pl.BlockSpec((None, None, past, nkv), lambda b, i: (b, cache_layer, 0, 0)),
            pl.BlockSpec((None, None, past, nkv), lambda b, i: (b, cache_layer, 0, 0)),
        ],
        out_specs=pl.BlockSpec((WIN_QBLKS * QBLK, nq), lambda b, i: (b * nb + i, 0)),
        out_shape=jax.ShapeDtypeStruct((nseq * t, nq), f32),
        compiler_params=_cparams(("arbitrary", "arbitrary")),
        name="win_attn",
    )(sink, qb, kb, vb, cache_k, cache_v)


def _mix_out_kernel(np_tiles, n_y, *refs):
    y_refs = refs[:n_y]
    oap_ref, oas_ref, obp_ref, obs_ref, gate_ref, dn_ref, w_ref, gm_ref = refs[n_y:n_y + 8]
    o_refs = refs[n_y + 8:]
    i = pl.program_id(0)
    is_p = i < np_tiles
    oa = jnp.where(is_p, oap_ref[...], oas_ref[...])
    ob = jnp.where(is_p, obp_ref[...], obs_ref[...])
    oan = oa * lax.rsqrt(_group_sum64(oa * oa) * (1.0 / DV_A) + EPS) * dn_ref[...] * gate_ref[...]
    ka = H_A * DV_A
    mix = _dot(oan, w_ref[:ka, :]) + _dot(ob, w_ref[ka:, :])
    _stream_store(np_tiles, o_refs, _stream_load(np_tiles, y_refs) + gm_ref[...] * mix)


def _mix_out(ys, mods, layer, row_of_tile, np_tiles, oa_p, oa_s, ob_p, ob_s, gate, dn_row, w_out, split_out):
    nt = _stream_tiles(ys)
    ka = H_A * DV_A
    p_idx = lambda i: (jnp.minimum(i, np_tiles - 1), 0)
    s_idx = lambda i: (jnp.maximum(i - np_tiles, 0), 0)
    kern = functools.partial(_mix_out_kernel, np_tiles, len(ys))
    return pl.pallas_call(
        kern,
        grid=(nt,),
        in_specs=_stream_specs(np_tiles, 1, len(ys) == 2) + [
            pl.BlockSpec((TM, ka), p_idx),
            pl.BlockSpec((TM, ka), s_idx),
            pl.BlockSpec((TM, ka), p_idx),
            pl.BlockSpec((TM, ka), s_idx),
            pl.BlockSpec((TM, ka), lambda i: (i, 0)),
            pl.BlockSpec((1, ka), lambda i: (0, 0)),
            pl.BlockSpec((None, 2 * ka, D_MODEL), lambda i: (layer // 2, 0, 0)),
            _mod_spec(layer, 2, row_of_tile, 1),
        ],
        out_specs=_stream_specs(np_tiles, 1, split_out),
        out_shape=_stream_shapes(np_tiles, nt, split_out),
        compiler_params=_cparams(("arbitrary",)),
        name="mix_out",
    )(*ys, oa_p, oa_s, ob_p, ob_s, gate, dn_row, w_out, mods)


def _gelu_tanh(x):
    return 0.5 * x * (1.0 + jnp.tanh(math.sqrt(2.0 / math.pi) * (x + 0.044715 * (x * x * x))))


def _lru_kernel(np_tiles, t_p, t_s, y_ref, g_ref, sh_ref, sc_ref, wx_ref, wg_ref, cw_ref, cb_ref,
                wl_ref, bl_ref, lam_ref, h0_ref, o_ref, fin_ref, h_s, x_s, gt_s, hf_s, hb_s, mk_s):
    is_p = pl.program_id(0) < np_tiles
    tseq = jnp.where(is_p, t_p, t_s)

    @pl.when(pl.program_id(1) == 0)
    def _():
        h_s[...] = _modulate(y_ref[...], g_ref[...], sh_ref[...], sc_ref[...]).astype(bf16)
        _conv_masks(mk_s, tseq)

    w = jnp.concatenate([wx_ref[...], wg_ref[...]], axis=1).astype(bf16)
    z = jnp.dot(h_s[...], w, preferred_element_type=f32)
    x = _seq_conv(z[:, :LRU_W], cw_ref, cb_ref, mk_s)
    x_s[...] = x
    ngrp = LRU_W // LRU_GW
    gcols = [slice(g * LRU_GW, (g + 1) * LRU_GW) for g in range(ngrp)]
    for g in range(ngrp):
        gt_s[g] = _dot(x[:, gcols[g]], wl_ref[g]) + bl_ref[g]
    decay = [[RG_C * _softplus(-lam_ref[g, :, d * LRU_GW:(d + 1) * LRU_GW]) for g in range(ngrp)]
             for d in range(2)]
    sub = lax.broadcasted_iota(jnp.int32, (SUBLANES, 1), 0)
    nblk = TM // SUBLANES
    seg_shift = (t_p // SUBLANES).bit_length() - 1

    def body(kf, carry):
        nxt = []
        for d, kk in ((0, kf), (1, nblk - 1 - kf)):
            r0 = pl.multiple_of(kk * SUBLANES, SUBLANES)
            rows = pl.ds(r0, SUBLANES)
            edge = r0 if d == 0 else r0 + SUBLANES
            at_edge = (edge & (tseq - 1)) == 0
            seg = jnp.where(is_p, kk >> seg_shift, 0)
            for g in range(ngrp):
                xb = x_s[rows, gcols[g]]
                gb = gt_s[g, rows, :]
                r = _sigmoid(gb[:, (2 * d) * LRU_GW:(2 * d + 1) * LRU_GW])
                ig = _sigmoid(gb[:, (2 * d + 1) * LRU_GW:(2 * d + 2) * LRU_GW])
                log_a = -(r * decay[d][g])
                a = jnp.exp(log_a)
                w2 = -jnp.tanh(log_a) * (a * a + 1.0)
                u = jnp.where(w2 > 0.0, w2 * lax.rsqrt(w2), 0.0) * (ig * xb)
                s = 1
                while s < SUBLANES:
                    ok = (sub >= s) if d == 0 else (sub + s < SUBLANES)
                    sh = s if d == 0 else SUBLANES - s
                    u = u + a * jnp.where(ok, pltpu.roll(u, sh, axis=0), 0.0)
                    a = a * jnp.where(ok, pltpu.roll(a, sh, axis=0), 1.0)
                    s *= 2
                c_in = jnp.where(at_edge, h0_ref[seg, d:d + 1, gcols[g]], carry[d * ngrp + g])
                hblk = u + a * c_in
                (hf_s if d == 0 else hb_s)[rows, gcols[g]] = hblk
                nxt.append(hblk[SUBLANES - 1:SUBLANES, :] if d == 0 else hblk[0:1, :])
        return tuple(nxt)

    zero = jnp.zeros((1, LRU_GW), f32)
    lax.fori_loop(0, nblk, body, (zero,) * (2 * ngrp), unroll=4)
    for sg in range(TM // t_p):
        fin_ref[sg, 0:1, :] = hf_s[sg * t_p + t_p - 1:sg * t_p + t_p, :]
        fin_ref[sg, 1:2, :] = hb_s[sg * t_p:sg * t_p + 1, :]
    o_ref[...] = (hf_s[...] + hb_s[...]) * _gelu_tanh(z[:, LRU_W:])


def _lru(y, mods, layer, row_of_tile, np_tiles, t_p, t_s, norm_g, w_in, cw, cb, wl, bl, lam, h0):
    ntok = y.shape[0]
    nt = ntok // TM
    ncol = D_RNN // LRU_W
    ngrp = LRU_W // LRU_GW
    nseg = TM // t_p
    kern = functools.partial(_lru_kernel, np_tiles, t_p, t_s)
    return pl.pallas_call(
        kern,
        grid=(nt, ncol),
        in_specs=[
            pl.BlockSpec((TM, D_MODEL), lambda i, j: (i, 0)),
            pl.BlockSpec((1, D_MODEL), lambda i, j: (0, 0)),
            _mod_spec(layer, 0, row_of_tile, 2),
            _mod_spec(layer, 1, row_of_tile, 2),
            pl.BlockSpec((None, D_MODEL, LRU_W), lambda i, j: (layer // 2, 0, j)),
            pl.BlockSpec((None, D_MODEL, LRU_W), lambda i, j: (layer // 2, 0, ncol + j)),
            pl.BlockSpec((None, CONV_W, LRU_W), lambda i, j: (layer // 2, 0, j)),
            pl.BlockSpec((1, LRU_W), lambda i, j: (0, j)),
            pl.BlockSpec((ngrp, LRU_GW, 4 * LRU_GW), lambda i, j: (j, 0, 0)),
            pl.BlockSpec((ngrp, 1, 4 * LRU_GW), lambda i, j: (j, 0, 0)),
            pl.BlockSpec((ngrp, 1, 2 * LRU_GW), lambda i, j: (j, 0, 0)),
            pl.BlockSpec((None, nseg, 2, LRU_W), lambda i, j: (i, 0, 0, j)),
        ],
        out_specs=[
            pl.BlockSpec((TM, LRU_W), lambda i, j: (i, j)),
            pl.BlockSpec((None, nseg, 2, LRU_W), lambda i, j: (i, 0, 0, j)),
        ],
        out_shape=[jax.ShapeDtypeStruct((ntok, D_RNN), f32),
                   jax.ShapeDtypeStruct((nt, nseg, 2, D_RNN), f32)],
        scratch_shapes=[pltpu.VMEM((TM, D_MODEL), bf16), pltpu.VMEM((TM, LRU_W), f32),
                        pltpu.VMEM((ngrp, TM, 4 * LRU_GW), f32), pltpu.VMEM((TM, LRU_W), f32),
                        pltpu.VMEM((TM, LRU_W), f32), pltpu.VMEM((CONV_W - 1, TM, LANES), f32)],
        compiler_params=_cparams(("arbitrary", "arbitrary")),
        name="lru",
    )(y, norm_g, mods, mods, w_in, w_in, cw, cb, wl, bl, lam, h0)


def _proj_out_kernel(y_ref, a_ref, w_ref, gm_ref, o_ref):
    o_ref[...] = y_ref[...] + gm_ref[...] * _dot(a_ref[...], w_ref[...])


def _proj_out(y, mods, layer, row_of_tile, a, w):
    ntok = y.shape[0]
    nt = ntok // TM
    k = a.shape[1]
    return pl.pallas_call(
        _proj_out_kernel,
        grid=(nt,),
        in_specs=[
            pl.BlockSpec((TM, D_MODEL), lambda i: (i, 0)),
            pl.BlockSpec((TM, k), lambda i: (i, 0)),
            pl.BlockSpec((None, k, D_MODEL), lambda i: (layer // 2, 0, 0)),
            _mod_spec(layer, 2, row_of_tile, 1),
        ],
        out_specs=pl.BlockSpec((TM, D_MODEL), lambda i: (i, 0)),
        out_shape=jax.ShapeDtypeStruct((ntok, D_MODEL), f32),
        compiler_params=_cparams(("arbitrary",)),
        name="proj_out",
    )(y, a, w, mods)


FF_BLK = 1024


def _mlp_kernel(np_tiles, n_out, y_ref, g_ref, sh_ref, sc_ref, gm_ref, w1_ref, w2_ref, *refs):
    o_refs = refs[:n_out]
    h_s, acc_s = refs[n_out:]
    k = pl.program_id(1)

    @pl.when(k == 0)
    def _():
        h_s[...] = _modulate(y_ref[...], g_ref[...], sh_ref[...], sc_ref[...]).astype(bf16)
        acc_s[...] = jnp.zeros_like(acc_s)

    a = jnp.dot(h_s[...], w1_ref[...].astype(bf16), preferred_element_type=f32)
    a = jnp.maximum(a, 0.0)
    acc_s[...] += _dot(a * a, w2_ref[...])

    @pl.when(k == pl.num_programs(1) - 1)
    def _():
        _stream_store(np_tiles, o_refs, y_ref[...] + gm_ref[...] * acc_s[...])


def _mlp(y, mods, layer, row_of_tile, np_tiles, norm_g, w1, w2, split_out):
    nt = y.shape[0] // TM
    return pl.pallas_call(
        functools.partial(_mlp_kernel, np_tiles, 2 if split_out else 1),
        grid=(nt, D_FF // FF_BLK),
        in_specs=[
            pl.BlockSpec((TM, D_MODEL), lambda i, k: (i, 0)),
            pl.BlockSpec((1, D_MODEL), lambda i, k: (0, 0)),
            _mod_spec(layer, 3, row_of_tile, 2),
            _mod_spec(layer, 4, row_of_tile, 2),
            _mod_spec(layer, 5, row_of_tile, 2),
            pl.BlockSpec((None, D_MODEL, FF_BLK), lambda i, k: (layer, 0, k)),
            pl.BlockSpec((None, FF_BLK, D_MODEL), lambda i, k: (layer, k, 0)),
        ],
        out_specs=_stream_specs(np_tiles, 2, split_out),
        out_shape=_stream_shapes(np_tiles, nt, split_out),
        scratch_shapes=[pltpu.VMEM((TM, D_MODEL), bf16), pltpu.VMEM((TM, D_MODEL), f32)],
        compiler_params=_cparams(("arbitrary", "arbitrary")),
        name="mlp",
    )(y, norm_g, mods, mods, mods, w1, w2)


def _attn_cols(w_in, a_log, dt_bias):
    s2 = N_QKV + H_A * DV_A
    s4 = s2 + 4 * H_A
    pad = jnp.zeros((D_MODEL, LANES - 4 * H_A), f32)
    w_b = jnp.concatenate([w_in[:, N_QKV:s2], w_in[:, s4:], w_in[:, s2:s4], pad], axis=1)
    row_pad = jnp.zeros((LANES - 2 * H_A,), f32)
    al_row = jnp.concatenate([a_log.reshape(-1), row_pad]).reshape(1, LANES)
    dt_row = jnp.concatenate([dt_bias.reshape(-1), row_pad]).reshape(1, LANES)
    return w_b, al_row, dt_row


def _rope_tables(t):
    rows = t // GRID_W
    r = np.repeat(np.arange(rows, dtype=np.float32), GRID_W)
    cc = np.tile(np.arange(GRID_W, dtype=np.float32), rows)
    inv = np.float32(ROPE_BASE) ** (-np.arange(0, ROPE_AXIS, 2, dtype=np.float32) / np.float32(ROPE_AXIS))
    ang_r = (r[:, None] * inv).astype(np.float32)
    ang_c = (cc[:, None] * inv).astype(np.float32)
    cos = np.concatenate([np.cos(ang_r), np.cos(ang_r), np.cos(ang_c), np.cos(ang_c)], axis=-1)
    sin = np.concatenate([-np.sin(ang_r), np.sin(ang_r), -np.sin(ang_c), np.sin(ang_c)], axis=-1)
    cos = np.tile(cos, (1, H_B)).astype(np.float32)
    sin = np.tile(sin, (1, H_B)).astype(np.float32)
    ident_c = np.ones((TM, H_B * HD_B), np.float32)
    ident_s = np.zeros((TM, H_B * HD_B), np.float32)
    return (jnp.asarray(np.concatenate([ident_c, cos], axis=0)),
            jnp.asarray(np.concatenate([ident_s, sin], axis=0)))


def _lru_cols(w_a, b_a, w_x, b_x, lam):
    ncol = D_RNN // LRU_GW
    per = LRU_GW // LRU_BW

    def bd(w):
        w = w.reshape(ncol, per, LRU_BW, LRU_BW)
        eye = jnp.eye(per, dtype=f32)
        return jnp.einsum('gpij,pq->gpiqj', w, eye).reshape(ncol, LRU_GW, LRU_GW)

    wl = jnp.concatenate([bd(w_a[0]), bd(w_x[0]), bd(w_a[1]), bd(w_x[1])], axis=-1)
    rows = lambda v: v.reshape(ncol, 1, LRU_GW)
    bl = jnp.concatenate([rows(b_a[0]), rows(b_x[0]), rows(b_a[1]), rows(b_x[1])], axis=-1)
    lm = jnp.concatenate([rows(lam[0]), rows(lam[1])], axis=-1)
    return wl, bl, lm


def kernel(x_prompt, x_sample, state_delta, cache_k, cache_v, state_lru, c, c_ctx, ada_w, ada_b, norm1_g, norm2_g, ff_w1, ff_w2, ab_w_in, ab_conv_w, ab_conv_b, dn_a_log, dn_dt_bias, dn_norm_g, attn_q_norm_g, attn_k_norm_g, attn_sink, ab_w_out, c_w_in, c_conv_w, c_conv_b, lru_w_a, lru_b_a, lru_w_x, lru_b_x, lru_lambda, c_w_out):
    b_p, t_p, _ = x_prompt.shape
    b_s, t_s, _ = x_sample.shape
    assert t_s == TM and TM % t_p == 0 and (b_p * t_p) % TM == 0
    np_tok = b_p * t_p
    np_tiles = np_tok // TM
    nkv = KV_B * HD_B

    ys = (x_prompt.reshape(np_tok, D_MODEL), x_sample.reshape(b_s * t_s, D_MODEL))
    nt = np_tiles + b_s * t_s // TM

    rows = -(-(b_s + 1) // 8) * 8
    cond = jnp.zeros((rows, D_MODEL), f32).at[:b_s].set(c).at[b_s].set(c_ctx)
    mods = _adaln(cond, ada_w, ada_b)
    row_of_tile = lambda i: jnp.where(i < np_tiles, b_s, i - np_tiles)

    cos_tab, sin_tab = _rope_tables(t_s)
    ck = cache_k.reshape(cache_k.shape[0], cache_k.shape[1], cache_k.shape[2], nkv)
    cv = cache_v.reshape(ck.shape)

    new_dn, new_k, new_v, new_lru = [], [], [], []
    for l in range(DEPTH):
        j = l // 2
        n1 = norm1_g[l].reshape(1, D_MODEL)
        if l % 2 == 0:
            qkv = _delta_proj(ys, mods, l, row_of_tile, np_tiles, t_p, t_s, n1, ab_w_in, ab_conv_w,
                              ab_conv_b[j].reshape(1, N_QKV))
            w_b, al_row, dt_row = _attn_cols(ab_w_in[j], dn_a_log[j], dn_dt_bias[j])
            qn_row = jnp.tile(attn_q_norm_g[j], H_B).reshape(1, H_B * HD_B)
            kn_row = jnp.tile(attn_k_norm_g[j], KV_B).reshape(1, nkv)
            gate, qb, kb, vb, gb, kc, vc = _attn_proj(ys, mods, l, row_of_tile, np_tiles, n1, w_b, qn_row, kn_row,
                                                      cos_tab, sin_tab, al_row, dt_row)
            oa_p, s_fin = _delta(qkv, gb, t_p, b_p, DELTA_SEQS_P, 0, None, 0)
            oa_s, _ = _delta(qkv, gb, t_s, b_s, 1, np_tok // t_s, state_delta, j)
            ob_p = _ctx_attn(attn_sink[j], qb, kb, vb, t_p, b_p)
            ob_s = _win_attn(attn_sink[j], qb, kb, vb, ck, cv, j, t_s, b_s, np_tok // t_s)
            dn_row = jnp.tile(dn_norm_g[j], H_A).reshape(1, H_A * DV_A)
            (y,) = _mix_out(ys, mods, l, row_of_tile, np_tiles, oa_p, oa_s, ob_p, ob_s, gate, dn_row, ab_w_out,
                            False)
            new_dn.append(s_fin)
            new_k.append(kc.reshape(b_p, t_p, KV_B, HD_B))
            new_v.append(vc.reshape(b_p, t_p, KV_B, HD_B))
        else:
            wl, bl, lm = _lru_cols(lru_w_a[j], lru_b_a[j], lru_w_x[j], lru_b_x[j], lru_lambda[j])
            nseg = TM // t_p
            h0 = jnp.zeros((nt, nseg, 2, D_RNN), f32).at[np_tiles:, 0].set(state_lru[:, j])
            mixed, fin = _lru(ys[0], mods, l, row_of_tile, np_tiles, t_p, t_s, n1, c_w_in, c_conv_w,
                              c_conv_b[j].reshape(1, D_RNN), wl, bl, lm, h0)
            y = _proj_out(ys[0], mods, l, row_of_tile, mixed, c_w_out)
            new_lru.append(fin[:np_tiles].reshape(b_p, 2, D_RNN))
        ys = tuple(_mlp(y, mods, l, row_of_tile, np_tiles, norm2_g[l].reshape(1, D_MODEL), ff_w1, ff_w2,
                        l == DEPTH - 1))

    yp = ys[0].reshape(b_p, t_p, D_MODEL)
    ysm = ys[1].reshape(b_s, t_s, D_MODEL)
    return (yp, ysm, jnp.stack(new_dn, axis=1), jnp.stack(new_k, axis=1), jnp.stack(new_v, axis=1),
            jnp.stack(new_lru, axis=1))
```

```python
import functools
import math

import jax
import jax.numpy as jnp
import numpy as np
from jax import lax
from jax.experimental import pallas as pl
from jax.experimental.pallas import tpu as pltpu

f32 = jnp.float32
bf16 = jnp.bfloat16

D_MODEL = 1024
DEPTH = 4
GRID_W = 64
H_A = 8
DK_A = 64
DV_A = 64
CHUNK = 64
CONV_W = 4
H_B = 8
KV_B = 2
G_B = H_B // KV_B
HD_B = 64
WINDOW = 128
QBLK = 128
ROPE_AXIS = HD_B // 2
ROPE_BASE = 10000.0
D_RNN = D_MODEL
LRU_BLOCKS = 16
LRU_BW = D_RNN // LRU_BLOCKS
RG_C = 8.0
D_FF = 4 * D_MODEL
EPS = 1e-6
NEG_INF = -1e30

TM = 1024
LANES = 128
SUBLANES = 8
LRU_W = 512
LRU_GW = 256
MXU_DIM = 256
N_QKV = 3 * H_A * DK_A
N_ATTN = H_A * DV_A + (H_B + 2 * KV_B) * HD_B + LANES
VMEM_LIMIT = 56 * 1024 * 1024


def _cparams(sem):
    return pltpu.CompilerParams(dimension_semantics=sem, vmem_limit_bytes=VMEM_LIMIT)


def _dot(a, b):
    return jnp.dot(a.astype(bf16), b.astype(bf16), preferred_element_type=f32)


def _dot_nt(a, b):
    return lax.dot_general(a.astype(bf16), b.astype(bf16), (((1,), (1,)), ((), ())),
                           preferred_element_type=f32)


def _dot_tn(a, b):
    return lax.dot_general(a.astype(bf16), b.astype(bf16), (((0,), (0,)), ((), ())),
                           preferred_element_type=f32)


def _sigmoid(x):
    return 1.0 / (1.0 + jnp.exp(-x))


def _softplus(x):
    return jnp.maximum(x, 0.0) + jnp.log1p(jnp.exp(-jnp.abs(x)))


def _modulate(x, g, shift, scale):
    ms = jnp.mean(x * x, axis=-1, keepdims=True)
    return (x * lax.rsqrt(ms + EPS)) * (g * (1.0 + scale)) + shift


def _split3(x):
    p1 = x.astype(bf16)
    r1 = x - p1.astype(f32)
    p2 = r1.astype(bf16)
    p3 = (r1 - p2.astype(f32)).astype(bf16)
    return p1, p2, p3


def _group_sum64(v):
    w = v.shape[-1]
    r = lax.broadcasted_iota(jnp.int32, (w, w), 0) >> 6
    c = lax.broadcasted_iota(jnp.int32, (w, w), 1) >> 6
    ones_bd = jnp.where(r == c, 1.0, 0.0).astype(bf16)
    return jnp.dot(v.astype(bf16), ones_bd, preferred_element_type=f32)


CONV_LEFT = CONV_W // 2


def _conv_masks(mk_s, tseq):
    n = mk_s.shape[1]
    pos = lax.broadcasted_iota(jnp.int32, (n, LANES), 0) & (tseq - 1)
    t = 0
    for j in range(CONV_W):
        o = j - CONV_LEFT
        if o != 0:
            mk_s[t] = jnp.where((pos + o >= 0) & (pos + o < tseq), 1.0, 0.0)
            t += 1


def _seq_conv(z, cw_ref, cb_ref, mk_s):
    n, c = z.shape
    acc = z * cw_ref[CONV_LEFT:CONV_LEFT + 1, :] + cb_ref[...]
    t = 0
    for j in range(CONV_W):
        o = j - CONV_LEFT
        if o != 0:
            mask = jnp.tile(mk_s[t], (1, c // LANES)) if c != LANES else mk_s[t]
            acc = acc + (pltpu.roll(z, (-o) % n, axis=0) * mask) * cw_ref[j:j + 1, :]
            t += 1
    return acc


def _adaln_kernel(c_ref, w_ref, b_ref, o_ref):
    c = c_ref[...]
    a = c * _sigmoid(c)
    o_ref[...] = _dot(a, w_ref[...]) + b_ref[...]


def _adaln(cond, ada_w, ada_b):
    rows = cond.shape[0]
    out = pl.pallas_call(
        _adaln_kernel,
        grid=(DEPTH, 6),
        in_specs=[
            pl.BlockSpec((rows, D_MODEL), lambda l, k: (0, 0)),
            pl.BlockSpec((None, D_MODEL, D_MODEL), lambda l, k: (l, 0, k)),
            pl.BlockSpec((None, None, 1, D_MODEL), lambda l, k: (l, k, 0, 0)),
        ],
        out_specs=pl.BlockSpec((None, None, rows, D_MODEL), lambda l, k: (l, k, 0, 0)),
        out_shape=jax.ShapeDtypeStruct((DEPTH, 6, rows, D_MODEL), f32),
        compiler_params=_cparams(("arbitrary", "arbitrary")),
        name="adaln",
    )(cond, ada_w, ada_b.reshape(DEPTH, 6, 1, D_MODEL))
    return out.reshape(DEPTH, 6, rows, 1, D_MODEL)


def _stream_specs(np_tiles, ngrid, split):
    def spec(block_of_tile):
        if ngrid == 1:
            return pl.BlockSpec((TM, D_MODEL), lambda i: (block_of_tile(i), 0))
        return pl.BlockSpec((TM, D_MODEL), lambda i, j: (block_of_tile(i), 0))

    if not split:
        return [spec(lambda i: i)]
    return [spec(lambda i: jnp.minimum(i, np_tiles - 1)), spec(lambda i: jnp.maximum(i - np_tiles, 0))]


def _stream_load(np_tiles, refs):
    if len(refs) == 1:
        return refs[0][...]
    return jnp.where(pl.program_id(0) < np_tiles, refs[0][...], refs[1][...])


def _stream_store(np_tiles, refs, val):
    if len(refs) == 1:
        refs[0][...] = val
        return
    i = pl.program_id(0)

    @pl.when(i < np_tiles)
    def _():
        refs[0][...] = val

    @pl.when(i >= np_tiles)
    def _():
        refs[1][...] = val


def _stream_shapes(np_tiles, nt, split):
    if not split:
        return [jax.ShapeDtypeStruct((nt * TM, D_MODEL), f32)]
    return [jax.ShapeDtypeStruct((np_tiles * TM, D_MODEL), f32),
            jax.ShapeDtypeStruct(((nt - np_tiles) * TM, D_MODEL), f32)]


def _stream_tiles(ys):
    return sum(y.shape[0] for y in ys) // TM


def _mod_spec(layer, k, row_of_tile, ngrid):
    if ngrid == 1:
        return pl.BlockSpec((None, None, None, 1, D_MODEL), lambda i: (layer, k, row_of_tile(i), 0, 0))
    return pl.BlockSpec((None, None, None, 1, D_MODEL), lambda i, j: (layer, k, row_of_tile(i), 0, 0))


QKV_BLK = H_A * DK_A


def _delta_proj_kernel(np_tiles, t_p, t_s, n_y, *refs):
    y_refs = refs[:n_y]
    g_ref, sh_ref, sc_ref, w_ref, cw_ref, cb_ref, o_ref, h_s, mk_s = refs[n_y:]
    n = pl.program_id(1)

    @pl.when(n == 0)
    def _():
        x = _stream_load(np_tiles, y_refs)
        h_s[...] = _modulate(x, g_ref[...], sh_ref[...], sc_ref[...]).astype(bf16)
        _conv_masks(mk_s, jnp.where(pl.program_id(0) < np_tiles, t_p, t_s))

    z = jnp.dot(h_s[...], w_ref[...].astype(bf16), preferred_element_type=f32)
    y = _seq_conv(z, cw_ref, cb_ref, mk_s)
    y = y * _sigmoid(y)

    @pl.when(n < 2)
    def _():
        o_ref[...] = y * lax.rsqrt(_group_sum64(y * y) + EPS)

    @pl.when(n == 2)
    def _():
        o_ref[...] = y


def _delta_proj(ys, mods, layer, row_of_tile, np_tiles, t_p, t_s, norm_g, w_in, conv_w, conv_b):
    nt = _stream_tiles(ys)
    ntok = nt * TM
    kern = functools.partial(_delta_proj_kernel, np_tiles, t_p, t_s, len(ys))
    return pl.pallas_call(
        kern,
        grid=(nt, N_QKV // QKV_BLK),
        in_specs=_stream_specs(np_tiles, 2, len(ys) == 2) + [
            pl.BlockSpec((1, D_MODEL), lambda i, n: (0, 0)),
            _mod_spec(layer, 0, row_of_tile, 2),
            _mod_spec(layer, 1, row_of_tile, 2),
            pl.BlockSpec((D_MODEL, QKV_BLK), lambda i, n: (0, n)),
            pl.BlockSpec((None, CONV_W, QKV_BLK), lambda i, n: (layer // 2, 0, n)),
            pl.BlockSpec((1, QKV_BLK), lambda i, n: (0, n)),
        ],
        out_specs=pl.BlockSpec((TM, QKV_BLK), lambda i, n: (i, n)),
        out_shape=jax.ShapeDtypeStruct((ntok, N_QKV), f32),
        scratch_shapes=[pltpu.VMEM((TM, D_MODEL), bf16), pltpu.VMEM((CONV_W - 1, TM, LANES), f32)],
        compiler_params=_cparams(("arbitrary", "arbitrary")),
        name="delta_proj",
    )(*ys, norm_g, mods, mods, w_in, conv_w, conv_b)


def _rope_swap(x):
    w = x.shape[1]
    lane = lax.broadcasted_iota(jnp.int32, (1, w), 1)
    first = (lane & (ROPE_AXIS - 1)) < ROPE_AXIS // 2
    return jnp.where(first, pltpu.roll(x, w - ROPE_AXIS // 2, axis=1), pltpu.roll(x, ROPE_AXIS // 2, axis=1))


def _attn_proj_kernel(np_tiles, n_y, *refs):
    y_refs = refs[:n_y]
    (g_ref, sh_ref, sc_ref, w_ref, qn_ref, kn_ref, cos_ref, sin_ref, al_ref, dt_ref,
     gate_ref, q_ref, k_ref, v_ref, gb_ref, kc_ref, vc_ref) = refs[n_y:]
    x = _stream_load(np_tiles, y_refs)
    h = _modulate(x, g_ref[...], sh_ref[...], sc_ref[...]).astype(bf16)
    z = jnp.dot(h, w_ref[...].astype(bf16), preferred_element_type=f32)
    nq = H_B * HD_B
    nkv = KV_B * HD_B
    o1 = H_A * DV_A
    o2 = o1 + nq + 2 * nkv
    gz = z[:, :o1]
    gate_ref[...] = gz * _sigmoid(gz)
    q = z[:, o1:o1 + nq]
    k = z[:, o1 + nq:o1 + nq + nkv]
    v = z[:, o1 + nq + nkv:o2]
    v_ref[...] = v
    qn = q * lax.rsqrt(_group_sum64(q * q) * (1.0 / HD_B) + EPS) * qn_ref[...]
    kn = k * lax.rsqrt(_group_sum64(k * k) * (1.0 / HD_B) + EPS) * kn_ref[...]
    cos = cos_ref[...]
    sin = sin_ref[...]
    q_ref[...] = qn * cos + _rope_swap(qn) * sin
    k_ref[...] = kn * cos[:, :nkv] + _rope_swap(kn) * sin[:, :nkv]

    @pl.when(pl.program_id(0) < np_tiles)
    def _():
        kc_ref[...] = kn
        vc_ref[...] = v
    zs = z[:, o2:]
    lane = lax.broadcasted_iota(jnp.int32, (1, LANES), 1)
    g_val = -jnp.exp(al_ref[...]) * _softplus(zs + dt_ref[...])
    gb_ref[...] = jnp.where(lane < 2 * H_A, g_val, jnp.where(lane < 4 * H_A, _sigmoid(zs), 0.0))


def _attn_proj(ys, mods, layer, row_of_tile, np_tiles, norm_g, w_b, qn_row, kn_row, cos_tab, sin_tab,
               al_row, dt_row):
    nt = _stream_tiles(ys)
    ntok = nt * TM
    nq = H_B * HD_B
    nkv = KV_B * HD_B
    tab_idx = lambda i: (jnp.where(i < np_tiles, 0, 1), 0)
    cache_idx = lambda i: (jnp.minimum(i, np_tiles - 1), 0)
    return pl.pallas_call(
        functools.partial(_attn_proj_kernel, np_tiles, len(ys)),
        grid=(nt,),
        in_specs=_stream_specs(np_tiles, 1, len(ys) == 2) + [
            pl.BlockSpec((1, D_MODEL), lambda i: (0, 0)),
            _mod_spec(layer, 0, row_of_tile, 1),
            _mod_spec(layer, 1, row_of_tile, 1),
            pl.BlockSpec((D_MODEL, N_ATTN), lambda i: (0, 0)),
            pl.BlockSpec((1, nq), lambda i: (0, 0)),
            pl.BlockSpec((1, nkv), lambda i: (0, 0)),
            pl.BlockSpec((TM, nq), tab_idx),
            pl.BlockSpec((TM, nq), tab_idx),
            pl.BlockSpec((1, LANES), lambda i: (0, 0)),
            pl.BlockSpec((1, LANES), lambda i: (0, 0)),
        ],
        out_specs=[
            pl.BlockSpec((TM, H_A * DV_A), lambda i: (i, 0)),
            pl.BlockSpec((TM, nq), lambda i: (i, 0)),
            pl.BlockSpec((TM, nkv), lambda i: (i, 0)),
            pl.BlockSpec((TM, nkv), lambda i: (i, 0)),
            pl.BlockSpec((TM, LANES), lambda i: (i, 0)),
            pl.BlockSpec((TM, nkv), cache_idx),
            pl.BlockSpec((TM, nkv), cache_idx),
        ],
        out_shape=[jax.ShapeDtypeStruct((ntok, H_A * DV_A), f32),
                   jax.ShapeDtypeStruct((ntok, nq), f32),
                   jax.ShapeDtypeStruct((ntok, nkv), f32),
                   jax.ShapeDtypeStruct((ntok, nkv), f32),
                   jax.ShapeDtypeStruct((ntok, LANES), f32),
                   jax.ShapeDtypeStruct((np_tiles * TM, nkv), f32),
                   jax.ShapeDtypeStruct((np_tiles * TM, nkv), f32)],
        compiler_params=_cparams(("arbitrary",)),
        name="attn_proj",
    )(*ys, norm_g, mods, mods, w_b, qn_row, kn_row, cos_tab, sin_tab, al_row, dt_row)


INV_BASE = 8
PRE_CHUNKS = 4
DELTA_SEQS_P = 4
HPG = MXU_DIM // DK_A
NLG = H_A // HPG
W_ALL = H_A * DK_A


def _chunk_scan(x, reverse):
    t = x.shape[0]
    r = lax.broadcasted_iota(jnp.int32, (t, 1), 0) & (CHUNK - 1)
    s = 1
    while s < CHUNK:
        if reverse:
            x = x + jnp.where(r + s < CHUNK, pltpu.roll(x, t - s, axis=0), 0.0)
        else:
            x = x + jnp.where(r >= s, pltpu.roll(x, s, axis=0), 0.0)
        s *= 2
    return x


def _delta_kernel(t, nsq, has_s0, *refs):
    if has_s0:
        q_ref, k_ref, v_ref, gb_ref, s0_ref = refs[:5]
        rest = refs[5:]
    else:
        q_ref, k_ref, v_ref, gb_ref = refs[:4]
        s0_ref = None
        rest = refs[4:]
    o_ref, sf_ref, exp_s, kbg_s, qg_s, kd_s, vb_s, egt_s, ti_s, in_s, st_s = rest
    n = t // CHUNK
    scale = DK_A ** -0.5

    gb = gb_ref[...]
    lane = lax.broadcasted_iota(jnp.int32, (1, LANES), 1)
    sc = jnp.where(lane < H_A, _chunk_scan(gb, False), jnp.where(lane < 2 * H_A, _chunk_scan(gb, True), gb))
    er = lax.broadcasted_iota(jnp.int32, (LANES, 4 * W_ALL), 0)
    ec = lax.broadcasted_iota(jnp.int32, (LANES, 4 * W_ALL), 1) >> 6
    expand = jnp.where(er == ec, 1.0, 0.0).astype(bf16)
    rb = min(t, 256)
    for r0 in range(0, nsq * t, rb):
        p1, p2, p3 = _split3(sc[r0:r0 + rb])
        exp_s[r0:r0 + rb, :] = ((jnp.dot(p1, expand, preferred_element_type=f32)
                                 + jnp.dot(p2, expand, preferred_element_type=f32))
                                + jnp.dot(p3, expand, preferred_element_type=f32))

    ri = lax.broadcasted_iota(jnp.int32, (CHUNK, MXU_DIM), 0)
    col_i = lax.broadcasted_iota(jnp.int32, (CHUNK, MXU_DIM), 1) & (CHUNK - 1)
    eye_ls = ri == col_i
    bd_mask = ((lax.broadcasted_iota(jnp.int32, (MXU_DIM, MXU_DIM), 0) >> 6)
               == (lax.broadcasted_iota(jnp.int32, (MXU_DIM, MXU_DIM), 1) >> 6))
    ones_c = jnp.ones((CHUNK, CHUNK), bf16)

    def bd(x):
        xb = x.astype(bf16)
        return jnp.where(bd_mask, jnp.concatenate([xb] * HPG, axis=0), jnp.zeros((), bf16))

    def blk(b):
        sh = b.bit_length() - 1
        return (ri >> sh) == (col_i >> sh)

    def hmm(x, y):
        return jnp.dot(x.astype(bf16), bd(y), preferred_element_type=f32)

    def pre_body(cp, carry):
        units = [(ci, d, lg) for ci in range(PRE_CHUNKS) for lg in range(NLG) for d in range(2)]
        cidx = [cp * PRE_CHUNKS + ci for ci in range(PRE_CHUNKS)]
        gcs, prods = {}, {}
        for ci, c in enumerate(cidx):
            rows = pl.ds(pl.multiple_of(c * CHUNK, CHUNK), CHUNK)
            q = q_ref[rows, :]
            k = k_ref[rows, :]
            v = v_ref[rows, :]
            kbs = []
            for d in range(2):
                gc = exp_s[rows, d * W_ALL:(d + 1) * W_ALL]
                beta = exp_s[rows, (2 + d) * W_ALL:(3 + d) * W_ALL]
                gtot = gc[CHUNK - 1:CHUNK, :] if d == 0 else gc[0:1, :]
                eg = jnp.exp(gc)
                kb = k * beta
                kbg_s[d, rows, :] = (kb * eg).astype(bf16)
                qg_s[d, rows, :] = (q * (scale * eg)).astype(bf16)
                kd_s[d, rows, :] = (k * jnp.exp(gtot - gc)).astype(bf16)
                vb_s[d, rows, :] = v * beta
                egt_s[d, c] = jnp.zeros((8, W_ALL), f32) + jnp.exp(gtot)
                gcs[ci, d] = gc
                kbs.append(kb)
            for lg in range(NLG):
                cols = slice(lg * MXU_DIM, (lg + 1) * MXU_DIM)
                lhs = jnp.concatenate([kbs[0][:, cols], kbs[1][:, cols], q[:, cols] * scale], axis=0)
                prods[ci, lg] = _dot_nt(lhs, bd(k[:, cols]))
        ps, ys, avs = [], [], []
        for ci, d, lg in units:
            c = cidx[ci]
            cols = slice(lg * MXU_DIM, (lg + 1) * MXU_DIM)
            gc = gcs[ci, d][:, cols]
            prod = prods[ci, lg]
            d1, d2, d3 = _split3(jnp.where(eye_ls, gc, 0.0))
            grow = ((jnp.dot(ones_c, d1, preferred_element_type=f32)
                     + jnp.dot(ones_c, d2, preferred_element_type=f32))
                    + jnp.dot(ones_c, d3, preferred_element_type=f32))
            keep = (ri >= col_i) if d == 0 else (ri <= col_i)
            strict = (ri > col_i) if d == 0 else (ri < col_i)
            decay = jnp.where(keep, jnp.exp(jnp.where(keep, gc - grow, 0.0)), 0.0)
            a = jnp.where(strict, prod[d * CHUNK:(d + 1) * CHUNK] * decay, 0.0)
            in_s[d, c, :, cols] = jnp.where(keep, prod[2 * CHUNK:] * decay, 0.0).astype(bf16)
            ab = a.astype(bf16)
            avs.append(ab)
            ys.append(jnp.where(blk(INV_BASE), -ab, jnp.zeros((), bf16)))
            ps.append(jnp.where(eye_ls, 1.0, 0.0) - jnp.where(blk(INV_BASE), a, 0.0))
        m = 1
        while m < INV_BASE:
            last = 2 * m >= INV_BASE
            for u in range(len(units)):
                y = ys[u]
                if m == 1:
                    ys[u] = hmm(y, y).astype(bf16)
                elif last:
                    ps[u] = ps[u] + hmm(ps[u], y)
                else:
                    r = hmm(jnp.concatenate([ps[u].astype(bf16), y], axis=0), y)
                    ps[u] = ps[u] + r[:CHUNK]
                    ys[u] = r[CHUNK:].astype(bf16)
            m *= 2
        b = INV_BASE
        while b < CHUNK:
            off = blk(2 * b) & jnp.logical_not(blk(b))
            ws = [hmm(jnp.where(off, avs[u], jnp.zeros((), bf16)), ps[u]) for u in range(len(units))]
            for u in range(len(units)):
                ps[u] = ps[u] - hmm(ps[u], ws[u])
            b *= 2
        for u, (ci, d, lg) in enumerate(units):
            ti_s[d, cidx[ci], :, lg * MXU_DIM:(lg + 1) * MXU_DIM] = ps[u].astype(bf16)
        return carry

    lax.fori_loop(0, nsq * n // PRE_CHUNKS, pre_body, 0)

    for sq in range(nsq):
        for d in range(2):
            for lg in range(NLG):
                if has_s0:
                    blocks = []
                    for hh in range(HPG):
                        s_h = s0_ref[sq, d, lg * HPG + hh]
                        z_l = jnp.zeros((DK_A, hh * DV_A), f32)
                        z_r = jnp.zeros((DK_A, (HPG - 1 - hh) * DV_A), f32)
                        parts = ([z_l] if hh > 0 else []) + [s_h] + ([z_r] if hh < HPG - 1 else [])
                        blocks.append(jnp.concatenate(parts, axis=1) if len(parts) > 1 else s_h)
                    st_s[sq, d, lg] = jnp.concatenate(blocks, axis=0)
                else:
                    st_s[sq, d, lg] = jnp.zeros((MXU_DIM, MXU_DIM), f32)
    o_ref[...] = jnp.zeros_like(o_ref)

    def seq_body(j, carry):
        units = [(sq, d, lg) for sq in range(nsq) for lg in range(NLG) for d in range(2)]
        step = [j, n - 1 - j]
        cls = [slice(lg * MXU_DIM, (lg + 1) * MXU_DIM) for lg in range(NLG)]
        chunk = {(sq, d): sq * n + step[d] for sq in range(nsq) for d in range(2)}
        rws = {key: pl.ds(pl.multiple_of(cc * CHUNK, CHUNK), CHUNK) for key, cc in chunk.items()}
        boths, vnbs = [], []
        for sq, d, lg in units:
            r = rws[sq, d]
            lhs = jnp.concatenate([kbg_s[d, r, cls[lg]], qg_s[d, r, cls[lg]]], axis=0)
            boths.append(jnp.dot(lhs, st_s[sq, d, lg].astype(bf16), preferred_element_type=f32))
        for u, (sq, d, lg) in enumerate(units):
            resid = vb_s[d, rws[sq, d], cls[lg]] - boths[u][:CHUNK]
            v_new = jnp.dot(ti_s[d, chunk[sq, d], :, cls[lg]], bd(resid), preferred_element_type=f32)
            vnbs.append(v_new.astype(bf16))
        for u, (sq, d, lg) in enumerate(units):
            r = rws[sq, d]
            c = chunk[sq, d]
            o = boths[u][CHUNK:] + jnp.dot(in_s[d, c, :, cls[lg]], bd(vnbs[u]), preferred_element_type=f32)
            upd = lax.dot_general(kd_s[d, r, cls[lg]], vnbs[u], (((0,), (0,)), ((), ())),
                                  preferred_element_type=f32)
            st_s[sq, d, lg] = st_s[sq, d, lg] * egt_s[d, c, 0:1, cls[lg]] + jnp.where(bd_mask, upd, 0.0)
            o_ref[r, cls[lg]] = o_ref[r, cls[lg]] + o
        return carry

    lax.fori_loop(0, n, seq_body, 0)
    for sq in range(nsq):
        for d in range(2):
            for lg in range(NLG):
                s_fin = st_s[sq, d, lg]
                for hh in range(HPG):
                    sf_ref[sq, d, lg * HPG + hh] = s_fin[hh * DK_A:(hh + 1) * DK_A, hh * DV_A:(hh + 1) * DV_A]


def _delta(qkv, gb, t, nseq, nsq, tok_block0, s0, s0_layer):
    n = t // CHUNK
    tb = nsq * t
    nch = nsq * n
    has_s0 = s0 is not None
    kern = functools.partial(_delta_kernel, t, nsq, has_s0)
    in_specs = [
        pl.BlockSpec((tb, W_ALL), lambda b: (tok_block0 + b, 0)),
        pl.BlockSpec((tb, W_ALL), lambda b: (tok_block0 + b, 1)),
        pl.BlockSpec((tb, W_ALL), lambda b: (tok_block0 + b, 2)),
        pl.BlockSpec((tb, LANES), lambda b: (tok_block0 + b, 0)),
    ]
    args = [qkv, qkv, qkv, gb]
    if has_s0:
        in_specs.append(pl.BlockSpec((nsq, None, 2, H_A, DK_A, DV_A), lambda b: (b, s0_layer, 0, 0, 0, 0)))
        args.append(s0)
    return pl.pallas_call(
        kern,
        grid=(nseq // nsq,),
        in_specs=in_specs,
        out_specs=[
            pl.BlockSpec((tb, W_ALL), lambda b: (b, 0)),
            pl.BlockSpec((nsq, 2, H_A, DK_A, DV_A), lambda b: (b, 0, 0, 0, 0)),
        ],
        out_shape=[jax.ShapeDtypeStruct((nseq * t, W_ALL), f32),
                   jax.ShapeDtypeStruct((nseq, 2, H_A, DK_A, DV_A), f32)],
        scratch_shapes=[
            pltpu.VMEM((tb, 4 * W_ALL), f32),
            pltpu.VMEM((2, tb, W_ALL), bf16),
            pltpu.VMEM((2, tb, W_ALL), bf16),
            pltpu.VMEM((2, tb, W_ALL), bf16),
            pltpu.VMEM((2, tb, W_ALL), f32),
            pltpu.VMEM((2, nch, 8, W_ALL), f32),
            pltpu.VMEM((2, nch, CHUNK, W_ALL), bf16),
            pltpu.VMEM((2, nch, CHUNK, W_ALL), bf16),
            pltpu.VMEM((nsq, 2, NLG, MXU_DIM, MXU_DIM), f32),
        ],
        compiler_params=_cparams(("arbitrary",)),
        name="delta_p" if not has_s0 else "delta_s",
    )(*args)


def _attend(q_all, key_sets, sink_ref):
    nrow = q_all.shape[0]
    scores = []
    for h in range(H_B):
        kv = h // G_B
        q = q_all[:, h * HD_B:(h + 1) * HD_B]
        row = []
        for k_all, _, mask in key_sets:
            s = _dot_nt(q, k_all[:, kv * HD_B:(kv + 1) * HD_B])
            row.append(s if mask is None else jnp.where(mask, s, NEG_INF))
        scores.append(row)
    probs, sink_terms = [], []
    for h in range(H_B):
        sink = jnp.zeros((nrow, 1), f32) + sink_ref[h]
        m = sink
        for s in scores[h]:
            m = jnp.maximum(m, jnp.max(s, axis=-1, keepdims=True))
        probs.append([jnp.exp(s - m).astype(bf16) for s in scores[h]])
        sink_terms.append(jnp.exp(sink - m))
    v_ext = [[jnp.concatenate([v_all[:, kv * HD_B:(kv + 1) * HD_B].astype(bf16),
                               jnp.ones((v_all.shape[0], HD_B), bf16)], axis=1) for kv in range(KV_B)]
             for _, v_all, _ in key_sets]
    outs = []
    for h in range(H_B):
        acc = None
        for p, vs in zip(probs[h], v_ext):
            pv = jnp.dot(p, vs[h // G_B], preferred_element_type=f32)
            acc = pv if acc is None else acc + pv
        outs.append(acc[:, :HD_B] / (acc[:, HD_B:HD_B + 1] + sink_terms[h]))
    return jnp.concatenate(outs, axis=1)


ATTN_SEQS_P = 4
WIN_QBLKS = 2


def _ctx_attn_kernel(t, sink_ref, q_ref, k_ref, v_ref, o_ref):
    scale = HD_B ** -0.5
    for sq in range(ATTN_SEQS_P):
        rows = slice(sq * t, (sq + 1) * t)
        o_ref[rows, :] = _attend(q_ref[rows, :] * scale, [(k_ref[rows, :], v_ref[rows, :], None)], sink_ref)


def _ctx_attn(sink, qb, kb, vb, t, nseq):
    nq = H_B * HD_B
    nkv = KV_B * HD_B
    tb = ATTN_SEQS_P * t
    return pl.pallas_call(
        functools.partial(_ctx_attn_kernel, t),
        grid=(nseq // ATTN_SEQS_P,),
        in_specs=[
            pl.BlockSpec(memory_space=pltpu.SMEM),
            pl.BlockSpec((tb, nq), lambda b: (b, 0)),
            pl.BlockSpec((tb, nkv), lambda b: (b, 0)),
            pl.BlockSpec((tb, nkv), lambda b: (b, 0)),
        ],
        out_specs=pl.BlockSpec((tb, nq), lambda b: (b, 0)),
        out_shape=jax.ShapeDtypeStruct((nseq * t, nq), f32),
        compiler_params=_cparams(("arbitrary",)),
        name="ctx_attn",
    )(sink, qb, kb, vb)


def _win_attn_kernel(t, sink_ref, q_ref, k_ref, v_ref, kc_ref, vc_ref, o_ref):
    scale = HD_B ** -0.5
    span = QBLK + 2 * WINDOW
    for qb in range(WIN_QBLKS):
        start = (pl.program_id(1) * WIN_QBLKS + qb) * QBLK
        lo = jnp.clip(start - WINDOW, 0, t - span)
        lo = pl.multiple_of(lo, QBLK)
        q_pos = start + lax.broadcasted_iota(jnp.int32, (QBLK, span), 0)
        k_pos = lo + lax.broadcasted_iota(jnp.int32, (QBLK, span), 1)
        valid = jnp.abs(q_pos - k_pos) <= WINDOW
        key_sets = [(k_ref[pl.ds(lo, span), :], v_ref[pl.ds(lo, span), :], valid),
                    (kc_ref[...], vc_ref[...], None)]
        rows = slice(qb * QBLK, (qb + 1) * QBLK)
        o_ref[rows, :] = _attend(q_ref[rows, :] * scale, key_sets, sink_ref)


def _win_attn(sink, qb, kb, vb, cache_k, cache_v, cache_layer, t, nseq, tok_block0):
    nq = H_B * HD_B
    nkv = KV_B * HD_B
    nb = t // (WIN_QBLKS * QBLK)
    past = cache_k.shape[2]
    kern = functools.partial(_win_attn_kernel, t)
    return pl.pallas_call(
        kern,
        grid=(nseq, nb),
        in_specs=[
            pl.BlockSpec(memory_space=pltpu.SMEM),
            pl.BlockSpec((WIN_QBLKS * QBLK, nq), lambda b, i: ((tok_block0 + b) * nb + i, 0)),
            pl.BlockSpec((t, nkv), lambda b, i: (tok_block0 + b, 0)),
            pl.BlockSpec((t, nkv), lambda b, i: (tok_block0 + b, 0)),
            pl.BlockSpec((None, None, past, nkv), lambda b, i: (b, cache_layer, 0, 0)),
            pl.BlockSpec((None, None, past, nkv), lambda b, i: (b, cache_layer, 0, 0)),
        ],
        out_specs=pl.BlockSpec((WIN_QBLKS * QBLK, nq), lambda b, i: (b * nb + i, 0)),
        out_shape=jax.ShapeDtypeStruct((nseq * t, nq), f32),
        compiler_params=_cparams(("arbitrary", "arbitrary")),
        name="win_attn",
    )(sink, qb, kb, vb, cache_k, cache_v)


def _mix_out_kernel(np_tiles, n_y, *refs):
    y_refs = refs[:n_y]
    oap_ref, oas_ref, obp_ref, obs_ref, gate_ref, dn_ref, w_ref, gm_ref = refs[n_y:n_y + 8]
    o_refs = refs[n_y + 8:]
    i = pl.program_id(0)
    is_p = i < np_tiles
    oa = jnp.where(is_p, oap_ref[...], oas_ref[...])
    ob = jnp.where(is_p, obp_ref[...], obs_ref[...])
    oan = oa * lax.rsqrt(_group_sum64(oa * oa) * (1.0 / DV_A) + EPS) * dn_ref[...] * gate_ref[...]
    ka = H_A * DV_A
    mix = _dot(oan, w_ref[:ka, :]) + _dot(ob, w_ref[ka:, :])
    _stream_store(np_tiles, o_refs, _stream_load(np_tiles, y_refs) + gm_ref[...] * mix)


def _mix_out(ys, mods, layer, row_of_tile, np_tiles, oa_p, oa_s, ob_p, ob_s, gate, dn_row, w_out, split_out):
    nt = _stream_tiles(ys)
    ka = H_A * DV_A
    p_idx = lambda i: (jnp.minimum(i, np_tiles - 1), 0)
    s_idx = lambda i: (jnp.maximum(i - np_tiles, 0), 0)
    kern = functools.partial(_mix_out_kernel, np_tiles, len(ys))
    return pl.pallas_call(
        kern,
        grid=(nt,),
        in_specs=_stream_specs(np_tiles, 1, len(ys) == 2) + [
            pl.BlockSpec((TM, ka), p_idx),
            pl.BlockSpec((TM, ka), s_idx),
            pl.BlockSpec((TM, ka), p_idx),
            pl.BlockSpec((TM, ka), s_idx),
            pl.BlockSpec((TM, ka), lambda i: (i, 0)),
            pl.BlockSpec((1, ka), lambda i: (0, 0)),
            pl.BlockSpec((None, 2 * ka, D_MODEL), lambda i: (layer // 2, 0, 0)),
            _mod_spec(layer, 2, row_of_tile, 1),
        ],
        out_specs=_stream_specs(np_tiles, 1, split_out),
        out_shape=_stream_shapes(np_tiles, nt, split_out),
        compiler_params=_cparams(("arbitrary",)),
        name="mix_out",
    )(*ys, oa_p, oa_s, ob_p, ob_s, gate, dn_row, w_out, mods)


def _gelu_tanh(x):
    return 0.5 * x * (1.0 + jnp.tanh(math.sqrt(2.0 / math.pi) * (x + 0.044715 * (x * x * x))))


def _lru_kernel(np_tiles, t_p, t_s, y_ref, g_ref, sh_ref, sc_ref, wx_ref, wg_ref, cw_ref, cb_ref,
                wl_ref, bl_ref, lam_ref, h0_ref, o_ref, fin_ref, h_s, x_s, gt_s, hf_s, hb_s, mk_s):
    is_p = pl.program_id(0) < np_tiles
    tseq = jnp.where(is_p, t_p, t_s)

    @pl.when(pl.program_id(1) == 0)
    def _():
        h_s[...] = _modulate(y_ref[...], g_ref[...], sh_ref[...], sc_ref[...]).astype(bf16)
        _conv_masks(mk_s, tseq)

    w = jnp.concatenate([wx_ref[...], wg_ref[...]], axis=1).astype(bf16)
    z = jnp.dot(h_s[...], w, preferred_element_type=f32)
    x = _seq_conv(z[:, :LRU_W], cw_ref, cb_ref, mk_s)
    x_s[...] = x
    ngrp = LRU_W // LRU_GW
    gcols = [slice(g * LRU_GW, (g + 1) * LRU_GW) for g in range(ngrp)]
    for g in range(ngrp):
        gt_s[g] = _dot(x[:, gcols[g]], wl_ref[g]) + bl_ref[g]
    decay = [[RG_C * _softplus(-lam_ref[g, :, d * LRU_GW:(d + 1) * LRU_GW]) for g in range(ngrp)]
             for d in range(2)]
    sub = lax.broadcasted_iota(jnp.int32, (SUBLANES, 1), 0)
    nblk = TM // SUBLANES
    seg_shift = (t_p // SUBLANES).bit_length() - 1

    def body(kf, carry):
        nxt = []
        for d, kk in ((0, kf), (1, nblk - 1 - kf)):
            r0 = pl.multiple_of(kk * SUBLANES, SUBLANES)
            rows = pl.ds(r0, SUBLANES)
            edge = r0 if d == 0 else r0 + SUBLANES
            at_edge = (edge & (tseq - 1)) == 0
            seg = jnp.where(is_p, kk >> seg_shift, 0)
            for g in range(ngrp):
                xb = x_s[rows, gcols[g]]
                gb = gt_s[g, rows, :]
                r = _sigmoid(gb[:, (2 * d) * LRU_GW:(2 * d + 1) * LRU_GW])
                ig = _sigmoid(gb[:, (2 * d + 1) * LRU_GW:(2 * d + 2) * LRU_GW])
                log_a = -(r * decay[d][g])
                a = jnp.exp(log_a)
                w2 = -jnp.tanh(log_a) * (a * a + 1.0)
                u = jnp.where(w2 > 0.0, w2 * lax.rsqrt(w2), 0.0) * (ig * xb)
                s = 1
                while s < SUBLANES:
                    ok = (sub >= s) if d == 0 else (sub + s < SUBLANES)
                    sh = s if d == 0 else SUBLANES - s
                    u = u + a * jnp.where(ok, pltpu.roll(u, sh, axis=0), 0.0)
                    a = a * jnp.where(ok, pltpu.roll(a, sh, axis=0), 1.0)
                    s *= 2
                c_in = jnp.where(at_edge, h0_ref[seg, d:d + 1, gcols[g]], carry[d * ngrp + g])
                hblk = u + a * c_in
                (hf_s if d == 0 else hb_s)[rows, gcols[g]] = hblk
                nxt.append(hblk[SUBLANES - 1:SUBLANES, :] if d == 0 else hblk[0:1, :])
        return tuple(nxt)

    zero = jnp.zeros((1, LRU_GW), f32)
    lax.fori_loop(0, nblk, body, (zero,) * (2 * ngrp), unroll=4)
    for sg in range(TM // t_p):
        fin_ref[sg, 0:1, :] = hf_s[sg * t_p + t_p - 1:sg * t_p + t_p, :]
        fin_ref[sg, 1:2, :] = hb_s[sg * t_p:sg * t_p + 1, :]
    o_ref[...] = (hf_s[...] + hb_s[...]) * _gelu_tanh(z[:, LRU_W:])


def _lru(y, mods, layer, row_of_tile, np_tiles, t_p, t_s, norm_g, w_in, cw, cb, wl, bl, lam, h0):
    ntok = y.shape[0]
    nt = ntok // TM
    ncol = D_RNN // LRU_W
    ngrp = LRU_W // LRU_GW
    nseg = TM // t_p
    kern = functools.partial(_lru_kernel, np_tiles, t_p, t_s)
    return pl.pallas_call(
        kern,
        grid=(nt, ncol),
        in_specs=[
            pl.BlockSpec((TM, D_MODEL), lambda i, j: (i, 0)),
            pl.BlockSpec((1, D_MODEL), lambda i, j: (0, 0)),
            _mod_spec(layer, 0, row_of_tile, 2),
            _mod_spec(layer, 1, row_of_tile, 2),
            pl.BlockSpec((None, D_MODEL, LRU_W), lambda i, j: (layer // 2, 0, j)),
            pl.BlockSpec((None, D_MODEL, LRU_W), lambda i, j: (layer // 2, 0, ncol + j)),
            pl.BlockSpec((None, CONV_W, LRU_W), lambda i, j: (layer // 2, 0, j)),
            pl.BlockSpec((1, LRU_W), lambda i, j: (0, j)),
            pl.BlockSpec((ngrp, LRU_GW, 4 * LRU_GW), lambda i, j: (j, 0, 0)),
            pl.BlockSpec((ngrp, 1, 4 * LRU_GW), lambda i, j: (j, 0, 0)),
            pl.BlockSpec((ngrp, 1, 2 * LRU_GW), lambda i, j: (j, 0, 0)),
            pl.BlockSpec((None, nseg, 2, LRU_W), lambda i, j: (i, 0, 0, j)),
        ],
        out_specs=[
            pl.BlockSpec((TM, LRU_W), lambda i, j: (i, j)),
            pl.BlockSpec((None, nseg, 2, LRU_W), lambda i, j: (i, 0, 0, j)),
        ],
        out_shape=[jax.ShapeDtypeStruct((ntok, D_RNN), f32),
                   jax.ShapeDtypeStruct((nt, nseg, 2, D_RNN), f32)],
        scratch_shapes=[pltpu.VMEM((TM, D_MODEL), bf16), pltpu.VMEM((TM, LRU_W), f32),
                        pltpu.VMEM((ngrp, TM, 4 * LRU_GW), f32), pltpu.VMEM((TM, LRU_W), f32),
                        pltpu.VMEM((TM, LRU_W), f32), pltpu.VMEM((CONV_W - 1, TM, LANES), f32)],
        compiler_params=_cparams(("arbitrary", "arbitrary")),
        name="lru",
    )(y, norm_g, mods, mods, w_in, w_in, cw, cb, wl, bl, lam, h0)


def _proj_out_kernel(y_ref, a_ref, w_ref, gm_ref, o_ref):
    o_ref[...] = y_ref[...] + gm_ref[...] * _dot(a_ref[...], w_ref[...])


def _proj_out(y, mods, layer, row_of_tile, a, w):
    ntok = y.shape[0]
    nt = ntok // TM
    k = a.shape[1]
    return pl.pallas_call(
        _proj_out_kernel,
        grid=(nt,),
        in_specs=[
            pl.BlockSpec((TM, D_MODEL), lambda i: (i, 0)),
            pl.BlockSpec((TM, k), lambda i: (i, 0)),
            pl.BlockSpec((None, k, D_MODEL), lambda i: (layer // 2, 0, 0)),
            _mod_spec(layer, 2, row_of_tile, 1),
        ],
        out_specs=pl.BlockSpec((TM, D_MODEL), lambda i: (i, 0)),
        out_shape=jax.ShapeDtypeStruct((ntok, D_MODEL), f32),
        compiler_params=_cparams(("arbitrary",)),
        name="proj_out",
    )(y, a, w, mods)


FF_BLK = 1024


def _mlp_kernel(np_tiles, n_out, y_ref, g_ref, sh_ref, sc_ref, gm_ref, w1_ref, w2_ref, *refs):
    o_refs = refs[:n_out]
    h_s, acc_s = refs[n_out:]
    k = pl.program_id(1)

    @pl.when(k == 0)
    def _():
        h_s[...] = _modulate(y_ref[...], g_ref[...], sh_ref[...], sc_ref[...]).astype(bf16)
        acc_s[...] = jnp.zeros_like(acc_s)

    a = jnp.dot(h_s[...], w1_ref[...].astype(bf16), preferred_element_type=f32)
    a = jnp.maximum(a, 0.0)
    acc_s[...] += _dot(a * a, w2_ref[...])

    @pl.when(k == pl.num_programs(1) - 1)
    def _():
        _stream_store(np_tiles, o_refs, y_ref[...] + gm_ref[...] * acc_s[...])


def _mlp(y, mods, layer, row_of_tile, np_tiles, norm_g, w1, w2, split_out):
    nt = y.shape[0] // TM
    return pl.pallas_call(
        functools.partial(_mlp_kernel, np_tiles, 2 if split_out else 1),
        grid=(nt, D_FF // FF_BLK),
        in_specs=[
            pl.BlockSpec((TM, D_MODEL), lambda i, k: (i, 0)),
            pl.BlockSpec((1, D_MODEL), lambda i, k: (0, 0)),
            _mod_spec(layer, 3, row_of_tile, 2),
            _mod_spec(layer, 4, row_of_tile, 2),
            _mod_spec(layer, 5, row_of_tile, 2),
            pl.BlockSpec((None, D_MODEL, FF_BLK), lambda i, k: (layer, 0, k)),
            pl.BlockSpec((None, FF_BLK, D_MODEL), lambda i, k: (layer, k, 0)),
        ],
        out_specs=_stream_specs(np_tiles, 2, split_out),
        out_shape=_stream_shapes(np_tiles, nt, split_out),
        scratch_shapes=[pltpu.VMEM((TM, D_MODEL), bf16), pltpu.VMEM((TM, D_MODEL), f32)],
        compiler_params=_cparams(("arbitrary", "arbitrary")),
        name="mlp",
    )(y, norm_g, mods, mods, mods, w1, w2)


def _attn_cols(w_in, a_log, dt_bias):
    s2 = N_QKV + H_A * DV_A
    s4 = s2 + 4 * H_A
    pad = jnp.zeros((D_MODEL, LANES - 4 * H_A), f32)
    w_b = jnp.concatenate([w_in[:, N_QKV:s2], w_in[:, s4:], w_in[:, s2:s4], pad], axis=1)
    row_pad = jnp.zeros((LANES - 2 * H_A,), f32)
    al_row = jnp.concatenate([a_log.reshape(-1), row_pad]).reshape(1, LANES)
    dt_row = jnp.concatenate([dt_bias.reshape(-1), row_pad]).reshape(1, LANES)
    return w_b, al_row, dt_row


def _rope_tables(t):
    rows = t // GRID_W
    r = np.repeat(np.arange(rows, dtype=np.float32), GRID_W)
    cc = np.tile(np.arange(GRID_W, dtype=np.float32), rows)
    inv = np.float32(ROPE_BASE) ** (-np.arange(0, ROPE_AXIS, 2, dtype=np.float32) / np.float32(ROPE_AXIS))
    ang_r = (r[:, None] * inv).astype(np.float32)
    ang_c = (cc[:, None] * inv).astype(np.float32)
    cos = np.concatenate([np.cos(ang_r), np.cos(ang_r), np.cos(ang_c), np.cos(ang_c)], axis=-1)
    sin = np.concatenate([-np.sin(ang_r), np.sin(ang_r), -np.sin(ang_c), np.sin(ang_c)], axis=-1)
    cos = np.tile(cos, (1, H_B)).astype(np.float32)
    sin = np.tile(sin, (1, H_B)).astype(np.float32)
    ident_c = np.ones((TM, H_B * HD_B), np.float32)
    ident_s = np.zeros((TM, H_B * HD_B), np.float32)
    return (jnp.asarray(np.concatenate([ident_c, cos], axis=0)),
            jnp.asarray(np.concatenate([ident_s, sin], axis=0)))


def _lru_cols(w_a, b_a, w_x, b_x, lam):
    ncol = D_RNN // LRU_GW
    per = LRU_GW // LRU_BW

    def bd(w):
        w = w.reshape(ncol, per, LRU_BW, LRU_BW)
        eye = jnp.eye(per, dtype=f32)
        return jnp.einsum('gpij,pq->gpiqj', w, eye).reshape(ncol, LRU_GW, LRU_GW)

    wl = jnp.concatenate([bd(w_a[0]), bd(w_x[0]), bd(w_a[1]), bd(w_x[1])], axis=-1)
    rows = lambda v: v.reshape(ncol, 1, LRU_GW)
    bl = jnp.concatenate([rows(b_a[0]), rows(b_x[0]), rows(b_a[1]), rows(b_x[1])], axis=-1)
    lm = jnp.concatenate([rows(lam[0]), rows(lam[1])], axis=-1)
    return wl, bl, lm


def kernel(x_prompt, x_sample, state_delta, cache_k, cache_v, state_lru, c, c_ctx, ada_w, ada_b, norm1_g, norm2_g, ff_w1, ff_w2, ab_w_in, ab_conv_w, ab_conv_b, dn_a_log, dn_dt_bias, dn_norm_g, attn_q_norm_g, attn_k_norm_g, attn_sink, ab_w_out, c_w_in, c_conv_w, c_conv_b, lru_w_a, lru_b_a, lru_w_x, lru_b_x, lru_lambda, c_w_out):
    b_p, t_p, _ = x_prompt.shape
    b_s, t_s, _ = x_sample.shape
    assert t_s == TM and TM % t_p == 0 and (b_p * t_p) % TM == 0
    np_tok = b_p * t_p
    np_tiles = np_tok // TM
    nkv = KV_B * HD_B

    ys = (x_prompt.reshape(np_tok, D_MODEL), x_sample.reshape(b_s * t_s, D_MODEL))
    nt = np_tiles + b_s * t_s // TM

    rows = -(-(b_s + 1) // 8) * 8
    cond = jnp.zeros((rows, D_MODEL), f32).at[:b_s].set(c).at[b_s].set(c_ctx)
    mods = _adaln(cond, ada_w, ada_b)
    row_of_tile = lambda i: jnp.where(i < np_tiles, b_s, i - np_tiles)

    cos_tab, sin_tab = _rope_tables(t_s)
    ck = cache_k.reshape(cache_k.shape[0], cache_k.shape[1], cache_k.shape[2], nkv)
    cv = cache_v.reshape(ck.shape)

    new_dn, new_k, new_v, new_lru = [], [], [], []
    for l in range(DEPTH):
        j = l // 2
        n1 = norm1_g[l].reshape(1, D_MODEL)
        if l % 2 == 0:
            qkv = _delta_proj(ys, mods, l, row_of_tile, np_tiles, t_p, t_s, n1, ab_w_in[j], ab_conv_w,
                              ab_conv_b[j].reshape(1, N_QKV))
            w_b, al_row, dt_row = _attn_cols(ab_w_in[j], dn_a_log[j], dn_dt_bias[j])
            qn_row = jnp.tile(attn_q_norm_g[j], H_B).reshape(1, H_B * HD_B)
            kn_row = jnp.tile(attn_k_norm_g[j], KV_B).reshape(1, nkv)
            gate, qb, kb, vb, gb, kc, vc = _attn_proj(ys, mods, l, row_of_tile, np_tiles, n1, w_b, qn_row, kn_row,
                                                      cos_tab, sin_tab, al_row, dt_row)
            oa_p, s_fin = _delta(qkv, gb, t_p, b_p, DELTA_SEQS_P, 0, None, 0)
            oa_s, _ = _delta(qkv, gb, t_s, b_s, 1, np_tok // t_s, state_delta, j)
            ob_p = _ctx_attn(attn_sink[j], qb, kb, vb, t_p, b_p)
            ob_s = _win_attn(attn_sink[j], qb, kb, vb, ck, cv, j, t_s, b_s, np_tok // t_s)
            dn_row = jnp.tile(dn_norm_g[j], H_A).reshape(1, H_A * DV_A)
            (y,) = _mix_out(ys, mods, l, row_of_tile, np_tiles, oa_p, oa_s, ob_p, ob_s, gate, dn_row, ab_w_out,
                            False)
            new_dn.append(s_fin)
            new_k.append(kc.reshape(b_p, t_p, KV_B, HD_B))
            new_v.append(vc.reshape(b_p, t_p, KV_B, HD_B))
        else:
            wl, bl, lm = _lru_cols(lru_w_a[j], lru_b_a[j], lru_w_x[j], lru_b_x[j], lru_lambda[j])
            nseg = TM // t_p
            h0 = jnp.zeros((nt, nseg, 2, D_RNN), f32).at[np_tiles:, 0].set(state_lru[:, j])
            mixed, fin = _lru(ys[0], mods, l, row_of_tile, np_tiles, t_p, t_s, n1, c_w_in, c_conv_w,
                              c_conv_b[j].reshape(1, D_RNN), wl, bl, lm, h0)
            y = _proj_out(ys[0], mods, l, row_of_tile, mixed, c_w_out)
            new_lru.append(fin[:np_tiles].reshape(b_p, 2, D_RNN))
        ys = tuple(_mlp(y, mods, l, row_of_tile, np_tiles, norm2_g[l].reshape(1, D_MODEL), ff_w1, ff_w2,
                        l == DEPTH - 1))

    yp = ys[0].reshape(b_p, t_p, D_MODEL)
    ysm = ys[1].reshape(b_s, t_s, D_MODEL)
    return (yp, ysm, jnp.stack(new_dn, axis=1), jnp.stack(new_k, axis=1), jnp.stack(new_v, axis=1),
            jnp.stack(new_lru, axis=1))
```
